```python
import math
import jax, jax.numpy as jnp
from jax import lax
import numpy as np

D_MODEL = 1024
BATCH = 8
SEQ = 2048
DEPTH = 1
DEC_BATCH = 128
DEC_SEQ = 4
PAST_LEN = 2048
PAGE_SIZE = 128

MIX_WIDTH = D_MODEL
ATTN_WIDTH = MIX_WIDTH // 2
CONV_WIDTH = MIX_WIDTH - ATTN_WIDTH
N_HEADS = 4
N_SUB = 2 * N_HEADS
HEAD_DIM = ATTN_WIDTH // N_SUB
CONV_K = 3
ROPE_THETA = 10000.0
Q_BLOCK = 128
PROJ_COLS = 3 * ATTN_WIDTH + 3 * CONV_WIDTH
N_GROUPS = 4
EXP_PER_GROUP = 8
N_EXPERTS = N_GROUPS * EXP_PER_GROUP
TOP_K_EXP = 2
D_EXPERT = D_MODEL // 4
NORM_EPS = 1e-6
SUBLN_EPS = 1e-5

kernel_name = 'hymba_diffattn_shortconv_hmoe_adaln_step'


def _rmsnorm(x, g, eps=NORM_EPS):
    xf = x.astype(jnp.float32)
    y = xf * lax.rsqrt(jnp.mean(xf * xf, axis=-1, keepdims=True) + eps)
    return (y * g.astype(jnp.float32)).astype(x.dtype)


def _rope(x, pos):
    d = x.shape[-1]
    inv = 1.0 / (ROPE_THETA ** (jnp.arange(0, d, 2, dtype=jnp.float32) / d))
    ang = pos.astype(jnp.float32)[:, None] * inv[None, :]
    ang = jnp.concatenate([ang, ang], axis=-1)
    cos = jnp.cos(ang)[None, :, None, :]
    sin = jnp.sin(ang)[None, :, None, :]
    xf = x.astype(jnp.float32)
    rot = jnp.concatenate([-xf[..., d // 2:], xf[..., :d // 2]], axis=-1)
    return (xf * cos + rot * sin).astype(x.dtype)


def _diff_weights(s, lam):
    p = jax.nn.softmax(s, axis=-1)
    b, _, q, k = p.shape
    p = p.reshape(b, N_HEADS, 2, q, k)
    return p[:, :, 0] - lam * p[:, :, 1]


def _diff_attn_prompt(q, k, v, lam):
    b, s_len = q.shape[0], q.shape[1]
    nb = s_len // Q_BLOCK
    scale = HEAD_DIM ** -0.5
    qb = q.reshape(b, nb, Q_BLOCK, N_SUB, HEAD_DIM).transpose(1, 0, 2, 3, 4)
    k_pos = jnp.arange(s_len)

    def block(args):
        q_blk, i = args
        s = jnp.einsum('bqhd,bkhd->bhqk', q_blk, k, preferred_element_type=jnp.float32) * scale
        q_pos = i * Q_BLOCK + jnp.arange(Q_BLOCK)
        s = jnp.where(k_pos[None, :] <= q_pos[:, None], s, -jnp.inf)
        a = _diff_weights(s, lam)
        return jnp.einsum('bhqk,bkhe->bqhe', a.astype(v.dtype), v,
                          preferred_element_type=jnp.float32).astype(v.dtype)

    out = lax.map(block, (qb, jnp.arange(nb)))
    return out.transpose(1, 0, 2, 3, 4).reshape(b, s_len, N_HEADS, 2 * HEAD_DIM)


def _diff_attn_sample(q, k, v, k_past, v_past, lam):
    l_new, p_len = q.shape[1], k_past.shape[1]
    scale = HEAD_DIM ** -0.5
    s_past = jnp.einsum('bqhd,bkhd->bhqk', q, k_past, preferred_element_type=jnp.float32) * scale
    s_new = jnp.einsum('bqhd,bkhd->bhqk', q, k, preferred_element_type=jnp.float32) * scale
    causal = jnp.tril(jnp.ones((l_new, l_new), dtype=bool))
    s_new = jnp.where(causal, s_new, -jnp.inf)
    a = _diff_weights(jnp.concatenate([s_past, s_new], axis=-1), lam).astype(v.dtype)
    out = (jnp.einsum('bhqk,bkhe->bqhe', a[..., :p_len], v_past, preferred_element_type=jnp.float32)
           + jnp.einsum('bhqk,bkhe->bqhe', a[..., p_len:], v, preferred_element_type=jnp.float32))
    return out.astype(v.dtype)


def _hier_moe(h, w_rg, b_rg, w_re, b_re, w_gate, w_up, w_down):
    b, l, d = h.shape
    t = h.reshape(-1, d)
    grp_prob = jax.nn.softmax((t @ w_rg + b_rg).astype(jnp.float32), axis=-1)
    g_p, g_idx = lax.top_k(grp_prob, 1)
    e_logits = (t @ w_re + b_re).astype(jnp.float32).reshape(-1, N_GROUPS, EXP_PER_GROUP)
    in_grp = jnp.take_along_axis(e_logits, g_idx[:, :, None], axis=1)[:, 0]
    e_val, e_idx = lax.top_k(in_grp, TOP_K_EXP)
    e_w = jax.nn.softmax(e_val, axis=-1) * g_p
    eid = g_idx * EXP_PER_GROUP + e_idx
    combine = jnp.sum(jax.nn.one_hot(eid, N_EXPERTS, dtype=jnp.float32) * e_w[..., None], axis=1)
    out = jnp.zeros((t.shape[0], d), jnp.float32)
    for e in range(N_EXPERTS):
        hid = jax.nn.silu(t @ w_gate[e]) * (t @ w_up[e])
        out = out + combine[:, e:e + 1] * (hid @ w_down[e]).astype(jnp.float32)
    return out.astype(h.dtype).reshape(b, l, d)


def _layer(x, c, pos, conv_prev, k_past, v_past, lam_init,
           w_ada, b_ada, ln_mix, w_in, lam_q1, lam_k1, lam_q2, lam_k2, subln_g, w_conv, w_out,
           ln_ffn, w_rg, b_rg, w_re, b_re, w_gate, w_up, w_down):
    b, l, _ = x.shape
    mod = (jax.nn.silu(c) @ w_ada + b_ada)[:, None, :]
    sh_a, sc_a, g_a, sh_f, sc_f, g_f = jnp.split(mod, 6, axis=-1)

    h = _rmsnorm(x, ln_mix) * (1 + sc_a) + sh_a
    proj = h @ w_in
    cuts = [ATTN_WIDTH, 2 * ATTN_WIDTH, 3 * ATTN_WIDTH,
            3 * ATTN_WIDTH + CONV_WIDTH, 3 * ATTN_WIDTH + 2 * CONV_WIDTH]
    q, k, v, bg, cg, u = jnp.split(proj, cuts, axis=-1)

    q = _rope(q.reshape(b, l, N_SUB, HEAD_DIM), pos)
    k = _rope(k.reshape(b, l, N_SUB, HEAD_DIM), pos)
    v = v.reshape(b, l, N_HEADS, 2 * HEAD_DIM)
    f32 = jnp.float32
    lam = (jnp.exp(jnp.sum(lam_q1.astype(f32) * lam_k1.astype(f32)))
           - jnp.exp(jnp.sum(lam_q2.astype(f32) * lam_k2.astype(f32))) + lam_init)
    if k_past is None:
        att = _diff_attn_prompt(q, k, v, lam)
    else:
        att = _diff_attn_sample(q, k, v, k_past, v_past, lam)
    att = (_rmsnorm(att, subln_g, SUBLN_EPS) * (1.0 - lam_init)).reshape(b, l, ATTN_WIDTH)

    cu = cg * u
    padded = jnp.concatenate([conv_prev.astype(cu.dtype), cu], axis=1)
    conv = lax.conv_general_dilated(padded, w_conv[:, None, :].astype(cu.dtype), window_strides=(1,),
                                    padding='VALID', dimension_numbers=('NWC', 'WIO', 'NWC'),
                                    feature_group_count=CONV_WIDTH)
    conv_out = bg * conv
    new_conv = padded[:, -(CONV_K - 1):]

    mix = jnp.concatenate([att, conv_out], axis=-1) @ w_out
    x = x + g_a * mix

    h2 = _rmsnorm(x, ln_ffn) * (1 + sc_f) + sh_f
    x = x + g_f * _hier_moe(h2, w_rg, b_rg, w_re, b_re, w_gate, w_up, w_down)
    return x, k, v, new_conv


def setup_inputs(seed: int = 0) -> dict:
    key = jax.random.key(seed)
    ks = jax.random.split(key, 32)
    f32 = jnp.float32

    def nrm(k, shape, s):
        return jax.random.normal(k, shape, f32) * s

    n_pages = PAST_LEN // PAGE_SIZE
    n_used = DEC_BATCH * n_pages
    n_phys = n_used + (n_used + 3) // 4
    page_table = jax.random.permutation(ks[5], n_phys)[:n_used].reshape(DEC_BATCH, n_pages).astype(jnp.int32)
    return {
        'x_prompt': nrm(ks[0], (BATCH, SEQ, D_MODEL), 1.0),
        'x_sample': nrm(ks[1], (DEC_BATCH, DEC_SEQ, D_MODEL), 1.0),
        'cache_k': nrm(ks[2], (DEPTH, n_phys, PAGE_SIZE, N_SUB, HEAD_DIM), 1.0),
        'cache_v': nrm(ks[3], (DEPTH, n_phys, PAGE_SIZE, N_HEADS, 2 * HEAD_DIM), 1.0),
        'state_conv': nrm(ks[4], (DEPTH, DEC_BATCH, CONV_K - 1, CONV_WIDTH), 1.0),
        'page_table': page_table,
        'c_prompt': nrm(ks[6], (BATCH, D_MODEL), 1.0),
        'c_sample': nrm(ks[7], (DEC_BATCH, D_MODEL), 1.0),
        'w_ada': nrm(ks[8], (DEPTH, D_MODEL, 6 * D_MODEL), 0.5 * D_MODEL ** -0.5),
        'b_ada': nrm(ks[9], (DEPTH, 6 * D_MODEL), 0.02),
        'ln_mix': 1.0 + nrm(ks[10], (DEPTH, D_MODEL), 0.02),
        'w_in': nrm(ks[11], (DEPTH, D_MODEL, PROJ_COLS), D_MODEL ** -0.5),
        'lam_q1': nrm(ks[12], (DEPTH, HEAD_DIM), 0.1),
        'lam_k1': nrm(ks[13], (DEPTH, HEAD_DIM), 0.1),
        'lam_q2': nrm(ks[14], (DEPTH, HEAD_DIM), 0.1),
        'lam_k2': nrm(ks[15], (DEPTH, HEAD_DIM), 0.1),
        'subln_g': 1.0 + nrm(ks[16], (DEPTH, 2 * HEAD_DIM), 0.02),
        'w_conv': nrm(ks[17], (DEPTH, CONV_K, CONV_WIDTH), CONV_K ** -0.5),
        'w_out': nrm(ks[18], (DEPTH, MIX_WIDTH, D_MODEL), MIX_WIDTH ** -0.5),
        'ln_ffn': 1.0 + nrm(ks[19], (DEPTH, D_MODEL), 0.02),
        'w_router_grp': nrm(ks[20], (DEPTH, D_MODEL, N_GROUPS), D_MODEL ** -0.5),
        'b_router_grp': nrm(ks[21], (DEPTH, N_GROUPS), 0.01),
        'w_router_exp': nrm(ks[22], (DEPTH, D_MODEL, N_EXPERTS), D_MODEL ** -0.5),
        'b_router_exp': nrm(ks[23], (DEPTH, N_EXPERTS), 0.01),
        'w_gate': nrm(ks[24], (DEPTH, N_EXPERTS, D_MODEL, D_EXPERT), D_MODEL ** -0.5),
        'w_up': nrm(ks[25], (DEPTH, N_EXPERTS, D_MODEL, D_EXPERT), D_MODEL ** -0.5),
        'w_down': nrm(ks[26], (DEPTH, N_EXPERTS, D_EXPERT, D_MODEL), D_EXPERT ** -0.5),
        'ln_final': 1.0 + nrm(ks[27], (D_MODEL,), 0.02),
    }


def reference(x_prompt, x_sample, cache_k, cache_v, state_conv, page_table, c_prompt, c_sample,
              w_ada, b_ada, ln_mix, w_in, lam_q1, lam_k1, lam_q2, lam_k2, subln_g, w_conv, w_out,
              ln_ffn, w_router_grp, b_router_grp, w_router_exp, b_router_exp, w_gate, w_up, w_down,
              ln_final):
    b_p, s_p = x_prompt.shape[0], x_prompt.shape[1]
    b_s, s_s = x_sample.shape[0], x_sample.shape[1]
    past = page_table.shape[1] * cache_k.shape[2]
    pos_p = jnp.arange(s_p, dtype=jnp.int32)
    pos_s = past + jnp.arange(s_s, dtype=jnp.int32)

    xp, xs = x_prompt, x_sample
    kp_l, vp_l, cp_l, ks_l, vs_l, cs_l = [], [], [], [], [], []
    for l in range(DEPTH):
        lam_init = 0.8 - 0.6 * math.exp(-0.3 * l)
        wl = (w_ada[l], b_ada[l], ln_mix[l], w_in[l], lam_q1[l], lam_k1[l], lam_q2[l], lam_k2[l],
              subln_g[l], w_conv[l], w_out[l], ln_ffn[l], w_router_grp[l], b_router_grp[l],
              w_router_exp[l], b_router_exp[l], w_gate[l], w_up[l], w_down[l])
        conv0 = jnp.zeros((b_p, CONV_K - 1, CONV_WIDTH), xp.dtype)
        xp, kp, vp, cp = _layer(xp, c_prompt, pos_p, conv0, None, None, lam_init, *wl)
        k_past = cache_k[l, page_table].reshape(b_s, past, N_SUB, HEAD_DIM)
        v_past = cache_v[l, page_table].reshape(b_s, past, N_HEADS, 2 * HEAD_DIM)
        xs, ksm, vsm, csm = _layer(xs, c_sample, pos_s, state_conv[l], k_past, v_past, lam_init, *wl)
        kp_l.append(kp); vp_l.append(vp); cp_l.append(cp)
        ks_l.append(ksm); vs_l.append(vsm); cs_l.append(csm)

    y_prompt = _rmsnorm(xp, ln_final)
    y_sample = _rmsnorm(xs, ln_final)
    k_prompt = jnp.stack(kp_l); v_prompt = jnp.stack(vp_l); conv_prompt = jnp.stack(cp_l)
    k_sample = jnp.stack(ks_l); v_sample = jnp.stack(vs_l); conv_sample = jnp.stack(cs_l)
    return (y_prompt, y_sample, k_prompt, v_prompt, conv_prompt, k_sample, v_sample, conv_sample)
```

```python
import functools
import math

import jax
import jax.numpy as jnp
from jax import lax
from jax.experimental import pallas as pl
from jax.experimental.pallas import tpu as pltpu

D_MODEL = 1024
ATTN_WIDTH = 512
CONV_WIDTH = 512
N_HEADS = 4
N_SUB = 8
HEAD_DIM = 64
V_DIM = 2 * HEAD_DIM
CONV_K = 3
ROPE_THETA = 10000.0
N_GROUPS = 4
EXP_PER_GROUP = 8
N_EXPERTS = 32
D_EXPERT = 256
NORM_EPS = 1e-6
SUBLN_EPS = 1e-5
LAM_INIT = 0.8 - 0.6 * math.exp(-0.3 * 0)
PAGE = 128
LANES = 128
ROUTE_GRP_LANE = 32

TM = 512
TQ = 512
TE = 256
VMEM_LIMIT = 56 * 1024 * 1024

f32 = jnp.float32
bf16 = jnp.bfloat16


def _cparams(sem):
    return pltpu.CompilerParams(dimension_semantics=sem, vmem_limit_bytes=VMEM_LIMIT)


def _rms(x, g, eps):
    return x * lax.rsqrt(jnp.mean(x * x, axis=-1, keepdims=True) + eps) * g


def _silu(x):
    return x * (1.0 / (1.0 + jnp.exp(-x)))


def _mod_kernel(c_ref, w_ref, b_ref, o_ref):
    a = _silu(c_ref[...]).astype(bf16)
    o_ref[...] = jnp.dot(a, w_ref[...].astype(bf16), preferred_element_type=f32) + b_ref[...]


def _mod_call(c_all, w_ada, b_ada):
    n = c_all.shape[0]
    return pl.pallas_call(
        _mod_kernel,
        grid=(6,),
        in_specs=[pl.BlockSpec((n, D_MODEL), lambda j: (0, 0)),
                  pl.BlockSpec((D_MODEL, D_MODEL), lambda j: (0, j)),
                  pl.BlockSpec((1, D_MODEL), lambda j: (0, j))],
        out_specs=pl.BlockSpec((n, D_MODEL), lambda j: (0, j)),
        out_shape=jax.ShapeDtypeStruct((n, 6 * D_MODEL), f32),
        compiler_params=_cparams(("arbitrary",)),
        name="mod",
    )(c_all, w_ada, b_ada)


def _rope(t, cos, sin_signed, lo_mask):
    n = t.shape[-1]
    rot = jnp.where(lo_mask, pltpu.roll(t, n - HEAD_DIM // 2, 1), pltpu.roll(t, HEAD_DIM // 2, 1))
    return t * cos + rot * sin_signed


def _inproj_common(x, sh, sc, ln, w_ref, cos, sin_signed):
    h = (_rms(x, ln, NORM_EPS) * (1.0 + sc) + sh).astype(bf16)

    def sec(i):
        return jnp.dot(h, w_ref[:, i * 512:(i + 1) * 512], preferred_element_type=f32)

    lane = lax.broadcasted_iota(jnp.int32, (x.shape[0], 512), 1)
    lo_mask = (lane % HEAD_DIM) < (HEAD_DIM // 2)
    q = _rope(sec(0), cos, sin_signed, lo_mask) * (HEAD_DIM ** -0.5)
    k = _rope(sec(1), cos, sin_signed, lo_mask)
    v = sec(2)
    bg = sec(3)
    cu = sec(4) * sec(5)
    return q, k, v, bg, cu


def _inproj_p_kernel(x_ref, sh_ref, sc_ref, ln_ref, w_ref, cos_ref, sin_ref, wc_ref,
                     q_ref, kf_ref, kb_ref, vf_ref, vb_ref, cv_ref, st_ref, carry_ref):
    s = pl.program_id(0)
    b = pl.program_id(1)
    q, k, v, bg, cu = _inproj_common(x_ref[...], sh_ref[...], sc_ref[...], ln_ref[...], w_ref,
                                     cos_ref[...], sin_ref[...])
    q_ref[...] = q.astype(bf16)
    kf_ref[...] = k
    kb_ref[...] = k.astype(bf16)
    vf_ref[...] = v
    vb_ref[...] = v.astype(bf16)
    tm = cu.shape[0]
    prev = jnp.where(s > 0, carry_ref[b], 0.0)
    row = lax.broadcasted_iota(jnp.int32, cu.shape, 0)
    cu1 = jnp.where(row == 0, prev[1:2], pltpu.roll(cu, 1, 0))
    cu2 = jnp.where(row == 0, prev[0:1], jnp.where(row == 1, prev[1:2], pltpu.roll(cu, 2, 0)))
    wc = wc_ref[...]
    conv = wc[0:1] * cu2 + wc[1:2] * cu1 + wc[2:3] * cu
    cv_ref[...] = (bg * conv).astype(bf16)
    last2 = cu[tm - 2:tm]
    carry_ref[b, 0:2, :] = last2
    st_ref[b] = last2


def _inproj_p_call(x, mod3, ln_mix, w_in_bf, cos, sin_signed, w_conv):
    B, S, _ = x.shape
    ns = S // TM
    row = lambda s, b: (b, s, 0)
    o512 = lambda dt: jax.ShapeDtypeStruct((B, S, 512), dt)
    return pl.pallas_call(
        _inproj_p_kernel,
        grid=(ns, B),
        in_specs=[pl.BlockSpec((None, TM, D_MODEL), row),
                  pl.BlockSpec((None, 1, D_MODEL), lambda s, b: (b, 0, 0)),
                  pl.BlockSpec((None, 1, D_MODEL), lambda s, b: (b, 0, 1)),
                  pl.BlockSpec((1, D_MODEL), lambda s, b: (0, 0)),
                  pl.BlockSpec((D_MODEL, 3072), lambda s, b: (0, 0)),
                  pl.BlockSpec((TM, 512), lambda s, b: (s, 0)),
                  pl.BlockSpec((TM, 512), lambda s, b: (s, 0)),
                  pl.BlockSpec((CONV_K, CONV_WIDTH), lambda s, b: (0, 0))],
        out_specs=[pl.BlockSpec((None, TM, 512), row)] * 6
                  + [pl.BlockSpec((B, 2, CONV_WIDTH), lambda s, b: (0, 0, 0))],
        out_shape=[o512(bf16), o512(f32), o512(bf16), o512(f32), o512(bf16), o512(bf16),
                   jax.ShapeDtypeStruct((B, 2, CONV_WIDTH), f32)],
        scratch_shapes=[pltpu.VMEM((B, 8, CONV_WIDTH), f32)],
        compiler_params=_cparams(("arbitrary", "arbitrary")),
        name="inproj_p",
    )(x, mod3, mod3, ln_mix, w_in_bf, cos, sin_signed, w_conv)


def _inproj_s_kernel(x_ref, sh_ref, sc_ref, ln_ref, w_ref, cos_ref, sin_ref, wc_ref, st_in_ref,
                     q_ref, kf_ref, vf_ref, cv_ref, st_ref):
    q, k, v, bg, cu = _inproj_common(x_ref[...], sh_ref[...], sc_ref[...], ln_ref[...], w_ref,
                                     cos_ref[...], sin_ref[...])
    q_ref[...] = q.astype(bf16)
    kf_ref[...] = k
    vf_ref[...] = v
    nb = st_in_ref.shape[1]
    st0 = st_in_ref[0]
    st1 = st_in_ref[1]
    cu1 = jnp.concatenate([st1, cu[:3 * nb]], axis=0)
    cu2 = jnp.concatenate([st0, st1, cu[:2 * nb]], axis=0)
    wc = wc_ref[...]
    conv = wc[0:1] * cu2 + wc[1:2] * cu1 + wc[2:3] * cu
    cv_ref[...] = (bg * conv).astype(bf16)
    st_ref[0] = cu[2 * nb:3 * nb]
    st_ref[1] = cu[3 * nb:4 * nb]


def _inproj_s_call(x, sh, sc, ln_mix, w_in_bf, cos, sin_signed, w_conv, st_in):
    n = x.shape[0]
    nb = st_in.shape[1]
    full = lambda shape: pl.BlockSpec(shape, lambda i: (0,) * len(shape))
    return pl.pallas_call(
        _inproj_s_kernel,
        grid=(1,),
        in_specs=[full((n, D_MODEL)), full((n, D_MODEL)), full((n, D_MODEL)), full((1, D_MODEL)),
                  full((D_MODEL, 3072)), full((n, 512)), full((n, 512)), full((CONV_K, CONV_WIDTH)),
                  full((2, nb, CONV_WIDTH))],
        out_specs=[full((n, 512))] * 4 + [full((2, nb, CONV_WIDTH))],
        out_shape=[jax.ShapeDtypeStruct((n, 512), bf16), jax.ShapeDtypeStruct((n, 512), f32),
                   jax.ShapeDtypeStruct((n, 512), f32), jax.ShapeDtypeStruct((n, 512), bf16),
                   jax.ShapeDtypeStruct((2, nb, CONV_WIDTH), f32)],
        compiler_params=_cparams(("arbitrary",)),
        name="inproj_s",
    )(x, sh, sc, ln_mix, w_in_bf, cos, sin_signed, w_conv, st_in)


def _lam(lq1, lk1, lq2, lk2):
    a = jnp.sum(lq1 * lk1, axis=-1, keepdims=True)
    b = jnp.sum(lq2 * lk2, axis=-1, keepdims=True)
    return jnp.exp(a) - jnp.exp(b) + LAM_INIT


def _attn_p_kernel(q_ref, k_ref, v_ref, lq1_ref, lk1_ref, lq2_ref, lk2_ref, g_ref, o_ref,
                   m_ref, l_ref, acc_ref):
    i = pl.program_id(2)
    j = pl.program_id(3)
    tq = q_ref.shape[0]
    tk = k_ref.shape[0]

    @pl.when(j == 0)
    def _():
        m_ref[...] = jnp.full(m_ref.shape, -jnp.inf, f32)
        l_ref[...] = jnp.zeros(l_ref.shape, f32)
        acc_ref[...] = jnp.zeros(acc_ref.shape, f32)

    @pl.when(j <= i)
    def _():
        q = q_ref[...]
        lane = lax.broadcasted_iota(jnp.int32, q.shape, 1)
        zero = jnp.zeros_like(q)
        q2 = jnp.concatenate([jnp.where(lane < HEAD_DIM, q, zero), jnp.where(lane >= HEAD_DIM, q, zero)], axis=0)
        s = lax.dot_general(q2, k_ref[...], (((1,), (1,)), ((), ())), preferred_element_type=f32)
        r = lax.broadcasted_iota(jnp.int32, s.shape, 0)
        c = lax.broadcasted_iota(jnp.int32, s.shape, 1)
        qpos = i * tq + jnp.where(r >= tq, r - tq, r)
        s = jnp.where(j * tk + c <= qpos, s, -jnp.inf)
        m_prev = m_ref[...]
        m_new = jnp.maximum(m_prev, jnp.max(s, axis=-1, keepdims=True))
        alpha = jnp.exp(m_prev - m_new)
        p = jnp.exp(s - m_new)
        l_ref[...] = alpha * l_ref[...] + jnp.sum(p, axis=-1, keepdims=True)
        acc_ref[...] = alpha * acc_ref[...] + jnp.dot(p.astype(bf16), v_ref[...], preferred_element_type=f32)
        m_ref[...] = m_new

    @pl.when(j == i)
    def _():
        lam = _lam(lq1_ref[...], lk1_ref[...], lq2_ref[...], lk2_ref[...])
        o = acc_ref[...] / l_ref[...]
        d = o[:tq] - lam * o[tq:]
        o_ref[...] = (_rms(d, g_ref[...], SUBLN_EPS) * (1.0 - LAM_INIT)).astype(bf16)


def _attn_p_call(q, k, v, lq1, lk1, lq2, lk2, g):
    B, S, _ = q.shape
    nq = S // TQ
    small = lambda n: pl.BlockSpec((1, n), lambda b, h, i, j: (0, 0))
    return pl.pallas_call(
        _attn_p_kernel,
        grid=(B, N_HEADS, nq, nq),
        in_specs=[pl.BlockSpec((None, TQ, V_DIM), lambda b, h, i, j: (b, i, h)),
                  pl.BlockSpec((None, TQ, V_DIM), lambda b, h, i, j: (b, jnp.minimum(i, j), h)),
                  pl.BlockSpec((None, TQ, V_DIM), lambda b, h, i, j: (b, jnp.minimum(i, j), h)),
                  small(HEAD_DIM), small(HEAD_DIM), small(HEAD_DIM), small(HEAD_DIM), small(V_DIM)],
        out_specs=pl.BlockSpec((None, TQ, V_DIM), lambda b, h, i, j: (b, i, h)),
        out_shape=jax.ShapeDtypeStruct((B, S, ATTN_WIDTH), bf16),
        scratch_shapes=[pltpu.VMEM((2 * TQ, 1), f32), pltpu.VMEM((2 * TQ, 1), f32),
                        pltpu.VMEM((2 * TQ, V_DIM), f32)],
        compiler_params=_cparams(("arbitrary",) * 4),
        name="attn_p",
    )(q, k, v, lq1, lk1, lq2, lk2, g)


def _attn_s_kernel(n_pages, pt_ref, q_ref, kn_ref, vn_ref, lq1_ref, lk1_ref, lq2_ref, lk2_ref, g_ref, *rest):
    kt_refs = rest[:n_pages]
    v_refs = rest[n_pages:2 * n_pages]
    o_ref = rest[2 * n_pages]
    nq = q_ref.shape[0]
    nr = nq * N_SUB
    qf = q_ref[...].astype(f32)
    sub = lax.broadcasted_iota(jnp.int32, (N_SUB, ATTN_WIDTH), 0)
    col = lax.broadcasted_iota(jnp.int32, (N_SUB, ATTN_WIDTH), 1)
    diag = (col // HEAD_DIM) == sub
    qbd = jnp.concatenate([jnp.where(diag, jnp.broadcast_to(qf[a:a + 1], (N_SUB, ATTN_WIDTH)), 0.0)
                           for a in range(nq)], axis=0).astype(bf16)
    s_past = jnp.concatenate([jnp.dot(qbd, kt_refs[p][...].astype(bf16), preferred_element_type=f32)
                              for p in range(n_pages)], axis=1)
    s_new = lax.dot_general(qbd, kn_ref[...].astype(bf16), (((1,), (1,)), ((), ())),
                            preferred_element_type=f32)
    r = lax.broadcasted_iota(jnp.int32, s_new.shape, 0)
    c = lax.broadcasted_iota(jnp.int32, s_new.shape, 1)
    s_new = jnp.where((c <= r // N_SUB) & (c < nq), s_new, -jnp.inf)
    m = jnp.maximum(jnp.max(s_past, axis=-1, keepdims=True), jnp.max(s_new, axis=-1, keepdims=True))
    p_past = jnp.exp(s_past - m)
    p_new = jnp.exp(s_new - m)
    l = jnp.sum(p_past, axis=-1, keepdims=True) + jnp.sum(p_new, axis=-1, keepdims=True)
    p_bf = p_past.astype(bf16)
    vn = vn_ref[...]
    rowhead = (lax.broadcasted_iota(jnp.int32, (nr, V_DIM), 0) % N_SUB) // 2
    o = jnp.zeros((nr, V_DIM), f32)
    for h in range(N_HEADS):
        acc = jnp.zeros((nr, V_DIM), f32)
        for p in range(n_pages):
            vh = v_refs[p][pl.ds(h, PAGE, stride=N_HEADS), :].astype(bf16)
            acc = acc + jnp.dot(p_bf[:, p * PAGE:(p + 1) * PAGE], vh, preferred_element_type=f32)
        for a in range(nq):
            acc = acc + p_new[:, a:a + 1] * vn[a:a + 1, h * V_DIM:(h + 1) * V_DIM]
        o = jnp.where(rowhead == h, acc, o)
    o = o / l
    lam = _lam(lq1_ref[...], lk1_ref[...], lq2_ref[...], lk2_ref[...])
    d = o - lam * pltpu.roll(o, nr - 1, 0)
    d = _rms(d, g_ref[...], SUBLN_EPS) * (1.0 - LAM_INIT)
    for a in range(nq):
        rowv = jnp.concatenate([d[a * N_SUB + 2 * h:a * N_SUB + 2 * h + 1] for h in range(N_HEADS)], axis=1)
        o_ref[a:a + 1, :] = rowv.astype(o_ref.dtype)


def _attn_s_call(pt_flat, q, kn, vn, lq1, lk1, lq2, lk2, g, kt, v2, n_pages):
    nb, nq, _ = q.shape
    small = lambda n: pl.BlockSpec((1, n), lambda b, pt: (0, 0))

    def page_spec(p):
        return pl.BlockSpec((None, 512, PAGE), lambda b, pt: (pt[b * n_pages + p], 0, 0))

    return pl.pallas_call(
        functools.partial(_attn_s_kernel, n_pages),
        grid_spec=pltpu.PrefetchScalarGridSpec(
            num_scalar_prefetch=1,
            grid=(nb,),
            in_specs=[pl.BlockSpec((None, nq, ATTN_WIDTH), lambda b, pt: (b, 0, 0)),
                      pl.BlockSpec((None, 8, ATTN_WIDTH), lambda b, pt: (b, 0, 0)),
                      pl.BlockSpec((None, 8, ATTN_WIDTH), lambda b, pt: (b, 0, 0)),
                      small(HEAD_DIM), small(HEAD_DIM), small(HEAD_DIM), small(HEAD_DIM), small(V_DIM)]
                     + [page_spec(p) for p in range(n_pages)] * 2,
            out_specs=pl.BlockSpec((None, nq, ATTN_WIDTH), lambda b, pt: (b, 0, 0)),
        ),
        out_shape=jax.ShapeDtypeStruct((nb, nq, ATTN_WIDTH), bf16),
        compiler_params=_cparams(("arbitrary",)),
        name="attn_s",
    )(pt_flat, q, kn, vn, lq1, lk1, lq2, lk2, g, *([kt] * n_pages), *([v2] * n_pages))


def _post_kernel(x_ref, att_ref, cv_ref, ga_ref, shf_ref, scf_ref, ln_ref, wo_ref, wr_ref, br_ref, cnt_in_ref,
                 x1_ref, h2_ref, ri_ref, rw_ref, cnt_ref, carry_ref):
    i = pl.program_id(0)

    @pl.when(i == 0)
    def _():
        carry_ref[...] = cnt_in_ref[...]

    mix = (jnp.dot(att_ref[...], wo_ref[0:ATTN_WIDTH, :], preferred_element_type=f32)
           + jnp.dot(cv_ref[...], wo_ref[ATTN_WIDTH:, :], preferred_element_type=f32))
    x1 = x_ref[...] + ga_ref[...] * mix
    x1_ref[...] = x1
    h2 = _rms(x1, ln_ref[...], NORM_EPS) * (1.0 + scf_ref[...]) + shf_ref[...]
    h2_ref[...] = h2
    logits = jnp.dot(h2.astype(bf16), wr_ref[...], preferred_element_type=f32) + br_ref[...]
    tm = logits.shape[0]
    lane = lax.broadcasted_iota(jnp.int32, logits.shape, 1)
    lane_f = lane.astype(f32)
    big = jnp.float32(1e9)
    neg = -jnp.inf

    def first_max(vals):
        mx = jnp.max(vals, axis=-1, keepdims=True)
        idx = jnp.min(jnp.where(vals == mx, lane_f, big), axis=-1, keepdims=True)
        return mx, idx

    gl = jnp.where((lane >= ROUTE_GRP_LANE) & (lane < ROUTE_GRP_LANE + N_GROUPS), logits, neg)
    gmax, gidx = first_max(gl)
    g_p = 1.0 / jnp.sum(jnp.exp(gl - gmax), axis=-1, keepdims=True)
    lo = (gidx - ROUTE_GRP_LANE) * EXP_PER_GROUP
    el = jnp.where((lane_f >= lo) & (lane_f < lo + EXP_PER_GROUP), logits, neg)
    v1, i1 = first_max(el)
    el2 = jnp.where(lane_f == i1, neg, el)
    v2, i2 = first_max(el2)
    t = jnp.exp(v2 - v1)
    w1 = g_p / (1.0 + t)
    w2 = g_p * t / (1.0 + t)
    oh1 = lane_f == i1
    oh2 = lane_f == i2
    cnt = jnp.where(oh1 | oh2, 1.0, 0.0)
    rr = lax.broadcasted_iota(jnp.int32, (tm, tm), 0)
    cc = lax.broadcasted_iota(jnp.int32, (tm, tm), 1)
    ltri = jnp.where(rr > cc, 1.0, 0.0).astype(bf16)
    prefix = jnp.dot(ltri, cnt.astype(bf16), preferred_element_type=f32) + carry_ref[...]
    r1 = jnp.sum(jnp.where(oh1, prefix, 0.0), axis=-1, keepdims=True)
    r2 = jnp.sum(jnp.where(oh2, prefix, 0.0), axis=-1, keepdims=True)
    ri = jnp.where(lane == 0, i1, jnp.where(lane == 1, i2, jnp.where(lane == 2, r1, jnp.where(lane == 3, r2, 0.0))))
    ri_ref[...] = ri.astype(jnp.int32)
    rw_ref[...] = jnp.where(lane == 0, w1, jnp.where(lane == 1, w2, 0.0))
    new_carry = carry_ref[...] + jnp.sum(cnt, axis=0, keepdims=True)
    carry_ref[...] = new_carry
    cnt_ref[...] = new_carry


def _post_call(x, att, cv, mod, mod_spec, ln_ffn, w_out_bf, w_r_bf, b_r, cnt_in, name):
    n = x.shape[0]
    row = lambda w: pl.BlockSpec((TM, w), lambda i: (i, 0))
    const = lambda shape: pl.BlockSpec(shape, lambda i: (0, 0))
    return pl.pallas_call(
        _post_kernel,
        grid=(n // TM,),
        in_specs=[row(D_MODEL), row(ATTN_WIDTH), row(CONV_WIDTH), mod_spec(2), mod_spec(3), mod_spec(4),
                  const((1, D_MODEL)), const((D_MODEL, D_MODEL)), const((D_MODEL, LANES)), const((1, LANES)),
                  const((1, LANES))],
        out_specs=[row(D_MODEL), row(D_MODEL), row(LANES), row(LANES), const((1, LANES))],
        out_shape=[jax.ShapeDtypeStruct((n, D_MODEL), f32), jax.ShapeDtypeStruct((n, D_MODEL), f32),
                   jax.ShapeDtypeStruct((n, LANES), jnp.int32), jax.ShapeDtypeStruct((n, LANES), f32),
                   jax.ShapeDtypeStruct((1, LANES), f32)],
        scratch_shapes=[pltpu.VMEM((1, LANES), f32)],
        compiler_params=_cparams(("arbitrary",)),
        name=name,
    )(x, att, cv, mod, mod, mod, ln_ffn, w_out_bf, w_r_bf, b_r, cnt_in)


def _row_copy(src_ref, src_row, dst_ref, dst_row, sem):
    return pltpu.make_async_copy(src_ref.at[pl.ds(src_row, 1), :], dst_ref.at[pl.ds(dst_row, 1), :], sem)


def _dispatch_rows(pos_ref, src_ref, t0, xs_ref, sem):
    tm = src_ref.shape[0]

    def issue(r, c):
        t = t0 + r
        _row_copy(src_ref, r, xs_ref, pos_ref[2 * t], sem).start()
        _row_copy(src_ref, r, xs_ref, pos_ref[2 * t + 1], sem).start()
        return c

    lax.fori_loop(0, tm, issue, 0)

    def drain(r, c):
        _row_copy(src_ref, 0, xs_ref, 0, sem).wait()
        _row_copy(src_ref, 0, xs_ref, 0, sem).wait()
        return c

    lax.fori_loop(0, tm, drain, 0)


def _dispatch_kernel(n_tiles_p, pos_ref, hp_ref, hs_ref, xs_ref, sem):
    i = pl.program_id(0)

    @pl.when(i < n_tiles_p)
    def _():
        _dispatch_rows(pos_ref, hp_ref, i * TM, xs_ref, sem)

    @pl.when(i >= n_tiles_p)
    def _():
        _dispatch_rows(pos_ref, hs_ref, i * TM, xs_ref, sem)


def _dispatch_call(pos_flat, h2_p, h2_s):
    tp = h2_p.shape[0] // TM
    ts = h2_s.shape[0] // TM
    return pl.pallas_call(
        functools.partial(_dispatch_kernel, tp),
        grid_spec=pltpu.PrefetchScalarGridSpec(
            num_scalar_prefetch=1,
            grid=(tp + ts,),
            in_specs=[pl.BlockSpec((TM, D_MODEL), lambda i, pos: (jnp.minimum(i, tp - 1), 0)),
                      pl.BlockSpec((TM, D_MODEL), lambda i, pos: (jnp.maximum(i - tp, 0), 0))],
            out_specs=pl.BlockSpec(memory_space=pl.ANY),
            scratch_shapes=[pltpu.SemaphoreType.DMA(())],
        ),
        out_shape=jax.ShapeDtypeStruct((2 * (tp + ts) * TM, D_MODEL), f32),
        compiler_params=_cparams(("arbitrary",)),
        name="dispatch",
    )(pos_flat, h2_p, h2_s)


def _experts_kernel(tile_ref, exp_ref, lo_ref, hi_ref, xs_ref, wg_ref, wu_ref, wd_ref, ys_ref):
    w = pl.program_id(0)
    lo = lo_ref[w]
    hi = hi_ref[w]

    @pl.when(hi > lo)
    def _():
        x = xs_ref[...].astype(bf16)
        g = jnp.dot(x, wg_ref[...].astype(bf16), preferred_element_type=f32)
        u = jnp.dot(x, wu_ref[...].astype(bf16), preferred_element_type=f32)
        hid = (_silu(g) * u).astype(bf16)
        y = jnp.dot(hid, wd_ref[...].astype(bf16), preferred_element_type=f32)
        base = tile_ref[w] * TE
        row = base + lax.broadcasted_iota(jnp.int32, y.shape, 0)
        mine = (row >= lo) & (row < hi)

        @pl.when(lo == base)
        def _():
            ys_ref[...] = jnp.where(mine, y, 0.0)

        @pl.when(lo != base)
        def _():
            ys_ref[...] = jnp.where(mine, y, ys_ref[...])


def _experts_call(tile_id, exp_id, seg_lo, seg_hi, xs, w_gate, w_up, w_down):
    n_items = tile_id.shape[0]
    return pl.pallas_call(
        _experts_kernel,
        grid_spec=pltpu.PrefetchScalarGridSpec(
            num_scalar_prefetch=4,
            grid=(n_items,),
            in_specs=[pl.BlockSpec((TE, D_MODEL), lambda w, t, e, lo, hi: (t[w], 0)),
                      pl.BlockSpec((None, D_MODEL, D_EXPERT), lambda w, t, e, lo, hi: (e[w], 0, 0)),
                      pl.BlockSpec((None, D_MODEL, D_EXPERT), lambda w, t, e, lo, hi: (e[w], 0, 0)),
                      pl.BlockSpec((None, D_EXPERT, D_MODEL), lambda w, t, e, lo, hi: (e[w], 0, 0))],
            out_specs=pl.BlockSpec((TE, D_MODEL), lambda w, t, e, lo, hi: (t[w], 0)),
        ),
        out_shape=jax.ShapeDtypeStruct(xs.shape, f32),
        compiler_params=_cparams(("arbitrary",)),
        name="experts",
    )(tile_id, exp_id, seg_lo, seg_hi, xs, w_gate, w_up, w_down)


def _combine_kernel(pos_ref, x1_ref, rw_ref, gf_ref, ln_ref, ys_ref, o_ref, ybuf, sem):
    i = pl.program_id(0)
    tm = x1_ref.shape[0]

    def issue(r, c):
        t = i * tm + r
        _row_copy(ys_ref, pos_ref[2 * t], ybuf.at[0], r, sem).start()
        _row_copy(ys_ref, pos_ref[2 * t + 1], ybuf.at[1], r, sem).start()
        return c

    lax.fori_loop(0, tm, issue, 0)

    def drain(r, c):
        _row_copy(ys_ref, 0, ybuf.at[0], 0, sem).wait()
        _row_copy(ys_ref, 0, ybuf.at[0], 0, sem).wait()
        return c

    lax.fori_loop(0, tm, drain, 0)
    rw = rw_ref[...]
    moe = rw[:, 0:1] * ybuf[0] + rw[:, 1:2] * ybuf[1]
    x2 = x1_ref[...] + gf_ref[...] * moe
    o_ref[...] = _rms(x2, ln_ref[...], NORM_EPS)


def _combine_call(pos_flat, x1, rw, mod, mod_spec, ln_final, ys, name):
    n = x1.shape[0]
    return pl.pallas_call(
        _combine_kernel,
        grid_spec=pltpu.PrefetchScalarGridSpec(
            num_scalar_prefetch=1,
            grid=(n // TM,),
            in_specs=[pl.BlockSpec((TM, D_MODEL), lambda i, pos: (i, 0)),
                      pl.BlockSpec((TM, LANES), lambda i, pos: (i, 0)),
                      mod_spec(5),
                      pl.BlockSpec((1, D_MODEL), lambda i, pos: (0, 0)),
                      pl.BlockSpec(memory_space=pl.ANY)],
            out_specs=pl.BlockSpec((TM, D_MODEL), lambda i, pos: (i, 0)),
            scratch_shapes=[pltpu.VMEM((2, TM, D_MODEL), f32), pltpu.SemaphoreType.DMA(())],
        ),
        out_shape=jax.ShapeDtypeStruct((n, D_MODEL), f32),
        compiler_params=_cparams(("arbitrary",)),
        name=name,
    )(pos_flat, x1, rw, mod, ln_final, ys)


def _rope_tables(pos):
    inv = 1.0 / (ROPE_THETA ** (jnp.arange(0, HEAD_DIM, 2, dtype=f32) / HEAD_DIM))
    ang = pos.astype(f32)[:, None] * inv[None, :]
    ang = jnp.concatenate([ang, ang], axis=-1)
    cos = jnp.tile(jnp.cos(ang), (1, N_SUB))
    sign = jnp.where(jnp.arange(HEAD_DIM) < HEAD_DIM // 2, -1.0, 1.0).astype(f32)
    sin_signed = jnp.tile(jnp.sin(ang) * sign[None, :], (1, N_SUB))
    return cos, sin_signed


def _segments(counts, n_rows):
    n_tiles = n_rows // TE
    offs = jnp.concatenate([jnp.zeros((1,), jnp.int32), jnp.cumsum(counts)[:-1].astype(jnp.int32)])
    bounds = jnp.sort(jnp.concatenate([jnp.arange(n_tiles, dtype=jnp.int32) * TE, offs]))
    seg_lo = bounds
    seg_hi = jnp.concatenate([bounds[1:], jnp.full((1,), n_rows, jnp.int32)])
    tile_id = jnp.minimum(seg_lo // TE, n_tiles - 1)
    exp_id = jnp.clip(jnp.searchsorted(offs, seg_lo, side="right").astype(jnp.int32) - 1, 0, N_EXPERTS - 1)
    return offs, tile_id, exp_id, seg_lo, seg_hi


def kernel(x_prompt, x_sample, cache_k, cache_v, state_conv, page_table, c_prompt, c_sample, w_ada, b_ada, ln_mix, w_in, lam_q1, lam_k1, lam_q2, lam_k2, subln_g, w_conv, w_out, ln_ffn, w_router_grp, b_router_grp, w_router_exp, b_router_exp, w_gate, w_up, w_down, ln_final):
    B, S, _ = x_prompt.shape
    DB, L, _ = x_sample.shape
    n_phys = cache_k.shape[1]
    n_pages = page_table.shape[1]
    past = n_pages * PAGE
    n_p = B * S
    n_s = DB * L
    n_tok = n_p + n_s

    w_in_bf = w_in[0].astype(bf16)
    w_out_bf = w_out[0].astype(bf16)
    w_r = jnp.zeros((D_MODEL, LANES), f32)
    w_r = w_r.at[:, :N_EXPERTS].set(w_router_exp[0]).at[:, ROUTE_GRP_LANE:ROUTE_GRP_LANE + N_GROUPS].set(w_router_grp[0])
    b_r = jnp.zeros((1, LANES), f32)
    b_r = b_r.at[0, :N_EXPERTS].set(b_router_exp[0]).at[0, ROUTE_GRP_LANE:ROUTE_GRP_LANE + N_GROUPS].set(b_router_grp[0])
    w_r_bf = w_r.astype(bf16)
    cos_p, sin_p = _rope_tables(jnp.arange(S, dtype=jnp.int32))
    pos_s = past + jnp.repeat(jnp.arange(L, dtype=jnp.int32), DB)
    cos_s, sin_s = _rope_tables(pos_s)

    mod = _mod_call(jnp.concatenate([c_prompt, c_sample], axis=0), w_ada[0], b_ada)
    mod_p = mod[:B].reshape(B, 1, 6 * D_MODEL)
    mod_s = jnp.tile(mod[B:], (L, 1))

    q_p, kf_p, kb_p, vf_p, vb_p, cv_p, st_p = _inproj_p_call(x_prompt, mod_p, ln_mix, w_in_bf, cos_p, sin_p, w_conv[0])
    att_p = _attn_p_call(q_p, kb_p, vb_p, lam_q1, lam_k1, lam_q2, lam_k2, subln_g)

    xs_l = x_sample.transpose(1, 0, 2).reshape(n_s, D_MODEL)
    st_in = state_conv[0].transpose(1, 0, 2)
    q_s, kf_s, vf_s, cv_s, st_s = _inproj_s_call(xs_l, mod_s[:, 0:D_MODEL], mod_s[:, D_MODEL:2 * D_MODEL], ln_mix,
                                                 w_in_bf, cos_s, sin_s, w_conv[0], st_in)
    to_b = lambda a: a.reshape(L, DB, -1).transpose(1, 0, 2)
    pad8 = lambda a: jnp.pad(a, ((0, 0), (0, 8 - L), (0, 0)))
    kt = jnp.transpose(cache_k[0], (0, 2, 3, 1)).reshape(n_phys, N_SUB * HEAD_DIM, PAGE)
    v2 = cache_v[0].reshape(n_phys, PAGE * N_HEADS, V_DIM)
    att_s_b = _attn_s_call(page_table.reshape(-1), to_b(q_s), pad8(to_b(kf_s)), pad8(to_b(vf_s)),
                           lam_q1, lam_k1, lam_q2, lam_k2, subln_g, kt, v2, n_pages)
    att_s = att_s_b.transpose(1, 0, 2).reshape(n_s, ATTN_WIDTH)

    zero_cnt = jnp.zeros((1, LANES), f32)
    spec_p = lambda sec: pl.BlockSpec((None, 1, D_MODEL), lambda i, *_: (i // (S // TM), 0, sec))
    spec_s = lambda sec: pl.BlockSpec((TM, D_MODEL), lambda i, *_: (i, sec))
    x1_p, h2_p, ri_p, rw_p, cnt_p = _post_call(x_prompt.reshape(n_p, D_MODEL), att_p.reshape(n_p, ATTN_WIDTH),
                                               cv_p.reshape(n_p, CONV_WIDTH), mod_p, spec_p, ln_ffn, w_out_bf, w_r_bf,
                                               b_r, zero_cnt, "post_p")
    x1_s, h2_s, ri_s, rw_s, cnt = _post_call(xs_l, att_s, cv_s, mod_s, spec_s, ln_ffn, w_out_bf, w_r_bf, b_r, cnt_p,
                                             "post_s")

    counts = cnt[0, :N_EXPERTS].astype(jnp.int32)
    offs, tile_id, exp_id, seg_lo, seg_hi = _segments(counts, 2 * n_tok)
    ri = jnp.concatenate([ri_p[:, :4], ri_s[:, :4]], axis=0)
    pos = (offs[ri[:, 0:2]] + ri[:, 2:4]).reshape(-1)

    xs_sorted = _dispatch_call(pos, h2_p, h2_s)
    ys = _experts_call(tile_id, exp_id, seg_lo, seg_hi, xs_sorted, w_gate[0], w_up[0], w_down[0])

    y_p = _combine_call(pos[:2 * n_p], x1_p, rw_p, mod_p, spec_p, ln_final.reshape(1, D_MODEL), ys, "combine_p")
    y_s = _combine_call(pos[2 * n_p:], x1_s, rw_s, mod_s, spec_s, ln_final.reshape(1, D_MODEL), ys, "combine_s")

    from_l = lambda a: a.reshape(L, DB, -1).transpose(1, 0, 2)
    y_prompt = y_p.reshape(B, S, D_MODEL)
    y_sample = from_l(y_s)
    k_prompt = kf_p.reshape(1, B, S, N_SUB, HEAD_DIM)
    v_prompt = vf_p.reshape(1, B, S, N_HEADS, V_DIM)
    conv_prompt = st_p[None]
    k_sample = from_l(kf_s).reshape(1, DB, L, N_SUB, HEAD_DIM)
    v_sample = from_l(vf_s).reshape(1, DB, L, N_HEADS, V_DIM)
    conv_sample = st_s.transpose(1, 0, 2)[None]
    return (y_prompt, y_sample, k_prompt, v_prompt, conv_prompt, k_sample, v_sample, conv_sample)
```

```python
import functools
import math

import jax
import jax.numpy as jnp
from jax import lax
from jax.experimental import pallas as pl
from jax.experimental.pallas import tpu as pltpu

D_MODEL = 1024
ATTN_WIDTH = 512
CONV_WIDTH = 512
N_HEADS = 4
N_SUB = 8
HEAD_DIM = 64
V_DIM = 2 * HEAD_DIM
CONV_K = 3
ROPE_THETA = 10000.0
N_GROUPS = 4
EXP_PER_GROUP = 8
N_EXPERTS = 32
D_EXPERT = 256
NORM_EPS = 1e-6
SUBLN_EPS = 1e-5
LAM_INIT = 0.8 - 0.6 * math.exp(-0.3 * 0)
PAGE = 128
LANES = 128
ROUTE_GRP_LANE = 32

TM = 512
TQ = 512
TE = 256
ROW_DMA_UNROLL = 8
VMEM_LIMIT = 56 * 1024 * 1024

f32 = jnp.float32
bf16 = jnp.bfloat16


def _cparams(sem):
    return pltpu.CompilerParams(dimension_semantics=sem, vmem_limit_bytes=VMEM_LIMIT)


def _rms(x, g, eps):
    return x * lax.rsqrt(jnp.mean(x * x, axis=-1, keepdims=True) + eps) * g


def _silu(x):
    return x * (1.0 / (1.0 + jnp.exp(-x)))


def _mod_kernel(c_ref, w_ref, b_ref, o_ref):
    a = _silu(c_ref[...]).astype(bf16)
    o_ref[...] = jnp.dot(a, w_ref[...].astype(bf16), preferred_element_type=f32) + b_ref[...]


def _mod_call(c_all, w_ada, b_ada):
    n = c_all.shape[0]
    return pl.pallas_call(
        _mod_kernel,
        grid=(6,),
        in_specs=[pl.BlockSpec((n, D_MODEL), lambda j: (0, 0)),
                  pl.BlockSpec((D_MODEL, D_MODEL), lambda j: (0, j)),
                  pl.BlockSpec((1, D_MODEL), lambda j: (0, j))],
        out_specs=pl.BlockSpec((n, D_MODEL), lambda j: (0, j)),
        out_shape=jax.ShapeDtypeStruct((n, 6 * D_MODEL), f32),
        compiler_params=_cparams(("arbitrary",)),
        name="mod",
    )(c_all, w_ada, b_ada)


def _rope(t, cos, sin_signed, lo_mask):
    n = t.shape[-1]
    rot = jnp.where(lo_mask, pltpu.roll(t, n - HEAD_DIM // 2, 1), pltpu.roll(t, HEAD_DIM // 2, 1))
    return t * cos + rot * sin_signed


def _inproj_common(x, sh, sc, ln, w_ref, cos, sin_signed):
    h = (_rms(x, ln, NORM_EPS) * (1.0 + sc) + sh).astype(bf16)

    def sec(i):
        return jnp.dot(h, w_ref[:, i * 512:(i + 1) * 512], preferred_element_type=f32)

    lane = lax.broadcasted_iota(jnp.int32, (x.shape[0], 512), 1)
    lo_mask = (lane % HEAD_DIM) < (HEAD_DIM // 2)
    q = _rope(sec(0), cos, sin_signed, lo_mask) * (HEAD_DIM ** -0.5)
    k = _rope(sec(1), cos, sin_signed, lo_mask)
    v = sec(2)
    bg = sec(3)
    cu = sec(4) * sec(5)
    return q, k, v, bg, cu


def _inproj_p_kernel(x_ref, sh_ref, sc_ref, ln_ref, w_ref, cos_ref, sin_ref, wc_ref,
                     qt_ref, kt_ref, kb_ref, v4_ref, vt_ref, cv_ref, st_ref, carry_ref):
    s = pl.program_id(0)
    b = pl.program_id(1)
    q, k, v, bg, cu = _inproj_common(x_ref[...], sh_ref[...], sc_ref[...], ln_ref[...], w_ref,
                                     cos_ref[...], sin_ref[...])
    qt_ref[...] = q.T.astype(bf16)
    kt_ref[...] = k.T
    kb_ref[...] = k.astype(bf16)
    vt_ref[...] = v.T.astype(bf16)
    for h in range(N_HEADS):
        v4_ref[pl.ds(h, v.shape[0], stride=N_HEADS), :] = v[:, h * V_DIM:(h + 1) * V_DIM]
    tm = cu.shape[0]
    prev = jnp.where(s > 0, carry_ref[b], 0.0)
    row = lax.broadcasted_iota(jnp.int32, cu.shape, 0)
    cu1 = jnp.where(row == 0, prev[1:2], pltpu.roll(cu, 1, 0))
    cu2 = jnp.where(row == 0, prev[0:1], jnp.where(row == 1, prev[1:2], pltpu.roll(cu, 2, 0)))
    wc = wc_ref[...]
    conv = wc[0:1] * cu2 + wc[1:2] * cu1 + wc[2:3] * cu
    cv_ref[...] = (bg * conv).astype(bf16)
    last2 = cu[tm - 2:tm]
    carry_ref[b, 0:2, :] = last2
    st_ref[b] = last2


def _inproj_p_call(x, mod3, ln_mix, w_in_bf, cos, sin_signed, w_conv):
    B, S, _ = x.shape
    ns = S // TM
    row = lambda s, b: (b, s, 0)
    col = lambda s, b: (b, 0, s)
    rows = lambda dt: jax.ShapeDtypeStruct((B, S, 512), dt)
    cols = lambda dt: jax.ShapeDtypeStruct((B, 512, S), dt)
    row_spec = pl.BlockSpec((None, TM, 512), row)
    col_spec = pl.BlockSpec((None, 512, TM), col)
    return pl.pallas_call(
        _inproj_p_kernel,
        grid=(ns, B),
        in_specs=[pl.BlockSpec((None, TM, D_MODEL), row),
                  pl.BlockSpec((None, 1, D_MODEL), lambda s, b: (b, 0, 0)),
                  pl.BlockSpec((None, 1, D_MODEL), lambda s, b: (b, 0, 1)),
                  pl.BlockSpec((1, D_MODEL), lambda s, b: (0, 0)),
                  pl.BlockSpec((D_MODEL, 3072), lambda s, b: (0, 0)),
                  pl.BlockSpec((TM, 512), lambda s, b: (s, 0)),
                  pl.BlockSpec((TM, 512), lambda s, b: (s, 0)),
                  pl.BlockSpec((CONV_K, CONV_WIDTH), lambda s, b: (0, 0))],
        out_specs=[col_spec, col_spec, row_spec, pl.BlockSpec((None, N_HEADS * TM, V_DIM), row), col_spec, row_spec,
                   pl.BlockSpec((B, 2, CONV_WIDTH), lambda s, b: (0, 0, 0))],
        out_shape=[cols(bf16), cols(f32), rows(bf16), jax.ShapeDtypeStruct((B, N_HEADS * S, V_DIM), f32),
                   cols(bf16), rows(bf16), jax.ShapeDtypeStruct((B, 2, CONV_WIDTH), f32)],
        scratch_shapes=[pltpu.VMEM((B, 8, CONV_WIDTH), f32)],
        compiler_params=_cparams(("arbitrary", "arbitrary")),
        name="inproj_p",
    )(x, mod3, mod3, ln_mix, w_in_bf, cos, sin_signed, w_conv)


def _inproj_s_kernel(x_ref, sh_ref, sc_ref, ln_ref, w_ref, cos_ref, sin_ref, wc_ref, st_in_ref,
                     q_ref, kf_ref, vf_ref, cv_ref, st_ref):
    q, k, v, bg, cu = _inproj_common(x_ref[...], sh_ref[...], sc_ref[...], ln_ref[...], w_ref,
                                     cos_ref[...], sin_ref[...])
    q_ref[...] = q.astype(bf16)
    kf_ref[...] = k
    vf_ref[...] = v
    nb = st_in_ref.shape[1]
    st0 = st_in_ref[0]
    st1 = st_in_ref[1]
    cu1 = jnp.concatenate([st1, cu[:3 * nb]], axis=0)
    cu2 = jnp.concatenate([st0, st1, cu[:2 * nb]], axis=0)
    wc = wc_ref[...]
    conv = wc[0:1] * cu2 + wc[1:2] * cu1 + wc[2:3] * cu
    cv_ref[...] = (bg * conv).astype(bf16)
    st_ref[0] = cu[2 * nb:3 * nb]
    st_ref[1] = cu[3 * nb:4 * nb]


def _inproj_s_call(x, sh, sc, ln_mix, w_in_bf, cos, sin_signed, w_conv, st_in):
    n = x.shape[0]
    nb = st_in.shape[1]
    full = lambda shape: pl.BlockSpec(shape, lambda i: (0,) * len(shape))
    return pl.pallas_call(
        _inproj_s_kernel,
        grid=(1,),
        in_specs=[full((n, D_MODEL)), full((n, D_MODEL)), full((n, D_MODEL)), full((1, D_MODEL)),
                  full((D_MODEL, 3072)), full((n, 512)), full((n, 512)), full((CONV_K, CONV_WIDTH)),
                  full((2, nb, CONV_WIDTH))],
        out_specs=[full((n, 512))] * 4 + [full((2, nb, CONV_WIDTH))],
        out_shape=[jax.ShapeDtypeStruct((n, 512), bf16), jax.ShapeDtypeStruct((n, 512), f32),
                   jax.ShapeDtypeStruct((n, 512), f32), jax.ShapeDtypeStruct((n, 512), bf16),
                   jax.ShapeDtypeStruct((2, nb, CONV_WIDTH), f32)],
        compiler_params=_cparams(("arbitrary",)),
        name="inproj_s",
    )(x, sh, sc, ln_mix, w_in_bf, cos, sin_signed, w_conv, st_in)


def _lam(lq1, lk1, lq2, lk2):
    a = jnp.sum(lq1 * lk1, axis=-1, keepdims=True)
    b = jnp.sum(lq2 * lk2, axis=-1, keepdims=True)
    return jnp.exp(a) - jnp.exp(b) + LAM_INIT


def _attn_p_kernel(ii_ref, jj_ref, qt_ref, k_ref, vt_ref, lq1_ref, lk1_ref, lq2_ref, lk2_ref, g_ref, o_ref,
                   m_ref, l_ref, acc_ref):
    p_idx = pl.program_id(2)
    i = ii_ref[p_idx]
    j = jj_ref[p_idx]
    tq = qt_ref.shape[1]
    tk = k_ref.shape[0]

    @pl.when(j == 0)
    def _():
        m_ref[...] = jnp.full(m_ref.shape, -jnp.inf, f32)
        l_ref[...] = jnp.zeros(l_ref.shape, f32)
        acc_ref[...] = jnp.zeros(acc_ref.shape, f32)

    def step(masked):
        qt = qt_ref[...]
        row = lax.broadcasted_iota(jnp.int32, qt.shape, 0)
        zero = jnp.zeros_like(qt)
        q2t = jnp.concatenate([jnp.where(row < HEAD_DIM, qt, zero), jnp.where(row >= HEAD_DIM, qt, zero)], axis=1)
        st = jnp.dot(k_ref[...], q2t, preferred_element_type=f32)
        if masked:
            kpos = j * tk + lax.broadcasted_iota(jnp.int32, st.shape, 0)
            c = lax.broadcasted_iota(jnp.int32, st.shape, 1)
            qpos = i * tq + jnp.where(c >= tq, c - tq, c)
            st = jnp.where(kpos <= qpos, st, -jnp.inf)
        m_prev = m_ref[...]
        m_new = jnp.maximum(m_prev, jnp.max(st, axis=0, keepdims=True))
        alpha = jnp.exp(m_prev - m_new)
        pt = jnp.exp(st - m_new)
        l_ref[...] = alpha * l_ref[...] + jnp.sum(pt, axis=0, keepdims=True)
        acc_ref[...] = alpha * acc_ref[...] + jnp.dot(vt_ref[...], pt.astype(bf16), preferred_element_type=f32)
        m_ref[...] = m_new

    @pl.when(j < i)
    def _():
        step(False)

    @pl.when(j == i)
    def _():
        step(True)
        lam = _lam(lq1_ref[...], lk1_ref[...], lq2_ref[...], lk2_ref[...])
        ot = acc_ref[...] / l_ref[...]
        dt = ot[:, :tq] - lam * ot[:, tq:]
        ms = jnp.mean(dt * dt, axis=0, keepdims=True)
        nt = dt * lax.rsqrt(ms + SUBLN_EPS) * g_ref[...] * (1.0 - LAM_INIT)
        o_ref[...] = nt.T.astype(bf16)


def _attn_p_call(qt, k, vt, lq1, lk1, lq2, lk2, g_col):
    B, S, _ = k.shape
    nq = S // TQ
    pairs = [(i, j) for i in range(nq) for j in range(i + 1)]
    ii = jnp.asarray([p[0] for p in pairs], jnp.int32)
    jj = jnp.asarray([p[1] for p in pairs], jnp.int32)
    small = lambda n: pl.BlockSpec((1, n), lambda b, h, p, ii, jj: (0, 0))
    return pl.pallas_call(
        _attn_p_kernel,
        grid_spec=pltpu.PrefetchScalarGridSpec(
            num_scalar_prefetch=2,
            grid=(B, N_HEADS, len(pairs)),
            in_specs=[pl.BlockSpec((None, V_DIM, TQ), lambda b, h, p, ii, jj: (b, h, ii[p])),
                      pl.BlockSpec((None, TQ, V_DIM), lambda b, h, p, ii, jj: (b, jj[p], h)),
                      pl.BlockSpec((None, V_DIM, TQ), lambda b, h, p, ii, jj: (b, h, jj[p])),
                      small(HEAD_DIM), small(HEAD_DIM), small(HEAD_DIM), small(HEAD_DIM),
                      pl.BlockSpec((V_DIM, 1), lambda b, h, p, ii, jj: (0, 0))],
            out_specs=pl.BlockSpec((None, TQ, V_DIM), lambda b, h, p, ii, jj: (b, ii[p], h)),
            scratch_shapes=[pltpu.VMEM((1, 2 * TQ), f32), pltpu.VMEM((1, 2 * TQ), f32),
                            pltpu.VMEM((V_DIM, 2 * TQ), f32)],
        ),
        out_shape=jax.ShapeDtypeStruct((B, S, ATTN_WIDTH), bf16),
        compiler_params=_cparams(("arbitrary",) * 3),
        name="attn_p",
    )(ii, jj, qt, k, vt, lq1, lk1, lq2, lk2, g_col)


def _attn_s_kernel(n_pages, pt_ref, q_ref, kn_ref, vn_ref, lq1_ref, lk1_ref, lq2_ref, lk2_ref, g_ref, *rest):
    kt_refs = rest[:n_pages]
    v_refs = rest[n_pages:2 * n_pages]
    o_ref = rest[2 * n_pages]
    nq = q_ref.shape[0]
    nr = nq * N_SUB
    qf = q_ref[...].astype(f32)
    sub = lax.broadcasted_iota(jnp.int32, (N_SUB, ATTN_WIDTH), 0)
    col = lax.broadcasted_iota(jnp.int32, (N_SUB, ATTN_WIDTH), 1)
    diag = (col // HEAD_DIM) == sub
    qbd = jnp.concatenate([jnp.where(diag, jnp.broadcast_to(qf[a:a + 1], (N_SUB, ATTN_WIDTH)), 0.0)
                           for a in range(nq)], axis=0).astype(bf16)
    s_past = jnp.concatenate([jnp.dot(qbd, kt_refs[p][...].astype(bf16), preferred_element_type=f32)
                              for p in range(n_pages)], axis=1)
    s_new = lax.dot_general(qbd, kn_ref[...].astype(bf16), (((1,), (1,)), ((), ())),
                            preferred_element_type=f32)
    r = lax.broadcasted_iota(jnp.int32, s_new.shape, 0)
    c = lax.broadcasted_iota(jnp.int32, s_new.shape, 1)
    s_new = jnp.where((c <= r // N_SUB) & (c < nq), s_new, -jnp.inf)
    m = jnp.maximum(jnp.max(s_past, axis=-1, keepdims=True), jnp.max(s_new, axis=-1, keepdims=True))
    p_past = jnp.exp(s_past - m)
    p_new = jnp.exp(s_new - m)
    l = jnp.sum(p_past, axis=-1, keepdims=True) + jnp.sum(p_new, axis=-1, keepdims=True)
    p_bf = p_past.astype(bf16)
    vn = vn_ref[...]
    rowhead = (lax.broadcasted_iota(jnp.int32, (nr, V_DIM), 0) % N_SUB) // 2
    o = jnp.zeros((nr, V_DIM), f32)
    for h in range(N_HEADS):
        acc = jnp.zeros((nr, V_DIM), f32)
        for p in range(n_pages):
            vh = v_refs[p][pl.ds(h, PAGE, stride=N_HEADS), :].astype(bf16)
            acc = acc + jnp.dot(p_bf[:, p * PAGE:(p + 1) * PAGE], vh, preferred_element_type=f32)
        for a in range(nq):
            acc = acc + p_new[:, a:a + 1] * vn[a:a + 1, h * V_DIM:(h + 1) * V_DIM]
        o = jnp.where(rowhead == h, acc, o)
    o = o / l
    lam = _lam(lq1_ref[...], lk1_ref[...], lq2_ref[...], lk2_ref[...])
    d = o - lam * pltpu.roll(o, nr - 1, 0)
    d = _rms(d, g_ref[...], SUBLN_EPS) * (1.0 - LAM_INIT)
    for a in range(nq):
        rowv = jnp.concatenate([d[a * N_SUB + 2 * h:a * N_SUB + 2 * h + 1] for h in range(N_HEADS)], axis=1)
        o_ref[a:a + 1, :] = rowv.astype(o_ref.dtype)


def _attn_s_call(pt_flat, q, kn, vn, lq1, lk1, lq2, lk2, g, kt, v2, n_pages):
    nb, nq, _ = q.shape
    small = lambda n: pl.BlockSpec((1, n), lambda b, pt: (0, 0))

    def page_spec(p):
        return pl.BlockSpec((None, 512, PAGE), lambda b, pt: (pt[b * n_pages + p], 0, 0))

    return pl.pallas_call(
        functools.partial(_attn_s_kernel, n_pages),
        grid_spec=pltpu.PrefetchScalarGridSpec(
            num_scalar_prefetch=1,
            grid=(nb,),
            in_specs=[pl.BlockSpec((None, nq, ATTN_WIDTH), lambda b, pt: (b, 0, 0)),
                      pl.BlockSpec((None, 8, ATTN_WIDTH), lambda b, pt: (b, 0, 0)),
                      pl.BlockSpec((None, 8, ATTN_WIDTH), lambda b, pt: (b, 0, 0)),
                      small(HEAD_DIM), small(HEAD_DIM), small(HEAD_DIM), small(HEAD_DIM), small(V_DIM)]
                     + [page_spec(p) for p in range(n_pages)] * 2,
            out_specs=pl.BlockSpec((None, nq, ATTN_WIDTH), lambda b, pt: (b, 0, 0)),
        ),
        out_shape=jax.ShapeDtypeStruct((nb, nq, ATTN_WIDTH), bf16),
        compiler_params=_cparams(("arbitrary",)),
        name="attn_s",
    )(pt_flat, q, kn, vn, lq1, lk1, lq2, lk2, g, *([kt] * n_pages), *([v2] * n_pages))


def _post_kernel(x_ref, att_ref, cv_ref, ga_ref, shf_ref, scf_ref, ln_ref, wo_ref, wr_ref, br_ref, cnt_in_ref,
                 x1_ref, h2_ref, ri_ref, rw_ref, cnt_ref, carry_ref):
    i = pl.program_id(0)

    @pl.when(i == 0)
    def _():
        carry_ref[...] = cnt_in_ref[...]

    mix = (jnp.dot(att_ref[...], wo_ref[0:ATTN_WIDTH, :], preferred_element_type=f32)
           + jnp.dot(cv_ref[...], wo_ref[ATTN_WIDTH:, :], preferred_element_type=f32))
    x1 = x_ref[...] + ga_ref[...] * mix
    x1_ref[...] = x1
    h2 = _rms(x1, ln_ref[...], NORM_EPS) * (1.0 + scf_ref[...]) + shf_ref[...]
    h2_ref[...] = h2
    logits = jnp.dot(h2.astype(bf16), wr_ref[...], preferred_element_type=f32) + br_ref[...]
    tm = logits.shape[0]
    lane = lax.broadcasted_iota(jnp.int32, logits.shape, 1)
    lane_f = lane.astype(f32)
    big = jnp.float32(1e9)
    neg = -jnp.inf

    def first_max(vals):
        mx = jnp.max(vals, axis=-1, keepdims=True)
        idx = jnp.min(jnp.where(vals == mx, lane_f, big), axis=-1, keepdims=True)
        return mx, idx

    gl = jnp.where((lane >= ROUTE_GRP_LANE) & (lane < ROUTE_GRP_LANE + N_GROUPS), logits, neg)
    gmax, gidx = first_max(gl)
    g_p = 1.0 / jnp.sum(jnp.exp(gl - gmax), axis=-1, keepdims=True)
    lo = (gidx - ROUTE_GRP_LANE) * EXP_PER_GROUP
    el = jnp.where((lane_f >= lo) & (lane_f < lo + EXP_PER_GROUP), logits, neg)
    v1, i1 = first_max(el)
    el2 = jnp.where(lane_f == i1, neg, el)
    v2, i2 = first_max(el2)
    t = jnp.exp(v2 - v1)
    w1 = g_p / (1.0 + t)
    w2 = g_p * t / (1.0 + t)
    oh1 = lane_f == i1
    oh2 = lane_f == i2
    cnt = jnp.where(oh1 | oh2, 1.0, 0.0)
    rr = lax.broadcasted_iota(jnp.int32, (tm, tm), 0)
    cc = lax.broadcasted_iota(jnp.int32, (tm, tm), 1)
    ltri = jnp.where(rr > cc, 1.0, 0.0).astype(bf16)
    prefix = jnp.dot(ltri, cnt.astype(bf16), preferred_element_type=f32) + carry_ref[...]
    r1 = jnp.sum(jnp.where(oh1, prefix, 0.0), axis=-1, keepdims=True)
    r2 = jnp.sum(jnp.where(oh2, prefix, 0.0), axis=-1, keepdims=True)
    ri = jnp.where(lane == 0, i1, jnp.where(lane == 1, i2, jnp.where(lane == 2, r1, jnp.where(lane == 3, r2, 0.0))))
    ri_ref[...] = ri.astype(jnp.int32)
    rw_ref[...] = jnp.where(lane == 0, w1, jnp.where(lane == 1, w2, 0.0))
    new_carry = carry_ref[...] + jnp.sum(cnt, axis=0, keepdims=True)
    carry_ref[...] = new_carry
    cnt_ref[...] = new_carry


def _post_call(x, att, cv, mod, mod_spec, ln_ffn, w_out_bf, w_r_bf, b_r, cnt_in, name):
    n = x.shape[0]
    row = lambda w: pl.BlockSpec((TM, w), lambda i: (i, 0))
    const = lambda shape: pl.BlockSpec(shape, lambda i: (0, 0))
    return pl.pallas_call(
        _post_kernel,
        grid=(n // TM,),
        in_specs=[row(D_MODEL), row(ATTN_WIDTH), row(CONV_WIDTH), mod_spec(2), mod_spec(3), mod_spec(4),
                  const((1, D_MODEL)), const((D_MODEL, D_MODEL)), const((D_MODEL, LANES)), const((1, LANES)),
                  const((1, LANES))],
        out_specs=[row(D_MODEL), row(D_MODEL), row(LANES), row(LANES), const((1, LANES))],
        out_shape=[jax.ShapeDtypeStruct((n, D_MODEL), f32), jax.ShapeDtypeStruct((n, D_MODEL), f32),
                   jax.ShapeDtypeStruct((n, LANES), jnp.int32), jax.ShapeDtypeStruct((n, LANES), f32),
                   jax.ShapeDtypeStruct((1, LANES), f32)],
        scratch_shapes=[pltpu.VMEM((1, LANES), f32)],
        compiler_params=_cparams(("arbitrary",)),
        name=name,
    )(x, att, cv, mod, mod, mod, ln_ffn, w_out_bf, w_r_bf, b_r, cnt_in)


def _row_copy(src_ref, src_row, dst_ref, dst_row, sem):
    return pltpu.make_async_copy(src_ref.at[pl.ds(src_row, 1), :], dst_ref.at[pl.ds(dst_row, 1), :], sem)


def _dispatch_rows(pos_ref, src_ref, t0, xs_ref, sem):
    tm = src_ref.shape[0]

    def issue(r, c):
        t = t0 + r
        _row_copy(src_ref, r, xs_ref, pos_ref[2 * t], sem).start(priority=0)
        _row_copy(src_ref, r, xs_ref, pos_ref[2 * t + 1], sem).start(priority=1)
        return c

    lax.fori_loop(0, tm, issue, 0, unroll=ROW_DMA_UNROLL)
    tile_copy = pltpu.make_async_copy(src_ref, xs_ref.at[pl.ds(0, tm), :], sem)
    tile_copy.wait()
    tile_copy.wait()


def _dispatch_kernel(n_tiles_p, pos_ref, hp_ref, hs_ref, xs_ref, sem):
    i = pl.program_id(0)

    @pl.when(i < n_tiles_p)
    def _():
        _dispatch_rows(pos_ref, hp_ref, i * TM, xs_ref, sem)

    @pl.when(i >= n_tiles_p)
    def _():
        _dispatch_rows(pos_ref, hs_ref, i * TM, xs_ref, sem)


def _dispatch_call(pos_flat, h2_p, h2_s):
    tp = h2_p.shape[0] // TM
    ts = h2_s.shape[0] // TM
    return pl.pallas_call(
        functools.partial(_dispatch_kernel, tp),
        grid_spec=pltpu.PrefetchScalarGridSpec(
            num_scalar_prefetch=1,
            grid=(tp + ts,),
            in_specs=[pl.BlockSpec((TM, D_MODEL), lambda i, pos: (jnp.minimum(i, tp - 1), 0)),
                      pl.BlockSpec((TM, D_MODEL), lambda i, pos: (jnp.maximum(i - tp, 0), 0))],
            out_specs=pl.BlockSpec(memory_space=pl.ANY),
            scratch_shapes=[pltpu.SemaphoreType.DMA(())],
        ),
        out_shape=jax.ShapeDtypeStruct((2 * (tp + ts) * TM, D_MODEL), f32),
        compiler_params=_cparams(("arbitrary",)),
        name="dispatch",
    )(pos_flat, h2_p, h2_s)


def _experts_kernel(tile_ref, exp_ref, lo_ref, hi_ref, xs_ref, wg_ref, wu_ref, wd_ref, ys_ref):
    w = pl.program_id(0)
    lo = lo_ref[w]
    hi = hi_ref[w]

    @pl.when(hi > lo)
    def _():
        x = xs_ref[...].astype(bf16)
        g = jnp.dot(x, wg_ref[...].astype(bf16), preferred_element_type=f32)
        u = jnp.dot(x, wu_ref[...].astype(bf16), preferred_element_type=f32)
        hid = (_silu(g) * u).astype(bf16)
        y = jnp.dot(hid, wd_ref[...].astype(bf16), preferred_element_type=f32)
        base = tile_ref[w] * TE
        row = base + lax.broadcasted_iota(jnp.int32, y.shape, 0)
        mine = (row >= lo) & (row < hi)

        @pl.when(lo == base)
        def _():
            ys_ref[...] = jnp.where(mine, y, 0.0)

        @pl.when(lo != base)
        def _():
            ys_ref[...] = jnp.where(mine, y, ys_ref[...])


def _experts_call(tile_id, exp_id, seg_lo, seg_hi, xs, w_gate, w_up, w_down):
    n_items = tile_id.shape[0]
    return pl.pallas_call(
        _experts_kernel,
        grid_spec=pltpu.PrefetchScalarGridSpec(
            num_scalar_prefetch=4,
            grid=(n_items,),
            in_specs=[pl.BlockSpec((TE, D_MODEL), lambda w, t, e, lo, hi: (t[w], 0)),
                      pl.BlockSpec((None, D_MODEL, D_EXPERT), lambda w, t, e, lo, hi: (e[w], 0, 0)),
                      pl.BlockSpec((None, D_MODEL, D_EXPERT), lambda w, t, e, lo, hi: (e[w], 0, 0)),
                      pl.BlockSpec((None, D_EXPERT, D_MODEL), lambda w, t, e, lo, hi: (e[w], 0, 0))],
            out_specs=pl.BlockSpec((TE, D_MODEL), lambda w, t, e, lo, hi: (t[w], 0)),
        ),
        out_shape=jax.ShapeDtypeStruct(xs.shape, f32),
        compiler_params=_cparams(("arbitrary",)),
        name="experts",
    )(tile_id, exp_id, seg_lo, seg_hi, xs, w_gate, w_up, w_down)


def _combine_kernel(pos_ref, x1_ref, rw_ref, gf_ref, ln_ref, ys_ref, o_ref, ybuf, sem):
    i = pl.program_id(0)
    tm = x1_ref.shape[0]

    def issue(r, c):
        t = i * tm + r
        _row_copy(ys_ref, pos_ref[2 * t], ybuf.at[0], r, sem).start(priority=0)
        _row_copy(ys_ref, pos_ref[2 * t + 1], ybuf.at[1], r, sem).start(priority=1)
        return c

    lax.fori_loop(0, tm, issue, 0, unroll=ROW_DMA_UNROLL)
    tile_copy = pltpu.make_async_copy(ys_ref.at[pl.ds(0, tm), :], ybuf.at[0], sem)
    tile_copy.wait()
    tile_copy.wait()
    rw = rw_ref[...]
    moe = rw[:, 0:1] * ybuf[0] + rw[:, 1:2] * ybuf[1]
    x2 = x1_ref[...] + gf_ref[...] * moe
    o_ref[...] = _rms(x2, ln_ref[...], NORM_EPS)


def _combine_call(pos_flat, x1, rw, mod, mod_spec, ln_final, ys, name):
    n = x1.shape[0]
    return pl.pallas_call(
        _combine_kernel,
        grid_spec=pltpu.PrefetchScalarGridSpec(
            num_scalar_prefetch=1,
            grid=(n // TM,),
            in_specs=[pl.BlockSpec((TM, D_MODEL), lambda i, pos: (i, 0)),
                      pl.BlockSpec((TM, LANES), lambda i, pos: (i, 0)),
                      mod_spec(5),
                      pl.BlockSpec((1, D_MODEL), lambda i, pos: (0, 0)),
                      pl.BlockSpec(memory_space=pl.ANY)],
            out_specs=pl.BlockSpec((TM, D_MODEL), lambda i, pos: (i, 0)),
            scratch_shapes=[pltpu.VMEM((2, TM, D_MODEL), f32), pltpu.SemaphoreType.DMA(())],
        ),
        out_shape=jax.ShapeDtypeStruct((n, D_MODEL), f32),
        compiler_params=_cparams(("arbitrary",)),
        name=name,
    )(pos_flat, x1, rw, mod, ln_final, ys)


def _rope_tables(pos):
    inv = 1.0 / (ROPE_THETA ** (jnp.arange(0, HEAD_DIM, 2, dtype=f32) / HEAD_DIM))
    ang = pos.astype(f32)[:, None] * inv[None, :]
    ang = jnp.concatenate([ang, ang], axis=-1)
    cos = jnp.tile(jnp.cos(ang), (1, N_SUB))
    sign = jnp.where(jnp.arange(HEAD_DIM) < HEAD_DIM // 2, -1.0, 1.0).astype(f32)
    sin_signed = jnp.tile(jnp.sin(ang) * sign[None, :], (1, N_SUB))
    return cos, sin_signed


def _segments(counts, n_rows):
    n_tiles = n_rows // TE
    offs = jnp.concatenate([jnp.zeros((1,), jnp.int32), jnp.cumsum(counts)[:-1].astype(jnp.int32)])
    tiles = jnp.arange(n_tiles, dtype=jnp.int32) * TE
    rank_t = jnp.arange(n_tiles, dtype=jnp.int32) + jnp.sum(offs[None, :] < tiles[:, None], axis=1).astype(jnp.int32)
    rank_o = jnp.arange(N_EXPERTS, dtype=jnp.int32) + jnp.minimum(offs // TE + 1, n_tiles)
    vals = jnp.concatenate([tiles, offs])
    ranks = jnp.concatenate([rank_t, rank_o])
    n_items = n_tiles + N_EXPERTS
    w = jnp.arange(n_items, dtype=jnp.int32)
    seg_lo = jnp.sum(jnp.where(ranks[None, :] == w[:, None], vals[None, :], 0), axis=1).astype(jnp.int32)
    seg_hi = jnp.concatenate([seg_lo[1:], jnp.full((1,), n_rows, jnp.int32)])
    tile_id = jnp.minimum(seg_lo // TE, n_tiles - 1)
    exp_id = jnp.sum(offs[None, :] <= seg_lo[:, None], axis=1).astype(jnp.int32) - 1
    return offs, tile_id, exp_id, seg_lo, seg_hi


def kernel(x_prompt, x_sample, cache_k, cache_v, state_conv, page_table, c_prompt, c_sample, w_ada, b_ada, ln_mix, w_in, lam_q1, lam_k1, lam_q2, lam_k2, subln_g, w_conv, w_out, ln_ffn, w_router_grp, b_router_grp, w_router_exp, b_router_exp, w_gate, w_up, w_down, ln_final):
    B, S, _ = x_prompt.shape
    DB, L, _ = x_sample.shape
    n_phys = cache_k.shape[1]
    n_pages = page_table.shape[1]
    past = n_pages * PAGE
    n_p = B * S
    n_s = DB * L
    n_tok = n_p + n_s

    w_in_bf = w_in[0].astype(bf16)
    w_out_bf = w_out[0].astype(bf16)
    w_r = jnp.zeros((D_MODEL, LANES), f32)
    w_r = w_r.at[:, :N_EXPERTS].set(w_router_exp[0]).at[:, ROUTE_GRP_LANE:ROUTE_GRP_LANE + N_GROUPS].set(w_router_grp[0])
    b_r = jnp.zeros((1, LANES), f32)
    b_r = b_r.at[0, :N_EXPERTS].set(b_router_exp[0]).at[0, ROUTE_GRP_LANE:ROUTE_GRP_LANE + N_GROUPS].set(b_router_grp[0])
    w_r_bf = w_r.astype(bf16)
    cos_p, sin_p = _rope_tables(jnp.arange(S, dtype=jnp.int32))
    pos_s = past + jnp.repeat(jnp.arange(L, dtype=jnp.int32), DB)
    cos_s, sin_s = _rope_tables(pos_s)

    mod = _mod_call(jnp.concatenate([c_prompt, c_sample], axis=0), w_ada[0], b_ada)
    mod_p = mod[:B].reshape(B, 1, 6 * D_MODEL)
    mod_s = jnp.tile(mod[B:], (L, 1))

    qt_p, kt_p, kb_p, v4_p, vt_p, cv_p, st_p = _inproj_p_call(x_prompt, mod_p, ln_mix, w_in_bf, cos_p, sin_p, w_conv[0])
    att_p = _attn_p_call(qt_p, kb_p, vt_p, lam_q1, lam_k1, lam_q2, lam_k2, subln_g.reshape(V_DIM, 1))

    xs_l = x_sample.transpose(1, 0, 2).reshape(n_s, D_MODEL)
    st_in = state_conv[0].transpose(1, 0, 2)
    q_s, kf_s, vf_s, cv_s, st_s = _inproj_s_call(xs_l, mod_s[:, 0:D_MODEL], mod_s[:, D_MODEL:2 * D_MODEL], ln_mix,
                                                 w_in_bf, cos_s, sin_s, w_conv[0], st_in)
    to_b = lambda a: a.reshape(L, DB, -1).transpose(1, 0, 2)
    pad8 = lambda a: jnp.pad(a, ((0, 0), (0, 8 - L), (0, 0)))
    kt = jnp.transpose(cache_k[0], (0, 2, 3, 1)).reshape(n_phys, N_SUB * HEAD_DIM, PAGE)
    v2 = cache_v[0].reshape(n_phys, PAGE * N_HEADS, V_DIM)
    att_s_b = _attn_s_call(page_table.reshape(-1), to_b(q_s), pad8(to_b(kf_s)), pad8(to_b(vf_s)),
                           lam_q1, lam_k1, lam_q2, lam_k2, subln_g, kt, v2, n_pages)
    att_s = att_s_b.transpose(1, 0, 2).reshape(n_s, ATTN_WIDTH)

    zero_cnt = jnp.zeros((1, LANES), f32)
    spec_p = lambda sec: pl.BlockSpec((None, 1, D_MODEL), lambda i, *_: (i // (S // TM), 0, sec))
    spec_s = lambda sec: pl.BlockSpec((TM, D_MODEL), lambda i, *_: (i, sec))
    x1_p, h2_p, ri_p, rw_p, cnt_p = _post_call(x_prompt.reshape(n_p, D_MODEL), att_p.reshape(n_p, ATTN_WIDTH),
                                               cv_p.reshape(n_p, CONV_WIDTH), mod_p, spec_p, ln_ffn, w_out_bf, w_r_bf,
                                               b_r, zero_cnt, "post_p")
    x1_s, h2_s, ri_s, rw_s, cnt = _post_call(xs_l, att_s, cv_s, mod_s, spec_s, ln_ffn, w_out_bf, w_r_bf, b_r, cnt_p,
                                             "post_s")

    counts = cnt[0, :N_EXPERTS].astype(jnp.int32)
    offs, tile_id, exp_id, seg_lo, seg_hi = _segments(counts, 2 * n_tok)
    ri = jnp.concatenate([ri_p[:, :4], ri_s[:, :4]], axis=0)
    pos = (offs[ri[:, 0:2]] + ri[:, 2:4]).reshape(-1)

    xs_sorted = _dispatch_call(pos, h2_p, h2_s)
    ys = _experts_call(tile_id, exp_id, seg_lo, seg_hi, xs_sorted, w_gate[0], w_up[0], w_down[0])

    y_p = _combine_call(pos[:2 * n_p], x1_p, rw_p, mod_p, spec_p, ln_final.reshape(1, D_MODEL), ys, "combine_p")
    y_s = _combine_call(pos[2 * n_p:], x1_s, rw_s, mod_s, spec_s, ln_final.reshape(1, D_MODEL), ys, "combine_s")

    from_l = lambda a: a.reshape(L, DB, -1).transpose(1, 0, 2)
    y_prompt = y_p.reshape(B, S, D_MODEL)
    y_sample = from_l(y_s)
    k_prompt = kt_p.reshape(B, N_SUB, HEAD_DIM, S).transpose(0, 3, 1, 2)[None]
    v_prompt = v4_p.reshape(1, B, S, N_HEADS, V_DIM)
    conv_prompt = st_p[None]
    k_sample = from_l(kf_s).reshape(1, DB, L, N_SUB, HEAD_DIM)
    v_sample = from_l(vf_s).reshape(1, DB, L, N_HEADS, V_DIM)
    conv_sample = st_s.transpose(1, 0, 2)[None]
    return (y_prompt, y_sample, k_prompt, v_prompt, conv_prompt, k_sample, v_sample, conv_sample)
```

```python
import functools
import math

import jax
import jax.numpy as jnp
from jax import lax
from jax.experimental import pallas as pl
from jax.experimental.pallas import tpu as pltpu

D_MODEL = 1024
ATTN_WIDTH = 512
CONV_WIDTH = 512
N_HEADS = 4
N_SUB = 8
HEAD_DIM = 64
V_DIM = 2 * HEAD_DIM
CONV_K = 3
ROPE_THETA = 10000.0
N_GROUPS = 4
EXP_PER_GROUP = 8
N_EXPERTS = 32
D_EXPERT = 256
NORM_EPS = 1e-6
SUBLN_EPS = 1e-5
LAM_INIT = 0.8 - 0.6 * math.exp(-0.3 * 0)
LOG2E = math.log2(math.e)
PAGE = 128
LANES = 128
ROUTE_GRP_LANE = 32

TM = 512
TQ = 512
TE = 256
ROW_DMA_UNROLL = 8
ONES_ROWS = 16
VMEM_LIMIT = 56 * 1024 * 1024

f32 = jnp.float32
bf16 = jnp.bfloat16


def _cparams(sem):
    return pltpu.CompilerParams(dimension_semantics=sem, vmem_limit_bytes=VMEM_LIMIT)


def _rms(x, g, eps):
    return x * lax.rsqrt(jnp.mean(x * x, axis=-1, keepdims=True) + eps) * g


def _silu(x):
    return x * (1.0 / (1.0 + jnp.exp(-x)))


def _mod_kernel(c_ref, w_ref, b_ref, o_ref):
    a = _silu(c_ref[...]).astype(bf16)
    o_ref[...] = jnp.dot(a, w_ref[...].astype(bf16), preferred_element_type=f32) + b_ref[...]


def _mod_call(c_all, w_ada, b_ada):
    n = c_all.shape[0]
    return pl.pallas_call(
        _mod_kernel,
        grid=(6,),
        in_specs=[pl.BlockSpec((n, D_MODEL), lambda j: (0, 0)),
                  pl.BlockSpec((D_MODEL, D_MODEL), lambda j: (0, j)),
                  pl.BlockSpec((1, D_MODEL), lambda j: (0, j))],
        out_specs=pl.BlockSpec((n, D_MODEL), lambda j: (0, j)),
        out_shape=jax.ShapeDtypeStruct((n, 6 * D_MODEL), f32),
        compiler_params=_cparams(("arbitrary",)),
        name="mod",
    )(c_all, w_ada, b_ada)


def _rope(t, cos, sin_signed, lo_mask):
    n = t.shape[-1]
    rot = jnp.where(lo_mask, pltpu.roll(t, n - HEAD_DIM // 2, 1), pltpu.roll(t, HEAD_DIM // 2, 1))
    return t * cos + rot * sin_signed


def _inproj_common(x, sh, sc, ln, w_ref, cos, sin_signed):
    h = (_rms(x, ln, NORM_EPS) * (1.0 + sc) + sh).astype(bf16)

    def sec(i):
        return jnp.dot(h, w_ref[:, i * 512:(i + 1) * 512], preferred_element_type=f32)

    lane = lax.broadcasted_iota(jnp.int32, (x.shape[0], 512), 1)
    lo_mask = (lane % HEAD_DIM) < (HEAD_DIM // 2)
    q = _rope(sec(0), cos, sin_signed, lo_mask) * (HEAD_DIM ** -0.5 * LOG2E)
    k = _rope(sec(1), cos, sin_signed, lo_mask)
    v = sec(2)
    bg = sec(3)
    cu = sec(4) * sec(5)
    return q, k, v, bg, cu


def _inproj_p_kernel(x_ref, sh_ref, sc_ref, ln_ref, w_ref, cos_ref, sin_ref, wc_ref,
                     qt_ref, kt_ref, kb_ref, v4_ref, vt_ref, cv_ref, st_ref, carry_ref):
    s = pl.program_id(0)
    b = pl.program_id(1)
    q, k, v, bg, cu = _inproj_common(x_ref[...], sh_ref[...], sc_ref[...], ln_ref[...], w_ref,
                                     cos_ref[...], sin_ref[...])
    qt_ref[...] = q.T.astype(bf16)
    kt_ref[...] = k.T
    kb_ref[...] = k.astype(bf16)
    vt_ref[...] = v.T.astype(bf16)
    for h in range(N_HEADS):
        v4_ref[pl.ds(h, v.shape[0], stride=N_HEADS), :] = v[:, h * V_DIM:(h + 1) * V_DIM]
    tm = cu.shape[0]
    prev = jnp.where(s > 0, carry_ref[b], 0.0)
    row = lax.broadcasted_iota(jnp.int32, cu.shape, 0)
    cu1 = jnp.where(row == 0, prev[1:2], pltpu.roll(cu, 1, 0))
    cu2 = jnp.where(row == 0, prev[0:1], jnp.where(row == 1, prev[1:2], pltpu.roll(cu, 2, 0)))
    wc = wc_ref[...]
    conv = wc[0:1] * cu2 + wc[1:2] * cu1 + wc[2:3] * cu
    cv_ref[...] = (bg * conv).astype(bf16)
    last2 = cu[tm - 2:tm]
    carry_ref[b, 0:2, :] = last2
    st_ref[b] = last2


def _inproj_p_call(x, mod3, ln_mix, w_in_bf, cos, sin_signed, w_conv):
    B, S, _ = x.shape
    ns = S // TM
    row = lambda s, b: (b, s, 0)
    col = lambda s, b: (b, 0, s)
    rows = lambda dt: jax.ShapeDtypeStruct((B, S, 512), dt)
    cols = lambda dt: jax.ShapeDtypeStruct((B, 512, S), dt)
    row_spec = pl.BlockSpec((None, TM, 512), row)
    col_spec = pl.BlockSpec((None, 512, TM), col)
    return pl.pallas_call(
        _inproj_p_kernel,
        grid=(ns, B),
        in_specs=[pl.BlockSpec((None, TM, D_MODEL), row),
                  pl.BlockSpec((None, 1, D_MODEL), lambda s, b: (b, 0, 0)),
                  pl.BlockSpec((None, 1, D_MODEL), lambda s, b: (b, 0, 1)),
                  pl.BlockSpec((1, D_MODEL), lambda s, b: (0, 0)),
                  pl.BlockSpec((D_MODEL, 3072), lambda s, b: (0, 0)),
                  pl.BlockSpec((TM, 512), lambda s, b: (s, 0)),
                  pl.BlockSpec((TM, 512), lambda s, b: (s, 0)),
                  pl.BlockSpec((CONV_K, CONV_WIDTH), lambda s, b: (0, 0))],
        out_specs=[col_spec, col_spec, row_spec, pl.BlockSpec((None, N_HEADS * TM, V_DIM), row), col_spec, row_spec,
                   pl.BlockSpec((B, 2, CONV_WIDTH), lambda s, b: (0, 0, 0))],
        out_shape=[cols(bf16), cols(f32), rows(bf16), jax.ShapeDtypeStruct((B, N_HEADS * S, V_DIM), f32),
                   cols(bf16), rows(bf16), jax.ShapeDtypeStruct((B, 2, CONV_WIDTH), f32)],
        scratch_shapes=[pltpu.VMEM((B, 8, CONV_WIDTH), f32)],
        compiler_params=_cparams(("arbitrary", "arbitrary")),
        name="inproj_p",
    )(x, mod3, mod3, ln_mix, w_in_bf, cos, sin_signed, w_conv)


def _inproj_s_kernel(x_ref, sh_ref, sc_ref, ln_ref, w_ref, cos_ref, sin_ref, wc_ref, st_in_ref,
                     q_ref, kf_ref, vf_ref, cv_ref, st_ref):
    q, k, v, bg, cu = _inproj_common(x_ref[...], sh_ref[...], sc_ref[...], ln_ref[...], w_ref,
                                     cos_ref[...], sin_ref[...])
    q_ref[...] = q.astype(bf16)
    kf_ref[...] = k
    vf_ref[...] = v
    nb = st_in_ref.shape[1]
    st0 = st_in_ref[0]
    st1 = st_in_ref[1]
    cu1 = jnp.concatenate([st1, cu[:3 * nb]], axis=0)
    cu2 = jnp.concatenate([st0, st1, cu[:2 * nb]], axis=0)
    wc = wc_ref[...]
    conv = wc[0:1] * cu2 + wc[1:2] * cu1 + wc[2:3] * cu
    cv_ref[...] = (bg * conv).astype(bf16)
    st_ref[0] = cu[2 * nb:3 * nb]
    st_ref[1] = cu[3 * nb:4 * nb]


def _inproj_s_call(x, sh, sc, ln_mix, w_in_bf, cos, sin_signed, w_conv, st_in):
    n = x.shape[0]
    nb = st_in.shape[1]
    full = lambda shape: pl.BlockSpec(shape, lambda i: (0,) * len(shape))
    return pl.pallas_call(
        _inproj_s_kernel,
        grid=(1,),
        in_specs=[full((n, D_MODEL)), full((n, D_MODEL)), full((n, D_MODEL)), full((1, D_MODEL)),
                  full((D_MODEL, 3072)), full((n, 512)), full((n, 512)), full((CONV_K, CONV_WIDTH)),
                  full((2, nb, CONV_WIDTH))],
        out_specs=[full((n, 512))] * 4 + [full((2, nb, CONV_WIDTH))],
        out_shape=[jax.ShapeDtypeStruct((n, 512), bf16), jax.ShapeDtypeStruct((n, 512), f32),
                   jax.ShapeDtypeStruct((n, 512), f32), jax.ShapeDtypeStruct((n, 512), bf16),
                   jax.ShapeDtypeStruct((2, nb, CONV_WIDTH), f32)],
        compiler_params=_cparams(("arbitrary",)),
        name="inproj_s",
    )(x, sh, sc, ln_mix, w_in_bf, cos, sin_signed, w_conv, st_in)


def _lam(lq1, lk1, lq2, lk2):
    a = jnp.sum(lq1 * lk1, axis=-1, keepdims=True)
    b = jnp.sum(lq2 * lk2, axis=-1, keepdims=True)
    return jnp.exp(a) - jnp.exp(b) + LAM_INIT


def _attn_p_kernel(ii_ref, jj_ref, qt_ref, k_ref, vt_ref, lq1_ref, lk1_ref, lq2_ref, lk2_ref, g_ref, o_ref,
                   m_ref, acc_ref):
    p_idx = pl.program_id(2)
    i = ii_ref[p_idx]
    j = jj_ref[p_idx]
    tq = qt_ref.shape[1]
    tk = k_ref.shape[0]

    @pl.when(j == 0)
    def _():
        m_ref[...] = jnp.full(m_ref.shape, -jnp.inf, f32)
        acc_ref[...] = jnp.zeros(acc_ref.shape, f32)

    def step(masked):
        qt = qt_ref[...]
        row = lax.broadcasted_iota(jnp.int32, qt.shape, 0)
        zero = jnp.zeros_like(qt)
        q2t = jnp.concatenate([jnp.where(row < HEAD_DIM, qt, zero), jnp.where(row >= HEAD_DIM, qt, zero)], axis=1)
        st = jnp.dot(k_ref[...], q2t, preferred_element_type=f32)
        if masked:
            kpos = j * tk + lax.broadcasted_iota(jnp.int32, st.shape, 0)
            c = lax.broadcasted_iota(jnp.int32, st.shape, 1)
            qpos = i * tq + jnp.where(c >= tq, c - tq, c)
            st = jnp.where(kpos <= qpos, st, -jnp.inf)
        m_prev = m_ref[...]
        m_new = jnp.maximum(m_prev, jnp.max(st, axis=0, keepdims=True))
        alpha = jnp.exp2(m_prev - m_new)
        pt = jnp.exp2(st - m_new).astype(bf16)
        vt1 = jnp.concatenate([vt_ref[...], jnp.ones((ONES_ROWS, tk), bf16)], axis=0)
        acc_ref[...] = alpha * acc_ref[...] + jnp.dot(vt1, pt, preferred_element_type=f32)
        m_ref[...] = m_new

    @pl.when(j < i)
    def _():
        step(False)

    @pl.when(j == i)
    def _():
        step(True)
        lam = _lam(lq1_ref[...], lk1_ref[...], lq2_ref[...], lk2_ref[...])
        ot = acc_ref[0:V_DIM, :] / acc_ref[V_DIM:V_DIM + 1, :]
        dt = ot[:, :tq] - lam * ot[:, tq:]
        ms = jnp.mean(dt * dt, axis=0, keepdims=True)
        nt = dt * lax.rsqrt(ms + SUBLN_EPS) * g_ref[...] * (1.0 - LAM_INIT)
        o_ref[...] = nt.T.astype(bf16)


def _attn_p_call(qt, k, vt, lq1, lk1, lq2, lk2, g_col):
    B, S, _ = k.shape
    nq = S // TQ
    pairs = [(i, j) for i in range(nq) for j in range(i + 1)]
    ii = jnp.asarray([p[0] for p in pairs], jnp.int32)
    jj = jnp.asarray([p[1] for p in pairs], jnp.int32)
    small = lambda n: pl.BlockSpec((1, n), lambda b, h, p, ii, jj: (0, 0))
    return pl.pallas_call(
        _attn_p_kernel,
        grid_spec=pltpu.PrefetchScalarGridSpec(
            num_scalar_prefetch=2,
            grid=(B, N_HEADS, len(pairs)),
            in_specs=[pl.BlockSpec((None, V_DIM, TQ), lambda b, h, p, ii, jj: (b, h, ii[p])),
                      pl.BlockSpec((None, TQ, V_DIM), lambda b, h, p, ii, jj: (b, jj[p], h)),
                      pl.BlockSpec((None, V_DIM, TQ), lambda b, h, p, ii, jj: (b, h, jj[p])),
                      small(HEAD_DIM), small(HEAD_DIM), small(HEAD_DIM), small(HEAD_DIM),
                      pl.BlockSpec((V_DIM, 1), lambda b, h, p, ii, jj: (0, 0))],
            out_specs=pl.BlockSpec((None, TQ, V_DIM), lambda b, h, p, ii, jj: (b, ii[p], h)),
            scratch_shapes=[pltpu.VMEM((1, 2 * TQ), f32), pltpu.VMEM((V_DIM + ONES_ROWS, 2 * TQ), f32)],
        ),
        out_shape=jax.ShapeDtypeStruct((B, S, ATTN_WIDTH), bf16),
        compiler_params=_cparams(("arbitrary",) * 3),
        name="attn_p",
    )(ii, jj, qt, k, vt, lq1, lk1, lq2, lk2, g_col)


def _attn_s_kernel(n_pages, pt_ref, q_ref, kn_ref, vn_ref, lq1_ref, lk1_ref, lq2_ref, lk2_ref, g_ref, *rest):
    kt_refs = rest[:n_pages]
    v_refs = rest[n_pages:2 * n_pages]
    o_ref = rest[2 * n_pages]
    nq = q_ref.shape[0]
    nr = nq * N_SUB
    qf = q_ref[...].astype(f32)
    sub = lax.broadcasted_iota(jnp.int32, (N_SUB, ATTN_WIDTH), 0)
    col = lax.broadcasted_iota(jnp.int32, (N_SUB, ATTN_WIDTH), 1)
    diag = (col // HEAD_DIM) == sub
    qbd = jnp.concatenate([jnp.where(diag, jnp.broadcast_to(qf[a:a + 1], (N_SUB, ATTN_WIDTH)), 0.0)
                           for a in range(nq)], axis=0).astype(bf16)
    s_past = jnp.concatenate([jnp.dot(qbd, kt_refs[p][...].astype(bf16), preferred_element_type=f32)
                              for p in range(n_pages)], axis=1)
    s_new = lax.dot_general(qbd, kn_ref[...].astype(bf16), (((1,), (1,)), ((), ())),
                            preferred_element_type=f32)
    r = lax.broadcasted_iota(jnp.int32, s_new.shape, 0)
    c = lax.broadcasted_iota(jnp.int32, s_new.shape, 1)
    s_new = jnp.where((c <= r // N_SUB) & (c < nq), s_new, -jnp.inf)
    m = jnp.maximum(jnp.max(s_past, axis=-1, keepdims=True), jnp.max(s_new, axis=-1, keepdims=True))
    p_past = jnp.exp2(s_past - m)
    p_new = jnp.exp2(s_new - m)
    l = jnp.sum(p_past, axis=-1, keepdims=True) + jnp.sum(p_new, axis=-1, keepdims=True)
    p_bf = p_past.astype(bf16)
    vn = vn_ref[...]
    rowhead = (lax.broadcasted_iota(jnp.int32, (nr, V_DIM), 0) % N_SUB) // 2
    o = jnp.zeros((nr, V_DIM), f32)
    for h in range(N_HEADS):
        acc = jnp.zeros((nr, V_DIM), f32)
        for p in range(n_pages):
            vh = v_refs[p][pl.ds(h, PAGE, stride=N_HEADS), :].astype(bf16)
            acc = acc + jnp.dot(p_bf[:, p * PAGE:(p + 1) * PAGE], vh, preferred_element_type=f32)
        for a in range(nq):
            acc = acc + p_new[:, a:a + 1] * vn[a:a + 1, h * V_DIM:(h + 1) * V_DIM]
        o = jnp.where(rowhead == h, acc, o)
    o = o / l
    lam = _lam(lq1_ref[...], lk1_ref[...], lq2_ref[...], lk2_ref[...])
    d = o - lam * pltpu.roll(o, nr - 1, 0)
    d = _rms(d, g_ref[...], SUBLN_EPS) * (1.0 - LAM_INIT)
    for a in range(nq):
        rowv = jnp.concatenate([d[a * N_SUB + 2 * h:a * N_SUB + 2 * h + 1] for h in range(N_HEADS)], axis=1)
        o_ref[a:a + 1, :] = rowv.astype(o_ref.dtype)


def _attn_s_call(pt_flat, q, kn, vn, lq1, lk1, lq2, lk2, g, kt, v2, n_pages):
    nb, nq, _ = q.shape
    small = lambda n: pl.BlockSpec((1, n), lambda b, pt: (0, 0))

    def page_spec(p):
        return pl.BlockSpec((None, 512, PAGE), lambda b, pt: (pt[b * n_pages + p], 0, 0))

    return pl.pallas_call(
        functools.partial(_attn_s_kernel, n_pages),
        grid_spec=pltpu.PrefetchScalarGridSpec(
            num_scalar_prefetch=1,
            grid=(nb,),
            in_specs=[pl.BlockSpec((None, nq, ATTN_WIDTH), lambda b, pt: (b, 0, 0)),
                      pl.BlockSpec((None, 8, ATTN_WIDTH), lambda b, pt: (b, 0, 0)),
                      pl.BlockSpec((None, 8, ATTN_WIDTH), lambda b, pt: (b, 0, 0)),
                      small(HEAD_DIM), small(HEAD_DIM), small(HEAD_DIM), small(HEAD_DIM), small(V_DIM)]
                     + [page_spec(p) for p in range(n_pages)] * 2,
            out_specs=pl.BlockSpec((None, nq, ATTN_WIDTH), lambda b, pt: (b, 0, 0)),
        ),
        out_shape=jax.ShapeDtypeStruct((nb, nq, ATTN_WIDTH), bf16),
        compiler_params=_cparams(("arbitrary",)),
        name="attn_s",
    )(pt_flat, q, kn, vn, lq1, lk1, lq2, lk2, g, *([kt] * n_pages), *([v2] * n_pages))


SLAB = D_MODEL // LANES


def _store_slabs(ref, row0, x):
    n = x.shape[0]
    for a in range(SLAB):
        ref[pl.ds(SLAB * row0 + a, n, stride=SLAB), :] = x[:, a * LANES:(a + 1) * LANES]


def _load_slabs(ref, row0, n):
    return jnp.concatenate([ref[pl.ds(SLAB * row0 + a, n, stride=SLAB), :] for a in range(SLAB)], axis=1)


def _slab_copy(src_ref, src_row, dst_ref, dst_row, sem):
    src = src_ref.at[pl.ds(pl.multiple_of(src_row * SLAB, SLAB), SLAB), :]
    dst = dst_ref.at[pl.ds(pl.multiple_of(dst_row * SLAB, SLAB), SLAB), :]
    return pltpu.make_async_copy(src, dst, sem)


def _post_kernel(x_ref, att_ref, cv_ref, ga_ref, shf_ref, scf_ref, ln_ref, wo_ref, wr_ref, br_ref, cnt_in_ref,
                 x1_ref, h2_ref, ri_ref, rw_ref, cnt_ref, carry_ref):
    i = pl.program_id(0)

    @pl.when(i == 0)
    def _():
        carry_ref[...] = cnt_in_ref[...]

    mix = (jnp.dot(att_ref[...], wo_ref[0:ATTN_WIDTH, :], preferred_element_type=f32)
           + jnp.dot(cv_ref[...], wo_ref[ATTN_WIDTH:, :], preferred_element_type=f32))
    x1 = x_ref[...] + ga_ref[...] * mix
    x1_ref[...] = x1
    h2 = _rms(x1, ln_ref[...], NORM_EPS) * (1.0 + scf_ref[...]) + shf_ref[...]
    _store_slabs(h2_ref, 0, h2)
    logits = jnp.dot(h2.astype(bf16), wr_ref[...], preferred_element_type=f32) + br_ref[...]
    tm = logits.shape[0]
    lane = lax.broadcasted_iota(jnp.int32, logits.shape, 1)
    lane_f = lane.astype(f32)
    big = jnp.float32(1e9)
    neg = -jnp.inf

    def first_max(vals):
        mx = jnp.max(vals, axis=-1, keepdims=True)
        idx = jnp.min(jnp.where(vals == mx, lane_f, big), axis=-1, keepdims=True)
        return mx, idx

    gl = jnp.where((lane >= ROUTE_GRP_LANE) & (lane < ROUTE_GRP_LANE + N_GROUPS), logits, neg)
    gmax, gidx = first_max(gl)
    g_p = 1.0 / jnp.sum(jnp.exp(gl - gmax), axis=-1, keepdims=True)
    lo = (gidx - ROUTE_GRP_LANE) * EXP_PER_GROUP
    el = jnp.where((lane_f >= lo) & (lane_f < lo + EXP_PER_GROUP), logits, neg)
    v1, i1 = first_max(el)
    el2 = jnp.where(lane_f == i1, neg, el)
    v2, i2 = first_max(el2)
    t = jnp.exp(v2 - v1)
    w1 = g_p / (1.0 + t)
    w2 = g_p * t / (1.0 + t)
    oh1 = lane_f == i1
    oh2 = lane_f == i2
    cnt = jnp.where(oh1 | oh2, 1.0, 0.0)
    rr = lax.broadcasted_iota(jnp.int32, (tm, tm), 0)
    cc = lax.broadcasted_iota(jnp.int32, (tm, tm), 1)
    ltri = jnp.where(rr > cc, 1.0, 0.0).astype(bf16)
    prefix = jnp.dot(ltri, cnt.astype(bf16), preferred_element_type=f32) + carry_ref[...]
    r1 = jnp.sum(jnp.where(oh1, prefix, 0.0), axis=-1, keepdims=True)
    r2 = jnp.sum(jnp.where(oh2, prefix, 0.0), axis=-1, keepdims=True)
    ri = jnp.where(lane == 0, i1, jnp.where(lane == 1, i2, jnp.where(lane == 2, r1, jnp.where(lane == 3, r2, 0.0))))
    ri_ref[...] = ri.astype(jnp.int32)
    rw_ref[...] = jnp.where(lane == 0, w1, jnp.where(lane == 1, w2, 0.0))
    new_carry = carry_ref[...] + jnp.sum(cnt, axis=0, keepdims=True)
    carry_ref[...] = new_carry
    cnt_ref[...] = new_carry


def _post_call(x, att, cv, mod, mod_spec, ln_ffn, w_out_bf, w_r_bf, b_r, cnt_in, name):
    n = x.shape[0]
    row = lambda w: pl.BlockSpec((TM, w), lambda i: (i, 0))
    const = lambda shape: pl.BlockSpec(shape, lambda i: (0, 0))
    return pl.pallas_call(
        _post_kernel,
        grid=(n // TM,),
        in_specs=[row(D_MODEL), row(ATTN_WIDTH), row(CONV_WIDTH), mod_spec(2), mod_spec(3), mod_spec(4),
                  const((1, D_MODEL)), const((D_MODEL, D_MODEL)), const((D_MODEL, LANES)), const((1, LANES)),
                  const((1, LANES))],
        out_specs=[row(D_MODEL), pl.BlockSpec((TM * SLAB, LANES), lambda i: (i, 0)), row(LANES), row(LANES),
                   const((1, LANES))],
        out_shape=[jax.ShapeDtypeStruct((n, D_MODEL), f32), jax.ShapeDtypeStruct((n * SLAB, LANES), f32),
                   jax.ShapeDtypeStruct((n, LANES), jnp.int32), jax.ShapeDtypeStruct((n, LANES), f32),
                   jax.ShapeDtypeStruct((1, LANES), f32)],
        scratch_shapes=[pltpu.VMEM((1, LANES), f32)],
        compiler_params=_cparams(("arbitrary",)),
        name=name,
    )(x, att, cv, mod, mod, mod, ln_ffn, w_out_bf, w_r_bf, b_r, cnt_in)


def _dispatch_rows(pos_ref, src_ref, t0, xs_ref, sem):
    tm = src_ref.shape[0] // SLAB

    def issue(r, c):
        t = t0 + r
        _slab_copy(src_ref, r, xs_ref, pos_ref[2 * t], sem).start(priority=0)
        _slab_copy(src_ref, r, xs_ref, pos_ref[2 * t + 1], sem).start(priority=1)
        return c

    lax.fori_loop(0, tm, issue, 0, unroll=ROW_DMA_UNROLL)
    tile_copy = pltpu.make_async_copy(src_ref, xs_ref.at[pl.ds(0, tm * SLAB), :], sem)
    tile_copy.wait()
    tile_copy.wait()


def _dispatch_kernel(n_tiles_p, pos_ref, hp_ref, hs_ref, xs_ref, sem):
    i = pl.program_id(0)

    @pl.when(i < n_tiles_p)
    def _():
        _dispatch_rows(pos_ref, hp_ref, i * TM, xs_ref, sem)

    @pl.when(i >= n_tiles_p)
    def _():
        _dispatch_rows(pos_ref, hs_ref, i * TM, xs_ref, sem)


def _dispatch_call(pos_flat, h2_p, h2_s):
    tp = h2_p.shape[0] // (TM * SLAB)
    ts = h2_s.shape[0] // (TM * SLAB)
    return pl.pallas_call(
        functools.partial(_dispatch_kernel, tp),
        grid_spec=pltpu.PrefetchScalarGridSpec(
            num_scalar_prefetch=1,
            grid=(tp + ts,),
            in_specs=[pl.BlockSpec((TM * SLAB, LANES), lambda i, pos: (jnp.minimum(i, tp - 1), 0)),
                      pl.BlockSpec((TM * SLAB, LANES), lambda i, pos: (jnp.maximum(i - tp, 0), 0))],
            out_specs=pl.BlockSpec(memory_space=pl.ANY),
            scratch_shapes=[pltpu.SemaphoreType.DMA(())],
        ),
        out_shape=jax.ShapeDtypeStruct((2 * (tp + ts) * TM * SLAB, LANES), f32),
        compiler_params=_cparams(("arbitrary",)),
        name="dispatch",
    )(pos_flat, h2_p, h2_s)


def _experts_kernel(tile_ref, exp_ref, lo_ref, hi_ref, xs_ref, wg_ref, wu_ref, wd_ref, ys_ref,
                    wg_bf, wu_bf, wd_bf, cur_ref):
    w = pl.program_id(0)
    lo = lo_ref[w]
    hi = hi_ref[w]

    @pl.when(w == 0)
    def _():
        cur_ref[0] = -1

    @pl.when(hi > lo)
    def _():
        e = exp_ref[w]

        @pl.when(cur_ref[0] != e)
        def _():
            wg_bf[...] = wg_ref[...].astype(bf16)
            wu_bf[...] = wu_ref[...].astype(bf16)
            wd_bf[...] = wd_ref[...].astype(bf16)
            cur_ref[0] = e

        x = _load_slabs(xs_ref, 0, TE).astype(bf16)
        g = jnp.dot(x, wg_bf[...], preferred_element_type=f32)
        u = jnp.dot(x, wu_bf[...], preferred_element_type=f32)
        hid = (_silu(g) * u).astype(bf16)
        y = jnp.dot(hid, wd_bf[...], preferred_element_type=f32)
        base = tile_ref[w] * TE
        row = base + lax.broadcasted_iota(jnp.int32, y.shape, 0)
        mine = (row >= lo) & (row < hi)

        @pl.when(lo == base)
        def _():
            _store_slabs(ys_ref, 0, jnp.where(mine, y, 0.0))

        @pl.when(lo != base)
        def _():
            _store_slabs(ys_ref, 0, jnp.where(mine, y, _load_slabs(ys_ref, 0, TE)))


def _experts_call(tile_id, exp_id, seg_lo, seg_hi, xs, w_gate, w_up, w_down):
    n_items = tile_id.shape[0]
    return pl.pallas_call(
        _experts_kernel,
        grid_spec=pltpu.PrefetchScalarGridSpec(
            num_scalar_prefetch=4,
            grid=(n_items,),
            in_specs=[pl.BlockSpec((TE * SLAB, LANES), lambda w, t, e, lo, hi: (t[w], 0)),
                      pl.BlockSpec((None, D_MODEL, D_EXPERT), lambda w, t, e, lo, hi: (e[w], 0, 0)),
                      pl.BlockSpec((None, D_MODEL, D_EXPERT), lambda w, t, e, lo, hi: (e[w], 0, 0)),
                      pl.BlockSpec((None, D_EXPERT, D_MODEL), lambda w, t, e, lo, hi: (e[w], 0, 0))],
            out_specs=pl.BlockSpec((TE * SLAB, LANES), lambda w, t, e, lo, hi: (t[w], 0)),
            scratch_shapes=[pltpu.VMEM((D_MODEL, D_EXPERT), bf16), pltpu.VMEM((D_MODEL, D_EXPERT), bf16),
                            pltpu.VMEM((D_EXPERT, D_MODEL), bf16), pltpu.SMEM((1,), jnp.int32)],
        ),
        out_shape=jax.ShapeDtypeStruct(xs.shape, f32),
        compiler_params=_cparams(("arbitrary",)),
        name="experts",
    )(tile_id, exp_id, seg_lo, seg_hi, xs, w_gate, w_up, w_down)


def _combine_kernel(pos_ref, x1_ref, rw_ref, gf_ref, ln_ref, ys_ref, o_ref, ybuf, sem):
    i = pl.program_id(0)
    tm = x1_ref.shape[0]

    def issue(r, c):
        t = i * tm + r
        _slab_copy(ys_ref, pos_ref[2 * t], ybuf, r, sem).start(priority=0)
        _slab_copy(ys_ref, pos_ref[2 * t + 1], ybuf, tm + r, sem).start(priority=1)
        return c

    lax.fori_loop(0, tm, issue, 0, unroll=ROW_DMA_UNROLL)
    tile_copy = pltpu.make_async_copy(ys_ref.at[pl.ds(0, tm * SLAB), :], ybuf.at[pl.ds(0, tm * SLAB), :], sem)
    tile_copy.wait()
    tile_copy.wait()
    rw = rw_ref[...]
    moe = rw[:, 0:1] * _load_slabs(ybuf, 0, tm) + rw[:, 1:2] * _load_slabs(ybuf, tm, tm)
    x2 = x1_ref[...] + gf_ref[...] * moe
    o_ref[...] = _rms(x2, ln_ref[...], NORM_EPS)


def _combine_call(pos_flat, x1, rw, mod, mod_spec, ln_final, ys, name):
    n = x1.shape[0]
    return pl.pallas_call(
        _combine_kernel,
        grid_spec=pltpu.PrefetchScalarGridSpec(
            num_scalar_prefetch=1,
            grid=(n // TM,),
            in_specs=[pl.BlockSpec((TM, D_MODEL), lambda i, pos: (i, 0)),
                      pl.BlockSpec((TM, LANES), lambda i, pos: (i, 0)),
                      mod_spec(5),
                      pl.BlockSpec((1, D_MODEL), lambda i, pos: (0, 0)),
                      pl.BlockSpec(memory_space=pl.ANY)],
            out_specs=pl.BlockSpec((TM, D_MODEL), lambda i, pos: (i, 0)),
            scratch_shapes=[pltpu.VMEM((2 * TM * SLAB, LANES), f32), pltpu.SemaphoreType.DMA(())],
        ),
        out_shape=jax.ShapeDtypeStruct((n, D_MODEL), f32),
        compiler_params=_cparams(("arbitrary",)),
        name=name,
    )(pos_flat, x1, rw, mod, ln_final, ys)


def _rope_tables(pos):
    inv = 1.0 / (ROPE_THETA ** (jnp.arange(0, HEAD_DIM, 2, dtype=f32) / HEAD_DIM))
    ang = pos.astype(f32)[:, None] * inv[None, :]
    ang = jnp.concatenate([ang, ang], axis=-1)
    cos = jnp.tile(jnp.cos(ang), (1, N_SUB))
    sign = jnp.where(jnp.arange(HEAD_DIM) < HEAD_DIM // 2, -1.0, 1.0).astype(f32)
    sin_signed = jnp.tile(jnp.sin(ang) * sign[None, :], (1, N_SUB))
    return cos, sin_signed


def _segments(counts, n_rows):
    n_tiles = n_rows // TE
    offs = jnp.concatenate([jnp.zeros((1,), jnp.int32), jnp.cumsum(counts)[:-1].astype(jnp.int32)])
    tiles = jnp.arange(n_tiles, dtype=jnp.int32) * TE
    rank_t = jnp.arange(n_tiles, dtype=jnp.int32) + jnp.sum(offs[None, :] < tiles[:, None], axis=1).astype(jnp.int32)
    rank_o = jnp.arange(N_EXPERTS, dtype=jnp.int32) + jnp.minimum(offs // TE + 1, n_tiles)
    vals = jnp.concatenate([tiles, offs])
    ranks = jnp.concatenate([rank_t, rank_o])
    n_items = n_tiles + N_EXPERTS
    w = jnp.arange(n_items, dtype=jnp.int32)
    seg_lo = jnp.sum(jnp.where(ranks[None, :] == w[:, None], vals[None, :], 0), axis=1).astype(jnp.int32)
    seg_hi = jnp.concatenate([seg_lo[1:], jnp.full((1,), n_rows, jnp.int32)])
    tile_id = jnp.minimum(seg_lo // TE, n_tiles - 1)
    exp_id = jnp.sum(offs[None, :] <= seg_lo[:, None], axis=1).astype(jnp.int32) - 1
    return offs, tile_id, exp_id, seg_lo, seg_hi


def kernel(x_prompt, x_sample, cache_k, cache_v, state_conv, page_table, c_prompt, c_sample, w_ada, b_ada, ln_mix, w_in, lam_q1, lam_k1, lam_q2, lam_k2, subln_g, w_conv, w_out, ln_ffn, w_router_grp, b_router_grp, w_router_exp, b_router_exp, w_gate, w_up, w_down, ln_final):
    B, S, _ = x_prompt.shape
    DB, L, _ = x_sample.shape
    n_phys = cache_k.shape[1]
    n_pages = page_table.shape[1]
    past = n_pages * PAGE
    n_p = B * S
    n_s = DB * L
    n_tok = n_p + n_s

    w_in_bf = w_in[0].astype(bf16)
    w_out_bf = w_out[0].astype(bf16)
    w_r = jnp.zeros((D_MODEL, LANES), f32)
    w_r = w_r.at[:, :N_EXPERTS].set(w_router_exp[0]).at[:, ROUTE_GRP_LANE:ROUTE_GRP_LANE + N_GROUPS].set(w_router_grp[0])
    b_r = jnp.zeros((1, LANES), f32)
    b_r = b_r.at[0, :N_EXPERTS].set(b_router_exp[0]).at[0, ROUTE_GRP_LANE:ROUTE_GRP_LANE + N_GROUPS].set(b_router_grp[0])
    w_r_bf = w_r.astype(bf16)
    cos_p, sin_p = _rope_tables(jnp.arange(S, dtype=jnp.int32))
    pos_s = past + jnp.repeat(jnp.arange(L, dtype=jnp.int32), DB)
    cos_s, sin_s = _rope_tables(pos_s)

    mod = _mod_call(jnp.concatenate([c_prompt, c_sample], axis=0), w_ada[0], b_ada)
    mod_p = mod[:B].reshape(B, 1, 6 * D_MODEL)
    mod_s = jnp.tile(mod[B:], (L, 1))

    qt_p, kt_p, kb_p, v4_p, vt_p, cv_p, st_p = _inproj_p_call(x_prompt, mod_p, ln_mix, w_in_bf, cos_p, sin_p, w_conv[0])
    att_p = _attn_p_call(qt_p, kb_p, vt_p, lam_q1, lam_k1, lam_q2, lam_k2, subln_g.reshape(V_DIM, 1))

    xs_l = x_sample.transpose(1, 0, 2).reshape(n_s, D_MODEL)
    st_in = state_conv[0].transpose(1, 0, 2)
    q_s, kf_s, vf_s, cv_s, st_s = _inproj_s_call(xs_l, mod_s[:, 0:D_MODEL], mod_s[:, D_MODEL:2 * D_MODEL], ln_mix,
                                                 w_in_bf, cos_s, sin_s, w_conv[0], st_in)
    to_b = lambda a: a.reshape(L, DB, -1).transpose(1, 0, 2)
    pad8 = lambda a: jnp.pad(a, ((0, 0), (0, 8 - L), (0, 0)))
    kt = jnp.transpose(cache_k[0], (0, 2, 3, 1)).reshape(n_phys, N_SUB * HEAD_DIM, PAGE)
    v2 = cache_v[0].reshape(n_phys, PAGE * N_HEADS, V_DIM)
    att_s_b = _attn_s_call(page_table.reshape(-1), to_b(q_s), pad8(to_b(kf_s)), pad8(to_b(vf_s)),
                           lam_q1, lam_k1, lam_q2, lam_k2, subln_g, kt, v2, n_pages)
    att_s = att_s_b.transpose(1, 0, 2).reshape(n_s, ATTN_WIDTH)

    zero_cnt = jnp.zeros((1, LANES), f32)
    spec_p = lambda sec: pl.BlockSpec((None, 1, D_MODEL), lambda i, *_: (i // (S // TM), 0, sec))
    spec_s = lambda sec: pl.BlockSpec((TM, D_MODEL), lambda i, *_: (i, sec))
    x1_p, h2_p, ri_p, rw_p, cnt_p = _post_call(x_prompt.reshape(n_p, D_MODEL), att_p.reshape(n_p, ATTN_WIDTH),
                                               cv_p.reshape(n_p, CONV_WIDTH), mod_p, spec_p, ln_ffn, w_out_bf, w_r_bf,
                                               b_r, zero_cnt, "post_p")
    x1_s, h2_s, ri_s, rw_s, cnt = _post_call(xs_l, att_s, cv_s, mod_s, spec_s, ln_ffn, w_out_bf, w_r_bf, b_r, cnt_p,
                                             "post_s")

    counts = cnt[0, :N_EXPERTS].astype(jnp.int32)
    offs, tile_id, exp_id, seg_lo, seg_hi = _segments(counts, 2 * n_tok)
    ri = jnp.concatenate([ri_p[:, :4], ri_s[:, :4]], axis=0)
    pos = (offs[ri[:, 0:2]] + ri[:, 2:4]).reshape(-1)

    xs_sorted = _dispatch_call(pos, h2_p, h2_s)
    ys = _experts_call(tile_id, exp_id, seg_lo, seg_hi, xs_sorted, w_gate[0], w_up[0], w_down[0])

    y_p = _combine_call(pos[:2 * n_p], x1_p, rw_p, mod_p, spec_p, ln_final.reshape(1, D_MODEL), ys, "combine_p")
    y_s = _combine_call(pos[2 * n_p:], x1_s, rw_s, mod_s, spec_s, ln_final.reshape(1, D_MODEL), ys, "combine_s")

    from_l = lambda a: a.reshape(L, DB, -1).transpose(1, 0, 2)
    y_prompt = y_p.reshape(B, S, D_MODEL)
    y_sample = from_l(y_s)
    k_prompt = kt_p.reshape(B, N_SUB, HEAD_DIM, S).transpose(0, 3, 1, 2)[None]
    v_prompt = v4_p.reshape(1, B, S, N_HEADS, V_DIM)
    conv_prompt = st_p[None]
    k_sample = from_l(kf_s).reshape(1, DB, L, N_SUB, HEAD_DIM)
    v_sample = from_l(vf_s).reshape(1, DB, L, N_HEADS, V_DIM)
    conv_sample = st_s.transpose(1, 0, 2)[None]
    return (y_prompt, y_sample, k_prompt, v_prompt, conv_prompt, k_sample, v_sample, conv_sample)
```

```python
import functools
import math

import jax
import jax.numpy as jnp
import numpy as np
from jax import lax
from jax.experimental import pallas as pl
from jax.experimental.pallas import tpu as pltpu

D_MODEL = 1024
ATTN_WIDTH = 512
CONV_WIDTH = 512
N_HEADS = 4
N_SUB = 8
HEAD_DIM = 64
V_DIM = 2 * HEAD_DIM
CONV_K = 3
ROPE_THETA = 10000.0
N_GROUPS = 4
EXP_PER_GROUP = 8
N_EXPERTS = 32
D_EXPERT = 256
NORM_EPS = 1e-6
SUBLN_EPS = 1e-5
LAM_INIT = 0.8 - 0.6 * math.exp(-0.3 * 0)
LOG2E = math.log2(math.e)
PAGE = 128
LANES = 128
ROUTE_GRP_LANE = 32

TM = 512
TQ = 512
TE = 256
ROW_DMA_UNROLL = 8
ONES_ROWS = 16
VMEM_LIMIT = 56 * 1024 * 1024

f32 = jnp.float32
bf16 = jnp.bfloat16


def _cparams(sem):
    return pltpu.CompilerParams(dimension_semantics=sem, vmem_limit_bytes=VMEM_LIMIT)


def _rms(x, g, eps):
    return x * lax.rsqrt(jnp.mean(x * x, axis=-1, keepdims=True) + eps) * g


def _silu(x):
    return x * (1.0 / (1.0 + jnp.exp(-x)))


def _mod_kernel(c_ref, w_ref, b_ref, o_ref):
    a = _silu(c_ref[...]).astype(bf16)
    o_ref[...] = jnp.dot(a, w_ref[...].astype(bf16), preferred_element_type=f32) + b_ref[...]


def _mod_call(c_all, w_ada, b_ada):
    n = c_all.shape[0]
    return pl.pallas_call(
        _mod_kernel,
        grid=(6,),
        in_specs=[pl.BlockSpec((n, D_MODEL), lambda j: (0, 0)),
                  pl.BlockSpec((D_MODEL, D_MODEL), lambda j: (0, j)),
                  pl.BlockSpec((1, D_MODEL), lambda j: (0, j))],
        out_specs=pl.BlockSpec((n, D_MODEL), lambda j: (0, j)),
        out_shape=jax.ShapeDtypeStruct((n, 6 * D_MODEL), f32),
        compiler_params=_cparams(("arbitrary",)),
        name="mod",
    )(c_all, w_ada, b_ada)


def _rope(t, cos, sin_signed, lo_mask):
    n = t.shape[-1]
    rot = jnp.where(lo_mask, pltpu.roll(t, n - HEAD_DIM // 2, 1), pltpu.roll(t, HEAD_DIM // 2, 1))
    return t * cos + rot * sin_signed


def _inproj_common(x, sh, sc, ln, w_ref, cos, sin_signed):
    h = (_rms(x, ln, NORM_EPS) * (1.0 + sc) + sh).astype(bf16)

    def sec(i):
        return jnp.dot(h, w_ref[:, i * 512:(i + 1) * 512], preferred_element_type=f32)

    lane = lax.broadcasted_iota(jnp.int32, (x.shape[0], 512), 1)
    lo_mask = (lane % HEAD_DIM) < (HEAD_DIM // 2)
    q = _rope(sec(0), cos, sin_signed, lo_mask) * (HEAD_DIM ** -0.5 * LOG2E)
    k = _rope(sec(1), cos, sin_signed, lo_mask)
    v = sec(2)
    bg = sec(3)
    cu = sec(4) * sec(5)
    return q, k, v, bg, cu


def _inproj_p_kernel(x_ref, sh_ref, sc_ref, ln_ref, w_ref, cos_ref, sin_ref, wc_ref,
                     qt_ref, kt_ref, kb_ref, v4_ref, vt_ref, cv_ref, st_ref, carry_ref):
    s = pl.program_id(0)
    b = pl.program_id(1)
    q, k, v, bg, cu = _inproj_common(x_ref[...], sh_ref[...], sc_ref[...], ln_ref[...], w_ref,
                                     cos_ref[...], sin_ref[...])
    qt_ref[...] = q.T.astype(bf16)
    kt_ref[...] = k.T
    kb_ref[...] = k.astype(bf16)
    vt_ref[...] = v.T.astype(bf16)
    for h in range(N_HEADS):
        v4_ref[pl.ds(h, v.shape[0], stride=N_HEADS), :] = v[:, h * V_DIM:(h + 1) * V_DIM]
    tm = cu.shape[0]
    prev = jnp.where(s > 0, carry_ref[b], 0.0)
    row = lax.broadcasted_iota(jnp.int32, cu.shape, 0)
    cu1 = jnp.where(row == 0, prev[1:2], pltpu.roll(cu, 1, 0))
    cu2 = jnp.where(row == 0, prev[0:1], jnp.where(row == 1, prev[1:2], pltpu.roll(cu, 2, 0)))
    wc = wc_ref[...]
    conv = wc[0:1] * cu2 + wc[1:2] * cu1 + wc[2:3] * cu
    cv_ref[...] = (bg * conv).astype(bf16)
    last2 = cu[tm - 2:tm]
    carry_ref[b, 0:2, :] = last2
    st_ref[b] = last2


def _inproj_p_call(x, mod3, ln_mix, w_in_bf, cos, sin_signed, w_conv):
    B, S, _ = x.shape
    ns = S // TM
    row = lambda s, b: (b, s, 0)
    col = lambda s, b: (b, 0, s)
    rows = lambda dt: jax.ShapeDtypeStruct((B, S, 512), dt)
    cols = lambda dt: jax.ShapeDtypeStruct((B, 512, S), dt)
    row_spec = pl.BlockSpec((None, TM, 512), row)
    col_spec = pl.BlockSpec((None, 512, TM), col)
    return pl.pallas_call(
        _inproj_p_kernel,
        grid=(ns, B),
        in_specs=[pl.BlockSpec((None, TM, D_MODEL), row),
                  pl.BlockSpec((None, 1, D_MODEL), lambda s, b: (b, 0, 0)),
                  pl.BlockSpec((None, 1, D_MODEL), lambda s, b: (b, 0, 1)),
                  pl.BlockSpec((1, D_MODEL), lambda s, b: (0, 0)),
                  pl.BlockSpec((D_MODEL, 3072), lambda s, b: (0, 0)),
                  pl.BlockSpec((TM, 512), lambda s, b: (s, 0)),
                  pl.BlockSpec((TM, 512), lambda s, b: (s, 0)),
                  pl.BlockSpec((CONV_K, CONV_WIDTH), lambda s, b: (0, 0))],
        out_specs=[col_spec, col_spec, row_spec, pl.BlockSpec((None, N_HEADS * TM, V_DIM), row),
                   pl.BlockSpec((None, None, 512, TM), lambda s, b: (b, s, 0, 0)), row_spec,
                   pl.BlockSpec((B, 2, CONV_WIDTH), lambda s, b: (0, 0, 0))],
        out_shape=[cols(bf16), cols(f32), rows(bf16), jax.ShapeDtypeStruct((B, N_HEADS * S, V_DIM), f32),
                   jax.ShapeDtypeStruct((B, ns, 512, TM), bf16), rows(bf16),
                   jax.ShapeDtypeStruct((B, 2, CONV_WIDTH), f32)],
        scratch_shapes=[pltpu.VMEM((B, 8, CONV_WIDTH), f32)],
        compiler_params=_cparams(("arbitrary", "arbitrary")),
        name="inproj_p",
    )(x, mod3, mod3, ln_mix, w_in_bf, cos, sin_signed, w_conv)


def _inproj_s_kernel(x_ref, sh_ref, sc_ref, ln_ref, w_ref, cos_ref, sin_ref, wc_ref, st_in_ref,
                     q_ref, kf_ref, vf_ref, cv_ref, st_ref):
    q, k, v, bg, cu = _inproj_common(x_ref[...], sh_ref[...], sc_ref[...], ln_ref[...], w_ref,
                                     cos_ref[...], sin_ref[...])
    q_ref[...] = q.astype(bf16)
    kf_ref[...] = k
    vf_ref[...] = v
    nb = st_in_ref.shape[1]
    st0 = st_in_ref[0]
    st1 = st_in_ref[1]
    cu1 = jnp.concatenate([st1, cu[:3 * nb]], axis=0)
    cu2 = jnp.concatenate([st0, st1, cu[:2 * nb]], axis=0)
    wc = wc_ref[...]
    conv = wc[0:1] * cu2 + wc[1:2] * cu1 + wc[2:3] * cu
    cv_ref[...] = (bg * conv).astype(bf16)
    st_ref[0] = cu[2 * nb:3 * nb]
    st_ref[1] = cu[3 * nb:4 * nb]


def _inproj_s_call(x, sh, sc, ln_mix, w_in_bf, cos, sin_signed, w_conv, st_in):
    n = x.shape[0]
    nb = st_in.shape[1]
    full = lambda shape: pl.BlockSpec(shape, lambda i: (0,) * len(shape))
    return pl.pallas_call(
        _inproj_s_kernel,
        grid=(1,),
        in_specs=[full((n, D_MODEL)), full((n, D_MODEL)), full((n, D_MODEL)), full((1, D_MODEL)),
                  full((D_MODEL, 3072)), full((n, 512)), full((n, 512)), full((CONV_K, CONV_WIDTH)),
                  full((2, nb, CONV_WIDTH))],
        out_specs=[full((n, 512))] * 4 + [full((2, nb, CONV_WIDTH))],
        out_shape=[jax.ShapeDtypeStruct((n, 512), bf16), jax.ShapeDtypeStruct((n, 512), f32),
                   jax.ShapeDtypeStruct((n, 512), f32), jax.ShapeDtypeStruct((n, 512), bf16),
                   jax.ShapeDtypeStruct((2, nb, CONV_WIDTH), f32)],
        compiler_params=_cparams(("arbitrary",)),
        name="inproj_s",
    )(x, sh, sc, ln_mix, w_in_bf, cos, sin_signed, w_conv, st_in)


def _lam(lq1, lk1, lq2, lk2):
    a = jnp.sum(lq1 * lk1, axis=-1, keepdims=True)
    b = jnp.sum(lq2 * lk2, axis=-1, keepdims=True)
    return jnp.exp(a) - jnp.exp(b) + LAM_INIT


def _attn_p_kernel(qt_ref, k_ref, vt_ref, lq1_ref, lk1_ref, lq2_ref, lk2_ref, g_ref, o_ref, m_ref, acc_ref):
    i = pl.program_id(2)
    tq = qt_ref.shape[1]
    tk = tq
    m_ref[...] = jnp.full(m_ref.shape, -jnp.inf, f32)
    acc_ref[...] = jnp.zeros(acc_ref.shape, f32)

    def step(j, masked):
        k0 = pl.multiple_of(j * tk, tk)
        qt = qt_ref[...]
        row = lax.broadcasted_iota(jnp.int32, qt.shape, 0)
        zero = jnp.zeros_like(qt)
        q2t = jnp.concatenate([jnp.where(row < HEAD_DIM, qt, zero), jnp.where(row >= HEAD_DIM, qt, zero)], axis=1)
        st = jnp.dot(k_ref[pl.ds(k0, tk), :], q2t, preferred_element_type=f32)
        if masked:
            kpos = lax.broadcasted_iota(jnp.int32, st.shape, 0)
            c = lax.broadcasted_iota(jnp.int32, st.shape, 1)
            st = jnp.where(kpos <= jnp.where(c >= tq, c - tq, c), st, -jnp.inf)
        m_prev = m_ref[...]
        m_new = jnp.maximum(m_prev, jnp.max(st, axis=0, keepdims=True))
        alpha = jnp.exp2(m_prev - m_new)
        pt = jnp.exp2(st - m_new).astype(bf16)
        vt1 = jnp.concatenate([vt_ref[j], jnp.ones((ONES_ROWS, tk), bf16)], axis=0)
        acc_ref[...] = alpha * acc_ref[...] + jnp.dot(vt1, pt, preferred_element_type=f32)
        m_ref[...] = m_new

    def below_diagonal(j, c):
        step(j, False)
        return c

    lax.fori_loop(0, i, below_diagonal, 0)
    step(i, True)
    lam = _lam(lq1_ref[...], lk1_ref[...], lq2_ref[...], lk2_ref[...])
    ot = acc_ref[0:V_DIM, :] / acc_ref[V_DIM:V_DIM + 1, :]
    dt = ot[:, :tq] - lam * ot[:, tq:]
    ms = jnp.mean(dt * dt, axis=0, keepdims=True)
    nt = dt * lax.rsqrt(ms + SUBLN_EPS) * g_ref[...] * (1.0 - LAM_INIT)
    o_ref[...] = nt.T.astype(bf16)


def _attn_p_call(qt, k, vt, lq1, lk1, lq2, lk2, g_col):
    B, S, _ = k.shape
    small = lambda n: pl.BlockSpec((1, n), lambda b, h, i: (0, 0))
    return pl.pallas_call(
        _attn_p_kernel,
        grid=(B, N_HEADS, S // TQ),
        in_specs=[pl.BlockSpec((None, V_DIM, TQ), lambda b, h, i: (b, h, i)),
                  pl.BlockSpec((None, S, V_DIM), lambda b, h, i: (b, 0, h)),
                  pl.BlockSpec((None, S // TQ, V_DIM, TQ), lambda b, h, i: (b, 0, h, 0)),
                  small(HEAD_DIM), small(HEAD_DIM), small(HEAD_DIM), small(HEAD_DIM),
                  pl.BlockSpec((V_DIM, 1), lambda b, h, i: (0, 0))],
        out_specs=pl.BlockSpec((None, TQ, V_DIM), lambda b, h, i: (b, i, h)),
        out_shape=jax.ShapeDtypeStruct((B, S, ATTN_WIDTH), bf16),
        scratch_shapes=[pltpu.VMEM((1, 2 * TQ), f32), pltpu.VMEM((V_DIM + ONES_ROWS, 2 * TQ), f32)],
        compiler_params=_cparams(("arbitrary",) * 3),
        name="attn_p",
    )(qt, k, vt, lq1, lk1, lq2, lk2, g_col)


def _attn_s_kernel(n_pages, pt_ref, q_ref, kn_ref, vn_ref, lq1_ref, lk1_ref, lq2_ref, lk2_ref, g_ref, *rest):
    kt_refs = rest[:n_pages]
    v_refs = rest[n_pages:2 * n_pages]
    o_ref = rest[2 * n_pages]
    nq = q_ref.shape[0]
    nr = nq * N_SUB
    qf = q_ref[...].astype(f32)
    sub = lax.broadcasted_iota(jnp.int32, (N_SUB, ATTN_WIDTH), 0)
    col = lax.broadcasted_iota(jnp.int32, (N_SUB, ATTN_WIDTH), 1)
    diag = (col // HEAD_DIM) == sub
    qbd = jnp.concatenate([jnp.where(diag, jnp.broadcast_to(qf[a:a + 1], (N_SUB, ATTN_WIDTH)), 0.0)
                           for a in range(nq)], axis=0).astype(bf16)
    s_past = jnp.concatenate([jnp.dot(qbd, kt_refs[p][...].astype(bf16), preferred_element_type=f32)
                              for p in range(n_pages)], axis=1)
    s_new = lax.dot_general(qbd, kn_ref[...].astype(bf16), (((1,), (1,)), ((), ())),
                            preferred_element_type=f32)
    r = lax.broadcasted_iota(jnp.int32, s_new.shape, 0)
    c = lax.broadcasted_iota(jnp.int32, s_new.shape, 1)
    s_new = jnp.where((c <= r // N_SUB) & (c < nq), s_new, -jnp.inf)
    m = jnp.maximum(jnp.max(s_past, axis=-1, keepdims=True), jnp.max(s_new, axis=-1, keepdims=True))
    p_past = jnp.exp2(s_past - m)
    p_new = jnp.exp2(s_new - m)
    l = jnp.sum(p_past, axis=-1, keepdims=True) + jnp.sum(p_new, axis=-1, keepdims=True)
    p_bf = p_past.astype(bf16)
    vn = vn_ref[...]
    rowhead = (lax.broadcasted_iota(jnp.int32, (nr, V_DIM), 0) % N_SUB) // 2
    o = jnp.zeros((nr, V_DIM), f32)
    for h in range(N_HEADS):
        acc = jnp.zeros((nr, V_DIM), f32)
        for p in range(n_pages):
            vh = v_refs[p][pl.ds(h, PAGE, stride=N_HEADS), :].astype(bf16)
            acc = acc + jnp.dot(p_bf[:, p * PAGE:(p + 1) * PAGE], vh, preferred_element_type=f32)
        for a in range(nq):
            acc = acc + p_new[:, a:a + 1] * vn[a:a + 1, h * V_DIM:(h + 1) * V_DIM]
        o = jnp.where(rowhead == h, acc, o)
    o = o / l
    lam = _lam(lq1_ref[...], lk1_ref[...], lq2_ref[...], lk2_ref[...])
    d = o - lam * pltpu.roll(o, nr - 1, 0)
    d = _rms(d, g_ref[...], SUBLN_EPS) * (1.0 - LAM_INIT)
    for a in range(nq):
        rowv = jnp.concatenate([d[a * N_SUB + 2 * h:a * N_SUB + 2 * h + 1] for h in range(N_HEADS)], axis=1)
        o_ref[a:a + 1, :] = rowv.astype(o_ref.dtype)


def _attn_s_call(pt_flat, q, kn, vn, lq1, lk1, lq2, lk2, g, kt, v2, n_pages):
    nb, nq, _ = q.shape
    small = lambda n: pl.BlockSpec((1, n), lambda b, pt: (0, 0))

    def page_spec(p):
        return pl.BlockSpec((None, 512, PAGE), lambda b, pt: (pt[b * n_pages + p], 0, 0))

    return pl.pallas_call(
        functools.partial(_attn_s_kernel, n_pages),
        grid_spec=pltpu.PrefetchScalarGridSpec(
            num_scalar_prefetch=1,
            grid=(nb,),
            in_specs=[pl.BlockSpec((None, nq, ATTN_WIDTH), lambda b, pt: (b, 0, 0)),
                      pl.BlockSpec((None, 8, ATTN_WIDTH), lambda b, pt: (b, 0, 0)),
                      pl.BlockSpec((None, 8, ATTN_WIDTH), lambda b, pt: (b, 0, 0)),
                      small(HEAD_DIM), small(HEAD_DIM), small(HEAD_DIM), small(HEAD_DIM), small(V_DIM)]
                     + [page_spec(p) for p in range(n_pages)] * 2,
            out_specs=pl.BlockSpec((None, nq, ATTN_WIDTH), lambda b, pt: (b, 0, 0)),
        ),
        out_shape=jax.ShapeDtypeStruct((nb, nq, ATTN_WIDTH), bf16),
        compiler_params=_cparams(("arbitrary",)),
        name="attn_s",
    )(pt_flat, q, kn, vn, lq1, lk1, lq2, lk2, g, *([kt] * n_pages), *([v2] * n_pages))


SLAB = D_MODEL // LANES


def _store_slabs(ref, row0, x):
    n = x.shape[0]
    for a in range(SLAB):
        ref[pl.ds(SLAB * row0 + a, n, stride=SLAB), :] = x[:, a * LANES:(a + 1) * LANES]


def _load_slabs(ref, row0, n):
    return jnp.concatenate([ref[pl.ds(SLAB * row0 + a, n, stride=SLAB), :] for a in range(SLAB)], axis=1)


def _slab_copy(src_ref, src_row, dst_ref, dst_row, sem):
    src = src_ref.at[pl.ds(pl.multiple_of(src_row * SLAB, SLAB), SLAB), :]
    dst = dst_ref.at[pl.ds(pl.multiple_of(dst_row * SLAB, SLAB), SLAB), :]
    return pltpu.make_async_copy(src, dst, sem)


def _post_kernel(x_ref, att_ref, cv_ref, ga_ref, shf_ref, scf_ref, ln_ref, wo_ref, wr_ref, br_ref, cnt_in_ref,
                 ltri_ref, x1_ref, h2_ref, rt_ref, rw_ref, cnt_ref, carry_ref):
    i = pl.program_id(0)

    @pl.when(i == 0)
    def _():
        carry_ref[...] = cnt_in_ref[...]

    mix = (jnp.dot(att_ref[...], wo_ref[0:ATTN_WIDTH, :], preferred_element_type=f32)
           + jnp.dot(cv_ref[...], wo_ref[ATTN_WIDTH:, :], preferred_element_type=f32))
    x1 = x_ref[...] + ga_ref[...] * mix
    x1_ref[...] = x1
    h2 = _rms(x1, ln_ref[...], NORM_EPS) * (1.0 + scf_ref[...]) + shf_ref[...]
    _store_slabs(h2_ref, 0, h2)
    logits = jnp.dot(h2.astype(bf16), wr_ref[...], preferred_element_type=f32) + br_ref[...]
    lane = lax.broadcasted_iota(jnp.int32, logits.shape, 1)
    lane_f = lane.astype(f32)
    big = jnp.float32(1e9)
    neg = -jnp.inf

    def first_max(vals):
        mx = jnp.max(vals, axis=-1, keepdims=True)
        idx = jnp.min(jnp.where(vals == mx, lane_f, big), axis=-1, keepdims=True)
        return mx, idx

    gl = jnp.where((lane >= ROUTE_GRP_LANE) & (lane < ROUTE_GRP_LANE + N_GROUPS), logits, neg)
    gmax, gidx = first_max(gl)
    g_p = 1.0 / jnp.sum(jnp.exp(gl - gmax), axis=-1, keepdims=True)
    lo = (gidx - ROUTE_GRP_LANE) * EXP_PER_GROUP
    el = jnp.where((lane_f >= lo) & (lane_f < lo + EXP_PER_GROUP), logits, neg)
    v1, i1 = first_max(el)
    el2 = jnp.where(lane_f == i1, neg, el)
    v2, i2 = first_max(el2)
    t = jnp.exp(v2 - v1)
    w1 = g_p / (1.0 + t)
    w2 = g_p * t / (1.0 + t)
    oh1 = lane_f == i1
    oh2 = lane_f == i2
    cnt = jnp.where(oh1 | oh2, 1.0, 0.0)
    prefix = jnp.dot(ltri_ref[...], cnt.astype(bf16), preferred_element_type=f32) + carry_ref[...]
    r1 = jnp.sum(jnp.where(oh1, prefix, 0.0), axis=-1, keepdims=True)
    r2 = jnp.sum(jnp.where(oh2, prefix, 0.0), axis=-1, keepdims=True)
    ri = jnp.where(lane == 0, i1, jnp.where(lane == 1, i2, jnp.where(lane == 2, r1, jnp.where(lane == 3, r2, 0.0))))
    rt_ref[...] = ri.T[0:8, :].astype(jnp.int32)
    rw_ref[...] = jnp.where(lane == 0, w1, jnp.where(lane == 1, w2, 0.0))
    new_carry = carry_ref[...] + jnp.sum(cnt, axis=0, keepdims=True)
    carry_ref[...] = new_carry
    cnt_ref[...] = new_carry


def _post_call(x, att, cv, mod, mod_spec, ln_ffn, w_out_bf, w_r_bf, b_r, cnt_in, ltri, name):
    n = x.shape[0]
    row = lambda w: pl.BlockSpec((TM, w), lambda i: (i, 0))
    const = lambda shape: pl.BlockSpec(shape, lambda i: (0, 0))
    return pl.pallas_call(
        _post_kernel,
        grid=(n // TM,),
        in_specs=[row(D_MODEL), row(ATTN_WIDTH), row(CONV_WIDTH), mod_spec(2), mod_spec(3), mod_spec(4),
                  const((1, D_MODEL)), const((D_MODEL, D_MODEL)), const((D_MODEL, LANES)), const((1, LANES)),
                  const((1, LANES)), const((TM, TM))],
        out_specs=[row(D_MODEL), pl.BlockSpec((TM * SLAB, LANES), lambda i: (i, 0)),
                   pl.BlockSpec((8, TM), lambda i: (0, i)), row(LANES), const((1, LANES))],
        out_shape=[jax.ShapeDtypeStruct((n, D_MODEL), f32), jax.ShapeDtypeStruct((n * SLAB, LANES), f32),
                   jax.ShapeDtypeStruct((8, n), jnp.int32), jax.ShapeDtypeStruct((n, LANES), f32),
                   jax.ShapeDtypeStruct((1, LANES), f32)],
        scratch_shapes=[pltpu.VMEM((1, LANES), f32)],
        compiler_params=_cparams(("arbitrary",)),
        name=name,
    )(x, att, cv, mod, mod, mod, ln_ffn, w_out_bf, w_r_bf, b_r, cnt_in, ltri)


def _dispatch_rows(pos0_ref, pos1_ref, src_ref, t0, xs_ref, sem):
    tm = src_ref.shape[0] // SLAB

    def issue(r, c):
        t = t0 + r
        _slab_copy(src_ref, r, xs_ref, pos0_ref[t], sem).start(priority=0)
        _slab_copy(src_ref, r, xs_ref, pos1_ref[t], sem).start(priority=1)
        return c

    lax.fori_loop(0, tm, issue, 0, unroll=ROW_DMA_UNROLL)
    tile_copy = pltpu.make_async_copy(src_ref, xs_ref.at[pl.ds(0, tm * SLAB), :], sem)
    tile_copy.wait()
    tile_copy.wait()


def _dispatch_kernel(n_tiles_p, pos0_ref, pos1_ref, hp_ref, hs_ref, xs_ref, sem):
    i = pl.program_id(0)

    @pl.when(i < n_tiles_p)
    def _():
        _dispatch_rows(pos0_ref, pos1_ref, hp_ref, i * TM, xs_ref, sem)

    @pl.when(i >= n_tiles_p)
    def _():
        _dispatch_rows(pos0_ref, pos1_ref, hs_ref, i * TM, xs_ref, sem)


def _dispatch_call(pos0, pos1, h2_p, h2_s):
    tp = h2_p.shape[0] // (TM * SLAB)
    ts = h2_s.shape[0] // (TM * SLAB)
    return pl.pallas_call(
        functools.partial(_dispatch_kernel, tp),
        grid_spec=pltpu.PrefetchScalarGridSpec(
            num_scalar_prefetch=2,
            grid=(tp + ts,),
            in_specs=[pl.BlockSpec((TM * SLAB, LANES), lambda i, p0, p1: (jnp.minimum(i, tp - 1), 0)),
                      pl.BlockSpec((TM * SLAB, LANES), lambda i, p0, p1: (jnp.maximum(i - tp, 0), 0))],
            out_specs=pl.BlockSpec(memory_space=pl.ANY),
            scratch_shapes=[pltpu.SemaphoreType.DMA(())],
        ),
        out_shape=jax.ShapeDtypeStruct((2 * (tp + ts) * TM * SLAB, LANES), f32),
        compiler_params=_cparams(("arbitrary",)),
        name="dispatch",
    )(pos0, pos1, h2_p, h2_s)


def _experts_kernel(tile_ref, exp_ref, lo_ref, hi_ref, xs_ref, wg_ref, wu_ref, wd_ref, ys_ref,
                    wg_bf, wu_bf, wd_bf, cur_ref):
    w = pl.program_id(0)
    lo = lo_ref[w]
    hi = hi_ref[w]

    @pl.when(w == 0)
    def _():
        cur_ref[0] = -1

    @pl.when(hi > lo)
    def _():
        e = exp_ref[w]

        @pl.when(cur_ref[0] != e)
        def _():
            wg_bf[...] = wg_ref[...].astype(bf16)
            wu_bf[...] = wu_ref[...].astype(bf16)
            wd_bf[...] = wd_ref[...].astype(bf16)
            cur_ref[0] = e

        x = _load_slabs(xs_ref, 0, TE).astype(bf16)
        g = jnp.dot(x, wg_bf[...], preferred_element_type=f32)
        u = jnp.dot(x, wu_bf[...], preferred_element_type=f32)
        hid = (_silu(g) * u).astype(bf16)
        y = jnp.dot(hid, wd_bf[...], preferred_element_type=f32)
        base = tile_ref[w] * TE
        row = base + lax.broadcasted_iota(jnp.int32, y.shape, 0)
        mine = (row >= lo) & (row < hi)

        @pl.when(lo == base)
        def _():
            _store_slabs(ys_ref, 0, jnp.where(mine, y, 0.0))

        @pl.when(lo != base)
        def _():
            _store_slabs(ys_ref, 0, jnp.where(mine, y, _load_slabs(ys_ref, 0, TE)))


def _experts_call(tile_id, exp_id, seg_lo, seg_hi, xs, w_gate, w_up, w_down):
    n_items = tile_id.shape[0]
    return pl.pallas_call(
        _experts_kernel,
        grid_spec=pltpu.PrefetchScalarGridSpec(
            num_scalar_prefetch=4,
            grid=(n_items,),
            in_specs=[pl.BlockSpec((TE * SLAB, LANES), lambda w, t, e, lo, hi: (t[w], 0)),
                      pl.BlockSpec((None, D_MODEL, D_EXPERT), lambda w, t, e, lo, hi: (e[w], 0, 0)),
                      pl.BlockSpec((None, D_MODEL, D_EXPERT), lambda w, t, e, lo, hi: (e[w], 0, 0)),
                      pl.BlockSpec((None, D_EXPERT, D_MODEL), lambda w, t, e, lo, hi: (e[w], 0, 0))],
            out_specs=pl.BlockSpec((TE * SLAB, LANES), lambda w, t, e, lo, hi: (t[w], 0)),
            scratch_shapes=[pltpu.VMEM((D_MODEL, D_EXPERT), bf16), pltpu.VMEM((D_MODEL, D_EXPERT), bf16),
                            pltpu.VMEM((D_EXPERT, D_MODEL), bf16), pltpu.SMEM((1,), jnp.int32)],
        ),
        out_shape=jax.ShapeDtypeStruct(xs.shape, f32),
        compiler_params=_cparams(("arbitrary",)),
        name="experts",
    )(tile_id, exp_id, seg_lo, seg_hi, xs, w_gate, w_up, w_down)


def _combine_kernel(pos0_ref, pos1_ref, x1_ref, rw_ref, gf_ref, ln_ref, ys_ref, o_ref, ybuf, sem):
    i = pl.program_id(0)
    tm = x1_ref.shape[0]

    def issue(r, c):
        t = i * tm + r
        _slab_copy(ys_ref, pos0_ref[t], ybuf, r, sem).start(priority=0)
        _slab_copy(ys_ref, pos1_ref[t], ybuf, tm + r, sem).start(priority=1)
        return c

    lax.fori_loop(0, tm, issue, 0, unroll=ROW_DMA_UNROLL)
    tile_copy = pltpu.make_async_copy(ys_ref.at[pl.ds(0, tm * SLAB), :], ybuf.at[pl.ds(0, tm * SLAB), :], sem)
    tile_copy.wait()
    tile_copy.wait()
    rw = rw_ref[...]
    moe = rw[:, 0:1] * _load_slabs(ybuf, 0, tm) + rw[:, 1:2] * _load_slabs(ybuf, tm, tm)
    x2 = x1_ref[...] + gf_ref[...] * moe
    o_ref[...] = _rms(x2, ln_ref[...], NORM_EPS)


def _combine_call(pos0, pos1, x1, rw, mod, mod_spec, ln_final, ys, name):
    n = x1.shape[0]
    return pl.pallas_call(
        _combine_kernel,
        grid_spec=pltpu.PrefetchScalarGridSpec(
            num_scalar_prefetch=2,
            grid=(n // TM,),
            in_specs=[pl.BlockSpec((TM, D_MODEL), lambda i, p0, p1: (i, 0)),
                      pl.BlockSpec((TM, LANES), lambda i, p0, p1: (i, 0)),
                      mod_spec(5),
                      pl.BlockSpec((1, D_MODEL), lambda i, p0, p1: (0, 0)),
                      pl.BlockSpec(memory_space=pl.ANY)],
            out_specs=pl.BlockSpec((TM, D_MODEL), lambda i, p0, p1: (i, 0)),
            scratch_shapes=[pltpu.VMEM((2 * TM * SLAB, LANES), f32), pltpu.SemaphoreType.DMA(())],
        ),
        out_shape=jax.ShapeDtypeStruct((n, D_MODEL), f32),
        compiler_params=_cparams(("arbitrary",)),
        name=name,
    )(pos0, pos1, x1, rw, mod, ln_final, ys)


def _rope_tables(pos):
    inv = 1.0 / (ROPE_THETA ** (np.arange(0, HEAD_DIM, 2, dtype=np.float64) / HEAD_DIM))
    ang = np.asarray(pos, np.float64)[:, None] * inv[None, :]
    ang = np.concatenate([ang, ang], axis=-1)
    sign = np.where(np.arange(HEAD_DIM) < HEAD_DIM // 2, -1.0, 1.0)
    cos = np.tile(np.cos(ang), (1, N_SUB)).astype(np.float32)
    sin_signed = np.tile(np.sin(ang) * sign[None, :], (1, N_SUB)).astype(np.float32)
    return jnp.asarray(cos), jnp.asarray(sin_signed)


def _segments(counts, n_rows):
    n_tiles = n_rows // TE
    offs = jnp.concatenate([jnp.zeros((1,), jnp.int32), jnp.cumsum(counts)[:-1].astype(jnp.int32)])
    tiles = jnp.arange(n_tiles, dtype=jnp.int32) * TE
    rank_t = jnp.arange(n_tiles, dtype=jnp.int32) + jnp.sum(offs[None, :] < tiles[:, None], axis=1).astype(jnp.int32)
    rank_o = jnp.arange(N_EXPERTS, dtype=jnp.int32) + jnp.minimum(offs // TE + 1, n_tiles)
    vals = jnp.concatenate([tiles, offs])
    ranks = jnp.concatenate([rank_t, rank_o])
    n_items = n_tiles + N_EXPERTS
    w = jnp.arange(n_items, dtype=jnp.int32)
    seg_lo = jnp.sum(jnp.where(ranks[None, :] == w[:, None], vals[None, :], 0), axis=1).astype(jnp.int32)
    seg_hi = jnp.concatenate([seg_lo[1:], jnp.full((1,), n_rows, jnp.int32)])
    tile_id = jnp.minimum(seg_lo // TE, n_tiles - 1)
    exp_id = jnp.sum(offs[None, :] <= seg_lo[:, None], axis=1).astype(jnp.int32) - 1
    return offs, tile_id, exp_id, seg_lo, seg_hi


def kernel(x_prompt, x_sample, cache_k, cache_v, state_conv, page_table, c_prompt, c_sample, w_ada, b_ada, ln_mix, w_in, lam_q1, lam_k1, lam_q2, lam_k2, subln_g, w_conv, w_out, ln_ffn, w_router_grp, b_router_grp, w_router_exp, b_router_exp, w_gate, w_up, w_down, ln_final):
    B, S, _ = x_prompt.shape
    DB, L, _ = x_sample.shape
    n_phys = cache_k.shape[1]
    n_pages = page_table.shape[1]
    past = n_pages * PAGE
    n_p = B * S
    n_s = DB * L
    n_tok = n_p + n_s

    w_in_bf = w_in[0].astype(bf16)
    w_out_bf = w_out[0].astype(bf16)
    w_r = jnp.zeros((D_MODEL, LANES), f32)
    w_r = w_r.at[:, :N_EXPERTS].set(w_router_exp[0]).at[:, ROUTE_GRP_LANE:ROUTE_GRP_LANE + N_GROUPS].set(w_router_grp[0])
    b_r = jnp.zeros((1, LANES), f32)
    b_r = b_r.at[0, :N_EXPERTS].set(b_router_exp[0]).at[0, ROUTE_GRP_LANE:ROUTE_GRP_LANE + N_GROUPS].set(b_router_grp[0])
    w_r_bf = w_r.astype(bf16)
    cos_p, sin_p = _rope_tables(np.arange(S))
    cos_s, sin_s = _rope_tables(past + np.repeat(np.arange(L), DB))
    ltri = jnp.asarray(np.tril(np.ones((TM, TM), np.float32), -1), bf16)

    mod = _mod_call(jnp.concatenate([c_prompt, c_sample], axis=0), w_ada[0], b_ada)
    mod_p = mod[:B].reshape(B, 1, 6 * D_MODEL)
    mod_s = jnp.tile(mod[B:], (L, 1))

    qt_p, kt_p, kb_p, v4_p, vt_p, cv_p, st_p = _inproj_p_call(x_prompt, mod_p, ln_mix, w_in_bf, cos_p, sin_p, w_conv[0])
    att_p = _attn_p_call(qt_p, kb_p, vt_p, lam_q1, lam_k1, lam_q2, lam_k2, subln_g.reshape(V_DIM, 1))

    xs_l = x_sample.transpose(1, 0, 2).reshape(n_s, D_MODEL)
    st_in = state_conv[0].transpose(1, 0, 2)
    q_s, kf_s, vf_s, cv_s, st_s = _inproj_s_call(xs_l, mod_s[:, 0:D_MODEL], mod_s[:, D_MODEL:2 * D_MODEL], ln_mix,
                                                 w_in_bf, cos_s, sin_s, w_conv[0], st_in)
    to_b = lambda a: a.reshape(L, DB, -1).transpose(1, 0, 2)
    pad8 = lambda a: jnp.pad(a, ((0, 0), (0, 8 - L), (0, 0)))
    kt = jnp.transpose(cache_k[0], (0, 2, 3, 1)).reshape(n_phys, N_SUB * HEAD_DIM, PAGE)
    v2 = cache_v[0].reshape(n_phys, PAGE * N_HEADS, V_DIM)
    att_s_b = _attn_s_call(page_table.reshape(-1), to_b(q_s), pad8(to_b(kf_s)), pad8(to_b(vf_s)),
                           lam_q1, lam_k1, lam_q2, lam_k2, subln_g, kt, v2, n_pages)
    att_s = att_s_b.transpose(1, 0, 2).reshape(n_s, ATTN_WIDTH)

    zero_cnt = jnp.zeros((1, LANES), f32)
    spec_p = lambda sec: pl.BlockSpec((None, 1, D_MODEL), lambda i, *_: (i // (S // TM), 0, sec))
    spec_s = lambda sec: pl.BlockSpec((TM, D_MODEL), lambda i, *_: (i, sec))
    x1_p, h2_p, rt_p, rw_p, cnt_p = _post_call(x_prompt.reshape(n_p, D_MODEL), att_p.reshape(n_p, ATTN_WIDTH),
                                               cv_p.reshape(n_p, CONV_WIDTH), mod_p, spec_p, ln_ffn, w_out_bf, w_r_bf,
                                               b_r, zero_cnt, ltri, "post_p")
    x1_s, h2_s, rt_s, rw_s, cnt = _post_call(xs_l, att_s, cv_s, mod_s, spec_s, ln_ffn, w_out_bf, w_r_bf, b_r, cnt_p,
                                             ltri, "post_s")

    counts = cnt[0, :N_EXPERTS].astype(jnp.int32)
    offs, tile_id, exp_id, seg_lo, seg_hi = _segments(counts, 2 * n_tok)
    rt = jnp.concatenate([rt_p, rt_s], axis=1)
    pos0 = offs[rt[0]] + rt[2]
    pos1 = offs[rt[1]] + rt[3]

    xs_sorted = _dispatch_call(pos0, pos1, h2_p, h2_s)
    ys = _experts_call(tile_id, exp_id, seg_lo, seg_hi, xs_sorted, w_gate[0], w_up[0], w_down[0])

    y_p = _combine_call(pos0[:n_p], pos1[:n_p], x1_p, rw_p, mod_p, spec_p, ln_final.reshape(1, D_MODEL), ys, "combine_p")
    y_s = _combine_call(pos0[n_p:], pos1[n_p:], x1_s, rw_s, mod_s, spec_s, ln_final.reshape(1, D_MODEL), ys, "combine_s")

    from_l = lambda a: a.reshape(L, DB, -1).transpose(1, 0, 2)
    y_prompt = y_p.reshape(B, S, D_MODEL)
    y_sample = from_l(y_s)
    k_prompt = kt_p.reshape(B, N_SUB, HEAD_DIM, S).transpose(0, 3, 1, 2)[None]
    v_prompt = v4_p.reshape(1, B, S, N_HEADS, V_DIM)
    conv_prompt = st_p[None]
    k_sample = from_l(kf_s).reshape(1, DB, L, N_SUB, HEAD_DIM)
    v_sample = from_l(vf_s).reshape(1, DB, L, N_HEADS, V_DIM)
    conv_sample = st_s.transpose(1, 0, 2)[None]
    return (y_prompt, y_sample, k_prompt, v_prompt, conv_prompt, k_sample, v_sample, conv_sample)
```

```python
import functools
import math

import jax
import jax.numpy as jnp
import numpy as np
from jax import lax
from jax.experimental import pallas as pl
from jax.experimental.pallas import tpu as pltpu

D_MODEL = 1024
ATTN_WIDTH = 512
CONV_WIDTH = 512
N_HEADS = 4
N_SUB = 8
HEAD_DIM = 64
V_DIM = 2 * HEAD_DIM
CONV_K = 3
ROPE_THETA = 10000.0
N_GROUPS = 4
EXP_PER_GROUP = 8
N_EXPERTS = 32
D_EXPERT = 256
NORM_EPS = 1e-6
SUBLN_EPS = 1e-5
LAM_INIT = 0.8 - 0.6 * math.exp(-0.3 * 0)
LOG2E = math.log2(math.e)
PAGE = 128
LANES = 128
ROUTE_GRP_LANE = 32

TM = 512
TQ = 512
TE = 256
ROW_DMA_UNROLL = 8
ONES_ROWS = 16
VMEM_LIMIT = 56 * 1024 * 1024

f32 = jnp.float32
bf16 = jnp.bfloat16


def _cparams(sem):
    return pltpu.CompilerParams(dimension_semantics=sem, vmem_limit_bytes=VMEM_LIMIT)


def _rms(x, g, eps):
    return x * lax.rsqrt(jnp.mean(x * x, axis=-1, keepdims=True) + eps) * g


def _silu(x):
    return x * (1.0 / (1.0 + jnp.exp(-x)))


def _mod_kernel(c_ref, w_ref, b_ref, o_ref):
    a = _silu(c_ref[...]).astype(bf16)
    o_ref[...] = jnp.dot(a, w_ref[...].astype(bf16), preferred_element_type=f32) + b_ref[...]


def _mod_call(c_all, w_ada, b_ada):
    n = c_all.shape[0]
    return pl.pallas_call(
        _mod_kernel,
        grid=(6,),
        in_specs=[pl.BlockSpec((n, D_MODEL), lambda j: (0, 0)),
                  pl.BlockSpec((D_MODEL, D_MODEL), lambda j: (0, j)),
                  pl.BlockSpec((1, D_MODEL), lambda j: (0, j))],
        out_specs=pl.BlockSpec((n, D_MODEL), lambda j: (0, j)),
        out_shape=jax.ShapeDtypeStruct((n, 6 * D_MODEL), f32),
        compiler_params=_cparams(("arbitrary",)),
        name="mod",
    )(c_all, w_ada, b_ada)


def _rope(t, cos, sin_signed, lo_mask):
    n = t.shape[-1]
    rot = jnp.where(lo_mask, pltpu.roll(t, n - HEAD_DIM // 2, 1), pltpu.roll(t, HEAD_DIM // 2, 1))
    return t * cos + rot * sin_signed


def _inproj_common(x, sh, sc, ln, w_ref, cos, sin_signed):
    h = (_rms(x, ln, NORM_EPS) * (1.0 + sc) + sh).astype(bf16)

    def sec(i):
        return jnp.dot(h, w_ref[:, i * 512:(i + 1) * 512], preferred_element_type=f32)

    lane = lax.broadcasted_iota(jnp.int32, (x.shape[0], 512), 1)
    lo_mask = (lane % HEAD_DIM) < (HEAD_DIM // 2)
    q = _rope(sec(0), cos, sin_signed, lo_mask) * (HEAD_DIM ** -0.5 * LOG2E)
    k = _rope(sec(1), cos, sin_signed, lo_mask)
    v = sec(2)
    bg = sec(3)
    cu = sec(4) * sec(5)
    return q, k, v, bg, cu


def _inproj_p_kernel(x_ref, sh_ref, sc_ref, ln_ref, w_ref, cos_ref, sin_ref, wc_ref,
                     qt_ref, kt_ref, kb_ref, v4_ref, vt_ref, cv_ref, st_ref, carry_ref):
    s = pl.program_id(0)
    b = pl.program_id(1)
    q, k, v, bg, cu = _inproj_common(x_ref[...], sh_ref[...], sc_ref[...], ln_ref[...], w_ref,
                                     cos_ref[...], sin_ref[...])
    qt_ref[...] = q.T.astype(bf16)
    kt_ref[...] = k.T
    kb_ref[...] = k.astype(bf16)
    vt_ref[...] = v.T.astype(bf16)
    for h in range(N_HEADS):
        v4_ref[pl.ds(h, v.shape[0], stride=N_HEADS), :] = v[:, h * V_DIM:(h + 1) * V_DIM]
    tm = cu.shape[0]
    prev = jnp.where(s > 0, carry_ref[b], 0.0)
    row = lax.broadcasted_iota(jnp.int32, cu.shape, 0)
    cu1 = jnp.where(row == 0, prev[1:2], pltpu.roll(cu, 1, 0))
    cu2 = jnp.where(row == 0, prev[0:1], jnp.where(row == 1, prev[1:2], pltpu.roll(cu, 2, 0)))
    wc = wc_ref[...]
    conv = wc[0:1] * cu2 + wc[1:2] * cu1 + wc[2:3] * cu
    cv_ref[...] = (bg * conv).astype(bf16)
    last2 = cu[tm - 2:tm]
    carry_ref[b, 0:2, :] = last2
    st_ref[b] = last2


def _inproj_p_call(x, mod3, ln_mix, w_in_bf, cos, sin_signed, w_conv):
    B, S, _ = x.shape
    ns = S // TM
    row = lambda s, b: (b, s, 0)
    col = lambda s, b: (b, 0, s)
    rows = lambda dt: jax.ShapeDtypeStruct((B, S, 512), dt)
    cols = lambda dt: jax.ShapeDtypeStruct((B, 512, S), dt)
    row_spec = pl.BlockSpec((None, TM, 512), row)
    col_spec = pl.BlockSpec((None, 512, TM), col)
    return pl.pallas_call(
        _inproj_p_kernel,
        grid=(ns, B),
        in_specs=[pl.BlockSpec((None, TM, D_MODEL), row),
                  pl.BlockSpec((None, 1, D_MODEL), lambda s, b: (b, 0, 0)),
                  pl.BlockSpec((None, 1, D_MODEL), lambda s, b: (b, 0, 1)),
                  pl.BlockSpec((1, D_MODEL), lambda s, b: (0, 0)),
                  pl.BlockSpec((D_MODEL, 3072), lambda s, b: (0, 0)),
                  pl.BlockSpec((TM, 512), lambda s, b: (s, 0)),
                  pl.BlockSpec((TM, 512), lambda s, b: (s, 0)),
                  pl.BlockSpec((CONV_K, CONV_WIDTH), lambda s, b: (0, 0))],
        out_specs=[col_spec, col_spec, row_spec, pl.BlockSpec((None, N_HEADS * TM, V_DIM), row),
                   pl.BlockSpec((None, None, 512, TM), lambda s, b: (b, s, 0, 0)), row_spec,
                   pl.BlockSpec((B, 2, CONV_WIDTH), lambda s, b: (0, 0, 0))],
        out_shape=[cols(bf16), cols(f32), rows(bf16), jax.ShapeDtypeStruct((B, N_HEADS * S, V_DIM), f32),
                   jax.ShapeDtypeStruct((B, ns, 512, TM), bf16), rows(bf16),
                   jax.ShapeDtypeStruct((B, 2, CONV_WIDTH), f32)],
        scratch_shapes=[pltpu.VMEM((B, 8, CONV_WIDTH), f32)],
        compiler_params=_cparams(("arbitrary", "arbitrary")),
        name="inproj_p",
    )(x, mod3, mod3, ln_mix, w_in_bf, cos, sin_signed, w_conv)


def _inproj_s_kernel(x_ref, sh_ref, sc_ref, ln_ref, w_ref, cos_ref, sin_ref, wc_ref, st_in_ref,
                     q_ref, kf_ref, vf_ref, cv_ref, st_ref):
    q, k, v, bg, cu = _inproj_common(x_ref[...], sh_ref[...], sc_ref[...], ln_ref[...], w_ref,
                                     cos_ref[...], sin_ref[...])
    q_ref[...] = q.astype(bf16)
    kf_ref[...] = k
    vf_ref[...] = v
    nb = st_in_ref.shape[1]
    st0 = st_in_ref[0]
    st1 = st_in_ref[1]
    cu1 = jnp.concatenate([st1, cu[:3 * nb]], axis=0)
    cu2 = jnp.concatenate([st0, st1, cu[:2 * nb]], axis=0)
    wc = wc_ref[...]
    conv = wc[0:1] * cu2 + wc[1:2] * cu1 + wc[2:3] * cu
    cv_ref[...] = (bg * conv).astype(bf16)
    st_ref[0] = cu[2 * nb:3 * nb]
    st_ref[1] = cu[3 * nb:4 * nb]


def _inproj_s_call(x, sh, sc, ln_mix, w_in_bf, cos, sin_signed, w_conv, st_in):
    n = x.shape[0]
    nb = st_in.shape[1]
    full = lambda shape: pl.BlockSpec(shape, lambda i: (0,) * len(shape))
    return pl.pallas_call(
        _inproj_s_kernel,
        grid=(1,),
        in_specs=[full((n, D_MODEL)), full((n, D_MODEL)), full((n, D_MODEL)), full((1, D_MODEL)),
                  full((D_MODEL, 3072)), full((n, 512)), full((n, 512)), full((CONV_K, CONV_WIDTH)),
                  full((2, nb, CONV_WIDTH))],
        out_specs=[full((n, 512))] * 4 + [full((2, nb, CONV_WIDTH))],
        out_shape=[jax.ShapeDtypeStruct((n, 512), bf16), jax.ShapeDtypeStruct((n, 512), f32),
                   jax.ShapeDtypeStruct((n, 512), f32), jax.ShapeDtypeStruct((n, 512), bf16),
                   jax.ShapeDtypeStruct((2, nb, CONV_WIDTH), f32)],
        compiler_params=_cparams(("arbitrary",)),
        name="inproj_s",
    )(x, sh, sc, ln_mix, w_in_bf, cos, sin_signed, w_conv, st_in)


def _lam(lq1, lk1, lq2, lk2):
    a = jnp.sum(lq1 * lk1, axis=-1, keepdims=True)
    b = jnp.sum(lq2 * lk2, axis=-1, keepdims=True)
    return jnp.exp(a) - jnp.exp(b) + LAM_INIT


def _attn_p_body(i, qt_ref, k_ref, vt_ref, lam, g_ref, o_ref, m_ref, acc_ref):
    tq = qt_ref.shape[1]
    tk = tq
    m_ref[...] = jnp.full(m_ref.shape, -jnp.inf, f32)
    acc_ref[...] = jnp.zeros(acc_ref.shape, f32)

    def step(j, masked):
        k0 = pl.multiple_of(j * tk, tk)
        qt = qt_ref[...]
        row = lax.broadcasted_iota(jnp.int32, qt.shape, 0)
        zero = jnp.zeros_like(qt)
        q2t = jnp.concatenate([jnp.where(row < HEAD_DIM, qt, zero), jnp.where(row >= HEAD_DIM, qt, zero)], axis=1)
        st = jnp.dot(k_ref[pl.ds(k0, tk), :], q2t, preferred_element_type=f32)
        if masked:
            kpos = lax.broadcasted_iota(jnp.int32, st.shape, 0)
            c = lax.broadcasted_iota(jnp.int32, st.shape, 1)
            st = jnp.where(kpos <= jnp.where(c >= tq, c - tq, c), st, -jnp.inf)
        m_prev = m_ref[...]
        m_new = jnp.maximum(m_prev, jnp.max(st, axis=0, keepdims=True))
        alpha = jnp.exp2(m_prev - m_new)
        pt = jnp.exp2(st - m_new).astype(bf16)
        vt1 = jnp.concatenate([vt_ref[j], jnp.ones((ONES_ROWS, tk), bf16)], axis=0)
        acc_ref[...] = alpha * acc_ref[...] + jnp.dot(vt1, pt, preferred_element_type=f32)
        m_ref[...] = m_new

    def below_diagonal(j, c):
        step(j, False)
        return c

    lax.fori_loop(0, i, below_diagonal, 0)
    step(i, True)
    ot = acc_ref[0:V_DIM, :] / acc_ref[V_DIM:V_DIM + 1, :]
    dt = ot[:, :tq] - lam * ot[:, tq:]
    ms = jnp.mean(dt * dt, axis=0, keepdims=True)
    nt = dt * lax.rsqrt(ms + SUBLN_EPS) * g_ref[...] * (1.0 - LAM_INIT)
    o_ref[...] = nt.T.astype(bf16)


def _attn_s_body(q_ref, kn_ref, vn_ref, lam, g_ref, kt_refs, v_refs, o_ref):
    n_pages = len(kt_refs)
    nq = q_ref.shape[0]
    nr = nq * N_SUB
    qf = q_ref[...].astype(f32)
    sub = lax.broadcasted_iota(jnp.int32, (N_SUB, ATTN_WIDTH), 0)
    col = lax.broadcasted_iota(jnp.int32, (N_SUB, ATTN_WIDTH), 1)
    diag = (col // HEAD_DIM) == sub
    qbd = jnp.concatenate([jnp.where(diag, jnp.broadcast_to(qf[a:a + 1], (N_SUB, ATTN_WIDTH)), 0.0)
                           for a in range(nq)], axis=0).astype(bf16)
    s_past = jnp.concatenate([jnp.dot(qbd, kt_refs[p][...].astype(bf16), preferred_element_type=f32)
                              for p in range(n_pages)], axis=1)
    s_new = lax.dot_general(qbd, kn_ref[...].astype(bf16), (((1,), (1,)), ((), ())),
                            preferred_element_type=f32)
    r = lax.broadcasted_iota(jnp.int32, s_new.shape, 0)
    c = lax.broadcasted_iota(jnp.int32, s_new.shape, 1)
    s_new = jnp.where((c <= r // N_SUB) & (c < nq), s_new, -jnp.inf)
    m = jnp.maximum(jnp.max(s_past, axis=-1, keepdims=True), jnp.max(s_new, axis=-1, keepdims=True))
    p_past = jnp.exp2(s_past - m)
    p_new = jnp.exp2(s_new - m)
    l = jnp.sum(p_past, axis=-1, keepdims=True) + jnp.sum(p_new, axis=-1, keepdims=True)
    p_bf = p_past.astype(bf16)
    vn = vn_ref[...]
    rowhead = (lax.broadcasted_iota(jnp.int32, (nr, V_DIM), 0) % N_SUB) // 2
    o = jnp.zeros((nr, V_DIM), f32)
    for h in range(N_HEADS):
        acc = jnp.zeros((nr, V_DIM), f32)
        for p in range(n_pages):
            vh = v_refs[p][pl.ds(h, PAGE, stride=N_HEADS), :].astype(bf16)
            acc = acc + jnp.dot(p_bf[:, p * PAGE:(p + 1) * PAGE], vh, preferred_element_type=f32)
        for a in range(nq):
            acc = acc + p_new[:, a:a + 1] * vn[a:a + 1, h * V_DIM:(h + 1) * V_DIM]
        o = jnp.where(rowhead == h, acc, o)
    o = o / l
    d = o - lam * pltpu.roll(o, nr - 1, 0)
    d = _rms(d, g_ref[...], SUBLN_EPS) * (1.0 - LAM_INIT)
    for a in range(nq):
        rowv = jnp.concatenate([d[a * N_SUB + 2 * h:a * N_SUB + 2 * h + 1] for h in range(N_HEADS)], axis=1)
        o_ref[a:a + 1, :] = rowv.astype(o_ref.dtype)


def _attn_kernel(n_pages, pt_ref, qt_ref, k_ref, vt_ref, lq1_ref, lk1_ref, lq2_ref, lk2_ref, gcol_ref,
                 q_ref, kn_ref, vn_ref, grow_ref, *rest):
    kt_refs = rest[:n_pages]
    v_refs = rest[n_pages:2 * n_pages]
    op_ref, os_ref, m_ref, acc_ref = rest[2 * n_pages:]
    lam = _lam(lq1_ref[...], lk1_ref[...], lq2_ref[...], lk2_ref[...])
    _attn_p_body(pl.program_id(2), qt_ref, k_ref, vt_ref, lam, gcol_ref, op_ref, m_ref, acc_ref)
    _attn_s_body(q_ref, kn_ref, vn_ref, lam, grow_ref, kt_refs, v_refs, os_ref)


def _attn_call(pt_flat, qt, k, vt, lq1, lk1, lq2, lk2, g, q_s, kn_s, vn_s, kt, v2, n_pages):
    B, S, _ = k.shape
    nq_tiles = S // TQ
    nb, nq, _ = q_s.shape
    assert nb == B * N_HEADS * nq_tiles, "one sample sequence per prompt grid step"
    seq = lambda b, h, i: (b * N_HEADS + h) * nq_tiles + i
    small = lambda n: pl.BlockSpec((1, n), lambda b, h, i, pt: (0, 0))

    def page_spec(p):
        return pl.BlockSpec((None, 512, PAGE), lambda b, h, i, pt: (pt[seq(b, h, i) * n_pages + p], 0, 0))

    return pl.pallas_call(
        functools.partial(_attn_kernel, n_pages),
        grid_spec=pltpu.PrefetchScalarGridSpec(
            num_scalar_prefetch=1,
            grid=(B, N_HEADS, nq_tiles),
            in_specs=[pl.BlockSpec((None, V_DIM, TQ), lambda b, h, i, pt: (b, h, i)),
                      pl.BlockSpec((None, S, V_DIM), lambda b, h, i, pt: (b, 0, h)),
                      pl.BlockSpec((None, nq_tiles, V_DIM, TQ), lambda b, h, i, pt: (b, 0, h, 0)),
                      small(HEAD_DIM), small(HEAD_DIM), small(HEAD_DIM), small(HEAD_DIM),
                      pl.BlockSpec((V_DIM, 1), lambda b, h, i, pt: (0, 0)),
                      pl.BlockSpec((None, nq, ATTN_WIDTH), lambda b, h, i, pt: (seq(b, h, i), 0, 0)),
                      pl.BlockSpec((None, 8, ATTN_WIDTH), lambda b, h, i, pt: (seq(b, h, i), 0, 0)),
                      pl.BlockSpec((None, 8, ATTN_WIDTH), lambda b, h, i, pt: (seq(b, h, i), 0, 0)),
                      small(V_DIM)]
                     + [page_spec(p) for p in range(n_pages)] * 2,
            out_specs=[pl.BlockSpec((None, TQ, V_DIM), lambda b, h, i, pt: (b, i, h)),
                       pl.BlockSpec((None, nq, ATTN_WIDTH), lambda b, h, i, pt: (seq(b, h, i), 0, 0))],
            scratch_shapes=[pltpu.VMEM((1, 2 * TQ), f32), pltpu.VMEM((V_DIM + ONES_ROWS, 2 * TQ), f32)],
        ),
        out_shape=[jax.ShapeDtypeStruct((B, S, ATTN_WIDTH), bf16), jax.ShapeDtypeStruct((nb, nq, ATTN_WIDTH), bf16)],
        compiler_params=_cparams(("arbitrary",) * 3),
        name="attn",
    )(pt_flat, qt, k, vt, lq1, lk1, lq2, lk2, g.reshape(V_DIM, 1), q_s, kn_s, vn_s, g,
      *([kt] * n_pages), *([v2] * n_pages))


SLAB = D_MODEL // LANES


def _store_slabs(ref, row0, x):
    n = x.shape[0]
    for a in range(SLAB):
        ref[pl.ds(SLAB * row0 + a, n, stride=SLAB), :] = x[:, a * LANES:(a + 1) * LANES]


def _load_slabs(ref, row0, n):
    return jnp.concatenate([ref[pl.ds(SLAB * row0 + a, n, stride=SLAB), :] for a in range(SLAB)], axis=1)


def _slab_copy(src_ref, src_row, dst_ref, dst_row, sem):
    src = src_ref.at[pl.ds(pl.multiple_of(src_row * SLAB, SLAB), SLAB), :]
    dst = dst_ref.at[pl.ds(pl.multiple_of(dst_row * SLAB, SLAB), SLAB), :]
    return pltpu.make_async_copy(src, dst, sem)


def _post_kernel(x_ref, att_ref, cv_ref, ga_ref, shf_ref, scf_ref, ln_ref, wo_ref, wr_ref, br_ref, cnt_in_ref,
                 ltri_ref, x1_ref, h2_ref, rt_ref, rw_ref, cnt_ref, carry_ref):
    i = pl.program_id(0)

    @pl.when(i == 0)
    def _():
        carry_ref[...] = cnt_in_ref[...]

    mix = (jnp.dot(att_ref[...], wo_ref[0:ATTN_WIDTH, :], preferred_element_type=f32)
           + jnp.dot(cv_ref[...], wo_ref[ATTN_WIDTH:, :], preferred_element_type=f32))
    x1 = x_ref[...] + ga_ref[...] * mix
    x1_ref[...] = x1
    h2 = _rms(x1, ln_ref[...], NORM_EPS) * (1.0 + scf_ref[...]) + shf_ref[...]
    _store_slabs(h2_ref, 0, h2)
    logits = jnp.dot(h2.astype(bf16), wr_ref[...], preferred_element_type=f32) + br_ref[...]
    lane = lax.broadcasted_iota(jnp.int32, logits.shape, 1)
    lane_f = lane.astype(f32)
    big = jnp.float32(1e9)
    neg = -jnp.inf

    def first_max(vals):
        mx = jnp.max(vals, axis=-1, keepdims=True)
        idx = jnp.min(jnp.where(vals == mx, lane_f, big), axis=-1, keepdims=True)
        return mx, idx

    gl = jnp.where((lane >= ROUTE_GRP_LANE) & (lane < ROUTE_GRP_LANE + N_GROUPS), logits, neg)
    gmax, gidx = first_max(gl)
    g_p = 1.0 / jnp.sum(jnp.exp(gl - gmax), axis=-1, keepdims=True)
    lo = (gidx - ROUTE_GRP_LANE) * EXP_PER_GROUP
    el = jnp.where((lane_f >= lo) & (lane_f < lo + EXP_PER_GROUP), logits, neg)
    v1, i1 = first_max(el)
    el2 = jnp.where(lane_f == i1, neg, el)
    v2, i2 = first_max(el2)
    t = jnp.exp(v2 - v1)
    w1 = g_p / (1.0 + t)
    w2 = g_p * t / (1.0 + t)
    oh1 = lane_f == i1
    oh2 = lane_f == i2
    cnt = jnp.where(oh1 | oh2, 1.0, 0.0)
    prefix = jnp.dot(ltri_ref[...], cnt.astype(bf16), preferred_element_type=f32) + carry_ref[...]
    r1 = jnp.sum(jnp.where(oh1, prefix, 0.0), axis=-1, keepdims=True)
    r2 = jnp.sum(jnp.where(oh2, prefix, 0.0), axis=-1, keepdims=True)
    ri = jnp.where(lane == 0, i1, jnp.where(lane == 1, i2, jnp.where(lane == 2, r1, jnp.where(lane == 3, r2, 0.0))))
    rt_ref[...] = ri.T[0:8, :].astype(jnp.int32)
    rw_ref[...] = jnp.where(lane == 0, w1, jnp.where(lane == 1, w2, 0.0))
    new_carry = carry_ref[...] + jnp.sum(cnt, axis=0, keepdims=True)
    carry_ref[...] = new_carry
    cnt_ref[...] = new_carry


def _post_call(x, att, cv, mod, mod_spec, ln_ffn, w_out_bf, w_r_bf, b_r, cnt_in, ltri, name):
    n = x.shape[0]
    row = lambda w: pl.BlockSpec((TM, w), lambda i: (i, 0))
    const = lambda shape: pl.BlockSpec(shape, lambda i: (0, 0))
    return pl.pallas_call(
        _post_kernel,
        grid=(n // TM,),
        in_specs=[row(D_MODEL), row(ATTN_WIDTH), row(CONV_WIDTH), mod_spec(2), mod_spec(3), mod_spec(4),
                  const((1, D_MODEL)), const((D_MODEL, D_MODEL)), const((D_MODEL, LANES)), const((1, LANES)),
                  const((1, LANES)), const((TM, TM))],
        out_specs=[row(D_MODEL), pl.BlockSpec((TM * SLAB, LANES), lambda i: (i, 0)),
                   pl.BlockSpec((8, TM), lambda i: (0, i)), row(LANES), const((1, LANES))],
        out_shape=[jax.ShapeDtypeStruct((n, D_MODEL), f32), jax.ShapeDtypeStruct((n * SLAB, LANES), f32),
                   jax.ShapeDtypeStruct((8, n), jnp.int32), jax.ShapeDtypeStruct((n, LANES), f32),
                   jax.ShapeDtypeStruct((1, LANES), f32)],
        scratch_shapes=[pltpu.VMEM((1, LANES), f32)],
        compiler_params=_cparams(("arbitrary",)),
        name=name,
    )(x, att, cv, mod, mod, mod, ln_ffn, w_out_bf, w_r_bf, b_r, cnt_in, ltri)


def _dispatch_rows(pos0_ref, pos1_ref, src_ref, t0, xs_ref, sem):
    tm = src_ref.shape[0] // SLAB

    def issue(r, c):
        t = t0 + r
        _slab_copy(src_ref, r, xs_ref, pos0_ref[t], sem).start(priority=0)
        _slab_copy(src_ref, r, xs_ref, pos1_ref[t], sem).start(priority=1)
        return c

    lax.fori_loop(0, tm, issue, 0, unroll=ROW_DMA_UNROLL)
    tile_copy = pltpu.make_async_copy(src_ref, xs_ref.at[pl.ds(0, tm * SLAB), :], sem)
    tile_copy.wait()
    tile_copy.wait()


def _dispatch_kernel(n_tiles_p, pos0_ref, pos1_ref, hp_ref, hs_ref, xs_ref, sem):
    i = pl.program_id(0)

    @pl.when(i < n_tiles_p)
    def _():
        _dispatch_rows(pos0_ref, pos1_ref, hp_ref, i * TM, xs_ref, sem)

    @pl.when(i >= n_tiles_p)
    def _():
        _dispatch_rows(pos0_ref, pos1_ref, hs_ref, i * TM, xs_ref, sem)


def _dispatch_call(pos0, pos1, h2_p, h2_s):
    tp = h2_p.shape[0] // (TM * SLAB)
    ts = h2_s.shape[0] // (TM * SLAB)
    return pl.pallas_call(
        functools.partial(_dispatch_kernel, tp),
        grid_spec=pltpu.PrefetchScalarGridSpec(
            num_scalar_prefetch=2,
            grid=(tp + ts,),
            in_specs=[pl.BlockSpec((TM * SLAB, LANES), lambda i, p0, p1: (jnp.minimum(i, tp - 1), 0)),
                      pl.BlockSpec((TM * SLAB, LANES), lambda i, p0, p1: (jnp.maximum(i - tp, 0), 0))],
            out_specs=pl.BlockSpec(memory_space=pl.ANY),
            scratch_shapes=[pltpu.SemaphoreType.DMA(())],
        ),
        out_shape=jax.ShapeDtypeStruct((2 * (tp + ts) * TM * SLAB, LANES), f32),
        compiler_params=_cparams(("arbitrary",)),
        name="dispatch",
    )(pos0, pos1, h2_p, h2_s)


def _experts_kernel(tile_ref, exp_ref, lo_ref, hi_ref, xs_ref, wg_ref, wu_ref, wd_ref, ys_ref,
                    wg_bf, wu_bf, wd_bf, cur_ref):
    w = pl.program_id(0)
    lo = lo_ref[w]
    hi = hi_ref[w]

    @pl.when(w == 0)
    def _():
        cur_ref[0] = -1

    @pl.when(hi > lo)
    def _():
        e = exp_ref[w]

        @pl.when(cur_ref[0] != e)
        def _():
            wg_bf[...] = wg_ref[...].astype(bf16)
            wu_bf[...] = wu_ref[...].astype(bf16)
            wd_bf[...] = wd_ref[...].astype(bf16)
            cur_ref[0] = e

        x = _load_slabs(xs_ref, 0, TE).astype(bf16)
        g = jnp.dot(x, wg_bf[...], preferred_element_type=f32)
        u = jnp.dot(x, wu_bf[...], preferred_element_type=f32)
        hid = (_silu(g) * u).astype(bf16)
        y = jnp.dot(hid, wd_bf[...], preferred_element_type=f32)
        base = tile_ref[w] * TE
        row = base + lax.broadcasted_iota(jnp.int32, y.shape, 0)
        mine = (row >= lo) & (row < hi)

        @pl.when(lo == base)
        def _():
            _store_slabs(ys_ref, 0, jnp.where(mine, y, 0.0))

        @pl.when(lo != base)
        def _():
            _store_slabs(ys_ref, 0, jnp.where(mine, y, _load_slabs(ys_ref, 0, TE)))


def _experts_call(tile_id, exp_id, seg_lo, seg_hi, xs, w_gate, w_up, w_down):
    n_items = tile_id.shape[0]
    return pl.pallas_call(
        _experts_kernel,
        grid_spec=pltpu.PrefetchScalarGridSpec(
            num_scalar_prefetch=4,
            grid=(n_items,),
            in_specs=[pl.BlockSpec((TE * SLAB, LANES), lambda w, t, e, lo, hi: (t[w], 0)),
                      pl.BlockSpec((None, D_MODEL, D_EXPERT), lambda w, t, e, lo, hi: (e[w], 0, 0)),
                      pl.BlockSpec((None, D_MODEL, D_EXPERT), lambda w, t, e, lo, hi: (e[w], 0, 0)),
                      pl.BlockSpec((None, D_EXPERT, D_MODEL), lambda w, t, e, lo, hi: (e[w], 0, 0))],
            out_specs=pl.BlockSpec((TE * SLAB, LANES), lambda w, t, e, lo, hi: (t[w], 0)),
            scratch_shapes=[pltpu.VMEM((D_MODEL, D_EXPERT), bf16), pltpu.VMEM((D_MODEL, D_EXPERT), bf16),
                            pltpu.VMEM((D_EXPERT, D_MODEL), bf16), pltpu.SMEM((1,), jnp.int32)],
        ),
        out_shape=jax.ShapeDtypeStruct(xs.shape, f32),
        compiler_params=_cparams(("arbitrary",)),
        name="experts",
    )(tile_id, exp_id, seg_lo, seg_hi, xs, w_gate, w_up, w_down)


def _combine_kernel(pos0_ref, pos1_ref, x1_ref, rw_ref, gf_ref, ln_ref, ys_ref, o_ref, ybuf, sem):
    i = pl.program_id(0)
    tm = x1_ref.shape[0]

    def issue(r, c):
        t = i * tm + r
        _slab_copy(ys_ref, pos0_ref[t], ybuf, r, sem).start(priority=0)
        _slab_copy(ys_ref, pos1_ref[t], ybuf, tm + r, sem).start(priority=1)
        return c

    lax.fori_loop(0, tm, issue, 0, unroll=ROW_DMA_UNROLL)
    tile_copy = pltpu.make_async_copy(ys_ref.at[pl.ds(0, tm * SLAB), :], ybuf.at[pl.ds(0, tm * SLAB), :], sem)
    tile_copy.wait()
    tile_copy.wait()
    rw = rw_ref[...]
    moe = rw[:, 0:1] * _load_slabs(ybuf, 0, tm) + rw[:, 1:2] * _load_slabs(ybuf, tm, tm)
    x2 = x1_ref[...] + gf_ref[...] * moe
    o_ref[...] = _rms(x2, ln_ref[...], NORM_EPS)


def _combine_call(pos0, pos1, x1, rw, mod, mod_spec, ln_final, ys, name):
    n = x1.shape[0]
    return pl.pallas_call(
        _combine_kernel,
        grid_spec=pltpu.PrefetchScalarGridSpec(
            num_scalar_prefetch=2,
            grid=(n // TM,),
            in_specs=[pl.BlockSpec((TM, D_MODEL), lambda i, p0, p1: (i, 0)),
                      pl.BlockSpec((TM, LANES), lambda i, p0, p1: (i, 0)),
                      mod_spec(5),
                      pl.BlockSpec((1, D_MODEL), lambda i, p0, p1: (0, 0)),
                      pl.BlockSpec(memory_space=pl.ANY)],
            out_specs=pl.BlockSpec((TM, D_MODEL), lambda i, p0, p1: (i, 0)),
            scratch_shapes=[pltpu.VMEM((2 * TM * SLAB, LANES), f32), pltpu.SemaphoreType.DMA(())],
        ),
        out_shape=jax.ShapeDtypeStruct((n, D_MODEL), f32),
        compiler_params=_cparams(("arbitrary",)),
        name=name,
    )(pos0, pos1, x1, rw, mod, ln_final, ys)


def _rope_tables(pos):
    inv = 1.0 / (ROPE_THETA ** (np.arange(0, HEAD_DIM, 2, dtype=np.float64) / HEAD_DIM))
    ang = np.asarray(pos, np.float64)[:, None] * inv[None, :]
    ang = np.concatenate([ang, ang], axis=-1)
    sign = np.where(np.arange(HEAD_DIM) < HEAD_DIM // 2, -1.0, 1.0)
    cos = np.tile(np.cos(ang), (1, N_SUB)).astype(np.float32)
    sin_signed = np.tile(np.sin(ang) * sign[None, :], (1, N_SUB)).astype(np.float32)
    return jnp.asarray(cos), jnp.asarray(sin_signed)


def _segments(counts, n_rows):
    n_tiles = n_rows // TE
    offs = jnp.concatenate([jnp.zeros((1,), jnp.int32), jnp.cumsum(counts)[:-1].astype(jnp.int32)])
    tiles = jnp.arange(n_tiles, dtype=jnp.int32) * TE
    rank_t = jnp.arange(n_tiles, dtype=jnp.int32) + jnp.sum(offs[None, :] < tiles[:, None], axis=1).astype(jnp.int32)
    rank_o = jnp.arange(N_EXPERTS, dtype=jnp.int32) + jnp.minimum(offs // TE + 1, n_tiles)
    vals = jnp.concatenate([tiles, offs])
    ranks = jnp.concatenate([rank_t, rank_o])
    n_items = n_tiles + N_EXPERTS
    w = jnp.arange(n_items, dtype=jnp.int32)
    seg_lo = jnp.sum(jnp.where(ranks[None, :] == w[:, None], vals[None, :], 0), axis=1).astype(jnp.int32)
    seg_hi = jnp.concatenate([seg_lo[1:], jnp.full((1,), n_rows, jnp.int32)])
    tile_id = jnp.minimum(seg_lo // TE, n_tiles - 1)
    exp_id = jnp.sum(offs[None, :] <= seg_lo[:, None], axis=1).astype(jnp.int32) - 1
    return offs, tile_id, exp_id, seg_lo, seg_hi


def kernel(x_prompt, x_sample, cache_k, cache_v, state_conv, page_table, c_prompt, c_sample, w_ada, b_ada, ln_mix, w_in, lam_q1, lam_k1, lam_q2, lam_k2, subln_g, w_conv, w_out, ln_ffn, w_router_grp, b_router_grp, w_router_exp, b_router_exp, w_gate, w_up, w_down, ln_final):
    B, S, _ = x_prompt.shape
    DB, L, _ = x_sample.shape
    n_phys = cache_k.shape[1]
    n_pages = page_table.shape[1]
    past = n_pages * PAGE
    n_p = B * S
    n_s = DB * L
    n_tok = n_p + n_s

    w_in_bf = w_in[0].astype(bf16)
    w_out_bf = w_out[0].astype(bf16)
    w_r = jnp.zeros((D_MODEL, LANES), f32)
    w_r = w_r.at[:, :N_EXPERTS].set(w_router_exp[0]).at[:, ROUTE_GRP_LANE:ROUTE_GRP_LANE + N_GROUPS].set(w_router_grp[0])
    b_r = jnp.zeros((1, LANES), f32)
    b_r = b_r.at[0, :N_EXPERTS].set(b_router_exp[0]).at[0, ROUTE_GRP_LANE:ROUTE_GRP_LANE + N_GROUPS].set(b_router_grp[0])
    w_r_bf = w_r.astype(bf16)
    cos_p, sin_p = _rope_tables(np.arange(S))
    cos_s, sin_s = _rope_tables(past + np.repeat(np.arange(L), DB))
    ltri = jnp.asarray(np.tril(np.ones((TM, TM), np.float32), -1), bf16)

    mod = _mod_call(jnp.concatenate([c_prompt, c_sample], axis=0), w_ada[0], b_ada)
    mod_p = mod[:B].reshape(B, 1, 6 * D_MODEL)
    mod_s = jnp.tile(mod[B:], (L, 1))

    qt_p, kt_p, kb_p, v4_p, vt_p, cv_p, st_p = _inproj_p_call(x_prompt, mod_p, ln_mix, w_in_bf, cos_p, sin_p, w_conv[0])
    xs_l = x_sample.transpose(1, 0, 2).reshape(n_s, D_MODEL)
    st_in = state_conv[0].transpose(1, 0, 2)
    q_s, kf_s, vf_s, cv_s, st_s = _inproj_s_call(xs_l, mod_s[:, 0:D_MODEL], mod_s[:, D_MODEL:2 * D_MODEL], ln_mix,
                                                 w_in_bf, cos_s, sin_s, w_conv[0], st_in)
    to_b = lambda a: a.reshape(L, DB, -1).transpose(1, 0, 2)
    pad8 = lambda a: jnp.pad(a, ((0, 0), (0, 8 - L), (0, 0)))
    kt = jnp.transpose(cache_k[0], (0, 2, 3, 1)).reshape(n_phys, N_SUB * HEAD_DIM, PAGE)
    v2 = cache_v[0].reshape(n_phys, PAGE * N_HEADS, V_DIM)
    att_p, att_s_b = _attn_call(page_table.reshape(-1), qt_p, kb_p, vt_p, lam_q1, lam_k1, lam_q2, lam_k2, subln_g,
                                to_b(q_s), pad8(to_b(kf_s)), pad8(to_b(vf_s)), kt, v2, n_pages)
    att_s = att_s_b.transpose(1, 0, 2).reshape(n_s, ATTN_WIDTH)

    zero_cnt = jnp.zeros((1, LANES), f32)
    spec_p = lambda sec: pl.BlockSpec((None, 1, D_MODEL), lambda i, *_: (i // (S // TM), 0, sec))
    spec_s = lambda sec: pl.BlockSpec((TM, D_MODEL), lambda i, *_: (i, sec))
    x1_p, h2_p, rt_p, rw_p, cnt_p = _post_call(x_prompt.reshape(n_p, D_MODEL), att_p.reshape(n_p, ATTN_WIDTH),
                                               cv_p.reshape(n_p, CONV_WIDTH), mod_p, spec_p, ln_ffn, w_out_bf, w_r_bf,
                                               b_r, zero_cnt, ltri, "post_p")
    x1_s, h2_s, rt_s, rw_s, cnt = _post_call(xs_l, att_s, cv_s, mod_s, spec_s, ln_ffn, w_out_bf, w_r_bf, b_r, cnt_p,
                                             ltri, "post_s")

    counts = cnt[0, :N_EXPERTS].astype(jnp.int32)
    offs, tile_id, exp_id, seg_lo, seg_hi = _segments(counts, 2 * n_tok)
    rt = jnp.concatenate([rt_p, rt_s], axis=1)
    e_col = jnp.arange(N_EXPERTS, dtype=jnp.int32)[:, None]
    start = lambda e_row: jnp.sum(jnp.where(e_row[None, :] == e_col, offs[:, None], 0), axis=0)
    pos0 = start(rt[0]) + rt[2]
    pos1 = start(rt[1]) + rt[3]

    xs_sorted = _dispatch_call(pos0, pos1, h2_p, h2_s)
    ys = _experts_call(tile_id, exp_id, seg_lo, seg_hi, xs_sorted, w_gate[0], w_up[0], w_down[0])

    y_p = _combine_call(pos0[:n_p], pos1[:n_p], x1_p, rw_p, mod_p, spec_p, ln_final.reshape(1, D_MODEL), ys, "combine_p")
    y_s = _combine_call(pos0[n_p:], pos1[n_p:], x1_s, rw_s, mod_s, spec_s, ln_final.reshape(1, D_MODEL), ys, "combine_s")

    from_l = lambda a: a.reshape(L, DB, -1).transpose(1, 0, 2)
    y_prompt = y_p.reshape(B, S, D_MODEL)
    y_sample = from_l(y_s)
    k_prompt = kt_p.reshape(B, N_SUB, HEAD_DIM, S).transpose(0, 3, 1, 2)[None]
    v_prompt = v4_p.reshape(1, B, S, N_HEADS, V_DIM)
    conv_prompt = st_p[None]
    k_sample = from_l(kf_s).reshape(1, DB, L, N_SUB, HEAD_DIM)
    v_sample = from_l(vf_s).reshape(1, DB, L, N_HEADS, V_DIM)
    conv_sample = st_s.transpose(1, 0, 2)[None]
    return (y_prompt, y_sample, k_prompt, v_prompt, conv_prompt, k_sample, v_sample, conv_sample)
```

```python
import functools
import math

import jax
import jax.numpy as jnp
import numpy as np
from jax import lax
from jax.experimental import pallas as pl
from jax.experimental.pallas import tpu as pltpu

D_MODEL = 1024
ATTN_WIDTH = 512
CONV_WIDTH = 512
N_HEADS = 4
N_SUB = 8
HEAD_DIM = 64
V_DIM = 2 * HEAD_DIM
CONV_K = 3
ROPE_THETA = 10000.0
N_GROUPS = 4
EXP_PER_GROUP = 8
N_EXPERTS = 32
D_EXPERT = 256
NORM_EPS = 1e-6
SUBLN_EPS = 1e-5
LAM_INIT = 0.8 - 0.6 * math.exp(-0.3 * 0)
LOG2E = math.log2(math.e)
PAGE = 128
LANES = 128
ROUTE_GRP_LANE = 32

TM = 512
TQ = 512
TE = 256
ROW_DMA_UNROLL = 8
ONES_ROWS = 16
VMEM_LIMIT = 56 * 1024 * 1024

f32 = jnp.float32
bf16 = jnp.bfloat16


def _cparams(sem):
    return pltpu.CompilerParams(dimension_semantics=sem, vmem_limit_bytes=VMEM_LIMIT)


def _rms(x, g, eps):
    return x * lax.rsqrt(jnp.mean(x * x, axis=-1, keepdims=True) + eps) * g


def _silu(x):
    return x * (1.0 / (1.0 + jnp.exp(-x)))


def _mod_kernel(c_ref, w_ref, b_ref, o_ref):
    a = _silu(c_ref[...]).astype(bf16)
    o_ref[...] = jnp.dot(a, w_ref[...].astype(bf16), preferred_element_type=f32) + b_ref[...]


def _mod_call(c_all, w_ada, b_ada):
    n = c_all.shape[0]
    return pl.pallas_call(
        _mod_kernel,
        grid=(6,),
        in_specs=[pl.BlockSpec((n, D_MODEL), lambda j: (0, 0)),
                  pl.BlockSpec((D_MODEL, D_MODEL), lambda j: (0, j)),
                  pl.BlockSpec((1, D_MODEL), lambda j: (0, j))],
        out_specs=pl.BlockSpec((n, D_MODEL), lambda j: (0, j)),
        out_shape=jax.ShapeDtypeStruct((n, 6 * D_MODEL), f32),
        compiler_params=_cparams(("arbitrary",)),
        name="mod",
    )(c_all, w_ada, b_ada)


def _rope(t, cos, sin_signed, lo_mask):
    n = t.shape[-1]
    rot = jnp.where(lo_mask, pltpu.roll(t, n - HEAD_DIM // 2, 1), pltpu.roll(t, HEAD_DIM // 2, 1))
    return t * cos + rot * sin_signed


def _inproj_common(x, sh, sc, ln, w_ref, cos, sin_signed):
    h = (_rms(x, ln, NORM_EPS) * (1.0 + sc) + sh).astype(bf16)

    def sec(i):
        return jnp.dot(h, w_ref[:, i * 512:(i + 1) * 512], preferred_element_type=f32)

    lane = lax.broadcasted_iota(jnp.int32, (x.shape[0], 512), 1)
    lo_mask = (lane % HEAD_DIM) < (HEAD_DIM // 2)
    q = _rope(sec(0), cos, sin_signed, lo_mask) * (HEAD_DIM ** -0.5 * LOG2E)
    k = _rope(sec(1), cos, sin_signed, lo_mask)
    v = sec(2)
    bg = sec(3)
    cu = sec(4) * sec(5)
    return q, k, v, bg, cu


def _inproj_p_kernel(x_ref, sh_ref, sc_ref, ln_ref, w_ref, cos_ref, sin_ref, wc_ref,
                     qt_ref, kt_ref, kb_ref, v4_ref, vt_ref, cv_ref, st_ref, carry_ref):
    s = pl.program_id(0)
    b = pl.program_id(1)
    q, k, v, bg, cu = _inproj_common(x_ref[...], sh_ref[...], sc_ref[...], ln_ref[...], w_ref,
                                     cos_ref[...], sin_ref[...])
    qt_ref[...] = q.T.astype(bf16)
    kt_ref[...] = k.T
    kb_ref[...] = k.astype(bf16)
    vt_ref[...] = v.T.astype(bf16)
    for h in range(N_HEADS):
        v4_ref[pl.ds(h, v.shape[0], stride=N_HEADS), :] = v[:, h * V_DIM:(h + 1) * V_DIM]
    tm = cu.shape[0]
    prev = jnp.where(s > 0, carry_ref[b], 0.0)
    row = lax.broadcasted_iota(jnp.int32, cu.shape, 0)
    cu1 = jnp.where(row == 0, prev[1:2], pltpu.roll(cu, 1, 0))
    cu2 = jnp.where(row == 0, prev[0:1], jnp.where(row == 1, prev[1:2], pltpu.roll(cu, 2, 0)))
    wc = wc_ref[...]
    conv = wc[0:1] * cu2 + wc[1:2] * cu1 + wc[2:3] * cu
    cv_ref[...] = (bg * conv).astype(bf16)
    last2 = cu[tm - 2:tm]
    carry_ref[b, 0:2, :] = last2
    st_ref[b] = last2


def _inproj_p_call(x, mod3, ln_mix, w_in_bf, cos, sin_signed, w_conv):
    B, S, _ = x.shape
    ns = S // TM
    row = lambda s, b: (b, s, 0)
    col = lambda s, b: (b, 0, s)
    rows = lambda dt: jax.ShapeDtypeStruct((B, S, 512), dt)
    cols = lambda dt: jax.ShapeDtypeStruct((B, 512, S), dt)
    row_spec = pl.BlockSpec((None, TM, 512), row)
    col_spec = pl.BlockSpec((None, 512, TM), col)
    return pl.pallas_call(
        _inproj_p_kernel,
        grid=(ns, B),
        in_specs=[pl.BlockSpec((None, TM, D_MODEL), row),
                  pl.BlockSpec((None, 1, D_MODEL), lambda s, b: (b, 0, 0)),
                  pl.BlockSpec((None, 1, D_MODEL), lambda s, b: (b, 0, 1)),
                  pl.BlockSpec((1, D_MODEL), lambda s, b: (0, 0)),
                  pl.BlockSpec((D_MODEL, 3072), lambda s, b: (0, 0)),
                  pl.BlockSpec((TM, 512), lambda s, b: (s, 0)),
                  pl.BlockSpec((TM, 512), lambda s, b: (s, 0)),
                  pl.BlockSpec((CONV_K, CONV_WIDTH), lambda s, b: (0, 0))],
        out_specs=[col_spec, col_spec, row_spec, pl.BlockSpec((None, N_HEADS * TM, V_DIM), row),
                   pl.BlockSpec((None, None, 512, TM), lambda s, b: (b, s, 0, 0)), row_spec,
                   pl.BlockSpec((B, 2, CONV_WIDTH), lambda s, b: (0, 0, 0))],
        out_shape=[cols(bf16), cols(f32), rows(bf16), jax.ShapeDtypeStruct((B, N_HEADS * S, V_DIM), f32),
                   jax.ShapeDtypeStruct((B, ns, 512, TM), bf16), rows(bf16),
                   jax.ShapeDtypeStruct((B, 2, CONV_WIDTH), f32)],
        scratch_shapes=[pltpu.VMEM((B, 8, CONV_WIDTH), f32)],
        compiler_params=_cparams(("arbitrary", "arbitrary")),
        name="inproj_p",
    )(x, mod3, mod3, ln_mix, w_in_bf, cos, sin_signed, w_conv)


def _inproj_s_kernel(x_ref, sh_ref, sc_ref, ln_ref, w_ref, cos_ref, sin_ref, wc_ref, st_in_ref,
                     q_ref, kf_ref, vf_ref, cv_ref, st_ref):
    q, k, v, bg, cu = _inproj_common(x_ref[...], sh_ref[...], sc_ref[...], ln_ref[...], w_ref,
                                     cos_ref[...], sin_ref[...])
    q_ref[...] = q.astype(bf16)
    kf_ref[...] = k
    vf_ref[...] = v
    nb = st_in_ref.shape[1]
    st0 = st_in_ref[0]
    st1 = st_in_ref[1]
    cu1 = jnp.concatenate([st1, cu[:3 * nb]], axis=0)
    cu2 = jnp.concatenate([st0, st1, cu[:2 * nb]], axis=0)
    wc = wc_ref[...]
    conv = wc[0:1] * cu2 + wc[1:2] * cu1 + wc[2:3] * cu
    cv_ref[...] = (bg * conv).astype(bf16)
    st_ref[0] = cu[2 * nb:3 * nb]
    st_ref[1] = cu[3 * nb:4 * nb]


def _inproj_s_call(x, sh, sc, ln_mix, w_in_bf, cos, sin_signed, w_conv, st_in):
    n = x.shape[0]
    nb = st_in.shape[1]
    full = lambda shape: pl.BlockSpec(shape, lambda i: (0,) * len(shape))
    return pl.pallas_call(
        _inproj_s_kernel,
        grid=(1,),
        in_specs=[full((n, D_MODEL)), full((n, D_MODEL)), full((n, D_MODEL)), full((1, D_MODEL)),
                  full((D_MODEL, 3072)), full((n, 512)), full((n, 512)), full((CONV_K, CONV_WIDTH)),
                  full((2, nb, CONV_WIDTH))],
        out_specs=[full((n, 512))] * 4 + [full((2, nb, CONV_WIDTH))],
        out_shape=[jax.ShapeDtypeStruct((n, 512), bf16), jax.ShapeDtypeStruct((n, 512), f32),
                   jax.ShapeDtypeStruct((n, 512), f32), jax.ShapeDtypeStruct((n, 512), bf16),
                   jax.ShapeDtypeStruct((2, nb, CONV_WIDTH), f32)],
        compiler_params=_cparams(("arbitrary",)),
        name="inproj_s",
    )(x, sh, sc, ln_mix, w_in_bf, cos, sin_signed, w_conv, st_in)


def _lam(lq1, lk1, lq2, lk2):
    a = jnp.sum(lq1 * lk1, axis=-1, keepdims=True)
    b = jnp.sum(lq2 * lk2, axis=-1, keepdims=True)
    return jnp.exp(a) - jnp.exp(b) + LAM_INIT


def _attn_p_body(i, qt_ref, k_ref, vt_ref, lam, g_ref, o_ref, m_ref, acc_ref):
    tq = qt_ref.shape[1]
    tk = tq
    m_ref[...] = jnp.full(m_ref.shape, -jnp.inf, f32)
    acc_ref[...] = jnp.zeros(acc_ref.shape, f32)

    def step(j, masked):
        k0 = pl.multiple_of(j * tk, tk)
        qt = qt_ref[...]
        row = lax.broadcasted_iota(jnp.int32, qt.shape, 0)
        zero = jnp.zeros_like(qt)
        q2t = jnp.concatenate([jnp.where(row < HEAD_DIM, qt, zero), jnp.where(row >= HEAD_DIM, qt, zero)], axis=1)
        st = jnp.dot(k_ref[pl.ds(k0, tk), :], q2t, preferred_element_type=f32)
        if masked:
            kpos = lax.broadcasted_iota(jnp.int32, st.shape, 0)
            c = lax.broadcasted_iota(jnp.int32, st.shape, 1)
            st = jnp.where(kpos <= jnp.where(c >= tq, c - tq, c), st, -jnp.inf)
        m_prev = m_ref[...]
        m_new = jnp.maximum(m_prev, jnp.max(st, axis=0, keepdims=True))
        alpha = jnp.exp2(m_prev - m_new)
        pt = jnp.exp2(st - m_new).astype(bf16)
        vt1 = jnp.concatenate([vt_ref[j], jnp.ones((ONES_ROWS, tk), bf16)], axis=0)
        acc_ref[...] = alpha * acc_ref[...] + jnp.dot(vt1, pt, preferred_element_type=f32)
        m_ref[...] = m_new

    def below_diagonal(j, c):
        step(j, False)
        return c

    lax.fori_loop(0, i, below_diagonal, 0)
    step(i, True)
    ot = acc_ref[0:V_DIM, :] / acc_ref[V_DIM:V_DIM + 1, :]
    dt = ot[:, :tq] - lam * ot[:, tq:]
    ms = jnp.mean(dt * dt, axis=0, keepdims=True)
    nt = dt * lax.rsqrt(ms + SUBLN_EPS) * g_ref[...] * (1.0 - LAM_INIT)
    o_ref[...] = nt.T.astype(bf16)


def _attn_s_body(q_ref, kn_ref, vn_ref, lam, g_ref, kt_refs, v_refs, o_ref):
    n_pages = len(kt_refs)
    nq = q_ref.shape[0]
    nr = nq * N_SUB
    qf = q_ref[...].astype(f32)
    sub = lax.broadcasted_iota(jnp.int32, (N_SUB, ATTN_WIDTH), 0)
    col = lax.broadcasted_iota(jnp.int32, (N_SUB, ATTN_WIDTH), 1)
    diag = (col // HEAD_DIM) == sub
    qbd = jnp.concatenate([jnp.where(diag, jnp.broadcast_to(qf[a:a + 1], (N_SUB, ATTN_WIDTH)), 0.0)
                           for a in range(nq)], axis=0).astype(bf16)
    s_past = jnp.concatenate([jnp.dot(qbd, kt_refs[p][...].astype(bf16), preferred_element_type=f32)
                              for p in range(n_pages)], axis=1)
    s_new = lax.dot_general(qbd, kn_ref[...].astype(bf16), (((1,), (1,)), ((), ())),
                            preferred_element_type=f32)
    r = lax.broadcasted_iota(jnp.int32, s_new.shape, 0)
    c = lax.broadcasted_iota(jnp.int32, s_new.shape, 1)
    s_new = jnp.where((c <= r // N_SUB) & (c < nq), s_new, -jnp.inf)
    m = jnp.maximum(jnp.max(s_past, axis=-1, keepdims=True), jnp.max(s_new, axis=-1, keepdims=True))
    p_past = jnp.exp2(s_past - m)
    p_new = jnp.exp2(s_new - m)
    l = jnp.sum(p_past, axis=-1, keepdims=True) + jnp.sum(p_new, axis=-1, keepdims=True)
    p_bf = p_past.astype(bf16)
    vn = vn_ref[...]
    rowhead = (lax.broadcasted_iota(jnp.int32, (nr, V_DIM), 0) % N_SUB) // 2
    o = jnp.zeros((nr, V_DIM), f32)
    for h in range(N_HEADS):
        acc = jnp.zeros((nr, V_DIM), f32)
        for p in range(n_pages):
            vh = v_refs[p][pl.ds(h, PAGE, stride=N_HEADS), :].astype(bf16)
            acc = acc + jnp.dot(p_bf[:, p * PAGE:(p + 1) * PAGE], vh, preferred_element_type=f32)
        for a in range(nq):
            acc = acc + p_new[:, a:a + 1] * vn[a:a + 1, h * V_DIM:(h + 1) * V_DIM]
        o = jnp.where(rowhead == h, acc, o)
    o = o / l
    d = o - lam * pltpu.roll(o, nr - 1, 0)
    d = _rms(d, g_ref[...], SUBLN_EPS) * (1.0 - LAM_INIT)
    for a in range(nq):
        rowv = jnp.concatenate([d[a * N_SUB + 2 * h:a * N_SUB + 2 * h + 1] for h in range(N_HEADS)], axis=1)
        o_ref[a:a + 1, :] = rowv.astype(o_ref.dtype)


def _attn_kernel(n_pages, pt_ref, qt_ref, k_ref, vt_ref, lq1_ref, lk1_ref, lq2_ref, lk2_ref, gcol_ref,
                 q_ref, kn_ref, vn_ref, grow_ref, *rest):
    kt_refs = rest[:n_pages]
    v_refs = rest[n_pages:2 * n_pages]
    op_ref, os_ref, m_ref, acc_ref = rest[2 * n_pages:]
    lam = _lam(lq1_ref[...], lk1_ref[...], lq2_ref[...], lk2_ref[...])
    _attn_p_body(pl.program_id(2), qt_ref, k_ref, vt_ref, lam, gcol_ref, op_ref, m_ref, acc_ref)
    _attn_s_body(q_ref, kn_ref, vn_ref, lam, grow_ref, kt_refs, v_refs, os_ref)


def _attn_call(pt_flat, qt, k, vt, lq1, lk1, lq2, lk2, g, q_s, kn_s, vn_s, kt, v2, n_pages):
    B, S, _ = k.shape
    nq_tiles = S // TQ
    nb, nq, _ = q_s.shape
    assert nb == B * N_HEADS * nq_tiles, "one sample sequence per prompt grid step"
    seq = lambda b, h, i: (b * N_HEADS + h) * nq_tiles + i
    small = lambda n: pl.BlockSpec((1, n), lambda b, h, i, pt: (0, 0))

    def page_spec(p):
        return pl.BlockSpec((None, 512, PAGE), lambda b, h, i, pt: (pt[seq(b, h, i) * n_pages + p], 0, 0))

    return pl.pallas_call(
        functools.partial(_attn_kernel, n_pages),
        grid_spec=pltpu.PrefetchScalarGridSpec(
            num_scalar_prefetch=1,
            grid=(B, N_HEADS, nq_tiles),
            in_specs=[pl.BlockSpec((None, V_DIM, TQ), lambda b, h, i, pt: (b, h, i)),
                      pl.BlockSpec((None, S, V_DIM), lambda b, h, i, pt: (b, 0, h)),
                      pl.BlockSpec((None, nq_tiles, V_DIM, TQ), lambda b, h, i, pt: (b, 0, h, 0)),
                      small(HEAD_DIM), small(HEAD_DIM), small(HEAD_DIM), small(HEAD_DIM),
                      pl.BlockSpec((V_DIM, 1), lambda b, h, i, pt: (0, 0)),
                      pl.BlockSpec((None, nq, ATTN_WIDTH), lambda b, h, i, pt: (seq(b, h, i), 0, 0)),
                      pl.BlockSpec((None, 8, ATTN_WIDTH), lambda b, h, i, pt: (seq(b, h, i), 0, 0)),
                      pl.BlockSpec((None, 8, ATTN_WIDTH), lambda b, h, i, pt: (seq(b, h, i), 0, 0)),
                      small(V_DIM)]
                     + [page_spec(p) for p in range(n_pages)] * 2,
            out_specs=[pl.BlockSpec((None, TQ, V_DIM), lambda b, h, i, pt: (b, i, h)),
                       pl.BlockSpec((None, nq, ATTN_WIDTH), lambda b, h, i, pt: (seq(b, h, i), 0, 0))],
            scratch_shapes=[pltpu.VMEM((1, 2 * TQ), f32), pltpu.VMEM((V_DIM + ONES_ROWS, 2 * TQ), f32)],
        ),
        out_shape=[jax.ShapeDtypeStruct((B, S, ATTN_WIDTH), bf16), jax.ShapeDtypeStruct((nb, nq, ATTN_WIDTH), bf16)],
        compiler_params=_cparams(("arbitrary",) * 3),
        name="attn",
    )(pt_flat, qt, k, vt, lq1, lk1, lq2, lk2, g.reshape(V_DIM, 1), q_s, kn_s, vn_s, g,
      *([kt] * n_pages), *([v2] * n_pages))


SLAB = D_MODEL // LANES


def _store_slabs(ref, row0, x):
    n = x.shape[0]
    for a in range(SLAB):
        ref[pl.ds(SLAB * row0 + a, n, stride=SLAB), :] = x[:, a * LANES:(a + 1) * LANES]


def _load_slabs(ref, row0, n):
    return jnp.concatenate([ref[pl.ds(SLAB * row0 + a, n, stride=SLAB), :] for a in range(SLAB)], axis=1)


def _slab_copy(src_ref, src_row, dst_ref, dst_row, sem):
    src = src_ref.at[pl.ds(pl.multiple_of(src_row * SLAB, SLAB), SLAB), :]
    dst = dst_ref.at[pl.ds(pl.multiple_of(dst_row * SLAB, SLAB), SLAB), :]
    return pltpu.make_async_copy(src, dst, sem)


def _post_tile(x_ref, att_ref, cv_ref, ga_ref, shf_ref, scf_ref, ln_ref, wo_ref, wr_ref, br_ref, ltri_ref,
               x1_ref, h2_ref, rt_ref, rw_ref, cnt_ref, carry_ref):
    mix = (jnp.dot(att_ref[...], wo_ref[0:ATTN_WIDTH, :], preferred_element_type=f32)
           + jnp.dot(cv_ref[...], wo_ref[ATTN_WIDTH:, :], preferred_element_type=f32))
    x1 = x_ref[...] + ga_ref[...] * mix
    x1_ref[...] = x1
    h2 = _rms(x1, ln_ref[...], NORM_EPS) * (1.0 + scf_ref[...]) + shf_ref[...]
    _store_slabs(h2_ref, 0, h2)
    logits = jnp.dot(h2.astype(bf16), wr_ref[...], preferred_element_type=f32) + br_ref[...]
    lane = lax.broadcasted_iota(jnp.int32, logits.shape, 1)
    lane_f = lane.astype(f32)
    big = jnp.float32(1e9)
    neg = -jnp.inf

    def first_max(vals):
        mx = jnp.max(vals, axis=-1, keepdims=True)
        idx = jnp.min(jnp.where(vals == mx, lane_f, big), axis=-1, keepdims=True)
        return mx, idx

    gl = jnp.where((lane >= ROUTE_GRP_LANE) & (lane < ROUTE_GRP_LANE + N_GROUPS), logits, neg)
    gmax, gidx = first_max(gl)
    g_p = 1.0 / jnp.sum(jnp.exp(gl - gmax), axis=-1, keepdims=True)
    lo = (gidx - ROUTE_GRP_LANE) * EXP_PER_GROUP
    el = jnp.where((lane_f >= lo) & (lane_f < lo + EXP_PER_GROUP), logits, neg)
    v1, i1 = first_max(el)
    el2 = jnp.where(lane_f == i1, neg, el)
    v2, i2 = first_max(el2)
    t = jnp.exp(v2 - v1)
    w1 = g_p / (1.0 + t)
    w2 = g_p * t / (1.0 + t)
    oh1 = lane_f == i1
    oh2 = lane_f == i2
    cnt = jnp.where(oh1 | oh2, 1.0, 0.0)
    prefix = jnp.dot(ltri_ref[...], cnt.astype(bf16), preferred_element_type=f32) + carry_ref[...]
    r1 = jnp.sum(jnp.where(oh1, prefix, 0.0), axis=-1, keepdims=True)
    r2 = jnp.sum(jnp.where(oh2, prefix, 0.0), axis=-1, keepdims=True)
    ri = jnp.where(lane == 0, i1, jnp.where(lane == 1, i2, jnp.where(lane == 2, r1, jnp.where(lane == 3, r2, 0.0))))
    rt_ref[...] = ri.T[0:8, :].astype(jnp.int32)
    rw_ref[...] = jnp.where(lane == 0, w1, jnp.where(lane == 1, w2, 0.0))
    new_carry = carry_ref[...] + jnp.sum(cnt, axis=0, keepdims=True)
    carry_ref[...] = new_carry
    cnt_ref[...] = new_carry


def _post_kernel(n_tiles_p, xp_ref, attp_ref, cvp_ref, gap_ref, shfp_ref, scfp_ref,
                 xs_ref, atts_ref, cvs_ref, gas_ref, shfs_ref, scfs_ref, *rest):
    i = pl.program_id(0)
    carry_ref = rest[-1]

    @pl.when(i == 0)
    def _():
        carry_ref[...] = jnp.zeros(carry_ref.shape, f32)

    @pl.when(i < n_tiles_p)
    def _():
        _post_tile(xp_ref, attp_ref, cvp_ref, gap_ref, shfp_ref, scfp_ref, *rest)

    @pl.when(i >= n_tiles_p)
    def _():
        _post_tile(xs_ref, atts_ref, cvs_ref, gas_ref, shfs_ref, scfs_ref, *rest)


def _post_call(x_p, att_p, cv_p, mod_p, x_s, att_s, cv_s, mod_s, tiles_per_batch, ln_ffn, w_out_bf, w_r_bf, b_r, ltri):
    n_p, n_s = x_p.shape[0], x_s.shape[0]
    tp, ts = n_p // TM, n_s // TM
    n = n_p + n_s
    prow = lambda w: pl.BlockSpec((TM, w), lambda i: (jnp.minimum(i, tp - 1), 0))
    srow = lambda w: pl.BlockSpec((TM, w), lambda i: (jnp.maximum(i - tp, 0), 0))
    pmod = lambda sec: pl.BlockSpec((None, 1, D_MODEL), lambda i: (jnp.minimum(i, tp - 1) // tiles_per_batch, 0, sec))
    smod = lambda sec: pl.BlockSpec((TM, D_MODEL), lambda i: (jnp.maximum(i - tp, 0), sec))
    row = lambda w: pl.BlockSpec((TM, w), lambda i: (i, 0))
    const = lambda shape: pl.BlockSpec(shape, lambda i: (0, 0))
    return pl.pallas_call(
        functools.partial(_post_kernel, tp),
        grid=(tp + ts,),
        in_specs=[prow(D_MODEL), prow(ATTN_WIDTH), prow(CONV_WIDTH), pmod(2), pmod(3), pmod(4),
                  srow(D_MODEL), srow(ATTN_WIDTH), srow(CONV_WIDTH), smod(2), smod(3), smod(4),
                  const((1, D_MODEL)), const((D_MODEL, D_MODEL)), const((D_MODEL, LANES)), const((1, LANES)),
                  const((TM, TM))],
        out_specs=[row(D_MODEL), pl.BlockSpec((TM * SLAB, LANES), lambda i: (i, 0)),
                   pl.BlockSpec((8, TM), lambda i: (0, i)), row(LANES), const((1, LANES))],
        out_shape=[jax.ShapeDtypeStruct((n, D_MODEL), f32), jax.ShapeDtypeStruct((n * SLAB, LANES), f32),
                   jax.ShapeDtypeStruct((8, n), jnp.int32), jax.ShapeDtypeStruct((n, LANES), f32),
                   jax.ShapeDtypeStruct((1, LANES), f32)],
        scratch_shapes=[pltpu.VMEM((1, LANES), f32)],
        compiler_params=_cparams(("arbitrary",)),
        name="post",
    )(x_p, att_p, cv_p, mod_p, mod_p, mod_p, x_s, att_s, cv_s, mod_s, mod_s, mod_s,
      ln_ffn, w_out_bf, w_r_bf, b_r, ltri)


def _dispatch_kernel(pos0_ref, pos1_ref, src_ref, xs_ref, sem):
    t0 = pl.program_id(0) * TM

    def issue(r, c):
        t = t0 + r
        _slab_copy(src_ref, r, xs_ref, pos0_ref[t], sem).start(priority=0)
        _slab_copy(src_ref, r, xs_ref, pos1_ref[t], sem).start(priority=1)
        return c

    lax.fori_loop(0, TM, issue, 0, unroll=ROW_DMA_UNROLL)
    tile_copy = pltpu.make_async_copy(src_ref, xs_ref.at[pl.ds(0, TM * SLAB), :], sem)
    tile_copy.wait()
    tile_copy.wait()


def _dispatch_call(pos0, pos1, h2):
    tiles = h2.shape[0] // (TM * SLAB)
    return pl.pallas_call(
        _dispatch_kernel,
        grid_spec=pltpu.PrefetchScalarGridSpec(
            num_scalar_prefetch=2,
            grid=(tiles,),
            in_specs=[pl.BlockSpec((TM * SLAB, LANES), lambda i, p0, p1: (i, 0))],
            out_specs=pl.BlockSpec(memory_space=pl.ANY),
            scratch_shapes=[pltpu.SemaphoreType.DMA(())],
        ),
        out_shape=jax.ShapeDtypeStruct((2 * h2.shape[0], LANES), f32),
        compiler_params=_cparams(("arbitrary",)),
        name="dispatch",
    )(pos0, pos1, h2)


def _experts_kernel(tile_ref, exp_ref, lo_ref, hi_ref, xs_ref, wg_ref, wu_ref, wd_ref, ys_ref,
                    wg_bf, wu_bf, wd_bf, cur_ref):
    w = pl.program_id(0)
    lo = lo_ref[w]
    hi = hi_ref[w]

    @pl.when(w == 0)
    def _():
        cur_ref[0] = -1

    @pl.when(hi > lo)
    def _():
        e = exp_ref[w]

        @pl.when(cur_ref[0] != e)
        def _():
            wg_bf[...] = wg_ref[...].astype(bf16)
            wu_bf[...] = wu_ref[...].astype(bf16)
            wd_bf[...] = wd_ref[...].astype(bf16)
            cur_ref[0] = e

        x = _load_slabs(xs_ref, 0, TE).astype(bf16)
        g = jnp.dot(x, wg_bf[...], preferred_element_type=f32)
        u = jnp.dot(x, wu_bf[...], preferred_element_type=f32)
        hid = (_silu(g) * u).astype(bf16)
        y = jnp.dot(hid, wd_bf[...], preferred_element_type=f32)
        base = tile_ref[w] * TE
        row = base + lax.broadcasted_iota(jnp.int32, y.shape, 0)
        mine = (row >= lo) & (row < hi)

        @pl.when(lo == base)
        def _():
            _store_slabs(ys_ref, 0, jnp.where(mine, y, 0.0))

        @pl.when(lo != base)
        def _():
            _store_slabs(ys_ref, 0, jnp.where(mine, y, _load_slabs(ys_ref, 0, TE)))


def _experts_call(tile_id, exp_id, seg_lo, seg_hi, xs, w_gate, w_up, w_down):
    n_items = tile_id.shape[0]
    return pl.pallas_call(
        _experts_kernel,
        grid_spec=pltpu.PrefetchScalarGridSpec(
            num_scalar_prefetch=4,
            grid=(n_items,),
            in_specs=[pl.BlockSpec((TE * SLAB, LANES), lambda w, t, e, lo, hi: (t[w], 0)),
                      pl.BlockSpec((None, D_MODEL, D_EXPERT), lambda w, t, e, lo, hi: (e[w], 0, 0)),
                      pl.BlockSpec((None, D_MODEL, D_EXPERT), lambda w, t, e, lo, hi: (e[w], 0, 0)),
                      pl.BlockSpec((None, D_EXPERT, D_MODEL), lambda w, t, e, lo, hi: (e[w], 0, 0))],
            out_specs=pl.BlockSpec((TE * SLAB, LANES), lambda w, t, e, lo, hi: (t[w], 0)),
            scratch_shapes=[pltpu.VMEM((D_MODEL, D_EXPERT), bf16), pltpu.VMEM((D_MODEL, D_EXPERT), bf16),
                            pltpu.VMEM((D_EXPERT, D_MODEL), bf16), pltpu.SMEM((1,), jnp.int32)],
        ),
        out_shape=jax.ShapeDtypeStruct(xs.shape, f32),
        compiler_params=_cparams(("arbitrary",)),
        name="experts",
    )(tile_id, exp_id, seg_lo, seg_hi, xs, w_gate, w_up, w_down)


def _combine_kernel(n_tiles_p, pos0_ref, pos1_ref, x1_ref, rw_ref, gfp_ref, gfs_ref, ln_ref, ys_ref,
                    op_ref, os_ref, ybuf, sems):
    i = pl.program_id(0)
    n = pl.num_programs(0)
    half = 2 * TM

    def issue(tile, buf):
        def body(r, c):
            t = tile * TM + r
            _slab_copy(ys_ref, pos0_ref[t], ybuf, buf * half + r, sems.at[buf]).start(priority=0)
            _slab_copy(ys_ref, pos1_ref[t], ybuf, buf * half + TM + r, sems.at[buf]).start(priority=1)
            return c

        lax.fori_loop(0, TM, body, 0, unroll=ROW_DMA_UNROLL)

    @pl.when(i == 0)
    def _():
        issue(0, 0)

    @pl.when(i + 1 < n)
    def _():
        issue(i + 1, (i + 1) % 2)

    buf = i % 2
    tile_copy = pltpu.make_async_copy(ys_ref.at[pl.ds(0, TM * SLAB), :], ybuf.at[pl.ds(0, TM * SLAB), :], sems.at[buf])
    tile_copy.wait()
    tile_copy.wait()
    rw = rw_ref[...]
    moe = rw[:, 0:1] * _load_slabs(ybuf, buf * half, TM) + rw[:, 1:2] * _load_slabs(ybuf, buf * half + TM, TM)

    def finish(gf_ref, o_ref):
        x2 = x1_ref[...] + gf_ref[...] * moe
        o_ref[...] = _rms(x2, ln_ref[...], NORM_EPS)

    @pl.when(i < n_tiles_p)
    def _():
        finish(gfp_ref, op_ref)

    @pl.when(i >= n_tiles_p)
    def _():
        finish(gfs_ref, os_ref)


def _combine_call(pos0, pos1, x1, rw, mod_p, mod_s, tiles_per_batch, ln_final, ys):
    n = x1.shape[0]
    n_s = mod_s.shape[0]
    tp, ts = (n - n_s) // TM, n_s // TM
    pidx = lambda i: jnp.minimum(i, tp - 1)
    sidx = lambda i: jnp.maximum(i - tp, 0)
    return pl.pallas_call(
        functools.partial(_combine_kernel, tp),
        grid_spec=pltpu.PrefetchScalarGridSpec(
            num_scalar_prefetch=2,
            grid=(tp + ts,),
            in_specs=[pl.BlockSpec((TM, D_MODEL), lambda i, p0, p1: (i, 0)),
                      pl.BlockSpec((TM, LANES), lambda i, p0, p1: (i, 0)),
                      pl.BlockSpec((None, 1, D_MODEL), lambda i, p0, p1: (pidx(i) // tiles_per_batch, 0, 5)),
                      pl.BlockSpec((TM, D_MODEL), lambda i, p0, p1: (sidx(i), 5)),
                      pl.BlockSpec((1, D_MODEL), lambda i, p0, p1: (0, 0)),
                      pl.BlockSpec(memory_space=pl.ANY)],
            out_specs=[pl.BlockSpec((TM, D_MODEL), lambda i, p0, p1: (pidx(i), 0)),
                       pl.BlockSpec((TM, D_MODEL), lambda i, p0, p1: (sidx(i), 0))],
            scratch_shapes=[pltpu.VMEM((2 * 2 * TM * SLAB, LANES), f32), pltpu.SemaphoreType.DMA((2,))],
        ),
        out_shape=[jax.ShapeDtypeStruct((n - n_s, D_MODEL), f32), jax.ShapeDtypeStruct((n_s, D_MODEL), f32)],
        compiler_params=_cparams(("arbitrary",)),
        name="combine",
    )(pos0, pos1, x1, rw, mod_p, mod_s, ln_final, ys)


def _rope_tables(pos):
    inv = 1.0 / (ROPE_THETA ** (np.arange(0, HEAD_DIM, 2, dtype=np.float64) / HEAD_DIM))
    ang = np.asarray(pos, np.float64)[:, None] * inv[None, :]
    ang = np.concatenate([ang, ang], axis=-1)
    sign = np.where(np.arange(HEAD_DIM) < HEAD_DIM // 2, -1.0, 1.0)
    cos = np.tile(np.cos(ang), (1, N_SUB)).astype(np.float32)
    sin_signed = np.tile(np.sin(ang) * sign[None, :], (1, N_SUB)).astype(np.float32)
    return jnp.asarray(cos), jnp.asarray(sin_signed)


def _segments(counts, n_rows):
    n_tiles = n_rows // TE
    offs = jnp.concatenate([jnp.zeros((1,), jnp.int32), jnp.cumsum(counts)[:-1].astype(jnp.int32)])
    tiles = jnp.arange(n_tiles, dtype=jnp.int32) * TE
    rank_t = jnp.arange(n_tiles, dtype=jnp.int32) + jnp.sum(offs[None, :] < tiles[:, None], axis=1).astype(jnp.int32)
    rank_o = jnp.arange(N_EXPERTS, dtype=jnp.int32) + jnp.minimum(offs // TE + 1, n_tiles)
    vals = jnp.concatenate([tiles, offs])
    ranks = jnp.concatenate([rank_t, rank_o])
    n_items = n_tiles + N_EXPERTS
    w = jnp.arange(n_items, dtype=jnp.int32)
    seg_lo = jnp.sum(jnp.where(ranks[None, :] == w[:, None], vals[None, :], 0), axis=1).astype(jnp.int32)
    seg_hi = jnp.concatenate([seg_lo[1:], jnp.full((1,), n_rows, jnp.int32)])
    tile_id = jnp.minimum(seg_lo // TE, n_tiles - 1)
    exp_id = jnp.sum(offs[None, :] <= seg_lo[:, None], axis=1).astype(jnp.int32) - 1
    return offs, tile_id, exp_id, seg_lo, seg_hi


def kernel(x_prompt, x_sample, cache_k, cache_v, state_conv, page_table, c_prompt, c_sample, w_ada, b_ada, ln_mix, w_in, lam_q1, lam_k1, lam_q2, lam_k2, subln_g, w_conv, w_out, ln_ffn, w_router_grp, b_router_grp, w_router_exp, b_router_exp, w_gate, w_up, w_down, ln_final):
    B, S, _ = x_prompt.shape
    DB, L, _ = x_sample.shape
    n_phys = cache_k.shape[1]
    n_pages = page_table.shape[1]
    past = n_pages * PAGE
    n_p = B * S
    n_s = DB * L
    n_tok = n_p + n_s

    w_in_bf = w_in[0].astype(bf16)
    w_out_bf = w_out[0].astype(bf16)
    w_r = jnp.zeros((D_MODEL, LANES), f32)
    w_r = w_r.at[:, :N_EXPERTS].set(w_router_exp[0]).at[:, ROUTE_GRP_LANE:ROUTE_GRP_LANE + N_GROUPS].set(w_router_grp[0])
    b_r = jnp.zeros((1, LANES), f32)
    b_r = b_r.at[0, :N_EXPERTS].set(b_router_exp[0]).at[0, ROUTE_GRP_LANE:ROUTE_GRP_LANE + N_GROUPS].set(b_router_grp[0])
    w_r_bf = w_r.astype(bf16)
    cos_p, sin_p = _rope_tables(np.arange(S))
    cos_s, sin_s = _rope_tables(past + np.repeat(np.arange(L), DB))
    ltri = jnp.asarray(np.tril(np.ones((TM, TM), np.float32), -1), bf16)

    mod = _mod_call(jnp.concatenate([c_prompt, c_sample], axis=0), w_ada[0], b_ada)
    mod_p = mod[:B].reshape(B, 1, 6 * D_MODEL)
    mod_s = jnp.tile(mod[B:], (L, 1))

    qt_p, kt_p, kb_p, v4_p, vt_p, cv_p, st_p = _inproj_p_call(x_prompt, mod_p, ln_mix, w_in_bf, cos_p, sin_p, w_conv[0])
    xs_l = x_sample.transpose(1, 0, 2).reshape(n_s, D_MODEL)
    st_in = state_conv[0].transpose(1, 0, 2)
    q_s, kf_s, vf_s, cv_s, st_s = _inproj_s_call(xs_l, mod_s[:, 0:D_MODEL], mod_s[:, D_MODEL:2 * D_MODEL], ln_mix,
                                                 w_in_bf, cos_s, sin_s, w_conv[0], st_in)
    to_b = lambda a: a.reshape(L, DB, -1).transpose(1, 0, 2)
    pad8 = lambda a: jnp.pad(a, ((0, 0), (0, 8 - L), (0, 0)))
    kt = jnp.transpose(cache_k[0], (0, 2, 3, 1)).reshape(n_phys, N_SUB * HEAD_DIM, PAGE)
    v2 = cache_v[0].reshape(n_phys, PAGE * N_HEADS, V_DIM)
    att_p, att_s_b = _attn_call(page_table.reshape(-1), qt_p, kb_p, vt_p, lam_q1, lam_k1, lam_q2, lam_k2, subln_g,
                                to_b(q_s), pad8(to_b(kf_s)), pad8(to_b(vf_s)), kt, v2, n_pages)
    att_s = att_s_b.transpose(1, 0, 2).reshape(n_s, ATTN_WIDTH)

    x1, h2, rt, rw, cnt = _post_call(x_prompt.reshape(n_p, D_MODEL), att_p.reshape(n_p, ATTN_WIDTH),
                                     cv_p.reshape(n_p, CONV_WIDTH), mod_p, xs_l, att_s, cv_s, mod_s, S // TM,
                                     ln_ffn, w_out_bf, w_r_bf, b_r, ltri)

    counts = cnt[0, :N_EXPERTS].astype(jnp.int32)
    offs, tile_id, exp_id, seg_lo, seg_hi = _segments(counts, 2 * n_tok)
    e_col = jnp.arange(N_EXPERTS, dtype=jnp.int32)[:, None]
    start = lambda e_row: jnp.sum(jnp.where(e_row[None, :] == e_col, offs[:, None], 0), axis=0)
    pos0 = start(rt[0]) + rt[2]
    pos1 = start(rt[1]) + rt[3]

    xs_sorted = _dispatch_call(pos0, pos1, h2)
    ys = _experts_call(tile_id, exp_id, seg_lo, seg_hi, xs_sorted, w_gate[0], w_up[0], w_down[0])
    y_p, y_s = _combine_call(pos0, pos1, x1, rw, mod_p, mod_s, S // TM, ln_final.reshape(1, D_MODEL), ys)

    from_l = lambda a: a.reshape(L, DB, -1).transpose(1, 0, 2)
    y_prompt = y_p.reshape(B, S, D_MODEL)
    y_sample = from_l(y_s)
    k_prompt = kt_p.reshape(B, N_SUB, HEAD_DIM, S).transpose(0, 3, 1, 2)[None]
    v_prompt = v4_p.reshape(1, B, S, N_HEADS, V_DIM)
    conv_prompt = st_p[None]
    k_sample = from_l(kf_s).reshape(1, DB, L, N_SUB, HEAD_DIM)
    v_sample = from_l(vf_s).reshape(1, DB, L, N_HEADS, V_DIM)
    conv_sample = st_s.transpose(1, 0, 2)[None]
    return (y_prompt, y_sample, k_prompt, v_prompt, conv_prompt, k_sample, v_sample, conv_sample)
```

```python
import functools
import math

import jax
import jax.numpy as jnp
import numpy as np
from jax import lax
from jax.experimental import pallas as pl
from jax.experimental.pallas import tpu as pltpu

D_MODEL = 1024
ATTN_WIDTH = 512
CONV_WIDTH = 512
N_HEADS = 4
N_SUB = 8
HEAD_DIM = 64
V_DIM = 2 * HEAD_DIM
CONV_K = 3
ROPE_THETA = 10000.0
N_GROUPS = 4
EXP_PER_GROUP = 8
N_EXPERTS = 32
D_EXPERT = 256
NORM_EPS = 1e-6
SUBLN_EPS = 1e-5
LAM_INIT = 0.8 - 0.6 * math.exp(-0.3 * 0)
LOG2E = math.log2(math.e)
PAGE = 128
LANES = 128
ROUTE_GRP_LANE = 32

TM = 512
TQ = 512
TE = 256
ROW_DMA_UNROLL = 8
ONES_ROWS = 16
VMEM_LIMIT = 56 * 1024 * 1024

f32 = jnp.float32
bf16 = jnp.bfloat16


def _cparams(sem):
    return pltpu.CompilerParams(dimension_semantics=sem, vmem_limit_bytes=VMEM_LIMIT)


def _rms(x, g, eps):
    return x * lax.rsqrt(jnp.mean(x * x, axis=-1, keepdims=True) + eps) * g


def _silu(x):
    return x * (1.0 / (1.0 + jnp.exp(-x)))


def _mod_kernel(c_ref, w_ref, b_ref, o_ref):
    a = _silu(c_ref[...]).astype(bf16)
    o_ref[...] = jnp.dot(a, w_ref[...].astype(bf16), preferred_element_type=f32) + b_ref[...]


def _mod_call(c_all, w_ada, b_ada):
    n = c_all.shape[0]
    return pl.pallas_call(
        _mod_kernel,
        grid=(6,),
        in_specs=[pl.BlockSpec((n, D_MODEL), lambda j: (0, 0)),
                  pl.BlockSpec((D_MODEL, D_MODEL), lambda j: (0, j)),
                  pl.BlockSpec((1, D_MODEL), lambda j: (0, j))],
        out_specs=pl.BlockSpec((n, D_MODEL), lambda j: (0, j)),
        out_shape=jax.ShapeDtypeStruct((n, 6 * D_MODEL), f32),
        compiler_params=_cparams(("arbitrary",)),
        name="mod",
    )(c_all, w_ada, b_ada)


def _rope(t, cos, sin_signed, lo_mask):
    n = t.shape[-1]
    rot = jnp.where(lo_mask, pltpu.roll(t, n - HEAD_DIM // 2, 1), pltpu.roll(t, HEAD_DIM // 2, 1))
    return t * cos + rot * sin_signed


def _inproj_common(x, sh, sc, ln, w_ref, cos, sin_signed):
    h = (_rms(x, ln, NORM_EPS) * (1.0 + sc) + sh).astype(bf16)

    def sec(i):
        return jnp.dot(h, w_ref[:, i * 512:(i + 1) * 512], preferred_element_type=f32)

    lane = lax.broadcasted_iota(jnp.int32, (x.shape[0], 512), 1)
    lo_mask = (lane % HEAD_DIM) < (HEAD_DIM // 2)
    q = _rope(sec(0), cos, sin_signed, lo_mask) * (HEAD_DIM ** -0.5 * LOG2E)
    k = _rope(sec(1), cos, sin_signed, lo_mask)
    v = sec(2)
    bg = sec(3)
    cu = sec(4) * sec(5)
    return q, k, v, bg, cu


def _inproj_p_kernel(x_ref, sh_ref, sc_ref, ln_ref, w_ref, cos_ref, sin_ref, wc_ref,
                     qt_ref, kt_ref, kb_ref, v4_ref, vt_ref, cv_ref, st_ref, carry_ref):
    s = pl.program_id(0)
    b = pl.program_id(1)
    q, k, v, bg, cu = _inproj_common(x_ref[...], sh_ref[...], sc_ref[...], ln_ref[...], w_ref,
                                     cos_ref[...], sin_ref[...])
    qt_ref[...] = q.T.astype(bf16)
    kt_ref[...] = k.T
    kb_ref[...] = k.astype(bf16)
    vt_ref[...] = v.T.astype(bf16)
    for h in range(N_HEADS):
        v4_ref[pl.ds(h, v.shape[0], stride=N_HEADS), :] = v[:, h * V_DIM:(h + 1) * V_DIM]
    tm = cu.shape[0]
    prev = jnp.where(s > 0, carry_ref[b], 0.0)
    row = lax.broadcasted_iota(jnp.int32, cu.shape, 0)
    cu1 = jnp.where(row == 0, prev[1:2], pltpu.roll(cu, 1, 0))
    cu2 = jnp.where(row == 0, prev[0:1], jnp.where(row == 1, prev[1:2], pltpu.roll(cu, 2, 0)))
    wc = wc_ref[...]
    conv = wc[0:1] * cu2 + wc[1:2] * cu1 + wc[2:3] * cu
    cv_ref[...] = (bg * conv).astype(bf16)
    last2 = cu[tm - 2:tm]
    carry_ref[b, 0:2, :] = last2
    st_ref[b] = last2


def _inproj_p_call(x, mod3, ln_mix, w_in_bf, cos, sin_signed, w_conv):
    B, S, _ = x.shape
    ns = S // TM
    row = lambda s, b: (b, s, 0)
    col = lambda s, b: (b, 0, s)
    rows = lambda dt: jax.ShapeDtypeStruct((B, S, 512), dt)
    cols = lambda dt: jax.ShapeDtypeStruct((B, 512, S), dt)
    row_spec = pl.BlockSpec((None, TM, 512), row)
    col_spec = pl.BlockSpec((None, 512, TM), col)
    return pl.pallas_call(
        _inproj_p_kernel,
        grid=(ns, B),
        in_specs=[pl.BlockSpec((None, TM, D_MODEL), row),
                  pl.BlockSpec((None, 1, D_MODEL), lambda s, b: (b, 0, 0)),
                  pl.BlockSpec((None, 1, D_MODEL), lambda s, b: (b, 0, 1)),
                  pl.BlockSpec((1, D_MODEL), lambda s, b: (0, 0)),
                  pl.BlockSpec((D_MODEL, 3072), lambda s, b: (0, 0)),
                  pl.BlockSpec((TM, 512), lambda s, b: (s, 0)),
                  pl.BlockSpec((TM, 512), lambda s, b: (s, 0)),
                  pl.BlockSpec((CONV_K, CONV_WIDTH), lambda s, b: (0, 0))],
        out_specs=[col_spec, col_spec, row_spec, pl.BlockSpec((None, N_HEADS * TM, V_DIM), row),
                   pl.BlockSpec((None, None, 512, TM), lambda s, b: (b, s, 0, 0)), row_spec,
                   pl.BlockSpec((B, 2, CONV_WIDTH), lambda s, b: (0, 0, 0))],
        out_shape=[cols(bf16), cols(f32), rows(bf16), jax.ShapeDtypeStruct((B, N_HEADS * S, V_DIM), f32),
                   jax.ShapeDtypeStruct((B, ns, 512, TM), bf16), rows(bf16),
                   jax.ShapeDtypeStruct((B, 2, CONV_WIDTH), f32)],
        scratch_shapes=[pltpu.VMEM((B, 8, CONV_WIDTH), f32)],
        compiler_params=_cparams(("arbitrary", "arbitrary")),
        name="inproj_p",
    )(x, mod3, mod3, ln_mix, w_in_bf, cos, sin_signed, w_conv)


def _inproj_s_kernel(x_ref, sh_ref, sc_ref, ln_ref, w_ref, cos_ref, sin_ref, wc_ref, st_in_ref,
                     q_ref, kf_ref, vf_ref, cv_ref, st_ref):
    q, k, v, bg, cu = _inproj_common(x_ref[...], sh_ref[...], sc_ref[...], ln_ref[...], w_ref,
                                     cos_ref[...], sin_ref[...])
    q_ref[...] = q.astype(bf16)
    kf_ref[...] = k
    vf_ref[...] = v
    nb = st_in_ref.shape[1]
    st0 = st_in_ref[0]
    st1 = st_in_ref[1]
    cu1 = jnp.concatenate([st1, cu[:3 * nb]], axis=0)
    cu2 = jnp.concatenate([st0, st1, cu[:2 * nb]], axis=0)
    wc = wc_ref[...]
    conv = wc[0:1] * cu2 + wc[1:2] * cu1 + wc[2:3] * cu
    cv_ref[...] = (bg * conv).astype(bf16)
    st_ref[0] = cu[2 * nb:3 * nb]
    st_ref[1] = cu[3 * nb:4 * nb]


def _inproj_s_call(x, sh, sc, ln_mix, w_in_bf, cos, sin_signed, w_conv, st_in):
    n = x.shape[0]
    nb = st_in.shape[1]
    full = lambda shape: pl.BlockSpec(shape, lambda i: (0,) * len(shape))
    return pl.pallas_call(
        _inproj_s_kernel,
        grid=(1,),
        in_specs=[full((n, D_MODEL)), full((n, D_MODEL)), full((n, D_MODEL)), full((1, D_MODEL)),
                  full((D_MODEL, 3072)), full((n, 512)), full((n, 512)), full((CONV_K, CONV_WIDTH)),
                  full((2, nb, CONV_WIDTH))],
        out_specs=[full((n, 512))] * 4 + [full((2, nb, CONV_WIDTH))],
        out_shape=[jax.ShapeDtypeStruct((n, 512), bf16), jax.ShapeDtypeStruct((n, 512), f32),
                   jax.ShapeDtypeStruct((n, 512), f32), jax.ShapeDtypeStruct((n, 512), bf16),
                   jax.ShapeDtypeStruct((2, nb, CONV_WIDTH), f32)],
        compiler_params=_cparams(("arbitrary",)),
        name="inproj_s",
    )(x, sh, sc, ln_mix, w_in_bf, cos, sin_signed, w_conv, st_in)


def _lam(lq1, lk1, lq2, lk2):
    a = jnp.sum(lq1 * lk1, axis=-1, keepdims=True)
    b = jnp.sum(lq2 * lk2, axis=-1, keepdims=True)
    return jnp.exp(a) - jnp.exp(b) + LAM_INIT


def _attn_p_body(i, qt_ref, k_ref, vt_ref, lam, g_ref, o_ref, m_ref, acc_ref):
    tq = qt_ref.shape[1]
    tk = tq
    m_ref[...] = jnp.full(m_ref.shape, -jnp.inf, f32)
    acc_ref[...] = jnp.zeros(acc_ref.shape, f32)

    def step(j, masked):
        k0 = pl.multiple_of(j * tk, tk)
        qt = qt_ref[...]
        row = lax.broadcasted_iota(jnp.int32, qt.shape, 0)
        zero = jnp.zeros_like(qt)
        q2t = jnp.concatenate([jnp.where(row < HEAD_DIM, qt, zero), jnp.where(row >= HEAD_DIM, qt, zero)], axis=1)
        st = jnp.dot(k_ref[pl.ds(k0, tk), :], q2t, preferred_element_type=f32)
        if masked:
            kpos = lax.broadcasted_iota(jnp.int32, st.shape, 0)
            c = lax.broadcasted_iota(jnp.int32, st.shape, 1)
            st = jnp.where(kpos <= jnp.where(c >= tq, c - tq, c), st, -jnp.inf)
        m_prev = m_ref[...]
        m_new = jnp.maximum(m_prev, jnp.max(st, axis=0, keepdims=True))
        alpha = jnp.exp2(m_prev - m_new)
        pt = jnp.exp2(st - m_new).astype(bf16)
        vt1 = jnp.concatenate([vt_ref[j], jnp.ones((ONES_ROWS, tk), bf16)], axis=0)
        acc_ref[...] = alpha * acc_ref[...] + jnp.dot(vt1, pt, preferred_element_type=f32)
        m_ref[...] = m_new

    def below_diagonal(j, c):
        step(j, False)
        return c

    lax.fori_loop(0, i, below_diagonal, 0)
    step(i, True)
    ot = acc_ref[0:V_DIM, :] / acc_ref[V_DIM:V_DIM + 1, :]
    dt = ot[:, :tq] - lam * ot[:, tq:]
    ms = jnp.mean(dt * dt, axis=0, keepdims=True)
    nt = dt * lax.rsqrt(ms + SUBLN_EPS) * g_ref[...] * (1.0 - LAM_INIT)
    o_ref[...] = nt.T.astype(bf16)


def _attn_s_body(q_ref, kn_ref, vn_ref, lam, g_ref, kt_refs, v_refs, o_ref):
    n_pages = len(kt_refs)
    nq = q_ref.shape[0]
    nr = nq * N_SUB
    qf = q_ref[...].astype(f32)
    sub = lax.broadcasted_iota(jnp.int32, (N_SUB, ATTN_WIDTH), 0)
    col = lax.broadcasted_iota(jnp.int32, (N_SUB, ATTN_WIDTH), 1)
    diag = (col // HEAD_DIM) == sub
    qbd = jnp.concatenate([jnp.where(diag, jnp.broadcast_to(qf[a:a + 1], (N_SUB, ATTN_WIDTH)), 0.0)
                           for a in range(nq)], axis=0).astype(bf16)
    s_past = jnp.concatenate([jnp.dot(qbd, kt_refs[p][...].astype(bf16), preferred_element_type=f32)
                              for p in range(n_pages)], axis=1)
    s_new = lax.dot_general(qbd, kn_ref[...].astype(bf16), (((1,), (1,)), ((), ())),
                            preferred_element_type=f32)
    r = lax.broadcasted_iota(jnp.int32, s_new.shape, 0)
    c = lax.broadcasted_iota(jnp.int32, s_new.shape, 1)
    s_new = jnp.where((c <= r // N_SUB) & (c < nq), s_new, -jnp.inf)
    m = jnp.maximum(jnp.max(s_past, axis=-1, keepdims=True), jnp.max(s_new, axis=-1, keepdims=True))
    p_past = jnp.exp2(s_past - m)
    p_new = jnp.exp2(s_new - m)
    l = jnp.sum(p_past, axis=-1, keepdims=True) + jnp.sum(p_new, axis=-1, keepdims=True)
    p_bf = p_past.astype(bf16)
    vn = vn_ref[...]
    rowhead = (lax.broadcasted_iota(jnp.int32, (nr, V_DIM), 0) % N_SUB) // 2
    o = jnp.zeros((nr, V_DIM), f32)
    for h in range(N_HEADS):
        acc = jnp.zeros((nr, V_DIM), f32)
        for p in range(n_pages):
            vh = v_refs[p][pl.ds(h, PAGE, stride=N_HEADS), :].astype(bf16)
            acc = acc + jnp.dot(p_bf[:, p * PAGE:(p + 1) * PAGE], vh, preferred_element_type=f32)
        for a in range(nq):
            acc = acc + p_new[:, a:a + 1] * vn[a:a + 1, h * V_DIM:(h + 1) * V_DIM]
        o = jnp.where(rowhead == h, acc, o)
    o = o / l
    d = o - lam * pltpu.roll(o, nr - 1, 0)
    d = _rms(d, g_ref[...], SUBLN_EPS) * (1.0 - LAM_INIT)
    for a in range(nq):
        rowv = jnp.concatenate([d[a * N_SUB + 2 * h:a * N_SUB + 2 * h + 1] for h in range(N_HEADS)], axis=1)
        o_ref[a:a + 1, :] = rowv.astype(o_ref.dtype)


def _attn_kernel(n_pages, pt_ref, qt_ref, k_ref, vt_ref, lq1_ref, lk1_ref, lq2_ref, lk2_ref, gcol_ref,
                 q_ref, kn_ref, vn_ref, grow_ref, *rest):
    kt_refs = rest[:n_pages]
    v_refs = rest[n_pages:2 * n_pages]
    op_ref, os_ref, m_ref, acc_ref = rest[2 * n_pages:]
    lam = _lam(lq1_ref[...], lk1_ref[...], lq2_ref[...], lk2_ref[...])
    _attn_p_body(pl.program_id(2), qt_ref, k_ref, vt_ref, lam, gcol_ref, op_ref, m_ref, acc_ref)
    _attn_s_body(q_ref, kn_ref, vn_ref, lam, grow_ref, kt_refs, v_refs, os_ref)


def _attn_call(pt_flat, qt, k, vt, lq1, lk1, lq2, lk2, g, q_s, kn_s, vn_s, kt, v2, n_pages):
    B, S, _ = k.shape
    nq_tiles = S // TQ
    nb, nq, _ = q_s.shape
    assert nb == B * N_HEADS * nq_tiles, "one sample sequence per prompt grid step"
    seq = lambda b, h, i: (b * N_HEADS + h) * nq_tiles + i
    small = lambda n: pl.BlockSpec((1, n), lambda b, h, i, pt: (0, 0))

    def page_spec(p):
        return pl.BlockSpec((None, 512, PAGE), lambda b, h, i, pt: (pt[seq(b, h, i) * n_pages + p], 0, 0))

    return pl.pallas_call(
        functools.partial(_attn_kernel, n_pages),
        grid_spec=pltpu.PrefetchScalarGridSpec(
            num_scalar_prefetch=1,
            grid=(B, N_HEADS, nq_tiles),
            in_specs=[pl.BlockSpec((None, V_DIM, TQ), lambda b, h, i, pt: (b, h, i)),
                      pl.BlockSpec((None, S, V_DIM), lambda b, h, i, pt: (b, 0, h)),
                      pl.BlockSpec((None, nq_tiles, V_DIM, TQ), lambda b, h, i, pt: (b, 0, h, 0)),
                      small(HEAD_DIM), small(HEAD_DIM), small(HEAD_DIM), small(HEAD_DIM),
                      pl.BlockSpec((V_DIM, 1), lambda b, h, i, pt: (0, 0)),
                      pl.BlockSpec((None, nq, ATTN_WIDTH), lambda b, h, i, pt: (seq(b, h, i), 0, 0)),
                      pl.BlockSpec((None, 8, ATTN_WIDTH), lambda b, h, i, pt: (seq(b, h, i), 0, 0)),
                      pl.BlockSpec((None, 8, ATTN_WIDTH), lambda b, h, i, pt: (seq(b, h, i), 0, 0)),
                      small(V_DIM)]
                     + [page_spec(p) for p in range(n_pages)] * 2,
            out_specs=[pl.BlockSpec((None, TQ, V_DIM), lambda b, h, i, pt: (b, i, h)),
                       pl.BlockSpec((None, nq, ATTN_WIDTH), lambda b, h, i, pt: (seq(b, h, i), 0, 0))],
            scratch_shapes=[pltpu.VMEM((1, 2 * TQ), f32), pltpu.VMEM((V_DIM + ONES_ROWS, 2 * TQ), f32)],
        ),
        out_shape=[jax.ShapeDtypeStruct((B, S, ATTN_WIDTH), bf16), jax.ShapeDtypeStruct((nb, nq, ATTN_WIDTH), bf16)],
        compiler_params=_cparams(("arbitrary",) * 3),
        name="attn",
    )(pt_flat, qt, k, vt, lq1, lk1, lq2, lk2, g.reshape(V_DIM, 1), q_s, kn_s, vn_s, g,
      *([kt] * n_pages), *([v2] * n_pages))


SLAB = D_MODEL // LANES


def _store_slabs(ref, row0, x):
    n = x.shape[0]
    for a in range(SLAB):
        ref[pl.ds(SLAB * row0 + a, n, stride=SLAB), :] = x[:, a * LANES:(a + 1) * LANES]


def _load_slabs(ref, row0, n):
    return jnp.concatenate([ref[pl.ds(SLAB * row0 + a, n, stride=SLAB), :] for a in range(SLAB)], axis=1)


def _slab_copy(src_ref, src_row, dst_ref, dst_row, sem):
    src = src_ref.at[pl.ds(pl.multiple_of(src_row * SLAB, SLAB), SLAB), :]
    dst = dst_ref.at[pl.ds(pl.multiple_of(dst_row * SLAB, SLAB), SLAB), :]
    return pltpu.make_async_copy(src, dst, sem)


def _post_tile(x_ref, att_ref, cv_ref, ga_ref, shf_ref, scf_ref, ln_ref, wo_ref, wr_ref, br_ref, ltri_ref,
               x1_ref, h2_ref, rt_ref, rw_ref, cnt_ref, carry_ref):
    mix = (jnp.dot(att_ref[...], wo_ref[0:ATTN_WIDTH, :], preferred_element_type=f32)
           + jnp.dot(cv_ref[...], wo_ref[ATTN_WIDTH:, :], preferred_element_type=f32))
    x1 = x_ref[...] + ga_ref[...] * mix
    x1_ref[...] = x1
    h2 = _rms(x1, ln_ref[...], NORM_EPS) * (1.0 + scf_ref[...]) + shf_ref[...]
    _store_slabs(h2_ref, 0, h2)
    logits = jnp.dot(h2.astype(bf16), wr_ref[...], preferred_element_type=f32) + br_ref[...]
    lane = lax.broadcasted_iota(jnp.int32, logits.shape, 1)
    lane_f = lane.astype(f32)
    big = jnp.float32(1e9)
    neg = -jnp.inf

    def first_max(vals):
        mx = jnp.max(vals, axis=-1, keepdims=True)
        idx = jnp.min(jnp.where(vals == mx, lane_f, big), axis=-1, keepdims=True)
        return mx, idx

    gl = jnp.where((lane >= ROUTE_GRP_LANE) & (lane < ROUTE_GRP_LANE + N_GROUPS), logits, neg)
    gmax, gidx = first_max(gl)
    g_p = 1.0 / jnp.sum(jnp.exp(gl - gmax), axis=-1, keepdims=True)
    lo = (gidx - ROUTE_GRP_LANE) * EXP_PER_GROUP
    el = jnp.where((lane_f >= lo) & (lane_f < lo + EXP_PER_GROUP), logits, neg)
    v1, i1 = first_max(el)
    el2 = jnp.where(lane_f == i1, neg, el)
    v2, i2 = first_max(el2)
    t = jnp.exp(v2 - v1)
    w1 = g_p / (1.0 + t)
    w2 = g_p * t / (1.0 + t)
    oh1 = lane_f == i1
    oh2 = lane_f == i2
    cnt = jnp.where(oh1 | oh2, 1.0, 0.0)
    prefix = jnp.dot(ltri_ref[...], cnt.astype(bf16), preferred_element_type=f32) + carry_ref[...]
    r1 = jnp.sum(jnp.where(oh1, prefix, 0.0), axis=-1, keepdims=True)
    r2 = jnp.sum(jnp.where(oh2, prefix, 0.0), axis=-1, keepdims=True)
    ri = jnp.where(lane == 0, i1, jnp.where(lane == 1, i2, jnp.where(lane == 2, r1, jnp.where(lane == 3, r2, 0.0))))
    rt_ref[...] = ri.T[0:8, :].astype(jnp.int32)
    rw_ref[...] = jnp.where(lane == 0, w1, jnp.where(lane == 1, w2, 0.0))
    new_carry = carry_ref[...] + jnp.sum(cnt, axis=0, keepdims=True)
    carry_ref[...] = new_carry
    cnt_ref[...] = new_carry


def _post_kernel(n_tiles_p, xp_ref, attp_ref, cvp_ref, gap_ref, shfp_ref, scfp_ref,
                 xs_ref, atts_ref, cvs_ref, gas_ref, shfs_ref, scfs_ref, *rest):
    i = pl.program_id(0)
    carry_ref = rest[-1]

    @pl.when(i == 0)
    def _():
        carry_ref[...] = jnp.zeros(carry_ref.shape, f32)

    @pl.when(i < n_tiles_p)
    def _():
        _post_tile(xp_ref, attp_ref, cvp_ref, gap_ref, shfp_ref, scfp_ref, *rest)

    @pl.when(i >= n_tiles_p)
    def _():
        _post_tile(xs_ref, atts_ref, cvs_ref, gas_ref, shfs_ref, scfs_ref, *rest)


def _post_call(x_p, att_p, cv_p, mod_p, x_s, att_s, cv_s, mod_s, tiles_per_batch, ln_ffn, w_out_bf, w_r_bf, b_r, ltri):
    n_p, n_s = x_p.shape[0], x_s.shape[0]
    tp, ts = n_p // TM, n_s // TM
    n = n_p + n_s
    prow = lambda w: pl.BlockSpec((TM, w), lambda i: (jnp.minimum(i, tp - 1), 0))
    srow = lambda w: pl.BlockSpec((TM, w), lambda i: (jnp.maximum(i - tp, 0), 0))
    pmod = lambda sec: pl.BlockSpec((None, 1, D_MODEL), lambda i: (jnp.minimum(i, tp - 1) // tiles_per_batch, 0, sec))
    smod = lambda sec: pl.BlockSpec((TM, D_MODEL), lambda i: (jnp.maximum(i - tp, 0), sec))
    row = lambda w: pl.BlockSpec((TM, w), lambda i: (i, 0))
    const = lambda shape: pl.BlockSpec(shape, lambda i: (0, 0))
    return pl.pallas_call(
        functools.partial(_post_kernel, tp),
        grid=(tp + ts,),
        in_specs=[prow(D_MODEL), prow(ATTN_WIDTH), prow(CONV_WIDTH), pmod(2), pmod(3), pmod(4),
                  srow(D_MODEL), srow(ATTN_WIDTH), srow(CONV_WIDTH), smod(2), smod(3), smod(4),
                  const((1, D_MODEL)), const((D_MODEL, D_MODEL)), const((D_MODEL, LANES)), const((1, LANES)),
                  const((TM, TM))],
        out_specs=[row(D_MODEL), pl.BlockSpec((TM * SLAB, LANES), lambda i: (i, 0)),
                   pl.BlockSpec((8, TM), lambda i: (0, i)), row(LANES), const((1, LANES))],
        out_shape=[jax.ShapeDtypeStruct((n, D_MODEL), f32), jax.ShapeDtypeStruct((n * SLAB, LANES), f32),
                   jax.ShapeDtypeStruct((8, n), jnp.int32), jax.ShapeDtypeStruct((n, LANES), f32),
                   jax.ShapeDtypeStruct((1, LANES), f32)],
        scratch_shapes=[pltpu.VMEM((1, LANES), f32)],
        compiler_params=_cparams(("arbitrary",)),
        name="post",
    )(x_p, att_p, cv_p, mod_p, mod_p, mod_p, x_s, att_s, cv_s, mod_s, mod_s, mod_s,
      ln_ffn, w_out_bf, w_r_bf, b_r, ltri)


def _dispatch_kernel(pos0_ref, pos1_ref, src_ref, xs_ref, sem):
    t0 = pl.program_id(0) * TM

    def issue(r, c):
        t = t0 + r
        _slab_copy(src_ref, r, xs_ref, pos0_ref[t], sem).start(priority=0)
        _slab_copy(src_ref, r, xs_ref, pos1_ref[t], sem).start(priority=1)
        return c

    lax.fori_loop(0, TM, issue, 0, unroll=ROW_DMA_UNROLL)
    tile_copy = pltpu.make_async_copy(src_ref, xs_ref.at[pl.ds(0, TM * SLAB), :], sem)
    tile_copy.wait()
    tile_copy.wait()


def _dispatch_call(pos0, pos1, h2):
    tiles = h2.shape[0] // (TM * SLAB)
    return pl.pallas_call(
        _dispatch_kernel,
        grid_spec=pltpu.PrefetchScalarGridSpec(
            num_scalar_prefetch=2,
            grid=(tiles,),
            in_specs=[pl.BlockSpec((TM * SLAB, LANES), lambda i, p0, p1: (i, 0))],
            out_specs=pl.BlockSpec(memory_space=pl.ANY),
            scratch_shapes=[pltpu.SemaphoreType.DMA(())],
        ),
        out_shape=jax.ShapeDtypeStruct((2 * h2.shape[0], LANES), f32),
        compiler_params=_cparams(("arbitrary",)),
        name="dispatch",
    )(pos0, pos1, h2)


def _tile_chunk_copies(hbm_ref, tile, buf_ref, slot, sem, to_hbm):
    r0 = pl.multiple_of(tile * TE, TE)
    pairs = [(hbm_ref.at[pl.ds(r0, TE), a, :], buf_ref.at[slot, a]) for a in range(SLAB)]
    return [pltpu.make_async_copy(v, h, sem) if to_hbm else pltpu.make_async_copy(h, v, sem) for h, v in pairs]


def _experts_kernel(n_tiles, tile_ref, exp_ref, lo_ref, hi_ref, xs_ref, wg_ref, wu_ref, wd_ref, ys_ref,
                    wg_bf, wu_bf, wd_bf, cur_ref, xbuf, ybuf, xsem, ysem):
    w = pl.program_id(0)
    lo = lo_ref[w]
    hi = hi_ref[w]
    k = tile_ref[w]
    base = k * TE
    slot = k % 2

    @pl.when(w == 0)
    def _():
        cur_ref[0] = -1
        for c in _tile_chunk_copies(xs_ref, 0, xbuf, 0, xsem.at[0], False):
            c.start()

    @pl.when(hi > lo)
    def _():
        first = lo == base
        last = hi == base + TE

        @pl.when(first)
        def _():
            for c in _tile_chunk_copies(xs_ref, k, xbuf, slot, xsem.at[slot], False):
                c.wait()

            @pl.when(k + 1 < n_tiles)
            def _():
                for c in _tile_chunk_copies(xs_ref, k + 1, xbuf, 1 - slot, xsem.at[1 - slot], False):
                    c.start()

            @pl.when(k >= 2)
            def _():
                for c in _tile_chunk_copies(ys_ref, k - 2, ybuf, slot, ysem.at[slot], True):
                    c.wait()

        e = exp_ref[w]

        @pl.when(cur_ref[0] != e)
        def _():
            wg_bf[...] = wg_ref[...].astype(bf16)
            wu_bf[...] = wu_ref[...].astype(bf16)
            wd_bf[...] = wd_ref[...].astype(bf16)
            cur_ref[0] = e

        x = jnp.concatenate([xbuf[slot, a] for a in range(SLAB)], axis=1).astype(bf16)
        g = jnp.dot(x, wg_bf[...], preferred_element_type=f32)
        u = jnp.dot(x, wu_bf[...], preferred_element_type=f32)
        hid = (_silu(g) * u).astype(bf16)
        y = jnp.dot(hid, wd_bf[...], preferred_element_type=f32)
        row = base + lax.broadcasted_iota(jnp.int32, (TE, LANES), 0)
        mine = (row >= lo) & (row < hi)

        @pl.when(first)
        def _():
            for a in range(SLAB):
                ybuf[slot, a] = jnp.where(mine, y[:, a * LANES:(a + 1) * LANES], 0.0)

        @pl.when(jnp.logical_not(first))
        def _():
            for a in range(SLAB):
                ybuf[slot, a] = jnp.where(mine, y[:, a * LANES:(a + 1) * LANES], ybuf[slot, a])

        @pl.when(last)
        def _():
            for c in _tile_chunk_copies(ys_ref, k, ybuf, slot, ysem.at[slot], True):
                c.start()

    @pl.when(w == pl.num_programs(0) - 1)
    def _():
        for kk in (n_tiles - 2, n_tiles - 1):
            for c in _tile_chunk_copies(ys_ref, kk, ybuf, kk % 2, ysem.at[kk % 2], True):
                c.wait()


def _experts_call(tile_id, exp_id, seg_lo, seg_hi, xs, w_gate, w_up, w_down):
    n_items = tile_id.shape[0]
    n_rows = xs.shape[0] // SLAB
    n_tiles = n_rows // TE
    assert n_tiles >= 2
    idx = lambda w, t, e, lo, hi: (e[w], 0, 0)
    ys = pl.pallas_call(
        functools.partial(_experts_kernel, n_tiles),
        grid_spec=pltpu.PrefetchScalarGridSpec(
            num_scalar_prefetch=4,
            grid=(n_items,),
            in_specs=[pl.BlockSpec(memory_space=pl.ANY),
                      pl.BlockSpec((None, D_MODEL, D_EXPERT), idx),
                      pl.BlockSpec((None, D_MODEL, D_EXPERT), idx),
                      pl.BlockSpec((None, D_EXPERT, D_MODEL), idx)],
            out_specs=pl.BlockSpec(memory_space=pl.ANY),
            scratch_shapes=[pltpu.VMEM((D_MODEL, D_EXPERT), bf16), pltpu.VMEM((D_MODEL, D_EXPERT), bf16),
                            pltpu.VMEM((D_EXPERT, D_MODEL), bf16), pltpu.SMEM((1,), jnp.int32),
                            pltpu.VMEM((2, SLAB, TE, LANES), f32), pltpu.VMEM((2, SLAB, TE, LANES), f32),
                            pltpu.SemaphoreType.DMA((2,)), pltpu.SemaphoreType.DMA((2,))],
        ),
        out_shape=jax.ShapeDtypeStruct((n_rows, SLAB, LANES), f32),
        compiler_params=_cparams(("arbitrary",)),
        name="experts",
    )(tile_id, exp_id, seg_lo, seg_hi, xs.reshape(n_rows, SLAB, LANES), w_gate, w_up, w_down)
    return ys.reshape(xs.shape)


def _combine_kernel(n_tiles_p, pos0_ref, pos1_ref, x1_ref, rw_ref, gfp_ref, gfs_ref, ln_ref, ys_ref,
                    op_ref, os_ref, ybuf, sems):
    i = pl.program_id(0)
    n = pl.num_programs(0)
    half = 2 * TM

    def issue(tile, buf):
        def body(r, c):
            t = tile * TM + r
            _slab_copy(ys_ref, pos0_ref[t], ybuf, buf * half + r, sems.at[buf]).start(priority=0)
            _slab_copy(ys_ref, pos1_ref[t], ybuf, buf * half + TM + r, sems.at[buf]).start(priority=1)
            return c

        lax.fori_loop(0, TM, body, 0, unroll=ROW_DMA_UNROLL)

    @pl.when(i == 0)
    def _():
        issue(0, 0)

    @pl.when(i + 1 < n)
    def _():
        issue(i + 1, (i + 1) % 2)

    buf = i % 2
    tile_copy = pltpu.make_async_copy(ys_ref.at[pl.ds(0, TM * SLAB), :], ybuf.at[pl.ds(0, TM * SLAB), :], sems.at[buf])
    tile_copy.wait()
    tile_copy.wait()
    rw = rw_ref[...]
    moe = rw[:, 0:1] * _load_slabs(ybuf, buf * half, TM) + rw[:, 1:2] * _load_slabs(ybuf, buf * half + TM, TM)

    def finish(gf_ref, o_ref):
        x2 = x1_ref[...] + gf_ref[...] * moe
        o_ref[...] = _rms(x2, ln_ref[...], NORM_EPS)

    @pl.when(i < n_tiles_p)
    def _():
        finish(gfp_ref, op_ref)

    @pl.when(i >= n_tiles_p)
    def _():
        finish(gfs_ref, os_ref)


def _combine_call(pos0, pos1, x1, rw, mod_p, mod_s, tiles_per_batch, ln_final, ys):
    n = x1.shape[0]
    n_s = mod_s.shape[0]
    tp, ts = (n - n_s) // TM, n_s // TM
    pidx = lambda i: jnp.minimum(i, tp - 1)
    sidx = lambda i: jnp.maximum(i - tp, 0)
    return pl.pallas_call(
        functools.partial(_combine_kernel, tp),
        grid_spec=pltpu.PrefetchScalarGridSpec(
            num_scalar_prefetch=2,
            grid=(tp + ts,),
            in_specs=[pl.BlockSpec((TM, D_MODEL), lambda i, p0, p1: (i, 0)),
                      pl.BlockSpec((TM, LANES), lambda i, p0, p1: (i, 0)),
                      pl.BlockSpec((None, 1, D_MODEL), lambda i, p0, p1: (pidx(i) // tiles_per_batch, 0, 5)),
                      pl.BlockSpec((TM, D_MODEL), lambda i, p0, p1: (sidx(i), 5)),
                      pl.BlockSpec((1, D_MODEL), lambda i, p0, p1: (0, 0)),
                      pl.BlockSpec(memory_space=pl.ANY)],
            out_specs=[pl.BlockSpec((TM, D_MODEL), lambda i, p0, p1: (pidx(i), 0)),
                       pl.BlockSpec((TM, D_MODEL), lambda i, p0, p1: (sidx(i), 0))],
            scratch_shapes=[pltpu.VMEM((2 * 2 * TM * SLAB, LANES), f32), pltpu.SemaphoreType.DMA((2,))],
        ),
        out_shape=[jax.ShapeDtypeStruct((n - n_s, D_MODEL), f32), jax.ShapeDtypeStruct((n_s, D_MODEL), f32)],
        compiler_params=_cparams(("arbitrary",)),
        name="combine",
    )(pos0, pos1, x1, rw, mod_p, mod_s, ln_final, ys)


def _rope_tables(pos):
    inv = 1.0 / (ROPE_THETA ** (np.arange(0, HEAD_DIM, 2, dtype=np.float64) / HEAD_DIM))
    ang = np.asarray(pos, np.float64)[:, None] * inv[None, :]
    ang = np.concatenate([ang, ang], axis=-1)
    sign = np.where(np.arange(HEAD_DIM) < HEAD_DIM // 2, -1.0, 1.0)
    cos = np.tile(np.cos(ang), (1, N_SUB)).astype(np.float32)
    sin_signed = np.tile(np.sin(ang) * sign[None, :], (1, N_SUB)).astype(np.float32)
    return jnp.asarray(cos), jnp.asarray(sin_signed)


def _segments(counts, n_rows):
    n_tiles = n_rows // TE
    offs = jnp.concatenate([jnp.zeros((1,), jnp.int32), jnp.cumsum(counts)[:-1].astype(jnp.int32)])
    tiles = jnp.arange(n_tiles, dtype=jnp.int32) * TE
    rank_t = jnp.arange(n_tiles, dtype=jnp.int32) + jnp.sum(offs[None, :] < tiles[:, None], axis=1).astype(jnp.int32)
    rank_o = jnp.arange(N_EXPERTS, dtype=jnp.int32) + jnp.minimum(offs // TE + 1, n_tiles)
    vals = jnp.concatenate([tiles, offs])
    ranks = jnp.concatenate([rank_t, rank_o])
    n_items = n_tiles + N_EXPERTS
    w = jnp.arange(n_items, dtype=jnp.int32)
    seg_lo = jnp.sum(jnp.where(ranks[None, :] == w[:, None], vals[None, :], 0), axis=1).astype(jnp.int32)
    seg_hi = jnp.concatenate([seg_lo[1:], jnp.full((1,), n_rows, jnp.int32)])
    tile_id = jnp.minimum(seg_lo // TE, n_tiles - 1)
    exp_id = jnp.sum(offs[None, :] <= seg_lo[:, None], axis=1).astype(jnp.int32) - 1
    return offs, tile_id, exp_id, seg_lo, seg_hi


def kernel(x_prompt, x_sample, cache_k, cache_v, state_conv, page_table, c_prompt, c_sample, w_ada, b_ada, ln_mix, w_in, lam_q1, lam_k1, lam_q2, lam_k2, subln_g, w_conv, w_out, ln_ffn, w_router_grp, b_router_grp, w_router_exp, b_router_exp, w_gate, w_up, w_down, ln_final):
    B, S, _ = x_prompt.shape
    DB, L, _ = x_sample.shape
    n_phys = cache_k.shape[1]
    n_pages = page_table.shape[1]
    past = n_pages * PAGE
    n_p = B * S
    n_s = DB * L
    n_tok = n_p + n_s

    w_in_bf = w_in[0].astype(bf16)
    w_out_bf = w_out[0].astype(bf16)
    w_r = jnp.zeros((D_MODEL, LANES), f32)
    w_r = w_r.at[:, :N_EXPERTS].set(w_router_exp[0]).at[:, ROUTE_GRP_LANE:ROUTE_GRP_LANE + N_GROUPS].set(w_router_grp[0])
    b_r = jnp.zeros((1, LANES), f32)
    b_r = b_r.at[0, :N_EXPERTS].set(b_router_exp[0]).at[0, ROUTE_GRP_LANE:ROUTE_GRP_LANE + N_GROUPS].set(b_router_grp[0])
    w_r_bf = w_r.astype(bf16)
    cos_p, sin_p = _rope_tables(np.arange(S))
    cos_s, sin_s = _rope_tables(past + np.repeat(np.arange(L), DB))
    ltri = jnp.asarray(np.tril(np.ones((TM, TM), np.float32), -1), bf16)

    mod = _mod_call(jnp.concatenate([c_prompt, c_sample], axis=0), w_ada[0], b_ada)
    mod_p = mod[:B].reshape(B, 1, 6 * D_MODEL)
    mod_s = jnp.tile(mod[B:], (L, 1))

    qt_p, kt_p, kb_p, v4_p, vt_p, cv_p, st_p = _inproj_p_call(x_prompt, mod_p, ln_mix, w_in_bf, cos_p, sin_p, w_conv[0])
    xs_l = x_sample.transpose(1, 0, 2).reshape(n_s, D_MODEL)
    st_in = state_conv[0].transpose(1, 0, 2)
    q_s, kf_s, vf_s, cv_s, st_s = _inproj_s_call(xs_l, mod_s[:, 0:D_MODEL], mod_s[:, D_MODEL:2 * D_MODEL], ln_mix,
                                                 w_in_bf, cos_s, sin_s, w_conv[0], st_in)
    to_b = lambda a: a.reshape(L, DB, -1).transpose(1, 0, 2)
    pad8 = lambda a: jnp.pad(a, ((0, 0), (0, 8 - L), (0, 0)))
    kt = jnp.transpose(cache_k[0], (0, 2, 3, 1)).reshape(n_phys, N_SUB * HEAD_DIM, PAGE)
    v2 = cache_v[0].reshape(n_phys, PAGE * N_HEADS, V_DIM)
    att_p, att_s_b = _attn_call(page_table.reshape(-1), qt_p, kb_p, vt_p, lam_q1, lam_k1, lam_q2, lam_k2, subln_g,
                                to_b(q_s), pad8(to_b(kf_s)), pad8(to_b(vf_s)), kt, v2, n_pages)
    att_s = att_s_b.transpose(1, 0, 2).reshape(n_s, ATTN_WIDTH)

    x1, h2, rt, rw, cnt = _post_call(x_prompt.reshape(n_p, D_MODEL), att_p.reshape(n_p, ATTN_WIDTH),
                                     cv_p.reshape(n_p, CONV_WIDTH), mod_p, xs_l, att_s, cv_s, mod_s, S // TM,
                                     ln_ffn, w_out_bf, w_r_bf, b_r, ltri)

    counts = cnt[0, :N_EXPERTS].astype(jnp.int32)
    offs, tile_id, exp_id, seg_lo, seg_hi = _segments(counts, 2 * n_tok)
    e_col = jnp.arange(N_EXPERTS, dtype=jnp.int32)[:, None]
    start = lambda e_row: jnp.sum(jnp.where(e_row[None, :] == e_col, offs[:, None], 0), axis=0)
    pos0 = start(rt[0]) + rt[2]
    pos1 = start(rt[1]) + rt[3]

    xs_sorted = _dispatch_call(pos0, pos1, h2)
    ys = _experts_call(tile_id, exp_id, seg_lo, seg_hi, xs_sorted, w_gate[0], w_up[0], w_down[0])
    y_p, y_s = _combine_call(pos0, pos1, x1, rw, mod_p, mod_s, S // TM, ln_final.reshape(1, D_MODEL), ys)

    from_l = lambda a: a.reshape(L, DB, -1).transpose(1, 0, 2)
    y_prompt = y_p.reshape(B, S, D_MODEL)
    y_sample = from_l(y_s)
    k_prompt = kt_p.reshape(B, N_SUB, HEAD_DIM, S).transpose(0, 3, 1, 2)[None]
    v_prompt = v4_p.reshape(1, B, S, N_HEADS, V_DIM)
    conv_prompt = st_p[None]
    k_sample = from_l(kf_s).reshape(1, DB, L, N_SUB, HEAD_DIM)
    v_sample = from_l(vf_s).reshape(1, DB, L, N_HEADS, V_DIM)
    conv_sample = st_s.transpose(1, 0, 2)[None]
    return (y_prompt, y_sample, k_prompt, v_prompt, conv_prompt, k_sample, v_sample, conv_sample)
```

```python
import functools
import math

import jax
import jax.numpy as jnp
import numpy as np
from jax import lax
from jax.experimental import pallas as pl
from jax.experimental.pallas import tpu as pltpu

D_MODEL = 1024
ATTN_WIDTH = 512
CONV_WIDTH = 512
N_HEADS = 4
N_SUB = 8
HEAD_DIM = 64
V_DIM = 2 * HEAD_DIM
CONV_K = 3
ROPE_THETA = 10000.0
N_GROUPS = 4
EXP_PER_GROUP = 8
N_EXPERTS = 32
D_EXPERT = 256
NORM_EPS = 1e-6
SUBLN_EPS = 1e-5
LAM_INIT = 0.8 - 0.6 * math.exp(-0.3 * 0)
LOG2E = math.log2(math.e)
PAGE = 128
LANES = 128
ROUTE_GRP_LANE = 32

TM = 512
TQ = 512
TE = 256
ROW_DMA_UNROLL = 8
ONES_ROWS = 16
VMEM_LIMIT = 56 * 1024 * 1024

f32 = jnp.float32
bf16 = jnp.bfloat16


def _cparams(sem):
    return pltpu.CompilerParams(dimension_semantics=sem, vmem_limit_bytes=VMEM_LIMIT)


def _rms(x, g, eps):
    return x * lax.rsqrt(jnp.mean(x * x, axis=-1, keepdims=True) + eps) * g


def _silu(x):
    return x * (1.0 / (1.0 + jnp.exp(-x)))


def _mod_kernel(c_ref, w_ref, b_ref, o_ref):
    a = _silu(c_ref[...]).astype(bf16)
    o_ref[...] = jnp.dot(a, w_ref[...].astype(bf16), preferred_element_type=f32) + b_ref[...]


def _mod_call(c_all, w_ada, b_ada):
    n = c_all.shape[0]
    return pl.pallas_call(
        _mod_kernel,
        grid=(6,),
        in_specs=[pl.BlockSpec((n, D_MODEL), lambda j: (0, 0)),
                  pl.BlockSpec((D_MODEL, D_MODEL), lambda j: (0, j)),
                  pl.BlockSpec((1, D_MODEL), lambda j: (0, j))],
        out_specs=pl.BlockSpec((n, D_MODEL), lambda j: (0, j)),
        out_shape=jax.ShapeDtypeStruct((n, 6 * D_MODEL), f32),
        compiler_params=_cparams(("arbitrary",)),
        name="mod",
    )(c_all, w_ada, b_ada)


def _rope(t, cos, sin_signed, lo_mask):
    n = t.shape[-1]
    rot = jnp.where(lo_mask, pltpu.roll(t, n - HEAD_DIM // 2, 1), pltpu.roll(t, HEAD_DIM // 2, 1))
    return t * cos + rot * sin_signed


def _inproj_common(x, sh, sc, ln, w_ref, cos, sin_signed):
    h = (_rms(x, ln, NORM_EPS) * (1.0 + sc) + sh).astype(bf16)

    def sec(i):
        return jnp.dot(h, w_ref[:, i * 512:(i + 1) * 512], preferred_element_type=f32)

    lane = lax.broadcasted_iota(jnp.int32, (x.shape[0], 512), 1)
    lo_mask = (lane % HEAD_DIM) < (HEAD_DIM // 2)
    q = _rope(sec(0), cos, sin_signed, lo_mask) * (HEAD_DIM ** -0.5 * LOG2E)
    k = _rope(sec(1), cos, sin_signed, lo_mask)
    v = sec(2)
    bg = sec(3)
    cu = sec(4) * sec(5)
    return q, k, v, bg, cu


def _inproj_p_kernel(x_ref, sh_ref, sc_ref, ln_ref, w_ref, cos_ref, sin_ref, wc_ref,
                     qt_ref, kt_ref, kb_ref, v4_ref, vt_ref, cv_ref, st_ref, carry_ref):
    s = pl.program_id(0)
    b = pl.program_id(1)
    q, k, v, bg, cu = _inproj_common(x_ref[...], sh_ref[...], sc_ref[...], ln_ref[...], w_ref,
                                     cos_ref[...], sin_ref[...])
    qt_ref[...] = q.T.astype(bf16)
    kt_ref[...] = k.T
    kb_ref[...] = k.astype(bf16)
    vt_ref[...] = v.T.astype(bf16)
    for h in range(N_HEADS):
        v4_ref[pl.ds(h, v.shape[0], stride=N_HEADS), :] = v[:, h * V_DIM:(h + 1) * V_DIM]
    tm = cu.shape[0]
    prev = jnp.where(s > 0, carry_ref[b], 0.0)
    row = lax.broadcasted_iota(jnp.int32, cu.shape, 0)
    cu1 = jnp.where(row == 0, prev[1:2], pltpu.roll(cu, 1, 0))
    cu2 = jnp.where(row == 0, prev[0:1], jnp.where(row == 1, prev[1:2], pltpu.roll(cu, 2, 0)))
    wc = wc_ref[...]
    conv = wc[0:1] * cu2 + wc[1:2] * cu1 + wc[2:3] * cu
    cv_ref[...] = (bg * conv).astype(bf16)
    last2 = cu[tm - 2:tm]
    carry_ref[b, 0:2, :] = last2
    st_ref[b] = last2


def _inproj_p_call(x, mod3, ln_mix, w_in_bf, cos, sin_signed, w_conv):
    B, S, _ = x.shape
    ns = S // TM
    row = lambda s, b: (b, s, 0)
    col = lambda s, b: (b, 0, s)
    rows = lambda dt: jax.ShapeDtypeStruct((B, S, 512), dt)
    cols = lambda dt: jax.ShapeDtypeStruct((B, 512, S), dt)
    row_spec = pl.BlockSpec((None, TM, 512), row)
    col_spec = pl.BlockSpec((None, 512, TM), col)
    return pl.pallas_call(
        _inproj_p_kernel,
        grid=(ns, B),
        in_specs=[pl.BlockSpec((None, TM, D_MODEL), row),
                  pl.BlockSpec((None, 1, D_MODEL), lambda s, b: (b, 0, 0)),
                  pl.BlockSpec((None, 1, D_MODEL), lambda s, b: (b, 0, 1)),
                  pl.BlockSpec((1, D_MODEL), lambda s, b: (0, 0)),
                  pl.BlockSpec((D_MODEL, 3072), lambda s, b: (0, 0)),
                  pl.BlockSpec((TM, 512), lambda s, b: (s, 0)),
                  pl.BlockSpec((TM, 512), lambda s, b: (s, 0)),
                  pl.BlockSpec((CONV_K, CONV_WIDTH), lambda s, b: (0, 0))],
        out_specs=[col_spec, col_spec, row_spec, pl.BlockSpec((None, N_HEADS * TM, V_DIM), row),
                   pl.BlockSpec((None, None, 512, TM), lambda s, b: (b, s, 0, 0)), row_spec,
                   pl.BlockSpec((B, 2, CONV_WIDTH), lambda s, b: (0, 0, 0))],
        out_shape=[cols(bf16), cols(f32), rows(bf16), jax.ShapeDtypeStruct((B, N_HEADS * S, V_DIM), f32),
                   jax.ShapeDtypeStruct((B, ns, 512, TM), bf16), rows(bf16),
                   jax.ShapeDtypeStruct((B, 2, CONV_WIDTH), f32)],
        scratch_shapes=[pltpu.VMEM((B, 8, CONV_WIDTH), f32)],
        compiler_params=_cparams(("arbitrary", "arbitrary")),
        name="inproj_p",
    )(x, mod3, mod3, ln_mix, w_in_bf, cos, sin_signed, w_conv)


def _inproj_s_kernel(x_ref, sh_ref, sc_ref, ln_ref, w_ref, cos_ref, sin_ref, wc_ref, st_in_ref,
                     q_ref, kf_ref, vf_ref, cv_ref, st_ref):
    q, k, v, bg, cu = _inproj_common(x_ref[...], sh_ref[...], sc_ref[...], ln_ref[...], w_ref,
                                     cos_ref[...], sin_ref[...])
    q_ref[...] = q.astype(bf16)
    kf_ref[...] = k
    vf_ref[...] = v
    nb = st_in_ref.shape[1]
    st0 = st_in_ref[0]
    st1 = st_in_ref[1]
    cu1 = jnp.concatenate([st1, cu[:3 * nb]], axis=0)
    cu2 = jnp.concatenate([st0, st1, cu[:2 * nb]], axis=0)
    wc = wc_ref[...]
    conv = wc[0:1] * cu2 + wc[1:2] * cu1 + wc[2:3] * cu
    cv_ref[...] = (bg * conv).astype(bf16)
    st_ref[0] = cu[2 * nb:3 * nb]
    st_ref[1] = cu[3 * nb:4 * nb]


def _inproj_s_call(x, sh, sc, ln_mix, w_in_bf, cos, sin_signed, w_conv, st_in):
    n = x.shape[0]
    nb = st_in.shape[1]
    full = lambda shape: pl.BlockSpec(shape, lambda i: (0,) * len(shape))
    return pl.pallas_call(
        _inproj_s_kernel,
        grid=(1,),
        in_specs=[full((n, D_MODEL)), full((n, D_MODEL)), full((n, D_MODEL)), full((1, D_MODEL)),
                  full((D_MODEL, 3072)), full((n, 512)), full((n, 512)), full((CONV_K, CONV_WIDTH)),
                  full((2, nb, CONV_WIDTH))],
        out_specs=[full((n, 512))] * 4 + [full((2, nb, CONV_WIDTH))],
        out_shape=[jax.ShapeDtypeStruct((n, 512), bf16), jax.ShapeDtypeStruct((n, 512), f32),
                   jax.ShapeDtypeStruct((n, 512), f32), jax.ShapeDtypeStruct((n, 512), bf16),
                   jax.ShapeDtypeStruct((2, nb, CONV_WIDTH), f32)],
        compiler_params=_cparams(("arbitrary",)),
        name="inproj_s",
    )(x, sh, sc, ln_mix, w_in_bf, cos, sin_signed, w_conv, st_in)


def _lam(lq1, lk1, lq2, lk2):
    a = jnp.sum(lq1 * lk1, axis=-1, keepdims=True)
    b = jnp.sum(lq2 * lk2, axis=-1, keepdims=True)
    return jnp.exp(a) - jnp.exp(b) + LAM_INIT


def _attn_p_body(i, qt_ref, k_ref, vt_ref, lam, g_ref, o_ref, m_ref, acc_ref):
    tq = qt_ref.shape[1]
    tk = tq
    m_ref[...] = jnp.full(m_ref.shape, -jnp.inf, f32)
    acc_ref[...] = jnp.zeros(acc_ref.shape, f32)

    def step(j, masked):
        k0 = pl.multiple_of(j * tk, tk)
        qt = qt_ref[...]
        row = lax.broadcasted_iota(jnp.int32, qt.shape, 0)
        zero = jnp.zeros_like(qt)
        q2t = jnp.concatenate([jnp.where(row < HEAD_DIM, qt, zero), jnp.where(row >= HEAD_DIM, qt, zero)], axis=1)
        st = jnp.dot(k_ref[pl.ds(k0, tk), :], q2t, preferred_element_type=f32)
        if masked:
            kpos = lax.broadcasted_iota(jnp.int32, st.shape, 0)
            c = lax.broadcasted_iota(jnp.int32, st.shape, 1)
            st = jnp.where(kpos <= jnp.where(c >= tq, c - tq, c), st, -jnp.inf)
        m_prev = m_ref[...]
        m_new = jnp.maximum(m_prev, jnp.max(st, axis=0, keepdims=True))
        alpha = jnp.exp2(m_prev - m_new)
        pt = jnp.exp2(st - m_new).astype(bf16)
        vt1 = jnp.concatenate([vt_ref[j], jnp.ones((ONES_ROWS, tk), bf16)], axis=0)
        acc_ref[...] = alpha * acc_ref[...] + jnp.dot(vt1, pt, preferred_element_type=f32)
        m_ref[...] = m_new

    def below_diagonal(j, c):
        step(j, False)
        return c

    lax.fori_loop(0, i, below_diagonal, 0)
    step(i, True)
    ot = acc_ref[0:V_DIM, :] / acc_ref[V_DIM:V_DIM + 1, :]
    dt = ot[:, :tq] - lam * ot[:, tq:]
    ms = jnp.mean(dt * dt, axis=0, keepdims=True)
    nt = dt * lax.rsqrt(ms + SUBLN_EPS) * g_ref[...] * (1.0 - LAM_INIT)
    o_ref[...] = nt.T.astype(bf16)


def _attn_s_body(q_ref, kn_ref, vn_ref, lam, g_ref, kt_refs, v_refs, o_ref):
    n_pages = len(kt_refs)
    nq = q_ref.shape[0]
    nr = nq * N_SUB
    qf = q_ref[...].astype(f32)
    sub = lax.broadcasted_iota(jnp.int32, (N_SUB, ATTN_WIDTH), 0)
    col = lax.broadcasted_iota(jnp.int32, (N_SUB, ATTN_WIDTH), 1)
    diag = (col // HEAD_DIM) == sub
    qbd = jnp.concatenate([jnp.where(diag, jnp.broadcast_to(qf[a:a + 1], (N_SUB, ATTN_WIDTH)), 0.0)
                           for a in range(nq)], axis=0).astype(bf16)
    kt_all = jnp.concatenate([kt_refs[p][...].astype(bf16) for p in range(n_pages)], axis=1)
    s_past = jnp.dot(qbd, kt_all, preferred_element_type=f32)
    s_new = lax.dot_general(qbd, kn_ref[...].astype(bf16), (((1,), (1,)), ((), ())),
                            preferred_element_type=f32)
    r = lax.broadcasted_iota(jnp.int32, s_new.shape, 0)
    c = lax.broadcasted_iota(jnp.int32, s_new.shape, 1)
    s_new = jnp.where((c <= r // N_SUB) & (c < nq), s_new, -jnp.inf)
    m = jnp.maximum(jnp.max(s_past, axis=-1, keepdims=True), jnp.max(s_new, axis=-1, keepdims=True))
    p_past = jnp.exp2(s_past - m)
    p_new = jnp.exp2(s_new - m)
    l = jnp.sum(p_past, axis=-1, keepdims=True) + jnp.sum(p_new, axis=-1, keepdims=True)
    p_bf = p_past.astype(bf16)
    vn = vn_ref[...]
    rowhead = (lax.broadcasted_iota(jnp.int32, (nr, V_DIM), 0) % N_SUB) // 2
    o = jnp.zeros((nr, V_DIM), f32)
    for h in range(N_HEADS):
        v_all = jnp.concatenate([v_refs[p][pl.ds(h, PAGE, stride=N_HEADS), :].astype(bf16)
                                 for p in range(n_pages)], axis=0)
        acc = jnp.dot(p_bf, v_all, preferred_element_type=f32)
        for a in range(nq):
            acc = acc + p_new[:, a:a + 1] * vn[a:a + 1, h * V_DIM:(h + 1) * V_DIM]
        o = jnp.where(rowhead == h, acc, o)
    o = o / l
    d = o - lam * pltpu.roll(o, nr - 1, 0)
    d = _rms(d, g_ref[...], SUBLN_EPS) * (1.0 - LAM_INIT)
    for a in range(nq):
        rowv = jnp.concatenate([d[a * N_SUB + 2 * h:a * N_SUB + 2 * h + 1] for h in range(N_HEADS)], axis=1)
        o_ref[a:a + 1, :] = rowv.astype(o_ref.dtype)


def _attn_kernel(n_pages, pt_ref, qt_ref, k_ref, vt_ref, lq1_ref, lk1_ref, lq2_ref, lk2_ref, gcol_ref,
                 q_ref, kn_ref, vn_ref, grow_ref, kt_hbm, v_hbm, op_ref, os_ref, m_ref, acc_ref, kbuf, vbuf, sems):
    t = (pl.program_id(0) * pl.num_programs(1) + pl.program_id(1)) * pl.num_programs(2) + pl.program_id(2)
    n_steps = pl.num_programs(0) * pl.num_programs(1) * pl.num_programs(2)
    slot = t % 2

    def fetch(seq, dst):
        for p in range(n_pages):
            page = pt_ref[seq * n_pages + p]
            pltpu.make_async_copy(kt_hbm.at[page], kbuf.at[dst, p], sems.at[dst]).start()
            pltpu.make_async_copy(v_hbm.at[page], vbuf.at[dst, p], sems.at[dst]).start()

    @pl.when(t == 0)
    def _():
        fetch(0, 0)

    pltpu.make_async_copy(kt_hbm.at[pl.ds(0, n_pages)], kbuf.at[slot], sems.at[slot]).wait()
    pltpu.make_async_copy(v_hbm.at[pl.ds(0, n_pages)], vbuf.at[slot], sems.at[slot]).wait()

    @pl.when(t + 1 < n_steps)
    def _():
        fetch(t + 1, 1 - slot)

    lam = _lam(lq1_ref[...], lk1_ref[...], lq2_ref[...], lk2_ref[...])
    _attn_p_body(pl.program_id(2), qt_ref, k_ref, vt_ref, lam, gcol_ref, op_ref, m_ref, acc_ref)
    _attn_s_body(q_ref, kn_ref, vn_ref, lam, grow_ref, [kbuf.at[slot, p] for p in range(n_pages)],
                 [vbuf.at[slot, p] for p in range(n_pages)], os_ref)


def _attn_call(pt_flat, qt, k, vt, lq1, lk1, lq2, lk2, g, q_s, kn_s, vn_s, kt, v2, n_pages):
    B, S, _ = k.shape
    nq_tiles = S // TQ
    nb, nq, _ = q_s.shape
    assert nb == B * N_HEADS * nq_tiles, "one sample sequence per prompt grid step"
    seq = lambda b, h, i: (b * N_HEADS + h) * nq_tiles + i
    small = lambda n: pl.BlockSpec((1, n), lambda b, h, i, pt: (0, 0))
    page_rows = kt.shape[1]
    return pl.pallas_call(
        functools.partial(_attn_kernel, n_pages),
        grid_spec=pltpu.PrefetchScalarGridSpec(
            num_scalar_prefetch=1,
            grid=(B, N_HEADS, nq_tiles),
            in_specs=[pl.BlockSpec((None, V_DIM, TQ), lambda b, h, i, pt: (b, h, i)),
                      pl.BlockSpec((None, S, V_DIM), lambda b, h, i, pt: (b, 0, h)),
                      pl.BlockSpec((None, nq_tiles, V_DIM, TQ), lambda b, h, i, pt: (b, 0, h, 0)),
                      small(HEAD_DIM), small(HEAD_DIM), small(HEAD_DIM), small(HEAD_DIM),
                      pl.BlockSpec((V_DIM, 1), lambda b, h, i, pt: (0, 0)),
                      pl.BlockSpec((None, nq, ATTN_WIDTH), lambda b, h, i, pt: (seq(b, h, i), 0, 0)),
                      pl.BlockSpec((None, 8, ATTN_WIDTH), lambda b, h, i, pt: (seq(b, h, i), 0, 0)),
                      pl.BlockSpec((None, 8, ATTN_WIDTH), lambda b, h, i, pt: (seq(b, h, i), 0, 0)),
                      small(V_DIM), pl.BlockSpec(memory_space=pl.ANY), pl.BlockSpec(memory_space=pl.ANY)],
            out_specs=[pl.BlockSpec((None, TQ, V_DIM), lambda b, h, i, pt: (b, i, h)),
                       pl.BlockSpec((None, nq, ATTN_WIDTH), lambda b, h, i, pt: (seq(b, h, i), 0, 0))],
            scratch_shapes=[pltpu.VMEM((1, 2 * TQ), f32), pltpu.VMEM((V_DIM + ONES_ROWS, 2 * TQ), f32),
                            pltpu.VMEM((2, n_pages, page_rows, PAGE), f32),
                            pltpu.VMEM((2, n_pages, page_rows, PAGE), f32), pltpu.SemaphoreType.DMA((2,))],
        ),
        out_shape=[jax.ShapeDtypeStruct((B, S, ATTN_WIDTH), bf16), jax.ShapeDtypeStruct((nb, nq, ATTN_WIDTH), bf16)],
        compiler_params=_cparams(("arbitrary",) * 3),
        name="attn",
    )(pt_flat, qt, k, vt, lq1, lk1, lq2, lk2, g.reshape(V_DIM, 1), q_s, kn_s, vn_s, g, kt, v2)


SLAB = D_MODEL // LANES


def _store_slabs(ref, row0, x):
    n = x.shape[0]
    for a in range(SLAB):
        ref[pl.ds(SLAB * row0 + a, n, stride=SLAB), :] = x[:, a * LANES:(a + 1) * LANES]


def _load_slabs(ref, row0, n):
    return jnp.concatenate([ref[pl.ds(SLAB * row0 + a, n, stride=SLAB), :] for a in range(SLAB)], axis=1)


def _slab_copy(src_ref, src_row, dst_ref, dst_row, sem):
    src = src_ref.at[pl.ds(pl.multiple_of(src_row * SLAB, SLAB), SLAB), :]
    dst = dst_ref.at[pl.ds(pl.multiple_of(dst_row * SLAB, SLAB), SLAB), :]
    return pltpu.make_async_copy(src, dst, sem)


def _post_tile(x_ref, att_ref, cv_ref, ga_ref, shf_ref, scf_ref, ln_ref, wo_ref, wr_ref, br_ref, ltri_ref,
               x1_ref, h2_ref, rt_ref, rw_ref, cnt_ref, carry_ref):
    mix = (jnp.dot(att_ref[...], wo_ref[0:ATTN_WIDTH, :], preferred_element_type=f32)
           + jnp.dot(cv_ref[...], wo_ref[ATTN_WIDTH:, :], preferred_element_type=f32))
    x1 = x_ref[...] + ga_ref[...] * mix
    x1_ref[...] = x1
    h2 = _rms(x1, ln_ref[...], NORM_EPS) * (1.0 + scf_ref[...]) + shf_ref[...]
    _store_slabs(h2_ref, 0, h2)
    logits = jnp.dot(h2.astype(bf16), wr_ref[...], preferred_element_type=f32) + br_ref[...]
    lane = lax.broadcasted_iota(jnp.int32, logits.shape, 1)
    lane_f = lane.astype(f32)
    big = jnp.float32(1e9)
    neg = -jnp.inf

    def first_max(vals):
        mx = jnp.max(vals, axis=-1, keepdims=True)
        idx = jnp.min(jnp.where(vals == mx, lane_f, big), axis=-1, keepdims=True)
        return mx, idx

    gl = jnp.where((lane >= ROUTE_GRP_LANE) & (lane < ROUTE_GRP_LANE + N_GROUPS), logits, neg)
    gmax, gidx = first_max(gl)
    g_p = 1.0 / jnp.sum(jnp.exp(gl - gmax), axis=-1, keepdims=True)
    lo = (gidx - ROUTE_GRP_LANE) * EXP_PER_GROUP
    el = jnp.where((lane_f >= lo) & (lane_f < lo + EXP_PER_GROUP), logits, neg)
    v1, i1 = first_max(el)
    el2 = jnp.where(lane_f == i1, neg, el)
    v2, i2 = first_max(el2)
    t = jnp.exp(v2 - v1)
    w1 = g_p / (1.0 + t)
    w2 = g_p * t / (1.0 + t)
    oh1 = lane_f == i1
    oh2 = lane_f == i2
    cnt = jnp.where(oh1 | oh2, 1.0, 0.0)
    prefix = jnp.dot(ltri_ref[...], cnt.astype(bf16), preferred_element_type=f32) + carry_ref[...]
    r1 = jnp.sum(jnp.where(oh1, prefix, 0.0), axis=-1, keepdims=True)
    r2 = jnp.sum(jnp.where(oh2, prefix, 0.0), axis=-1, keepdims=True)
    ri = jnp.where(lane == 0, i1, jnp.where(lane == 1, i2, jnp.where(lane == 2, r1, jnp.where(lane == 3, r2, 0.0))))
    rt_ref[...] = ri.T[0:8, :].astype(jnp.int32)
    rw_ref[...] = jnp.where(lane == 0, w1, jnp.where(lane == 1, w2, 0.0))
    new_carry = carry_ref[...] + jnp.sum(cnt, axis=0, keepdims=True)
    carry_ref[...] = new_carry
    cnt_ref[...] = new_carry


def _post_kernel(n_tiles_p, xp_ref, attp_ref, cvp_ref, gap_ref, shfp_ref, scfp_ref,
                 xs_ref, atts_ref, cvs_ref, gas_ref, shfs_ref, scfs_ref, *rest):
    i = pl.program_id(0)
    carry_ref = rest[-1]

    @pl.when(i == 0)
    def _():
        carry_ref[...] = jnp.zeros(carry_ref.shape, f32)

    @pl.when(i < n_tiles_p)
    def _():
        _post_tile(xp_ref, attp_ref, cvp_ref, gap_ref, shfp_ref, scfp_ref, *rest)

    @pl.when(i >= n_tiles_p)
    def _():
        _post_tile(xs_ref, atts_ref, cvs_ref, gas_ref, shfs_ref, scfs_ref, *rest)


def _post_call(x_p, att_p, cv_p, mod_p, x_s, att_s, cv_s, mod_s, tiles_per_batch, ln_ffn, w_out_bf, w_r_bf, b_r, ltri):
    n_p, n_s = x_p.shape[0], x_s.shape[0]
    tp, ts = n_p // TM, n_s // TM
    n = n_p + n_s
    prow = lambda w: pl.BlockSpec((TM, w), lambda i: (jnp.minimum(i, tp - 1), 0))
    srow = lambda w: pl.BlockSpec((TM, w), lambda i: (jnp.maximum(i - tp, 0), 0))
    pmod = lambda sec: pl.BlockSpec((None, 1, D_MODEL), lambda i: (jnp.minimum(i, tp - 1) // tiles_per_batch, 0, sec))
    smod = lambda sec: pl.BlockSpec((TM, D_MODEL), lambda i: (jnp.maximum(i - tp, 0), sec))
    row = lambda w: pl.BlockSpec((TM, w), lambda i: (i, 0))
    const = lambda shape: pl.BlockSpec(shape, lambda i: (0, 0))
    return pl.pallas_call(
        functools.partial(_post_kernel, tp),
        grid=(tp + ts,),
        in_specs=[prow(D_MODEL), prow(ATTN_WIDTH), prow(CONV_WIDTH), pmod(2), pmod(3), pmod(4),
                  srow(D_MODEL), srow(ATTN_WIDTH), srow(CONV_WIDTH), smod(2), smod(3), smod(4),
                  const((1, D_MODEL)), const((D_MODEL, D_MODEL)), const((D_MODEL, LANES)), const((1, LANES)),
                  const((TM, TM))],
        out_specs=[row(D_MODEL), pl.BlockSpec((TM * SLAB, LANES), lambda i: (i, 0)),
                   pl.BlockSpec((8, TM), lambda i: (0, i)), row(LANES), const((1, LANES))],
        out_shape=[jax.ShapeDtypeStruct((n, D_MODEL), f32), jax.ShapeDtypeStruct((n * SLAB, LANES), f32),
                   jax.ShapeDtypeStruct((8, n), jnp.int32), jax.ShapeDtypeStruct((n, LANES), f32),
                   jax.ShapeDtypeStruct((1, LANES), f32)],
        scratch_shapes=[pltpu.VMEM((1, LANES), f32)],
        compiler_params=_cparams(("arbitrary",)),
        name="post",
    )(x_p, att_p, cv_p, mod_p, mod_p, mod_p, x_s, att_s, cv_s, mod_s, mod_s, mod_s,
      ln_ffn, w_out_bf, w_r_bf, b_r, ltri)


def _dispatch_kernel(pos0_ref, pos1_ref, src_ref, xs_ref, sem):
    t0 = pl.program_id(0) * TM

    def issue(r, c):
        t = t0 + r
        _slab_copy(src_ref, r, xs_ref, pos0_ref[t], sem).start(priority=0)
        _slab_copy(src_ref, r, xs_ref, pos1_ref[t], sem).start(priority=1)
        return c

    lax.fori_loop(0, TM, issue, 0, unroll=ROW_DMA_UNROLL)
    tile_copy = pltpu.make_async_copy(src_ref, xs_ref.at[pl.ds(0, TM * SLAB), :], sem)
    tile_copy.wait()
    tile_copy.wait()


def _dispatch_call(pos0, pos1, h2):
    tiles = h2.shape[0] // (TM * SLAB)
    return pl.pallas_call(
        _dispatch_kernel,
        grid_spec=pltpu.PrefetchScalarGridSpec(
            num_scalar_prefetch=2,
            grid=(tiles,),
            in_specs=[pl.BlockSpec((TM * SLAB, LANES), lambda i, p0, p1: (i, 0))],
            out_specs=pl.BlockSpec(memory_space=pl.ANY),
            scratch_shapes=[pltpu.SemaphoreType.DMA(())],
        ),
        out_shape=jax.ShapeDtypeStruct((2 * h2.shape[0], LANES), f32),
        compiler_params=_cparams(("arbitrary",)),
        name="dispatch",
    )(pos0, pos1, h2)


def _tile_chunk_copies(hbm_ref, tile, buf_ref, slot, sem, to_hbm):
    r0 = pl.multiple_of(tile * TE, TE)
    pairs = [(hbm_ref.at[pl.ds(r0, TE), a, :], buf_ref.at[slot, a]) for a in range(SLAB)]
    return [pltpu.make_async_copy(v, h, sem) if to_hbm else pltpu.make_async_copy(h, v, sem) for h, v in pairs]


def _experts_kernel(n_tiles, tile_ref, exp_ref, lo_ref, hi_ref, xs_ref, wg_ref, wu_ref, wd_ref, ys_ref,
                    wg_bf, wu_bf, wd_bf, cur_ref, xbuf, ybuf, xsem, ysem):
    w = pl.program_id(0)
    lo = lo_ref[w]
    hi = hi_ref[w]
    k = tile_ref[w]
    base = k * TE
    slot = k % 2

    @pl.when(w == 0)
    def _():
        cur_ref[0] = -1
        for c in _tile_chunk_copies(xs_ref, 0, xbuf, 0, xsem.at[0], False):
            c.start()

    @pl.when(hi > lo)
    def _():
        first = lo == base
        last = hi == base + TE

        @pl.when(first)
        def _():
            for c in _tile_chunk_copies(xs_ref, k, xbuf, slot, xsem.at[slot], False):
                c.wait()

            @pl.when(k + 1 < n_tiles)
            def _():
                for c in _tile_chunk_copies(xs_ref, k + 1, xbuf, 1 - slot, xsem.at[1 - slot], False):
                    c.start()

            @pl.when(k >= 2)
            def _():
                for c in _tile_chunk_copies(ys_ref, k - 2, ybuf, slot, ysem.at[slot], True):
                    c.wait()

        e = exp_ref[w]

        @pl.when(cur_ref[0] != e)
        def _():
            wg_bf[...] = wg_ref[...].astype(bf16)
            wu_bf[...] = wu_ref[...].astype(bf16)
            wd_bf[...] = wd_ref[...].astype(bf16)
            cur_ref[0] = e

        x = jnp.concatenate([xbuf[slot, a] for a in range(SLAB)], axis=1).astype(bf16)
        g = jnp.dot(x, wg_bf[...], preferred_element_type=f32)
        u = jnp.dot(x, wu_bf[...], preferred_element_type=f32)
        hid = (_silu(g) * u).astype(bf16)
        y = jnp.dot(hid, wd_bf[...], preferred_element_type=f32)
        row = base + lax.broadcasted_iota(jnp.int32, (TE, LANES), 0)
        mine = (row >= lo) & (row < hi)

        @pl.when(first)
        def _():
            for a in range(SLAB):
                ybuf[slot, a] = jnp.where(mine, y[:, a * LANES:(a + 1) * LANES], 0.0)

        @pl.when(jnp.logical_not(first))
        def _():
            for a in range(SLAB):
                ybuf[slot, a] = jnp.where(mine, y[:, a * LANES:(a + 1) * LANES], ybuf[slot, a])

        @pl.when(last)
        def _():
            for c in _tile_chunk_copies(ys_ref, k, ybuf, slot, ysem.at[slot], True):
                c.start()

    @pl.when(w == pl.num_programs(0) - 1)
    def _():
        for kk in (n_tiles - 2, n_tiles - 1):
            for c in _tile_chunk_copies(ys_ref, kk, ybuf, kk % 2, ysem.at[kk % 2], True):
                c.wait()


def _experts_call(tile_id, exp_id, seg_lo, seg_hi, xs, w_gate, w_up, w_down):
    n_items = tile_id.shape[0]
    n_rows = xs.shape[0] // SLAB
    n_tiles = n_rows // TE
    assert n_tiles >= 2
    idx = lambda w, t, e, lo, hi: (e[w], 0, 0)
    ys = pl.pallas_call(
        functools.partial(_experts_kernel, n_tiles),
        grid_spec=pltpu.PrefetchScalarGridSpec(
            num_scalar_prefetch=4,
            grid=(n_items,),
            in_specs=[pl.BlockSpec(memory_space=pl.ANY),
                      pl.BlockSpec((None, D_MODEL, D_EXPERT), idx),
                      pl.BlockSpec((None, D_MODEL, D_EXPERT), idx),
                      pl.BlockSpec((None, D_EXPERT, D_MODEL), idx)],
            out_specs=pl.BlockSpec(memory_space=pl.ANY),
            scratch_shapes=[pltpu.VMEM((D_MODEL, D_EXPERT), bf16), pltpu.VMEM((D_MODEL, D_EXPERT), bf16),
                            pltpu.VMEM((D_EXPERT, D_MODEL), bf16), pltpu.SMEM((1,), jnp.int32),
                            pltpu.VMEM((2, SLAB, TE, LANES), f32), pltpu.VMEM((2, SLAB, TE, LANES), f32),
                            pltpu.SemaphoreType.DMA((2,)), pltpu.SemaphoreType.DMA((2,))],
        ),
        out_shape=jax.ShapeDtypeStruct((n_rows, SLAB, LANES), f32),
        compiler_params=_cparams(("arbitrary",)),
        name="experts",
    )(tile_id, exp_id, seg_lo, seg_hi, xs.reshape(n_rows, SLAB, LANES), w_gate, w_up, w_down)
    return ys.reshape(xs.shape)


def _combine_kernel(n_tiles_p, pos0_ref, pos1_ref, x1_ref, rw_ref, gfp_ref, gfs_ref, ln_ref, ys_ref,
                    op_ref, os_ref, ybuf, sems):
    i = pl.program_id(0)
    n = pl.num_programs(0)
    half = 2 * TM

    def issue(tile, buf):
        def body(r, c):
            t = tile * TM + r
            _slab_copy(ys_ref, pos0_ref[t], ybuf, buf * half + r, sems.at[buf]).start(priority=0)
            _slab_copy(ys_ref, pos1_ref[t], ybuf, buf * half + TM + r, sems.at[buf]).start(priority=1)
            return c

        lax.fori_loop(0, TM, body, 0, unroll=ROW_DMA_UNROLL)

    @pl.when(i == 0)
    def _():
        issue(0, 0)

    @pl.when(i + 1 < n)
    def _():
        issue(i + 1, (i + 1) % 2)

    buf = i % 2
    tile_copy = pltpu.make_async_copy(ys_ref.at[pl.ds(0, TM * SLAB), :], ybuf.at[pl.ds(0, TM * SLAB), :], sems.at[buf])
    tile_copy.wait()
    tile_copy.wait()
    rw = rw_ref[...]
    moe = rw[:, 0:1] * _load_slabs(ybuf, buf * half, TM) + rw[:, 1:2] * _load_slabs(ybuf, buf * half + TM, TM)

    def finish(gf_ref, o_ref):
        x2 = x1_ref[...] + gf_ref[...] * moe
        o_ref[...] = _rms(x2, ln_ref[...], NORM_EPS)

    @pl.when(i < n_tiles_p)
    def _():
        finish(gfp_ref, op_ref)

    @pl.when(i >= n_tiles_p)
    def _():
        finish(gfs_ref, os_ref)


def _combine_call(pos0, pos1, x1, rw, mod_p, mod_s, tiles_per_batch, ln_final, ys):
    n = x1.shape[0]
    n_s = mod_s.shape[0]
    tp, ts = (n - n_s) // TM, n_s // TM
    pidx = lambda i: jnp.minimum(i, tp - 1)
    sidx = lambda i: jnp.maximum(i - tp, 0)
    return pl.pallas_call(
        functools.partial(_combine_kernel, tp),
        grid_spec=pltpu.PrefetchScalarGridSpec(
            num_scalar_prefetch=2,
            grid=(tp + ts,),
            in_specs=[pl.BlockSpec((TM, D_MODEL), lambda i, p0, p1: (i, 0)),
                      pl.BlockSpec((TM, LANES), lambda i, p0, p1: (i, 0)),
                      pl.BlockSpec((None, 1, D_MODEL), lambda i, p0, p1: (pidx(i) // tiles_per_batch, 0, 5)),
                      pl.BlockSpec((TM, D_MODEL), lambda i, p0, p1: (sidx(i), 5)),
                      pl.BlockSpec((1, D_MODEL), lambda i, p0, p1: (0, 0)),
                      pl.BlockSpec(memory_space=pl.ANY)],
            out_specs=[pl.BlockSpec((TM, D_MODEL), lambda i, p0, p1: (pidx(i), 0)),
                       pl.BlockSpec((TM, D_MODEL), lambda i, p0, p1: (sidx(i), 0))],
            scratch_shapes=[pltpu.VMEM((2 * 2 * TM * SLAB, LANES), f32), pltpu.SemaphoreType.DMA((2,))],
        ),
        out_shape=[jax.ShapeDtypeStruct((n - n_s, D_MODEL), f32), jax.ShapeDtypeStruct((n_s, D_MODEL), f32)],
        compiler_params=_cparams(("arbitrary",)),
        name="combine",
    )(pos0, pos1, x1, rw, mod_p, mod_s, ln_final, ys)


def _rope_tables(pos):
    inv = 1.0 / (ROPE_THETA ** (np.arange(0, HEAD_DIM, 2, dtype=np.float64) / HEAD_DIM))
    ang = np.asarray(pos, np.float64)[:, None] * inv[None, :]
    ang = np.concatenate([ang, ang], axis=-1)
    sign = np.where(np.arange(HEAD_DIM) < HEAD_DIM // 2, -1.0, 1.0)
    cos = np.tile(np.cos(ang), (1, N_SUB)).astype(np.float32)
    sin_signed = np.tile(np.sin(ang) * sign[None, :], (1, N_SUB)).astype(np.float32)
    return jnp.asarray(cos), jnp.asarray(sin_signed)


def _segments(counts, n_rows):
    n_tiles = n_rows // TE
    offs = jnp.concatenate([jnp.zeros((1,), jnp.int32), jnp.cumsum(counts)[:-1].astype(jnp.int32)])
    tiles = jnp.arange(n_tiles, dtype=jnp.int32) * TE
    rank_t = jnp.arange(n_tiles, dtype=jnp.int32) + jnp.sum(offs[None, :] < tiles[:, None], axis=1).astype(jnp.int32)
    rank_o = jnp.arange(N_EXPERTS, dtype=jnp.int32) + jnp.minimum(offs // TE + 1, n_tiles)
    vals = jnp.concatenate([tiles, offs])
    ranks = jnp.concatenate([rank_t, rank_o])
    n_items = n_tiles + N_EXPERTS
    w = jnp.arange(n_items, dtype=jnp.int32)
    seg_lo = jnp.sum(jnp.where(ranks[None, :] == w[:, None], vals[None, :], 0), axis=1).astype(jnp.int32)
    seg_hi = jnp.concatenate([seg_lo[1:], jnp.full((1,), n_rows, jnp.int32)])
    tile_id = jnp.minimum(seg_lo // TE, n_tiles - 1)
    exp_id = jnp.sum(offs[None, :] <= seg_lo[:, None], axis=1).astype(jnp.int32) - 1
    return offs, tile_id, exp_id, seg_lo, seg_hi


def kernel(x_prompt, x_sample, cache_k, cache_v, state_conv, page_table, c_prompt, c_sample, w_ada, b_ada, ln_mix, w_in, lam_q1, lam_k1, lam_q2, lam_k2, subln_g, w_conv, w_out, ln_ffn, w_router_grp, b_router_grp, w_router_exp, b_router_exp, w_gate, w_up, w_down, ln_final):
    B, S, _ = x_prompt.shape
    DB, L, _ = x_sample.shape
    n_phys = cache_k.shape[1]
    n_pages = page_table.shape[1]
    past = n_pages * PAGE
    n_p = B * S
    n_s = DB * L
    n_tok = n_p + n_s

    w_in_bf = w_in[0].astype(bf16)
    w_out_bf = w_out[0].astype(bf16)
    w_r = jnp.zeros((D_MODEL, LANES), f32)
    w_r = w_r.at[:, :N_EXPERTS].set(w_router_exp[0]).at[:, ROUTE_GRP_LANE:ROUTE_GRP_LANE + N_GROUPS].set(w_router_grp[0])
    b_r = jnp.zeros((1, LANES), f32)
    b_r = b_r.at[0, :N_EXPERTS].set(b_router_exp[0]).at[0, ROUTE_GRP_LANE:ROUTE_GRP_LANE + N_GROUPS].set(b_router_grp[0])
    w_r_bf = w_r.astype(bf16)
    cos_p, sin_p = _rope_tables(np.arange(S))
    cos_s, sin_s = _rope_tables(past + np.repeat(np.arange(L), DB))
    ltri = jnp.asarray(np.tril(np.ones((TM, TM), np.float32), -1), bf16)

    mod = _mod_call(jnp.concatenate([c_prompt, c_sample], axis=0), w_ada[0], b_ada)
    mod_p = mod[:B].reshape(B, 1, 6 * D_MODEL)
    mod_s = jnp.tile(mod[B:], (L, 1))

    qt_p, kt_p, kb_p, v4_p, vt_p, cv_p, st_p = _inproj_p_call(x_prompt, mod_p, ln_mix, w_in_bf, cos_p, sin_p, w_conv[0])
    xs_l = x_sample.transpose(1, 0, 2).reshape(n_s, D_MODEL)
    st_in = state_conv[0].transpose(1, 0, 2)
    q_s, kf_s, vf_s, cv_s, st_s = _inproj_s_call(xs_l, mod_s[:, 0:D_MODEL], mod_s[:, D_MODEL:2 * D_MODEL], ln_mix,
                                                 w_in_bf, cos_s, sin_s, w_conv[0], st_in)
    to_b = lambda a: a.reshape(L, DB, -1).transpose(1, 0, 2)
    pad8 = lambda a: jnp.pad(a, ((0, 0), (0, 8 - L), (0, 0)))
    kt = jnp.transpose(cache_k[0], (0, 2, 3, 1)).reshape(n_phys, N_SUB * HEAD_DIM, PAGE)
    v2 = cache_v[0].reshape(n_phys, PAGE * N_HEADS, V_DIM)
    att_p, att_s_b = _attn_call(page_table.reshape(-1), qt_p, kb_p, vt_p, lam_q1, lam_k1, lam_q2, lam_k2, subln_g,
                                to_b(q_s), pad8(to_b(kf_s)), pad8(to_b(vf_s)), kt, v2, n_pages)
    att_s = att_s_b.transpose(1, 0, 2).reshape(n_s, ATTN_WIDTH)

    x1, h2, rt, rw, cnt = _post_call(x_prompt.reshape(n_p, D_MODEL), att_p.reshape(n_p, ATTN_WIDTH),
                                     cv_p.reshape(n_p, CONV_WIDTH), mod_p, xs_l, att_s, cv_s, mod_s, S // TM,
                                     ln_ffn, w_out_bf, w_r_bf, b_r, ltri)

    counts = cnt[0, :N_EXPERTS].astype(jnp.int32)
    offs, tile_id, exp_id, seg_lo, seg_hi = _segments(counts, 2 * n_tok)
    e_col = jnp.arange(N_EXPERTS, dtype=jnp.int32)[:, None]
    start = lambda e_row: jnp.sum(jnp.where(e_row[None, :] == e_col, offs[:, None], 0), axis=0)
    pos0 = start(rt[0]) + rt[2]
    pos1 = start(rt[1]) + rt[3]

    xs_sorted = _dispatch_call(pos0, pos1, h2)
    ys = _experts_call(tile_id, exp_id, seg_lo, seg_hi, xs_sorted, w_gate[0], w_up[0], w_down[0])
    y_p, y_s = _combine_call(pos0, pos1, x1, rw, mod_p, mod_s, S // TM, ln_final.reshape(1, D_MODEL), ys)

    from_l = lambda a: a.reshape(L, DB, -1).transpose(1, 0, 2)
    y_prompt = y_p.reshape(B, S, D_MODEL)
    y_sample = from_l(y_s)
    k_prompt = kt_p.reshape(B, N_SUB, HEAD_DIM, S).transpose(0, 3, 1, 2)[None]
    v_prompt = v4_p.reshape(1, B, S, N_HEADS, V_DIM)
    conv_prompt = st_p[None]
    k_sample = from_l(kf_s).reshape(1, DB, L, N_SUB, HEAD_DIM)
    v_sample = from_l(vf_s).reshape(1, DB, L, N_HEADS, V_DIM)
    conv_sample = st_s.transpose(1, 0, 2)[None]
    return (y_prompt, y_sample, k_prompt, v_prompt, conv_prompt, k_sample, v_sample, conv_sample)
```

```python
import functools
import math

import jax
import jax.numpy as jnp
import numpy as np
from jax import lax
from jax.experimental import pallas as pl
from jax.experimental.pallas import tpu as pltpu

D_MODEL = 1024
ATTN_WIDTH = 512
CONV_WIDTH = 512
N_HEADS = 4
N_SUB = 8
HEAD_DIM = 64
V_DIM = 2 * HEAD_DIM
CONV_K = 3
ROPE_THETA = 10000.0
N_GROUPS = 4
EXP_PER_GROUP = 8
N_EXPERTS = 32
D_EXPERT = 256
NORM_EPS = 1e-6
SUBLN_EPS = 1e-5
LAM_INIT = 0.8 - 0.6 * math.exp(-0.3 * 0)
LOG2E = math.log2(math.e)
PAGE = 128
LANES = 128
ROUTE_GRP_LANE = 32

TM = 512
TQ = 512
TE = 256
ROW_DMA_UNROLL = 8
ONES_ROWS = 16
VMEM_LIMIT = 56 * 1024 * 1024

f32 = jnp.float32
bf16 = jnp.bfloat16


def _cparams(sem):
    return pltpu.CompilerParams(dimension_semantics=sem, vmem_limit_bytes=VMEM_LIMIT)


def _rms(x, g, eps):
    return x * lax.rsqrt(jnp.mean(x * x, axis=-1, keepdims=True) + eps) * g


def _silu(x):
    return x * (1.0 / (1.0 + jnp.exp(-x)))


def _mod_kernel(c_ref, w_ref, b_ref, o_ref):
    a = _silu(c_ref[...]).astype(bf16)
    o_ref[...] = jnp.dot(a, w_ref[...].astype(bf16), preferred_element_type=f32) + b_ref[...]


def _mod_call(c_all, w_ada, b_ada):
    n = c_all.shape[0]
    return pl.pallas_call(
        _mod_kernel,
        grid=(6,),
        in_specs=[pl.BlockSpec((n, D_MODEL), lambda j: (0, 0)),
                  pl.BlockSpec((D_MODEL, D_MODEL), lambda j: (0, j)),
                  pl.BlockSpec((1, D_MODEL), lambda j: (0, j))],
        out_specs=pl.BlockSpec((n, D_MODEL), lambda j: (0, j)),
        out_shape=jax.ShapeDtypeStruct((n, 6 * D_MODEL), f32),
        compiler_params=_cparams(("arbitrary",)),
        name="mod",
    )(c_all, w_ada, b_ada)


def _rope(t, cos, sin_signed, lo_mask):
    n = t.shape[-1]
    rot = jnp.where(lo_mask, pltpu.roll(t, n - HEAD_DIM // 2, 1), pltpu.roll(t, HEAD_DIM // 2, 1))
    return t * cos + rot * sin_signed


def _inproj_common(x, sh, sc, ln, w_ref, cos, sin_signed):
    h = (_rms(x, ln, NORM_EPS) * (1.0 + sc) + sh).astype(bf16)

    def sec(i):
        return jnp.dot(h, w_ref[:, i * 512:(i + 1) * 512], preferred_element_type=f32)

    lane = lax.broadcasted_iota(jnp.int32, (x.shape[0], 512), 1)
    lo_mask = (lane % HEAD_DIM) < (HEAD_DIM // 2)
    q = _rope(sec(0), cos, sin_signed, lo_mask) * (HEAD_DIM ** -0.5 * LOG2E)
    k = _rope(sec(1), cos, sin_signed, lo_mask)
    v = sec(2)
    bg = sec(3)
    cu = sec(4) * sec(5)
    return q, k, v, bg, cu


def _inproj_p_kernel(x_ref, sh_ref, sc_ref, ln_ref, w_ref, cos_ref, sin_ref, wc_ref,
                     qt_ref, kt_ref, kb_ref, v4_ref, vt_ref, cv_ref, st_ref, carry_ref):
    s = pl.program_id(0)
    b = pl.program_id(1)
    q, k, v, bg, cu = _inproj_common(x_ref[...], sh_ref[...], sc_ref[...], ln_ref[...], w_ref,
                                     cos_ref[...], sin_ref[...])
    qt_ref[...] = q.T.astype(bf16)
    kt_ref[...] = k.T
    kb_ref[...] = k.astype(bf16)
    vt_ref[...] = v.T.astype(bf16)
    for h in range(N_HEADS):
        v4_ref[pl.ds(h, v.shape[0], stride=N_HEADS), :] = v[:, h * V_DIM:(h + 1) * V_DIM]
    tm = cu.shape[0]
    prev = jnp.where(s > 0, carry_ref[b], 0.0)
    row = lax.broadcasted_iota(jnp.int32, cu.shape, 0)
    cu1 = jnp.where(row == 0, prev[1:2], pltpu.roll(cu, 1, 0))
    cu2 = jnp.where(row == 0, prev[0:1], jnp.where(row == 1, prev[1:2], pltpu.roll(cu, 2, 0)))
    wc = wc_ref[...]
    conv = wc[0:1] * cu2 + wc[1:2] * cu1 + wc[2:3] * cu
    cv_ref[...] = (bg * conv).astype(bf16)
    last2 = cu[tm - 2:tm]
    carry_ref[b, 0:2, :] = last2
    st_ref[b] = last2


def _inproj_p_call(x, mod3, ln_mix, w_in_bf, cos, sin_signed, w_conv):
    B, S, _ = x.shape
    ns = S // TM
    row = lambda s, b: (b, s, 0)
    col = lambda s, b: (b, 0, s)
    rows = lambda dt: jax.ShapeDtypeStruct((B, S, 512), dt)
    cols = lambda dt: jax.ShapeDtypeStruct((B, 512, S), dt)
    row_spec = pl.BlockSpec((None, TM, 512), row)
    col_spec = pl.BlockSpec((None, 512, TM), col)
    return pl.pallas_call(
        _inproj_p_kernel,
        grid=(ns, B),
        in_specs=[pl.BlockSpec((None, TM, D_MODEL), row),
                  pl.BlockSpec((None, 1, D_MODEL), lambda s, b: (b, 0, 0)),
                  pl.BlockSpec((None, 1, D_MODEL), lambda s, b: (b, 0, 1)),
                  pl.BlockSpec((1, D_MODEL), lambda s, b: (0, 0)),
                  pl.BlockSpec((D_MODEL, 3072), lambda s, b: (0, 0)),
                  pl.BlockSpec((TM, 512), lambda s, b: (s, 0)),
                  pl.BlockSpec((TM, 512), lambda s, b: (s, 0)),
                  pl.BlockSpec((CONV_K, CONV_WIDTH), lambda s, b: (0, 0))],
        out_specs=[col_spec, col_spec, row_spec, pl.BlockSpec((None, N_HEADS * TM, V_DIM), row),
                   pl.BlockSpec((None, None, 512, TM), lambda s, b: (b, s, 0, 0)), row_spec,
                   pl.BlockSpec((B, 2, CONV_WIDTH), lambda s, b: (0, 0, 0))],
        out_shape=[cols(bf16), cols(f32), rows(bf16), jax.ShapeDtypeStruct((B, N_HEADS * S, V_DIM), f32),
                   jax.ShapeDtypeStruct((B, ns, 512, TM), bf16), rows(bf16),
                   jax.ShapeDtypeStruct((B, 2, CONV_WIDTH), f32)],
        scratch_shapes=[pltpu.VMEM((B, 8, CONV_WIDTH), f32)],
        compiler_params=_cparams(("arbitrary", "arbitrary")),
        name="inproj_p",
    )(x, mod3, mod3, ln_mix, w_in_bf, cos, sin_signed, w_conv)


def _inproj_s_kernel(x_ref, sh_ref, sc_ref, ln_ref, w_ref, cos_ref, sin_ref, wc_ref, st_in_ref,
                     q_ref, kf_ref, vf_ref, cv_ref, st_ref):
    q, k, v, bg, cu = _inproj_common(x_ref[...], sh_ref[...], sc_ref[...], ln_ref[...], w_ref,
                                     cos_ref[...], sin_ref[...])
    q_ref[...] = q.astype(bf16)
    kf_ref[...] = k
    vf_ref[...] = v
    nb = st_in_ref.shape[1]
    st0 = st_in_ref[0]
    st1 = st_in_ref[1]
    cu1 = jnp.concatenate([st1, cu[:3 * nb]], axis=0)
    cu2 = jnp.concatenate([st0, st1, cu[:2 * nb]], axis=0)
    wc = wc_ref[...]
    conv = wc[0:1] * cu2 + wc[1:2] * cu1 + wc[2:3] * cu
    cv_ref[...] = (bg * conv).astype(bf16)
    st_ref[0] = cu[2 * nb:3 * nb]
    st_ref[1] = cu[3 * nb:4 * nb]


def _inproj_s_call(x, sh, sc, ln_mix, w_in_bf, cos, sin_signed, w_conv, st_in):
    n = x.shape[0]
    nb = st_in.shape[1]
    full = lambda shape: pl.BlockSpec(shape, lambda i: (0,) * len(shape))
    return pl.pallas_call(
        _inproj_s_kernel,
        grid=(1,),
        in_specs=[full((n, D_MODEL)), full((n, D_MODEL)), full((n, D_MODEL)), full((1, D_MODEL)),
                  full((D_MODEL, 3072)), full((n, 512)), full((n, 512)), full((CONV_K, CONV_WIDTH)),
                  full((2, nb, CONV_WIDTH))],
        out_specs=[full((n, 512))] * 4 + [full((2, nb, CONV_WIDTH))],
        out_shape=[jax.ShapeDtypeStruct((n, 512), bf16), jax.ShapeDtypeStruct((n, 512), f32),
                   jax.ShapeDtypeStruct((n, 512), f32), jax.ShapeDtypeStruct((n, 512), bf16),
                   jax.ShapeDtypeStruct((2, nb, CONV_WIDTH), f32)],
        compiler_params=_cparams(("arbitrary",)),
        name="inproj_s",
    )(x, sh, sc, ln_mix, w_in_bf, cos, sin_signed, w_conv, st_in)


def _lam(lq1, lk1, lq2, lk2):
    a = jnp.sum(lq1 * lk1, axis=-1, keepdims=True)
    b = jnp.sum(lq2 * lk2, axis=-1, keepdims=True)
    return jnp.exp(a) - jnp.exp(b) + LAM_INIT


def _attn_p_body(i, qt_ref, k_ref, vt_ref, lam, g_ref, o_ref, m_ref, acc_ref):
    tq = qt_ref.shape[1]
    tk = tq
    m_ref[...] = jnp.full(m_ref.shape, -jnp.inf, f32)
    acc_ref[...] = jnp.zeros(acc_ref.shape, f32)

    def step(j, masked):
        k0 = pl.multiple_of(j * tk, tk)
        qt = qt_ref[...]
        row = lax.broadcasted_iota(jnp.int32, qt.shape, 0)
        zero = jnp.zeros_like(qt)
        q2t = jnp.concatenate([jnp.where(row < HEAD_DIM, qt, zero), jnp.where(row >= HEAD_DIM, qt, zero)], axis=1)
        st = jnp.dot(k_ref[pl.ds(k0, tk), :], q2t, preferred_element_type=f32)
        if masked:
            kpos = lax.broadcasted_iota(jnp.int32, st.shape, 0)
            c = lax.broadcasted_iota(jnp.int32, st.shape, 1)
            st = jnp.where(kpos <= jnp.where(c >= tq, c - tq, c), st, -jnp.inf)
        m_prev = m_ref[...]
        m_new = jnp.maximum(m_prev, jnp.max(st, axis=0, keepdims=True))
        alpha = jnp.exp2(m_prev - m_new)
        pt = jnp.exp2(st - m_new).astype(bf16)
        vt1 = jnp.concatenate([vt_ref[j], jnp.ones((ONES_ROWS, tk), bf16)], axis=0)
        acc_ref[...] = alpha * acc_ref[...] + jnp.dot(vt1, pt, preferred_element_type=f32)
        m_ref[...] = m_new

    def below_diagonal(j, c):
        step(j, False)
        return c

    lax.fori_loop(0, i, below_diagonal, 0)
    step(i, True)
    ot = acc_ref[0:V_DIM, :] / acc_ref[V_DIM:V_DIM + 1, :]
    dt = ot[:, :tq] - lam * ot[:, tq:]
    ms = jnp.mean(dt * dt, axis=0, keepdims=True)
    nt = dt * lax.rsqrt(ms + SUBLN_EPS) * g_ref[...] * (1.0 - LAM_INIT)
    o_ref[...] = nt.T.astype(bf16)


def _attn_s_body(q_ref, kn_ref, vn_ref, lam, g_ref, kt_refs, v_refs, o_ref):
    n_pages = len(kt_refs)
    nq = q_ref.shape[0]
    nr = nq * N_SUB
    qf = q_ref[...].astype(f32)
    sub = lax.broadcasted_iota(jnp.int32, (N_SUB, ATTN_WIDTH), 0)
    col = lax.broadcasted_iota(jnp.int32, (N_SUB, ATTN_WIDTH), 1)
    diag = (col // HEAD_DIM) == sub
    qbd = jnp.concatenate([jnp.where(diag, jnp.broadcast_to(qf[a:a + 1], (N_SUB, ATTN_WIDTH)), 0.0)
                           for a in range(nq)], axis=0).astype(bf16)
    kt_all = jnp.concatenate([kt_refs[p][...].astype(bf16) for p in range(n_pages)], axis=1)
    s_past = jnp.dot(qbd, kt_all, preferred_element_type=f32)
    s_new = lax.dot_general(qbd, kn_ref[...].astype(bf16), (((1,), (1,)), ((), ())),
                            preferred_element_type=f32)
    r = lax.broadcasted_iota(jnp.int32, s_new.shape, 0)
    c = lax.broadcasted_iota(jnp.int32, s_new.shape, 1)
    s_new = jnp.where((c <= r // N_SUB) & (c < nq), s_new, -jnp.inf)
    m = jnp.maximum(jnp.max(s_past, axis=-1, keepdims=True), jnp.max(s_new, axis=-1, keepdims=True))
    p_past = jnp.exp2(s_past - m)
    p_new = jnp.exp2(s_new - m)
    l = jnp.sum(p_past, axis=-1, keepdims=True) + jnp.sum(p_new, axis=-1, keepdims=True)
    p_bf = p_past.astype(bf16)
    vn = vn_ref[...]
    rowhead = (lax.broadcasted_iota(jnp.int32, (nr, V_DIM), 0) % N_SUB) // 2
    o = jnp.zeros((nr, V_DIM), f32)
    for h in range(N_HEADS):
        v_all = jnp.concatenate([v_refs[p][pl.ds(h, PAGE, stride=N_HEADS), :].astype(bf16)
                                 for p in range(n_pages)], axis=0)
        acc = jnp.dot(p_bf, v_all, preferred_element_type=f32)
        for a in range(nq):
            acc = acc + p_new[:, a:a + 1] * vn[a:a + 1, h * V_DIM:(h + 1) * V_DIM]
        o = jnp.where(rowhead == h, acc, o)
    o = o / l
    d = o - lam * pltpu.roll(o, nr - 1, 0)
    d = _rms(d, g_ref[...], SUBLN_EPS) * (1.0 - LAM_INIT)
    for a in range(nq):
        rowv = jnp.concatenate([d[a * N_SUB + 2 * h:a * N_SUB + 2 * h + 1] for h in range(N_HEADS)], axis=1)
        o_ref[a:a + 1, :] = rowv.astype(o_ref.dtype)


def _attn_kernel(n_pages, pt_ref, qt_ref, k_ref, vt_ref, lq1_ref, lk1_ref, lq2_ref, lk2_ref, gcol_ref,
                 q_ref, kn_ref, vn_ref, grow_ref, kt_hbm, v_hbm, op_ref, os_ref, m_ref, acc_ref, kbuf, vbuf, sems):
    t = (pl.program_id(0) * pl.num_programs(1) + pl.program_id(1)) * pl.num_programs(2) + pl.program_id(2)
    n_steps = pl.num_programs(0) * pl.num_programs(1) * pl.num_programs(2)
    slot = t % 2

    def fetch(seq, dst):
        for p in range(n_pages):
            page = pt_ref[seq * n_pages + p]
            pltpu.make_async_copy(kt_hbm.at[page], kbuf.at[dst, p], sems.at[dst]).start()
            pltpu.make_async_copy(v_hbm.at[page], vbuf.at[dst, p], sems.at[dst]).start()

    @pl.when(t == 0)
    def _():
        fetch(0, 0)

    pltpu.make_async_copy(kt_hbm.at[pl.ds(0, n_pages)], kbuf.at[slot], sems.at[slot]).wait()
    pltpu.make_async_copy(v_hbm.at[pl.ds(0, n_pages)], vbuf.at[slot], sems.at[slot]).wait()

    @pl.when(t + 1 < n_steps)
    def _():
        fetch(t + 1, 1 - slot)

    lam = _lam(lq1_ref[...], lk1_ref[...], lq2_ref[...], lk2_ref[...])
    _attn_p_body(pl.program_id(2), qt_ref, k_ref, vt_ref, lam, gcol_ref, op_ref, m_ref, acc_ref)
    _attn_s_body(q_ref, kn_ref, vn_ref, lam, grow_ref, [kbuf.at[slot, p] for p in range(n_pages)],
                 [vbuf.at[slot, p] for p in range(n_pages)], os_ref)


def _attn_call(pt_flat, qt, k, vt, lq1, lk1, lq2, lk2, g, q_s, kn_s, vn_s, kt, v2, n_pages):
    B, S, _ = k.shape
    nq_tiles = S // TQ
    nb, nq, _ = q_s.shape
    assert nb == B * N_HEADS * nq_tiles, "one sample sequence per prompt grid step"
    seq = lambda b, h, i: (b * N_HEADS + h) * nq_tiles + i
    small = lambda n: pl.BlockSpec((1, n), lambda b, h, i, pt: (0, 0))
    page_rows = kt.shape[1]
    return pl.pallas_call(
        functools.partial(_attn_kernel, n_pages),
        grid_spec=pltpu.PrefetchScalarGridSpec(
            num_scalar_prefetch=1,
            grid=(B, N_HEADS, nq_tiles),
            in_specs=[pl.BlockSpec((None, V_DIM, TQ), lambda b, h, i, pt: (b, h, i)),
                      pl.BlockSpec((None, S, V_DIM), lambda b, h, i, pt: (b, 0, h)),
                      pl.BlockSpec((None, nq_tiles, V_DIM, TQ), lambda b, h, i, pt: (b, 0, h, 0)),
                      small(HEAD_DIM), small(HEAD_DIM), small(HEAD_DIM), small(HEAD_DIM),
                      pl.BlockSpec((V_DIM, 1), lambda b, h, i, pt: (0, 0)),
                      pl.BlockSpec((None, nq, ATTN_WIDTH), lambda b, h, i, pt: (seq(b, h, i), 0, 0)),
                      pl.BlockSpec((None, 8, ATTN_WIDTH), lambda b, h, i, pt: (seq(b, h, i), 0, 0)),
                      pl.BlockSpec((None, 8, ATTN_WIDTH), lambda b, h, i, pt: (seq(b, h, i), 0, 0)),
                      small(V_DIM), pl.BlockSpec(memory_space=pl.ANY), pl.BlockSpec(memory_space=pl.ANY)],
            out_specs=[pl.BlockSpec((None, TQ, V_DIM), lambda b, h, i, pt: (b, i, h)),
                       pl.BlockSpec((None, nq, ATTN_WIDTH), lambda b, h, i, pt: (seq(b, h, i), 0, 0))],
            scratch_shapes=[pltpu.VMEM((1, 2 * TQ), f32), pltpu.VMEM((V_DIM + ONES_ROWS, 2 * TQ), f32),
                            pltpu.VMEM((2, n_pages, page_rows, PAGE), f32),
                            pltpu.VMEM((2, n_pages, page_rows, PAGE), f32), pltpu.SemaphoreType.DMA((2,))],
        ),
        out_shape=[jax.ShapeDtypeStruct((B, S, ATTN_WIDTH), bf16), jax.ShapeDtypeStruct((nb, nq, ATTN_WIDTH), bf16)],
        compiler_params=_cparams(("arbitrary",) * 3),
        name="attn",
    )(pt_flat, qt, k, vt, lq1, lk1, lq2, lk2, g.reshape(V_DIM, 1), q_s, kn_s, vn_s, g, kt, v2)


SLAB = D_MODEL // LANES


def _store_slabs(ref, row0, x):
    n = x.shape[0]
    for a in range(SLAB):
        ref[pl.ds(SLAB * row0 + a, n, stride=SLAB), :] = x[:, a * LANES:(a + 1) * LANES]


def _load_slabs(ref, row0, n):
    return jnp.concatenate([ref[pl.ds(SLAB * row0 + a, n, stride=SLAB), :] for a in range(SLAB)], axis=1)


def _slab_copy(src_ref, src_row, dst_ref, dst_row, sem):
    src = src_ref.at[pl.ds(pl.multiple_of(src_row * SLAB, SLAB), SLAB), :]
    dst = dst_ref.at[pl.ds(pl.multiple_of(dst_row * SLAB, SLAB), SLAB), :]
    return pltpu.make_async_copy(src, dst, sem)


def _post_tile(x_ref, att_ref, cv_ref, ga_ref, shf_ref, scf_ref, ln_ref, wo_ref, wr_ref, br_ref, ltri_ref,
               x1_ref, h2_ref, rt_ref, rw_ref, cnt_ref, carry_ref):
    mix = (jnp.dot(att_ref[...], wo_ref[0:ATTN_WIDTH, :], preferred_element_type=f32)
           + jnp.dot(cv_ref[...], wo_ref[ATTN_WIDTH:, :], preferred_element_type=f32))
    x1 = x_ref[...] + ga_ref[...] * mix
    x1_ref[...] = x1
    h2 = _rms(x1, ln_ref[...], NORM_EPS) * (1.0 + scf_ref[...]) + shf_ref[...]
    _store_slabs(h2_ref, 0, h2)
    logits = jnp.dot(h2.astype(bf16), wr_ref[...], preferred_element_type=f32) + br_ref[...]
    lane = lax.broadcasted_iota(jnp.int32, logits.shape, 1)
    lane_f = lane.astype(f32)
    big = jnp.float32(1e9)
    neg = -jnp.inf

    def first_max(vals):
        mx = jnp.max(vals, axis=-1, keepdims=True)
        idx = jnp.min(jnp.where(vals == mx, lane_f, big), axis=-1, keepdims=True)
        return mx, idx

    gl = jnp.where((lane >= ROUTE_GRP_LANE) & (lane < ROUTE_GRP_LANE + N_GROUPS), logits, neg)
    gmax, gidx = first_max(gl)
    g_p = 1.0 / jnp.sum(jnp.exp(gl - gmax), axis=-1, keepdims=True)
    lo = (gidx - ROUTE_GRP_LANE) * EXP_PER_GROUP
    el = jnp.where((lane_f >= lo) & (lane_f < lo + EXP_PER_GROUP), logits, neg)
    v1, i1 = first_max(el)
    el2 = jnp.where(lane_f == i1, neg, el)
    v2, i2 = first_max(el2)
    t = jnp.exp(v2 - v1)
    w1 = g_p / (1.0 + t)
    w2 = g_p * t / (1.0 + t)
    oh1 = lane_f == i1
    oh2 = lane_f == i2
    cnt = jnp.where(oh1 | oh2, 1.0, 0.0)
    prefix = jnp.dot(ltri_ref[...], cnt.astype(bf16), preferred_element_type=f32) + carry_ref[...]
    r1 = jnp.sum(jnp.where(oh1, prefix, 0.0), axis=-1, keepdims=True)
    r2 = jnp.sum(jnp.where(oh2, prefix, 0.0), axis=-1, keepdims=True)
    ri = jnp.where(lane == 0, i1, jnp.where(lane == 1, i2, jnp.where(lane == 2, r1, jnp.where(lane == 3, r2, 0.0))))
    rt_ref[...] = ri.T[0:8, :].astype(jnp.int32)
    rw_ref[...] = jnp.where(lane == 0, w1, jnp.where(lane == 1, w2, 0.0))
    new_carry = carry_ref[...] + jnp.sum(cnt, axis=0, keepdims=True)
    carry_ref[...] = new_carry
    cnt_ref[...] = new_carry


def _post_kernel(n_tiles_p, xp_ref, attp_ref, cvp_ref, gap_ref, shfp_ref, scfp_ref,
                 xs_ref, atts_ref, cvs_ref, gas_ref, shfs_ref, scfs_ref, *rest):
    i = pl.program_id(0)
    carry_ref = rest[-1]

    @pl.when(i == 0)
    def _():
        carry_ref[...] = jnp.zeros(carry_ref.shape, f32)

    @pl.when(i < n_tiles_p)
    def _():
        _post_tile(xp_ref, attp_ref, cvp_ref, gap_ref, shfp_ref, scfp_ref, *rest)

    @pl.when(i >= n_tiles_p)
    def _():
        _post_tile(xs_ref, atts_ref, cvs_ref, gas_ref, shfs_ref, scfs_ref, *rest)


def _post_call(x_p, att_p, cv_p, mod_p, x_s, att_s, cv_s, mod_s, tiles_per_batch, ln_ffn, w_out_bf, w_r_bf, b_r, ltri):
    n_p, n_s = x_p.shape[0], x_s.shape[0]
    tp, ts = n_p // TM, n_s // TM
    n = n_p + n_s
    prow = lambda w: pl.BlockSpec((TM, w), lambda i: (jnp.minimum(i, tp - 1), 0))
    srow = lambda w: pl.BlockSpec((TM, w), lambda i: (jnp.maximum(i - tp, 0), 0))
    pmod = lambda sec: pl.BlockSpec((None, 1, D_MODEL), lambda i: (jnp.minimum(i, tp - 1) // tiles_per_batch, 0, sec))
    smod = lambda sec: pl.BlockSpec((TM, D_MODEL), lambda i: (jnp.maximum(i - tp, 0), sec))
    row = lambda w: pl.BlockSpec((TM, w), lambda i: (i, 0))
    const = lambda shape: pl.BlockSpec(shape, lambda i: (0, 0))
    return pl.pallas_call(
        functools.partial(_post_kernel, tp),
        grid=(tp + ts,),
        in_specs=[prow(D_MODEL), prow(ATTN_WIDTH), prow(CONV_WIDTH), pmod(2), pmod(3), pmod(4),
                  srow(D_MODEL), srow(ATTN_WIDTH), srow(CONV_WIDTH), smod(2), smod(3), smod(4),
                  const((1, D_MODEL)), const((D_MODEL, D_MODEL)), const((D_MODEL, LANES)), const((1, LANES)),
                  const((TM, TM))],
        out_specs=[row(D_MODEL), pl.BlockSpec((TM * SLAB, LANES), lambda i: (i, 0)),
                   pl.BlockSpec((8, TM), lambda i: (0, i)), row(LANES), const((1, LANES))],
        out_shape=[jax.ShapeDtypeStruct((n, D_MODEL), f32), jax.ShapeDtypeStruct((n * SLAB, LANES), f32),
                   jax.ShapeDtypeStruct((8, n), jnp.int32), jax.ShapeDtypeStruct((n, LANES), f32),
                   jax.ShapeDtypeStruct((1, LANES), f32)],
        scratch_shapes=[pltpu.VMEM((1, LANES), f32)],
        compiler_params=_cparams(("arbitrary",)),
        name="post",
    )(x_p, att_p, cv_p, mod_p, mod_p, mod_p, x_s, att_s, cv_s, mod_s, mod_s, mod_s,
      ln_ffn, w_out_bf, w_r_bf, b_r, ltri)


def _dispatch_kernel(pos0_ref, pos1_ref, src_ref, xs_ref, sem):
    t0 = pl.program_id(0) * TM

    def issue(r, c):
        t = t0 + r
        _slab_copy(src_ref, r, xs_ref, pos0_ref[t], sem).start(priority=0)
        _slab_copy(src_ref, r, xs_ref, pos1_ref[t], sem).start(priority=1)
        return c

    lax.fori_loop(0, TM, issue, 0, unroll=ROW_DMA_UNROLL)
    tile_copy = pltpu.make_async_copy(src_ref, xs_ref.at[pl.ds(0, TM * SLAB), :], sem)
    tile_copy.wait()
    tile_copy.wait()


def _dispatch_call(pos0, pos1, h2):
    tiles = h2.shape[0] // (TM * SLAB)
    return pl.pallas_call(
        _dispatch_kernel,
        grid_spec=pltpu.PrefetchScalarGridSpec(
            num_scalar_prefetch=2,
            grid=(tiles,),
            in_specs=[pl.BlockSpec((TM * SLAB, LANES), lambda i, p0, p1: (i, 0))],
            out_specs=pl.BlockSpec(memory_space=pl.ANY),
            scratch_shapes=[pltpu.SemaphoreType.DMA(())],
        ),
        out_shape=jax.ShapeDtypeStruct((2 * h2.shape[0], LANES), f32),
        compiler_params=_cparams(("arbitrary",)),
        name="dispatch",
    )(pos0, pos1, h2)


def _tile_chunk_copies(hbm_ref, tile, buf_ref, slot, sem):
    r0 = pl.multiple_of(tile * TE, TE)
    return [pltpu.make_async_copy(hbm_ref.at[pl.ds(r0, TE), a, :], buf_ref.at[slot, a], sem) for a in range(SLAB)]


def _experts_kernel(n_tiles, tile_ref, exp_ref, lo_ref, hi_ref, xs_ref, wg_ref, wu_ref, wd_ref, ys_ref,
                    wg_bf, wu_bf, wd_bf, cur_ref, xbuf, xsem):
    w = pl.program_id(0)
    lo = lo_ref[w]
    hi = hi_ref[w]
    k = tile_ref[w]
    base = k * TE
    slot = k % 2

    @pl.when(w == 0)
    def _():
        cur_ref[0] = -1
        for c in _tile_chunk_copies(xs_ref, 0, xbuf, 0, xsem.at[0]):
            c.start()

    @pl.when(hi > lo)
    def _():
        first = lo == base

        @pl.when(first)
        def _():
            for c in _tile_chunk_copies(xs_ref, k, xbuf, slot, xsem.at[slot]):
                c.wait()

            @pl.when(k + 1 < n_tiles)
            def _():
                for c in _tile_chunk_copies(xs_ref, k + 1, xbuf, 1 - slot, xsem.at[1 - slot]):
                    c.start()

        e = exp_ref[w]

        @pl.when(cur_ref[0] != e)
        def _():
            wg_bf[...] = wg_ref[...].astype(bf16)
            wu_bf[...] = wu_ref[...].astype(bf16)
            wd_bf[...] = wd_ref[...].astype(bf16)
            cur_ref[0] = e

        x = jnp.concatenate([xbuf[slot, a] for a in range(SLAB)], axis=1).astype(bf16)
        g = jnp.dot(x, wg_bf[...], preferred_element_type=f32)
        u = jnp.dot(x, wu_bf[...], preferred_element_type=f32)
        hid = (_silu(g) * u).astype(bf16)
        y = jnp.dot(hid, wd_bf[...], preferred_element_type=f32)
        row = base + lax.broadcasted_iota(jnp.int32, y.shape, 0)
        mine = (row >= lo) & (row < hi)

        @pl.when(first)
        def _():
            _store_slabs(ys_ref, 0, jnp.where(mine, y, 0.0))

        @pl.when(jnp.logical_not(first))
        def _():
            _store_slabs(ys_ref, 0, jnp.where(mine, y, _load_slabs(ys_ref, 0, TE)))


def _experts_call(tile_id, exp_id, seg_lo, seg_hi, xs, w_gate, w_up, w_down):
    n_items = tile_id.shape[0]
    n_rows = xs.shape[0] // SLAB
    n_tiles = n_rows // TE
    assert n_tiles >= 2
    idx = lambda w, t, e, lo, hi: (e[w], 0, 0)
    return pl.pallas_call(
        functools.partial(_experts_kernel, n_tiles),
        grid_spec=pltpu.PrefetchScalarGridSpec(
            num_scalar_prefetch=4,
            grid=(n_items,),
            in_specs=[pl.BlockSpec(memory_space=pl.ANY),
                      pl.BlockSpec((None, D_MODEL, D_EXPERT), idx),
                      pl.BlockSpec((None, D_MODEL, D_EXPERT), idx),
                      pl.BlockSpec((None, D_EXPERT, D_MODEL), idx)],
            out_specs=pl.BlockSpec((TE * SLAB, LANES), lambda w, t, e, lo, hi: (t[w], 0)),
            scratch_shapes=[pltpu.VMEM((D_MODEL, D_EXPERT), bf16), pltpu.VMEM((D_MODEL, D_EXPERT), bf16),
                            pltpu.VMEM((D_EXPERT, D_MODEL), bf16), pltpu.SMEM((1,), jnp.int32),
                            pltpu.VMEM((2, SLAB, TE, LANES), f32), pltpu.SemaphoreType.DMA((2,))],
        ),
        out_shape=jax.ShapeDtypeStruct(xs.shape, f32),
        compiler_params=_cparams(("arbitrary",)),
        name="experts",
    )(tile_id, exp_id, seg_lo, seg_hi, xs.reshape(n_rows, SLAB, LANES), w_gate, w_up, w_down)


def _combine_kernel(n_tiles_p, pos0_ref, pos1_ref, x1_ref, rw_ref, gfp_ref, gfs_ref, ln_ref, ys_ref,
                    op_ref, os_ref, ybuf, sems):
    i = pl.program_id(0)
    n = pl.num_programs(0)
    half = 2 * TM

    def issue(tile, buf):
        def body(r, c):
            t = tile * TM + r
            _slab_copy(ys_ref, pos0_ref[t], ybuf, buf * half + r, sems.at[buf]).start(priority=0)
            _slab_copy(ys_ref, pos1_ref[t], ybuf, buf * half + TM + r, sems.at[buf]).start(priority=1)
            return c

        lax.fori_loop(0, TM, body, 0, unroll=ROW_DMA_UNROLL)

    @pl.when(i == 0)
    def _():
        issue(0, 0)

    @pl.when(i + 1 < n)
    def _():
        issue(i + 1, (i + 1) % 2)

    buf = i % 2
    tile_copy = pltpu.make_async_copy(ys_ref.at[pl.ds(0, TM * SLAB), :], ybuf.at[pl.ds(0, TM * SLAB), :], sems.at[buf])
    tile_copy.wait()
    tile_copy.wait()
    rw = rw_ref[...]
    moe = rw[:, 0:1] * _load_slabs(ybuf, buf * half, TM) + rw[:, 1:2] * _load_slabs(ybuf, buf * half + TM, TM)

    def finish(gf_ref, o_ref):
        x2 = x1_ref[...] + gf_ref[...] * moe
        o_ref[...] = _rms(x2, ln_ref[...], NORM_EPS)

    @pl.when(i < n_tiles_p)
    def _():
        finish(gfp_ref, op_ref)

    @pl.when(i >= n_tiles_p)
    def _():
        finish(gfs_ref, os_ref)


def _combine_call(pos0, pos1, x1, rw, mod_p, mod_s, tiles_per_batch, ln_final, ys):
    n = x1.shape[0]
    n_s = mod_s.shape[0]
    tp, ts = (n - n_s) // TM, n_s // TM
    pidx = lambda i: jnp.minimum(i, tp - 1)
    sidx = lambda i: jnp.maximum(i - tp, 0)
    return pl.pallas_call(
        functools.partial(_combine_kernel, tp),
        grid_spec=pltpu.PrefetchScalarGridSpec(
            num_scalar_prefetch=2,
            grid=(tp + ts,),
            in_specs=[pl.BlockSpec((TM, D_MODEL), lambda i, p0, p1: (i, 0)),
                      pl.BlockSpec((TM, LANES), lambda i, p0, p1: (i, 0)),
                      pl.BlockSpec((None, 1, D_MODEL), lambda i, p0, p1: (pidx(i) // tiles_per_batch, 0, 5)),
                      pl.BlockSpec((TM, D_MODEL), lambda i, p0, p1: (sidx(i), 5)),
                      pl.BlockSpec((1, D_MODEL), lambda i, p0, p1: (0, 0)),
                      pl.BlockSpec(memory_space=pl.ANY)],
            out_specs=[pl.BlockSpec((TM, D_MODEL), lambda i, p0, p1: (pidx(i), 0)),
                       pl.BlockSpec((TM, D_MODEL), lambda i, p0, p1: (sidx(i), 0))],
            scratch_shapes=[pltpu.VMEM((2 * 2 * TM * SLAB, LANES), f32), pltpu.SemaphoreType.DMA((2,))],
        ),
        out_shape=[jax.ShapeDtypeStruct((n - n_s, D_MODEL), f32), jax.ShapeDtypeStruct((n_s, D_MODEL), f32)],
        compiler_params=_cparams(("arbitrary",)),
        name="combine",
    )(pos0, pos1, x1, rw, mod_p, mod_s, ln_final, ys)


def _rope_tables(pos):
    inv = 1.0 / (ROPE_THETA ** (np.arange(0, HEAD_DIM, 2, dtype=np.float64) / HEAD_DIM))
    ang = np.asarray(pos, np.float64)[:, None] * inv[None, :]
    ang = np.concatenate([ang, ang], axis=-1)
    sign = np.where(np.arange(HEAD_DIM) < HEAD_DIM // 2, -1.0, 1.0)
    cos = np.tile(np.cos(ang), (1, N_SUB)).astype(np.float32)
    sin_signed = np.tile(np.sin(ang) * sign[None, :], (1, N_SUB)).astype(np.float32)
    return jnp.asarray(cos), jnp.asarray(sin_signed)


def _segments(counts, n_rows):
    n_tiles = n_rows // TE
    offs = jnp.concatenate([jnp.zeros((1,), jnp.int32), jnp.cumsum(counts)[:-1].astype(jnp.int32)])
    tiles = jnp.arange(n_tiles, dtype=jnp.int32) * TE
    rank_t = jnp.arange(n_tiles, dtype=jnp.int32) + jnp.sum(offs[None, :] < tiles[:, None], axis=1).astype(jnp.int32)
    rank_o = jnp.arange(N_EXPERTS, dtype=jnp.int32) + jnp.minimum(offs // TE + 1, n_tiles)
    vals = jnp.concatenate([tiles, offs])
    ranks = jnp.concatenate([rank_t, rank_o])
    n_items = n_tiles + N_EXPERTS
    w = jnp.arange(n_items, dtype=jnp.int32)
    seg_lo = jnp.sum(jnp.where(ranks[None, :] == w[:, None], vals[None, :], 0), axis=1).astype(jnp.int32)
    seg_hi = jnp.concatenate([seg_lo[1:], jnp.full((1,), n_rows, jnp.int32)])
    tile_id = jnp.minimum(seg_lo // TE, n_tiles - 1)
    exp_id = jnp.sum(offs[None, :] <= seg_lo[:, None], axis=1).astype(jnp.int32) - 1
    return offs, tile_id, exp_id, seg_lo, seg_hi


def kernel(x_prompt, x_sample, cache_k, cache_v, state_conv, page_table, c_prompt, c_sample, w_ada, b_ada, ln_mix, w_in, lam_q1, lam_k1, lam_q2, lam_k2, subln_g, w_conv, w_out, ln_ffn, w_router_grp, b_router_grp, w_router_exp, b_router_exp, w_gate, w_up, w_down, ln_final):
    B, S, _ = x_prompt.shape
    DB, L, _ = x_sample.shape
    n_phys = cache_k.shape[1]
    n_pages = page_table.shape[1]
    past = n_pages * PAGE
    n_p = B * S
    n_s = DB * L
    n_tok = n_p + n_s

    w_in_bf = w_in[0].astype(bf16)
    w_out_bf = w_out[0].astype(bf16)
    w_r = jnp.zeros((D_MODEL, LANES), f32)
    w_r = w_r.at[:, :N_EXPERTS].set(w_router_exp[0]).at[:, ROUTE_GRP_LANE:ROUTE_GRP_LANE + N_GROUPS].set(w_router_grp[0])
    b_r = jnp.zeros((1, LANES), f32)
    b_r = b_r.at[0, :N_EXPERTS].set(b_router_exp[0]).at[0, ROUTE_GRP_LANE:ROUTE_GRP_LANE + N_GROUPS].set(b_router_grp[0])
    w_r_bf = w_r.astype(bf16)
    cos_p, sin_p = _rope_tables(np.arange(S))
    cos_s, sin_s = _rope_tables(past + np.repeat(np.arange(L), DB))
    ltri = jnp.asarray(np.tril(np.ones((TM, TM), np.float32), -1), bf16)

    mod = _mod_call(jnp.concatenate([c_prompt, c_sample], axis=0), w_ada[0], b_ada)
    mod_p = mod[:B].reshape(B, 1, 6 * D_MODEL)
    mod_s = jnp.tile(mod[B:], (L, 1))

    qt_p, kt_p, kb_p, v4_p, vt_p, cv_p, st_p = _inproj_p_call(x_prompt, mod_p, ln_mix, w_in_bf, cos_p, sin_p, w_conv[0])
    xs_l = x_sample.transpose(1, 0, 2).reshape(n_s, D_MODEL)
    st_in = state_conv[0].transpose(1, 0, 2)
    q_s, kf_s, vf_s, cv_s, st_s = _inproj_s_call(xs_l, mod_s[:, 0:D_MODEL], mod_s[:, D_MODEL:2 * D_MODEL], ln_mix,
                                                 w_in_bf, cos_s, sin_s, w_conv[0], st_in)
    to_b = lambda a: a.reshape(L, DB, -1).transpose(1, 0, 2)
    pad8 = lambda a: jnp.pad(a, ((0, 0), (0, 8 - L), (0, 0)))
    kt = jnp.transpose(cache_k[0], (0, 2, 3, 1)).reshape(n_phys, N_SUB * HEAD_DIM, PAGE)
    v2 = cache_v[0].reshape(n_phys, PAGE * N_HEADS, V_DIM)
    att_p, att_s_b = _attn_call(page_table.reshape(-1), qt_p, kb_p, vt_p, lam_q1, lam_k1, lam_q2, lam_k2, subln_g,
                                to_b(q_s), pad8(to_b(kf_s)), pad8(to_b(vf_s)), kt, v2, n_pages)
    att_s = att_s_b.transpose(1, 0, 2).reshape(n_s, ATTN_WIDTH)

    x1, h2, rt, rw, cnt = _post_call(x_prompt.reshape(n_p, D_MODEL), att_p.reshape(n_p, ATTN_WIDTH),
                                     cv_p.reshape(n_p, CONV_WIDTH), mod_p, xs_l, att_s, cv_s, mod_s, S // TM,
                                     ln_ffn, w_out_bf, w_r_bf, b_r, ltri)

    counts = cnt[0, :N_EXPERTS].astype(jnp.int32)
    offs, tile_id, exp_id, seg_lo, seg_hi = _segments(counts, 2 * n_tok)
    e_col = jnp.arange(N_EXPERTS, dtype=jnp.int32)[:, None]
    start = lambda e_row: jnp.sum(jnp.where(e_row[None, :] == e_col, offs[:, None], 0), axis=0)
    pos0 = start(rt[0]) + rt[2]
    pos1 = start(rt[1]) + rt[3]

    xs_sorted = _dispatch_call(pos0, pos1, h2)
    ys = _experts_call(tile_id, exp_id, seg_lo, seg_hi, xs_sorted, w_gate[0], w_up[0], w_down[0])
    y_p, y_s = _combine_call(pos0, pos1, x1, rw, mod_p, mod_s, S // TM, ln_final.reshape(1, D_MODEL), ys)

    from_l = lambda a: a.reshape(L, DB, -1).transpose(1, 0, 2)
    y_prompt = y_p.reshape(B, S, D_MODEL)
    y_sample = from_l(y_s)
    k_prompt = kt_p.reshape(B, N_SUB, HEAD_DIM, S).transpose(0, 3, 1, 2)[None]
    v_prompt = v4_p.reshape(1, B, S, N_HEADS, V_DIM)
    conv_prompt = st_p[None]
    k_sample = from_l(kf_s).reshape(1, DB, L, N_SUB, HEAD_DIM)
    v_sample = from_l(vf_s).reshape(1, DB, L, N_HEADS, V_DIM)
    conv_sample = st_s.transpose(1, 0, 2)[None]
    return (y_prompt, y_sample, k_prompt, v_prompt, conv_prompt, k_sample, v_sample, conv_sample)
```

```python
import functools
import math

import jax
import jax.numpy as jnp
import numpy as np
from jax import lax
from jax.experimental import pallas as pl
from jax.experimental.pallas import tpu as pltpu

D_MODEL = 1024
ATTN_WIDTH = 512
CONV_WIDTH = 512
N_HEADS = 4
N_SUB = 8
HEAD_DIM = 64
V_DIM = 2 * HEAD_DIM
CONV_K = 3
ROPE_THETA = 10000.0
N_GROUPS = 4
EXP_PER_GROUP = 8
N_EXPERTS = 32
D_EXPERT = 256
NORM_EPS = 1e-6
SUBLN_EPS = 1e-5
LAM_INIT = 0.8 - 0.6 * math.exp(-0.3 * 0)
LOG2E = math.log2(math.e)
PAGE = 128
LANES = 128
ROUTE_GRP_LANE = 32

TM = 512
TQ = 512
TE = 256
ROW_DMA_UNROLL = 8
ONES_ROWS = 16
VMEM_LIMIT = 56 * 1024 * 1024

f32 = jnp.float32
bf16 = jnp.bfloat16


def _cparams(sem):
    return pltpu.CompilerParams(dimension_semantics=sem, vmem_limit_bytes=VMEM_LIMIT)


def _rms(x, g, eps):
    return x * lax.rsqrt(jnp.mean(x * x, axis=-1, keepdims=True) + eps) * g


def _silu(x):
    return x * (1.0 / (1.0 + jnp.exp(-x)))


def _mod_kernel(c_ref, w_ref, b_ref, o_ref):
    a = _silu(c_ref[...]).astype(bf16)
    o_ref[...] = jnp.dot(a, w_ref[...].astype(bf16), preferred_element_type=f32) + b_ref[...]


def _mod_call(c_all, w_ada, b_ada):
    n = c_all.shape[0]
    return pl.pallas_call(
        _mod_kernel,
        grid=(6,),
        in_specs=[pl.BlockSpec((n, D_MODEL), lambda j: (0, 0)),
                  pl.BlockSpec((D_MODEL, D_MODEL), lambda j: (0, j)),
                  pl.BlockSpec((1, D_MODEL), lambda j: (0, j))],
        out_specs=pl.BlockSpec((n, D_MODEL), lambda j: (0, j)),
        out_shape=jax.ShapeDtypeStruct((n, 6 * D_MODEL), f32),
        compiler_params=_cparams(("arbitrary",)),
        name="mod",
    )(c_all, w_ada, b_ada)


def _rope(t, cos, sin_signed, lo_mask):
    n = t.shape[-1]
    rot = jnp.where(lo_mask, pltpu.roll(t, n - HEAD_DIM // 2, 1), pltpu.roll(t, HEAD_DIM // 2, 1))
    return t * cos + rot * sin_signed


def _inproj_common(x, sh, sc, ln, w_ref, cos, sin_signed):
    h = (_rms(x, ln, NORM_EPS) * (1.0 + sc) + sh).astype(bf16)

    def sec(i):
        return jnp.dot(h, w_ref[:, i * 512:(i + 1) * 512], preferred_element_type=f32)

    lane = lax.broadcasted_iota(jnp.int32, (x.shape[0], 512), 1)
    lo_mask = (lane % HEAD_DIM) < (HEAD_DIM // 2)
    q = _rope(sec(0), cos, sin_signed, lo_mask) * (HEAD_DIM ** -0.5 * LOG2E)
    k = _rope(sec(1), cos, sin_signed, lo_mask)
    v = sec(2)
    bg = sec(3)
    cu = sec(4) * sec(5)
    return q, k, v, bg, cu


def _inproj_p_kernel(x_ref, sh_ref, sc_ref, ln_ref, w_ref, cos_ref, sin_ref, wc_ref,
                     qt_ref, kt_ref, kb_ref, v4_ref, vt_ref, cv_ref, st_ref, carry_ref):
    s = pl.program_id(0)
    b = pl.program_id(1)
    q, k, v, bg, cu = _inproj_common(x_ref[...], sh_ref[...], sc_ref[...], ln_ref[...], w_ref,
                                     cos_ref[...], sin_ref[...])
    qt_ref[...] = q.T.astype(bf16)
    kt_ref[...] = k.T
    kb_ref[...] = k.astype(bf16)
    vt_ref[...] = v.T.astype(bf16)
    for h in range(N_HEADS):
        v4_ref[pl.ds(h, v.shape[0], stride=N_HEADS), :] = v[:, h * V_DIM:(h + 1) * V_DIM]
    tm = cu.shape[0]
    prev = jnp.where(s > 0, carry_ref[b], 0.0)
    row = lax.broadcasted_iota(jnp.int32, cu.shape, 0)
    cu1 = jnp.where(row == 0, prev[1:2], pltpu.roll(cu, 1, 0))
    cu2 = jnp.where(row == 0, prev[0:1], jnp.where(row == 1, prev[1:2], pltpu.roll(cu, 2, 0)))
    wc = wc_ref[...]
    conv = wc[0:1] * cu2 + wc[1:2] * cu1 + wc[2:3] * cu
    cv_ref[...] = (bg * conv).astype(bf16)
    last2 = cu[tm - 2:tm]
    carry_ref[b, 0:2, :] = last2
    st_ref[b] = last2


def _inproj_p_call(x, mod3, ln_mix, w_in_bf, cos, sin_signed, w_conv):
    B, S, _ = x.shape
    ns = S // TM
    row = lambda s, b: (b, s, 0)
    col = lambda s, b: (b, 0, s)
    rows = lambda dt: jax.ShapeDtypeStruct((B, S, 512), dt)
    cols = lambda dt: jax.ShapeDtypeStruct((B, 512, S), dt)
    row_spec = pl.BlockSpec((None, TM, 512), row)
    col_spec = pl.BlockSpec((None, 512, TM), col)
    return pl.pallas_call(
        _inproj_p_kernel,
        grid=(ns, B),
        in_specs=[pl.BlockSpec((None, TM, D_MODEL), row),
                  pl.BlockSpec((None, 1, D_MODEL), lambda s, b: (b, 0, 0)),
                  pl.BlockSpec((None, 1, D_MODEL), lambda s, b: (b, 0, 1)),
                  pl.BlockSpec((1, D_MODEL), lambda s, b: (0, 0)),
                  pl.BlockSpec((D_MODEL, 3072), lambda s, b: (0, 0)),
                  pl.BlockSpec((TM, 512), lambda s, b: (s, 0)),
                  pl.BlockSpec((TM, 512), lambda s, b: (s, 0)),
                  pl.BlockSpec((CONV_K, CONV_WIDTH), lambda s, b: (0, 0))],
        out_specs=[col_spec, col_spec, row_spec, pl.BlockSpec((None, N_HEADS * TM, V_DIM), row),
                   pl.BlockSpec((None, None, 512, TM), lambda s, b: (b, s, 0, 0)), row_spec,
                   pl.BlockSpec((B, 2, CONV_WIDTH), lambda s, b: (0, 0, 0))],
        out_shape=[cols(bf16), cols(f32), rows(bf16), jax.ShapeDtypeStruct((B, N_HEADS * S, V_DIM), f32),
                   jax.ShapeDtypeStruct((B, ns, 512, TM), bf16), rows(bf16),
                   jax.ShapeDtypeStruct((B, 2, CONV_WIDTH), f32)],
        scratch_shapes=[pltpu.VMEM((B, 8, CONV_WIDTH), f32)],
        compiler_params=_cparams(("arbitrary", "arbitrary")),
        name="inproj_p",
    )(x, mod3, mod3, ln_mix, w_in_bf, cos, sin_signed, w_conv)


def _inproj_s_kernel(seq_len, x_ref, sh_ref, sc_ref, ln_ref, w_ref, cos_ref, sin_ref, wc_ref, st0_ref, st1_ref,
                     q_ref, kf_ref, vf_ref, cv_ref, cu_ref):
    q, k, v, bg, cu = _inproj_common(x_ref[...], sh_ref[...], sc_ref[...], ln_ref[...], w_ref,
                                     cos_ref[...], sin_ref[...])
    q_ref[...] = q.astype(bf16)
    kf_ref[...] = k
    vf_ref[...] = v
    l = lax.broadcasted_iota(jnp.int32, cu.shape, 0) % seq_len
    st0 = st0_ref[...]
    st1 = st1_ref[...]
    cu1 = jnp.where(l == 0, st1, pltpu.roll(cu, 1, 0))
    cu2 = jnp.where(l == 0, st0, jnp.where(l == 1, st1, pltpu.roll(cu, 2, 0)))
    wc = wc_ref[...]
    conv = wc[0:1] * cu2 + wc[1:2] * cu1 + wc[2:3] * cu
    cv_ref[...] = (bg * conv).astype(bf16)
    cu_ref[...] = cu


def _inproj_s_call(x, sh, sc, ln_mix, w_in_bf, cos, sin_signed, w_conv, st0, st1, seq_len):
    n = x.shape[0]
    full = lambda shape: pl.BlockSpec(shape, lambda i: (0,) * len(shape))
    return pl.pallas_call(
        functools.partial(_inproj_s_kernel, seq_len),
        grid=(1,),
        in_specs=[full((n, D_MODEL)), full((n, D_MODEL)), full((n, D_MODEL)), full((1, D_MODEL)),
                  full((D_MODEL, 3072)), full((n, 512)), full((n, 512)), full((CONV_K, CONV_WIDTH)),
                  full((n, CONV_WIDTH)), full((n, CONV_WIDTH))],
        out_specs=[full((n, 512))] * 5,
        out_shape=[jax.ShapeDtypeStruct((n, 512), bf16), jax.ShapeDtypeStruct((n, 512), f32),
                   jax.ShapeDtypeStruct((n, 512), f32), jax.ShapeDtypeStruct((n, 512), bf16),
                   jax.ShapeDtypeStruct((n, CONV_WIDTH), f32)],
        compiler_params=_cparams(("arbitrary",)),
        name="inproj_s",
    )(x, sh, sc, ln_mix, w_in_bf, cos, sin_signed, w_conv, st0, st1)


def _lam(lq1, lk1, lq2, lk2):
    a = jnp.sum(lq1 * lk1, axis=-1, keepdims=True)
    b = jnp.sum(lq2 * lk2, axis=-1, keepdims=True)
    return jnp.exp(a) - jnp.exp(b) + LAM_INIT


def _attn_p_body(i, qt_ref, k_ref, vt_ref, lam, g_ref, o_ref, m_ref, acc_ref):
    tq = qt_ref.shape[1]
    tk = tq
    m_ref[...] = jnp.full(m_ref.shape, -jnp.inf, f32)
    acc_ref[...] = jnp.zeros(acc_ref.shape, f32)

    def step(j, masked):
        k0 = pl.multiple_of(j * tk, tk)
        qt = qt_ref[...]
        row = lax.broadcasted_iota(jnp.int32, qt.shape, 0)
        zero = jnp.zeros_like(qt)
        q2t = jnp.concatenate([jnp.where(row < HEAD_DIM, qt, zero), jnp.where(row >= HEAD_DIM, qt, zero)], axis=1)
        st = jnp.dot(k_ref[pl.ds(k0, tk), :], q2t, preferred_element_type=f32)
        if masked:
            kpos = lax.broadcasted_iota(jnp.int32, st.shape, 0)
            c = lax.broadcasted_iota(jnp.int32, st.shape, 1)
            st = jnp.where(kpos <= jnp.where(c >= tq, c - tq, c), st, -jnp.inf)
        m_prev = m_ref[...]
        m_new = jnp.maximum(m_prev, jnp.max(st, axis=0, keepdims=True))
        alpha = jnp.exp2(m_prev - m_new)
        pt = jnp.exp2(st - m_new).astype(bf16)
        vt1 = jnp.concatenate([vt_ref[j], jnp.ones((ONES_ROWS, tk), bf16)], axis=0)
        acc_ref[...] = alpha * acc_ref[...] + jnp.dot(vt1, pt, preferred_element_type=f32)
        m_ref[...] = m_new

    def below_diagonal(j, c):
        step(j, False)
        return c

    lax.fori_loop(0, i, below_diagonal, 0)
    step(i, True)
    ot = acc_ref[0:V_DIM, :] / acc_ref[V_DIM:V_DIM + 1, :]
    dt = ot[:, :tq] - lam * ot[:, tq:]
    ms = jnp.mean(dt * dt, axis=0, keepdims=True)
    nt = dt * lax.rsqrt(ms + SUBLN_EPS) * g_ref[...] * (1.0 - LAM_INIT)
    o_ref[...] = nt.T.astype(bf16)


def _attn_s_body(q_ref, kn_ref, vn_ref, lam, g_ref, kt_refs, v_refs, o_ref):
    n_pages = len(kt_refs)
    nq = q_ref.shape[0]
    nr = nq * N_SUB
    qf = q_ref[...].astype(f32)
    sub = lax.broadcasted_iota(jnp.int32, (N_SUB, ATTN_WIDTH), 0)
    col = lax.broadcasted_iota(jnp.int32, (N_SUB, ATTN_WIDTH), 1)
    diag = (col // HEAD_DIM) == sub
    qbd = jnp.concatenate([jnp.where(diag, jnp.broadcast_to(qf[a:a + 1], (N_SUB, ATTN_WIDTH)), 0.0)
                           for a in range(nq)], axis=0).astype(bf16)
    kt_all = jnp.concatenate([kt_refs[p][...].astype(bf16) for p in range(n_pages)], axis=1)
    s_past = jnp.dot(qbd, kt_all, preferred_element_type=f32)
    s_new = lax.dot_general(qbd, kn_ref[...].astype(bf16), (((1,), (1,)), ((), ())),
                            preferred_element_type=f32)
    r = lax.broadcasted_iota(jnp.int32, s_new.shape, 0)
    c = lax.broadcasted_iota(jnp.int32, s_new.shape, 1)
    s_new = jnp.where(c <= r // N_SUB, s_new, -jnp.inf)
    m = jnp.maximum(jnp.max(s_past, axis=-1, keepdims=True), jnp.max(s_new, axis=-1, keepdims=True))
    p_past = jnp.exp2(s_past - m)
    p_new = jnp.exp2(s_new - m)
    l = jnp.sum(p_past, axis=-1, keepdims=True) + jnp.sum(p_new, axis=-1, keepdims=True)
    p_bf = p_past.astype(bf16)
    vn = vn_ref[...]
    rowhead = (lax.broadcasted_iota(jnp.int32, (nr, V_DIM), 0) % N_SUB) // 2
    o = jnp.zeros((nr, V_DIM), f32)
    for h in range(N_HEADS):
        v_all = jnp.concatenate([v_refs[p][pl.ds(h, PAGE, stride=N_HEADS), :].astype(bf16)
                                 for p in range(n_pages)], axis=0)
        acc = jnp.dot(p_bf, v_all, preferred_element_type=f32)
        for a in range(nq):
            acc = acc + p_new[:, a:a + 1] * vn[a:a + 1, h * V_DIM:(h + 1) * V_DIM]
        o = jnp.where(rowhead == h, acc, o)
    o = o / l
    d = o - lam * pltpu.roll(o, nr - 1, 0)
    d = _rms(d, g_ref[...], SUBLN_EPS) * (1.0 - LAM_INIT)
    for a in range(nq):
        rowv = jnp.concatenate([d[a * N_SUB + 2 * h:a * N_SUB + 2 * h + 1] for h in range(N_HEADS)], axis=1)
        o_ref[a:a + 1, :] = rowv.astype(o_ref.dtype)


def _attn_kernel(n_pages, pt_ref, qt_ref, k_ref, vt_ref, lq1_ref, lk1_ref, lq2_ref, lk2_ref, gcol_ref,
                 q_ref, kn_ref, vn_ref, grow_ref, kt_hbm, v_hbm, op_ref, os_ref, m_ref, acc_ref, kbuf, vbuf, sems):
    t = (pl.program_id(0) * pl.num_programs(1) + pl.program_id(1)) * pl.num_programs(2) + pl.program_id(2)
    n_steps = pl.num_programs(0) * pl.num_programs(1) * pl.num_programs(2)
    slot = t % 2

    def fetch(seq, dst):
        for p in range(n_pages):
            page = pt_ref[seq * n_pages + p]
            pltpu.make_async_copy(kt_hbm.at[page], kbuf.at[dst, p], sems.at[dst]).start()
            pltpu.make_async_copy(v_hbm.at[page], vbuf.at[dst, p], sems.at[dst]).start()

    @pl.when(t == 0)
    def _():
        fetch(0, 0)

    pltpu.make_async_copy(kt_hbm.at[pl.ds(0, n_pages)], kbuf.at[slot], sems.at[slot]).wait()
    pltpu.make_async_copy(v_hbm.at[pl.ds(0, n_pages)], vbuf.at[slot], sems.at[slot]).wait()

    @pl.when(t + 1 < n_steps)
    def _():
        fetch(t + 1, 1 - slot)

    lam = _lam(lq1_ref[...], lk1_ref[...], lq2_ref[...], lk2_ref[...])
    _attn_p_body(pl.program_id(2), qt_ref, k_ref, vt_ref, lam, gcol_ref, op_ref, m_ref, acc_ref)
    _attn_s_body(q_ref, kn_ref, vn_ref, lam, grow_ref, [kbuf.at[slot, p] for p in range(n_pages)],
                 [vbuf.at[slot, p] for p in range(n_pages)], os_ref)


def _attn_call(pt_flat, qt, k, vt, lq1, lk1, lq2, lk2, g, q_s, kn_s, vn_s, kt, v2, n_pages):
    B, S, _ = k.shape
    nq_tiles = S // TQ
    nb, nq, _ = q_s.shape
    assert nb == B * N_HEADS * nq_tiles, "one sample sequence per prompt grid step"
    seq = lambda b, h, i: (b * N_HEADS + h) * nq_tiles + i
    small = lambda n: pl.BlockSpec((1, n), lambda b, h, i, pt: (0, 0))
    page_rows = kt.shape[1]
    return pl.pallas_call(
        functools.partial(_attn_kernel, n_pages),
        grid_spec=pltpu.PrefetchScalarGridSpec(
            num_scalar_prefetch=1,
            grid=(B, N_HEADS, nq_tiles),
            in_specs=[pl.BlockSpec((None, V_DIM, TQ), lambda b, h, i, pt: (b, h, i)),
                      pl.BlockSpec((None, S, V_DIM), lambda b, h, i, pt: (b, 0, h)),
                      pl.BlockSpec((None, nq_tiles, V_DIM, TQ), lambda b, h, i, pt: (b, 0, h, 0)),
                      small(HEAD_DIM), small(HEAD_DIM), small(HEAD_DIM), small(HEAD_DIM),
                      pl.BlockSpec((V_DIM, 1), lambda b, h, i, pt: (0, 0)),
                      pl.BlockSpec((None, nq, ATTN_WIDTH), lambda b, h, i, pt: (seq(b, h, i), 0, 0)),
                      pl.BlockSpec((None, nq, ATTN_WIDTH), lambda b, h, i, pt: (seq(b, h, i), 0, 0)),
                      pl.BlockSpec((None, nq, ATTN_WIDTH), lambda b, h, i, pt: (seq(b, h, i), 0, 0)),
                      small(V_DIM), pl.BlockSpec(memory_space=pl.ANY), pl.BlockSpec(memory_space=pl.ANY)],
            out_specs=[pl.BlockSpec((None, TQ, V_DIM), lambda b, h, i, pt: (b, i, h)),
                       pl.BlockSpec((None, nq, ATTN_WIDTH), lambda b, h, i, pt: (seq(b, h, i), 0, 0))],
            scratch_shapes=[pltpu.VMEM((1, 2 * TQ), f32), pltpu.VMEM((V_DIM + ONES_ROWS, 2 * TQ), f32),
                            pltpu.VMEM((2, n_pages, page_rows, PAGE), f32),
                            pltpu.VMEM((2, n_pages, page_rows, PAGE), f32), pltpu.SemaphoreType.DMA((2,))],
        ),
        out_shape=[jax.ShapeDtypeStruct((B, S, ATTN_WIDTH), bf16), jax.ShapeDtypeStruct((nb, nq, ATTN_WIDTH), bf16)],
        compiler_params=_cparams(("arbitrary",) * 3),
        name="attn",
    )(pt_flat, qt, k, vt, lq1, lk1, lq2, lk2, g.reshape(V_DIM, 1), q_s, kn_s, vn_s, g, kt, v2)


SLAB = D_MODEL // LANES


def _store_slabs(ref, row0, x):
    n = x.shape[0]
    for a in range(SLAB):
        ref[pl.ds(SLAB * row0 + a, n, stride=SLAB), :] = x[:, a * LANES:(a + 1) * LANES]


def _load_slabs(ref, row0, n):
    return jnp.concatenate([ref[pl.ds(SLAB * row0 + a, n, stride=SLAB), :] for a in range(SLAB)], axis=1)


def _slab_copy(src_ref, src_row, dst_ref, dst_row, sem):
    src = src_ref.at[pl.ds(pl.multiple_of(src_row * SLAB, SLAB), SLAB), :]
    dst = dst_ref.at[pl.ds(pl.multiple_of(dst_row * SLAB, SLAB), SLAB), :]
    return pltpu.make_async_copy(src, dst, sem)


def _post_tile(x_ref, att_ref, cv_ref, ga_ref, shf_ref, scf_ref, ln_ref, wo_ref, wr_ref, br_ref, ltri_ref,
               x1_ref, h2_ref, rt_ref, rw_ref, cnt_ref, carry_ref):
    mix = (jnp.dot(att_ref[...], wo_ref[0:ATTN_WIDTH, :], preferred_element_type=f32)
           + jnp.dot(cv_ref[...], wo_ref[ATTN_WIDTH:, :], preferred_element_type=f32))
    x1 = x_ref[...] + ga_ref[...] * mix
    x1_ref[...] = x1
    h2 = _rms(x1, ln_ref[...], NORM_EPS) * (1.0 + scf_ref[...]) + shf_ref[...]
    _store_slabs(h2_ref, 0, h2)
    logits = jnp.dot(h2.astype(bf16), wr_ref[...], preferred_element_type=f32) + br_ref[...]
    lane = lax.broadcasted_iota(jnp.int32, logits.shape, 1)
    lane_f = lane.astype(f32)
    big = jnp.float32(1e9)
    neg = -jnp.inf

    def first_max(vals):
        mx = jnp.max(vals, axis=-1, keepdims=True)
        idx = jnp.min(jnp.where(vals == mx, lane_f, big), axis=-1, keepdims=True)
        return mx, idx

    gl = jnp.where((lane >= ROUTE_GRP_LANE) & (lane < ROUTE_GRP_LANE + N_GROUPS), logits, neg)
    gmax, gidx = first_max(gl)
    g_p = 1.0 / jnp.sum(jnp.exp(gl - gmax), axis=-1, keepdims=True)
    lo = (gidx - ROUTE_GRP_LANE) * EXP_PER_GROUP
    el = jnp.where((lane_f >= lo) & (lane_f < lo + EXP_PER_GROUP), logits, neg)
    v1, i1 = first_max(el)
    el2 = jnp.where(lane_f == i1, neg, el)
    v2, i2 = first_max(el2)
    t = jnp.exp(v2 - v1)
    w1 = g_p / (1.0 + t)
    w2 = g_p * t / (1.0 + t)
    oh1 = lane_f == i1
    oh2 = lane_f == i2
    cnt = jnp.where(oh1 | oh2, 1.0, 0.0)
    prefix = jnp.dot(ltri_ref[...], cnt.astype(bf16), preferred_element_type=f32) + carry_ref[...]
    r1 = jnp.sum(jnp.where(oh1, prefix, 0.0), axis=-1, keepdims=True)
    r2 = jnp.sum(jnp.where(oh2, prefix, 0.0), axis=-1, keepdims=True)
    ri = jnp.where(lane == 0, i1, jnp.where(lane == 1, i2, jnp.where(lane == 2, r1, jnp.where(lane == 3, r2, 0.0))))
    rt_ref[...] = ri.T[0:8, :].astype(jnp.int32)
    rw_ref[...] = jnp.where(lane == 0, w1, jnp.where(lane == 1, w2, 0.0))
    new_carry = carry_ref[...] + jnp.sum(cnt, axis=0, keepdims=True)
    carry_ref[...] = new_carry
    cnt_ref[...] = new_carry


def _post_kernel(n_tiles_p, xp_ref, attp_ref, cvp_ref, gap_ref, shfp_ref, scfp_ref,
                 xs_ref, atts_ref, cvs_ref, gas_ref, shfs_ref, scfs_ref, *rest):
    i = pl.program_id(0)
    carry_ref = rest[-1]

    @pl.when(i == 0)
    def _():
        carry_ref[...] = jnp.zeros(carry_ref.shape, f32)

    @pl.when(i < n_tiles_p)
    def _():
        _post_tile(xp_ref, attp_ref, cvp_ref, gap_ref, shfp_ref, scfp_ref, *rest)

    @pl.when(i >= n_tiles_p)
    def _():
        _post_tile(xs_ref, atts_ref, cvs_ref, gas_ref, shfs_ref, scfs_ref, *rest)


def _post_call(x_p, att_p, cv_p, mod_p, x_s, att_s, cv_s, mod_s, tiles_per_batch, ln_ffn, w_out_bf, w_r_bf, b_r, ltri):
    n_p, n_s = x_p.shape[0], x_s.shape[0]
    tp, ts = n_p // TM, n_s // TM
    n = n_p + n_s
    prow = lambda w: pl.BlockSpec((TM, w), lambda i: (jnp.minimum(i, tp - 1), 0))
    srow = lambda w: pl.BlockSpec((TM, w), lambda i: (jnp.maximum(i - tp, 0), 0))
    pmod = lambda sec: pl.BlockSpec((None, 1, D_MODEL), lambda i: (jnp.minimum(i, tp - 1) // tiles_per_batch, 0, sec))
    smod = lambda sec: pl.BlockSpec((TM, D_MODEL), lambda i: (jnp.maximum(i - tp, 0), sec))
    row = lambda w: pl.BlockSpec((TM, w), lambda i: (i, 0))
    const = lambda shape: pl.BlockSpec(shape, lambda i: (0, 0))
    return pl.pallas_call(
        functools.partial(_post_kernel, tp),
        grid=(tp + ts,),
        in_specs=[prow(D_MODEL), prow(ATTN_WIDTH), prow(CONV_WIDTH), pmod(2), pmod(3), pmod(4),
                  srow(D_MODEL), srow(ATTN_WIDTH), srow(CONV_WIDTH), smod(2), smod(3), smod(4),
                  const((1, D_MODEL)), const((D_MODEL, D_MODEL)), const((D_MODEL, LANES)), const((1, LANES)),
                  const((TM, TM))],
        out_specs=[row(D_MODEL), pl.BlockSpec((TM * SLAB, LANES), lambda i: (i, 0)),
                   pl.BlockSpec((8, TM), lambda i: (0, i)), row(LANES), const((1, LANES))],
        out_shape=[jax.ShapeDtypeStruct((n, D_MODEL), f32), jax.ShapeDtypeStruct((n * SLAB, LANES), f32),
                   jax.ShapeDtypeStruct((8, n), jnp.int32), jax.ShapeDtypeStruct((n, LANES), f32),
                   jax.ShapeDtypeStruct((1, LANES), f32)],
        scratch_shapes=[pltpu.VMEM((1, LANES), f32)],
        compiler_params=_cparams(("arbitrary",)),
        name="post",
    )(x_p, att_p, cv_p, mod_p, mod_p, mod_p, x_s, att_s, cv_s, mod_s, mod_s, mod_s,
      ln_ffn, w_out_bf, w_r_bf, b_r, ltri)


def _dispatch_kernel(pos0_ref, pos1_ref, src_ref, xs_ref, sem):
    t0 = pl.program_id(0) * TM

    def issue(r, c):
        t = t0 + r
        _slab_copy(src_ref, r, xs_ref, pos0_ref[t], sem).start(priority=0)
        _slab_copy(src_ref, r, xs_ref, pos1_ref[t], sem).start(priority=1)
        return c

    lax.fori_loop(0, TM, issue, 0, unroll=ROW_DMA_UNROLL)
    tile_copy = pltpu.make_async_copy(src_ref, xs_ref.at[pl.ds(0, TM * SLAB), :], sem)
    tile_copy.wait()
    tile_copy.wait()


def _dispatch_call(pos0, pos1, h2):
    tiles = h2.shape[0] // (TM * SLAB)
    return pl.pallas_call(
        _dispatch_kernel,
        grid_spec=pltpu.PrefetchScalarGridSpec(
            num_scalar_prefetch=2,
            grid=(tiles,),
            in_specs=[pl.BlockSpec((TM * SLAB, LANES), lambda i, p0, p1: (i, 0))],
            out_specs=pl.BlockSpec(memory_space=pl.ANY),
            scratch_shapes=[pltpu.SemaphoreType.DMA(())],
        ),
        out_shape=jax.ShapeDtypeStruct((2 * h2.shape[0], LANES), f32),
        compiler_params=_cparams(("arbitrary",)),
        name="dispatch",
    )(pos0, pos1, h2)


def _tile_chunk_copies(hbm_ref, tile, buf_ref, slot, sem, to_hbm):
    r0 = pl.multiple_of(tile * TE, TE)
    pairs = [(hbm_ref.at[pl.ds(r0, TE), a, :], buf_ref.at[slot, a]) for a in range(SLAB)]
    return [pltpu.make_async_copy(v, h, sem) if to_hbm else pltpu.make_async_copy(h, v, sem) for h, v in pairs]


def _experts_kernel(n_tiles, tile_ref, exp_ref, lo_ref, hi_ref, xs_ref, wg_ref, wu_ref, wd_ref, ys_ref,
                    wg_bf, wu_bf, wd_bf, cur_ref, xbuf, ybuf, xsem, ysem):
    w = pl.program_id(0)
    lo = lo_ref[w]
    hi = hi_ref[w]
    k = tile_ref[w]
    base = k * TE
    slot = k % 2

    @pl.when(w == 0)
    def _():
        cur_ref[0] = -1
        for c in _tile_chunk_copies(xs_ref, 0, xbuf, 0, xsem.at[0], False):
            c.start()

    @pl.when(hi > lo)
    def _():
        first = lo == base
        last = hi == base + TE

        @pl.when(first)
        def _():
            for c in _tile_chunk_copies(xs_ref, k, xbuf, slot, xsem.at[slot], False):
                c.wait()

            @pl.when(k + 1 < n_tiles)
            def _():
                for c in _tile_chunk_copies(xs_ref, k + 1, xbuf, 1 - slot, xsem.at[1 - slot], False):
                    c.start()

            @pl.when(k >= 2)
            def _():
                for c in _tile_chunk_copies(ys_ref, k - 2, ybuf, slot, ysem.at[slot], True):
                    c.wait()

        e = exp_ref[w]

        @pl.when(cur_ref[0] != e)
        def _():
            wg_bf[...] = wg_ref[...].astype(bf16)
            wu_bf[...] = wu_ref[...].astype(bf16)
            wd_bf[...] = wd_ref[...].astype(bf16)
            cur_ref[0] = e

        x = jnp.concatenate([xbuf[slot, a] for a in range(SLAB)], axis=1).astype(bf16)
        g = jnp.dot(x, wg_bf[...], preferred_element_type=f32)
        u = jnp.dot(x, wu_bf[...], preferred_element_type=f32)
        hid = (_silu(g) * u).astype(bf16)
        y = jnp.dot(hid, wd_bf[...], preferred_element_type=f32)
        row = base + lax.broadcasted_iota(jnp.int32, (TE, LANES), 0)
        mine = (row >= lo) & (row < hi)

        @pl.when(first)
        def _():
            for a in range(SLAB):
                ybuf[slot, a] = jnp.where(mine, y[:, a * LANES:(a + 1) * LANES], 0.0)

        @pl.when(jnp.logical_not(first))
        def _():
            for a in range(SLAB):
                ybuf[slot, a] = jnp.where(mine, y[:, a * LANES:(a + 1) * LANES], ybuf[slot, a])

        @pl.when(last)
        def _():
            for c in _tile_chunk_copies(ys_ref, k, ybuf, slot, ysem.at[slot], True):
                c.start()

    @pl.when(w == pl.num_programs(0) - 1)
    def _():
        for kk in (n_tiles - 2, n_tiles - 1):
            for c in _tile_chunk_copies(ys_ref, kk, ybuf, kk % 2, ysem.at[kk % 2], True):
                c.wait()


def _experts_call(tile_id, exp_id, seg_lo, seg_hi, xs, w_gate, w_up, w_down):
    n_items = tile_id.shape[0]
    n_rows = xs.shape[0] // SLAB
    n_tiles = n_rows // TE
    assert n_tiles >= 2
    idx = lambda w, t, e, lo, hi: (e[w], 0, 0)
    ys = pl.pallas_call(
        functools.partial(_experts_kernel, n_tiles),
        grid_spec=pltpu.PrefetchScalarGridSpec(
            num_scalar_prefetch=4,
            grid=(n_items,),
            in_specs=[pl.BlockSpec(memory_space=pl.ANY),
                      pl.BlockSpec((None, D_MODEL, D_EXPERT), idx),
                      pl.BlockSpec((None, D_MODEL, D_EXPERT), idx),
                      pl.BlockSpec((None, D_EXPERT, D_MODEL), idx)],
            out_specs=pl.BlockSpec(memory_space=pl.ANY),
            scratch_shapes=[pltpu.VMEM((D_MODEL, D_EXPERT), bf16), pltpu.VMEM((D_MODEL, D_EXPERT), bf16),
                            pltpu.VMEM((D_EXPERT, D_MODEL), bf16), pltpu.SMEM((1,), jnp.int32),
                            pltpu.VMEM((2, SLAB, TE, LANES), f32), pltpu.VMEM((2, SLAB, TE, LANES), f32),
                            pltpu.SemaphoreType.DMA((2,)), pltpu.SemaphoreType.DMA((2,))],
        ),
        out_shape=jax.ShapeDtypeStruct((n_rows, SLAB, LANES), f32),
        compiler_params=_cparams(("arbitrary",)),
        name="experts",
    )(tile_id, exp_id, seg_lo, seg_hi, xs.reshape(n_rows, SLAB, LANES), w_gate, w_up, w_down)
    return ys.reshape(xs.shape)


def _combine_kernel(n_tiles_p, pos0_ref, pos1_ref, x1_ref, rw_ref, gfp_ref, gfs_ref, ln_ref, ys_ref,
                    op_ref, os_ref, ybuf, sems):
    i = pl.program_id(0)
    n = pl.num_programs(0)
    half = 2 * TM

    def issue(tile, buf):
        def body(r, c):
            t = tile * TM + r
            _slab_copy(ys_ref, pos0_ref[t], ybuf, buf * half + r, sems.at[buf]).start(priority=0)
            _slab_copy(ys_ref, pos1_ref[t], ybuf, buf * half + TM + r, sems.at[buf]).start(priority=1)
            return c

        lax.fori_loop(0, TM, body, 0, unroll=ROW_DMA_UNROLL)

    @pl.when(i == 0)
    def _():
        issue(0, 0)

    @pl.when(i + 1 < n)
    def _():
        issue(i + 1, (i + 1) % 2)

    buf = i % 2
    tile_copy = pltpu.make_async_copy(ys_ref.at[pl.ds(0, TM * SLAB), :], ybuf.at[pl.ds(0, TM * SLAB), :], sems.at[buf])
    tile_copy.wait()
    tile_copy.wait()
    rw = rw_ref[...]
    moe = rw[:, 0:1] * _load_slabs(ybuf, buf * half, TM) + rw[:, 1:2] * _load_slabs(ybuf, buf * half + TM, TM)

    def finish(gf_ref, o_ref):
        x2 = x1_ref[...] + gf_ref[...] * moe
        o_ref[...] = _rms(x2, ln_ref[...], NORM_EPS)

    @pl.when(i < n_tiles_p)
    def _():
        finish(gfp_ref, op_ref)

    @pl.when(i >= n_tiles_p)
    def _():
        finish(gfs_ref, os_ref)


def _combine_call(pos0, pos1, x1, rw, mod_p, mod_s, tiles_per_batch, ln_final, ys):
    n = x1.shape[0]
    n_s = mod_s.shape[0]
    tp, ts = (n - n_s) // TM, n_s // TM
    pidx = lambda i: jnp.minimum(i, tp - 1)
    sidx = lambda i: jnp.maximum(i - tp, 0)
    return pl.pallas_call(
        functools.partial(_combine_kernel, tp),
        grid_spec=pltpu.PrefetchScalarGridSpec(
            num_scalar_prefetch=2,
            grid=(tp + ts,),
            in_specs=[pl.BlockSpec((TM, D_MODEL), lambda i, p0, p1: (i, 0)),
                      pl.BlockSpec((TM, LANES), lambda i, p0, p1: (i, 0)),
                      pl.BlockSpec((None, 1, D_MODEL), lambda i, p0, p1: (pidx(i) // tiles_per_batch, 0, 5)),
                      pl.BlockSpec((TM, D_MODEL), lambda i, p0, p1: (sidx(i), 5)),
                      pl.BlockSpec((1, D_MODEL), lambda i, p0, p1: (0, 0)),
                      pl.BlockSpec(memory_space=pl.ANY)],
            out_specs=[pl.BlockSpec((TM, D_MODEL), lambda i, p0, p1: (pidx(i), 0)),
                       pl.BlockSpec((TM, D_MODEL), lambda i, p0, p1: (sidx(i), 0))],
            scratch_shapes=[pltpu.VMEM((2 * 2 * TM * SLAB, LANES), f32), pltpu.SemaphoreType.DMA((2,))],
        ),
        out_shape=[jax.ShapeDtypeStruct((n - n_s, D_MODEL), f32), jax.ShapeDtypeStruct((n_s, D_MODEL), f32)],
        compiler_params=_cparams(("arbitrary",)),
        name="combine",
    )(pos0, pos1, x1, rw, mod_p, mod_s, ln_final, ys)


def _rope_tables(pos):
    inv = 1.0 / (ROPE_THETA ** (np.arange(0, HEAD_DIM, 2, dtype=np.float64) / HEAD_DIM))
    ang = np.asarray(pos, np.float64)[:, None] * inv[None, :]
    ang = np.concatenate([ang, ang], axis=-1)
    sign = np.where(np.arange(HEAD_DIM) < HEAD_DIM // 2, -1.0, 1.0)
    cos = np.tile(np.cos(ang), (1, N_SUB)).astype(np.float32)
    sin_signed = np.tile(np.sin(ang) * sign[None, :], (1, N_SUB)).astype(np.float32)
    return jnp.asarray(cos), jnp.asarray(sin_signed)


def _segments(counts, n_rows):
    n_tiles = n_rows // TE
    offs = jnp.concatenate([jnp.zeros((1,), jnp.int32), jnp.cumsum(counts)[:-1].astype(jnp.int32)])
    tiles = jnp.arange(n_tiles, dtype=jnp.int32) * TE
    rank_t = jnp.arange(n_tiles, dtype=jnp.int32) + jnp.sum(offs[None, :] < tiles[:, None], axis=1).astype(jnp.int32)
    rank_o = jnp.arange(N_EXPERTS, dtype=jnp.int32) + jnp.minimum(offs // TE + 1, n_tiles)
    vals = jnp.concatenate([tiles, offs])
    ranks = jnp.concatenate([rank_t, rank_o])
    n_items = n_tiles + N_EXPERTS
    w = jnp.arange(n_items, dtype=jnp.int32)
    seg_lo = jnp.sum(jnp.where(ranks[None, :] == w[:, None], vals[None, :], 0), axis=1).astype(jnp.int32)
    seg_hi = jnp.concatenate([seg_lo[1:], jnp.full((1,), n_rows, jnp.int32)])
    tile_id = jnp.minimum(seg_lo // TE, n_tiles - 1)
    exp_id = jnp.sum(offs[None, :] <= seg_lo[:, None], axis=1).astype(jnp.int32) - 1
    return offs, tile_id, exp_id, seg_lo, seg_hi


def kernel(x_prompt, x_sample, cache_k, cache_v, state_conv, page_table, c_prompt, c_sample, w_ada, b_ada, ln_mix, w_in, lam_q1, lam_k1, lam_q2, lam_k2, subln_g, w_conv, w_out, ln_ffn, w_router_grp, b_router_grp, w_router_exp, b_router_exp, w_gate, w_up, w_down, ln_final):
    B, S, _ = x_prompt.shape
    DB, L, _ = x_sample.shape
    n_phys = cache_k.shape[1]
    n_pages = page_table.shape[1]
    past = n_pages * PAGE
    n_p = B * S
    n_s = DB * L
    n_tok = n_p + n_s

    w_in_bf = w_in[0].astype(bf16)
    w_out_bf = w_out[0].astype(bf16)
    assert ROUTE_GRP_LANE == N_EXPERTS
    pad = LANES - N_EXPERTS - N_GROUPS
    w_r_bf = jnp.concatenate([w_router_exp[0], w_router_grp[0], jnp.zeros((D_MODEL, pad), f32)], axis=1).astype(bf16)
    b_r = jnp.concatenate([b_router_exp[0], b_router_grp[0], jnp.zeros((pad,), f32)])[None]
    cos_p, sin_p = _rope_tables(np.arange(S))
    cos_s, sin_s = _rope_tables(past + np.tile(np.arange(L), DB))
    ltri = jnp.asarray(np.tril(np.ones((TM, TM), np.float32), -1), bf16)

    mod = _mod_call(jnp.concatenate([c_prompt, c_sample], axis=0), w_ada[0], b_ada)
    mod_p = mod[:B].reshape(B, 1, 6 * D_MODEL)
    mod_s = jnp.repeat(mod[B:], L, axis=0)

    qt_p, kt_p, kb_p, v4_p, vt_p, cv_p, st_p = _inproj_p_call(x_prompt, mod_p, ln_mix, w_in_bf, cos_p, sin_p, w_conv[0])
    xs_l = x_sample.reshape(n_s, D_MODEL)
    st_rows = jnp.repeat(state_conv[0], L, axis=0)
    q_s, kf_s, vf_s, cv_s, cu_s = _inproj_s_call(xs_l, mod_s[:, 0:D_MODEL], mod_s[:, D_MODEL:2 * D_MODEL], ln_mix,
                                                 w_in_bf, cos_s, sin_s, w_conv[0], st_rows[:, 0], st_rows[:, 1], L)
    by_seq = lambda a: a.reshape(DB, L, -1)
    kt = jnp.transpose(cache_k[0], (0, 2, 3, 1)).reshape(n_phys, N_SUB * HEAD_DIM, PAGE)
    v2 = cache_v[0].reshape(n_phys, PAGE * N_HEADS, V_DIM)
    att_p, att_s_b = _attn_call(page_table.reshape(-1), qt_p, kb_p, vt_p, lam_q1, lam_k1, lam_q2, lam_k2, subln_g,
                                by_seq(q_s), by_seq(kf_s), by_seq(vf_s), kt, v2, n_pages)
    att_s = att_s_b.reshape(n_s, ATTN_WIDTH)

    x1, h2, rt, rw, cnt = _post_call(x_prompt.reshape(n_p, D_MODEL), att_p.reshape(n_p, ATTN_WIDTH),
                                     cv_p.reshape(n_p, CONV_WIDTH), mod_p, xs_l, att_s, cv_s, mod_s, S // TM,
                                     ln_ffn, w_out_bf, w_r_bf, b_r, ltri)

    counts = cnt[0, :N_EXPERTS].astype(jnp.int32)
    offs, tile_id, exp_id, seg_lo, seg_hi = _segments(counts, 2 * n_tok)
    e_col = jnp.arange(N_EXPERTS, dtype=jnp.int32)[:, None]
    start = lambda e_row: jnp.sum(jnp.where(e_row[None, :] == e_col, offs[:, None], 0), axis=0)
    pos0 = start(rt[0]) + rt[2]
    pos1 = start(rt[1]) + rt[3]

    xs_sorted = _dispatch_call(pos0, pos1, h2)
    ys = _experts_call(tile_id, exp_id, seg_lo, seg_hi, xs_sorted, w_gate[0], w_up[0], w_down[0])
    y_p, y_s = _combine_call(pos0, pos1, x1, rw, mod_p, mod_s, S // TM, ln_final.reshape(1, D_MODEL), ys)

    y_prompt = y_p.reshape(B, S, D_MODEL)
    y_sample = y_s.reshape(DB, L, D_MODEL)
    k_prompt = kt_p.reshape(B, N_SUB, HEAD_DIM, S).transpose(0, 3, 1, 2)[None]
    v_prompt = v4_p.reshape(1, B, S, N_HEADS, V_DIM)
    conv_prompt = st_p[None]
    k_sample = kf_s.reshape(1, DB, L, N_SUB, HEAD_DIM)
    v_sample = vf_s.reshape(1, DB, L, N_HEADS, V_DIM)
    conv_sample = cu_s.reshape(DB, L, CONV_WIDTH)[None, :, L - (CONV_K - 1):]
    return (y_prompt, y_sample, k_prompt, v_prompt, conv_prompt, k_sample, v_sample, conv_sample)
```

```python
import functools
import math

import jax
import jax.numpy as jnp
import numpy as np
from jax import lax
from jax.experimental import pallas as pl
from jax.experimental.pallas import tpu as pltpu

D_MODEL = 1024
ATTN_WIDTH = 512
CONV_WIDTH = 512
N_HEADS = 4
N_SUB = 8
HEAD_DIM = 64
V_DIM = 2 * HEAD_DIM
CONV_K = 3
ROPE_THETA = 10000.0
N_GROUPS = 4
EXP_PER_GROUP = 8
N_EXPERTS = 32
D_EXPERT = 256
NORM_EPS = 1e-6
SUBLN_EPS = 1e-5
LAM_INIT = 0.8 - 0.6 * math.exp(-0.3 * 0)
LOG2E = math.log2(math.e)
PAGE = 128
LANES = 128
ROUTE_GRP_LANE = 32

TM = 512
TQ = 512
TE = 256
ROW_DMA_UNROLL = 8
ONES_ROWS = 16
VMEM_LIMIT = 56 * 1024 * 1024

f32 = jnp.float32
bf16 = jnp.bfloat16


def _cparams(sem):
    return pltpu.CompilerParams(dimension_semantics=sem, vmem_limit_bytes=VMEM_LIMIT)


def _rms(x, g, eps):
    return x * lax.rsqrt(jnp.mean(x * x, axis=-1, keepdims=True) + eps) * g


def _silu(x):
    return x * (1.0 / (1.0 + jnp.exp(-x)))


def _mod_kernel(c_ref, w_ref, b_ref, o_ref):
    a = _silu(c_ref[...]).astype(bf16)
    o_ref[...] = jnp.dot(a, w_ref[...].astype(bf16), preferred_element_type=f32) + b_ref[...]


def _mod_call(c_all, w_ada, b_ada):
    n = c_all.shape[0]
    return pl.pallas_call(
        _mod_kernel,
        grid=(6,),
        in_specs=[pl.BlockSpec((n, D_MODEL), lambda j: (0, 0)),
                  pl.BlockSpec((D_MODEL, D_MODEL), lambda j: (0, j)),
                  pl.BlockSpec((1, D_MODEL), lambda j: (0, j))],
        out_specs=pl.BlockSpec((n, D_MODEL), lambda j: (0, j)),
        out_shape=jax.ShapeDtypeStruct((n, 6 * D_MODEL), f32),
        compiler_params=_cparams(("arbitrary",)),
        name="mod",
    )(c_all, w_ada, b_ada)


def _rope(t, cos, sin_signed, lo_mask):
    n = t.shape[-1]
    rot = jnp.where(lo_mask, pltpu.roll(t, n - HEAD_DIM // 2, 1), pltpu.roll(t, HEAD_DIM // 2, 1))
    return t * cos + rot * sin_signed


def _inproj_common(x, sh, sc, ln, w_ref, cos, sin_signed):
    h = (_rms(x, ln, NORM_EPS) * (1.0 + sc) + sh).astype(bf16)

    def sec(i):
        return jnp.dot(h, w_ref[:, i * 512:(i + 1) * 512], preferred_element_type=f32)

    lane = lax.broadcasted_iota(jnp.int32, (x.shape[0], 512), 1)
    lo_mask = (lane % HEAD_DIM) < (HEAD_DIM // 2)
    q = _rope(sec(0), cos, sin_signed, lo_mask) * (HEAD_DIM ** -0.5 * LOG2E)
    k = _rope(sec(1), cos, sin_signed, lo_mask)
    v = sec(2)
    bg = sec(3)
    cu = sec(4) * sec(5)
    return q, k, v, bg, cu


def _inproj_p_kernel(x_ref, sh_ref, sc_ref, ln_ref, w_ref, cos_ref, sin_ref, wc_ref,
                     qt_ref, kt_ref, kb_ref, v4_ref, vt_ref, cv_ref, st_ref, carry_ref):
    s = pl.program_id(0)
    b = pl.program_id(1)
    q, k, v, bg, cu = _inproj_common(x_ref[...], sh_ref[...], sc_ref[...], ln_ref[...], w_ref,
                                     cos_ref[...], sin_ref[...])
    qt_ref[...] = q.T.astype(bf16)
    kt_ref[...] = k.T
    kb_ref[...] = k.astype(bf16)
    vt_ref[...] = v.T.astype(bf16)
    for h in range(N_HEADS):
        v4_ref[pl.ds(h, v.shape[0], stride=N_HEADS), :] = v[:, h * V_DIM:(h + 1) * V_DIM]
    tm = cu.shape[0]
    prev = jnp.where(s > 0, carry_ref[b], 0.0)
    row = lax.broadcasted_iota(jnp.int32, cu.shape, 0)
    cu1 = jnp.where(row == 0, prev[1:2], pltpu.roll(cu, 1, 0))
    cu2 = jnp.where(row == 0, prev[0:1], jnp.where(row == 1, prev[1:2], pltpu.roll(cu, 2, 0)))
    wc = wc_ref[...]
    conv = wc[0:1] * cu2 + wc[1:2] * cu1 + wc[2:3] * cu
    cv_ref[...] = (bg * conv).astype(bf16)
    last2 = cu[tm - 2:tm]
    carry_ref[b, 0:2, :] = last2
    st_ref[b] = last2


def _inproj_p_call(x, mod3, ln_mix, w_in_bf, cos, sin_signed, w_conv):
    B, S, _ = x.shape
    ns = S // TM
    row = lambda s, b: (b, s, 0)
    col = lambda s, b: (b, 0, s)
    rows = lambda dt: jax.ShapeDtypeStruct((B, S, 512), dt)
    cols = lambda dt: jax.ShapeDtypeStruct((B, 512, S), dt)
    row_spec = pl.BlockSpec((None, TM, 512), row)
    col_spec = pl.BlockSpec((None, 512, TM), col)
    return pl.pallas_call(
        _inproj_p_kernel,
        grid=(ns, B),
        in_specs=[pl.BlockSpec((None, TM, D_MODEL), row),
                  pl.BlockSpec((None, 1, D_MODEL), lambda s, b: (b, 0, 0)),
                  pl.BlockSpec((None, 1, D_MODEL), lambda s, b: (b, 0, 1)),
                  pl.BlockSpec((1, D_MODEL), lambda s, b: (0, 0)),
                  pl.BlockSpec((D_MODEL, 3072), lambda s, b: (0, 0)),
                  pl.BlockSpec((TM, 512), lambda s, b: (s, 0)),
                  pl.BlockSpec((TM, 512), lambda s, b: (s, 0)),
                  pl.BlockSpec((CONV_K, CONV_WIDTH), lambda s, b: (0, 0))],
        out_specs=[col_spec, col_spec, row_spec, pl.BlockSpec((None, N_HEADS * TM, V_DIM), row),
                   pl.BlockSpec((None, None, 512, TM), lambda s, b: (b, s, 0, 0)), row_spec,
                   pl.BlockSpec((B, 2, CONV_WIDTH), lambda s, b: (0, 0, 0))],
        out_shape=[cols(bf16), cols(f32), rows(bf16), jax.ShapeDtypeStruct((B, N_HEADS * S, V_DIM), f32),
                   jax.ShapeDtypeStruct((B, ns, 512, TM), bf16), rows(bf16),
                   jax.ShapeDtypeStruct((B, 2, CONV_WIDTH), f32)],
        scratch_shapes=[pltpu.VMEM((B, 8, CONV_WIDTH), f32)],
        compiler_params=_cparams(("arbitrary", "arbitrary")),
        name="inproj_p",
    )(x, mod3, mod3, ln_mix, w_in_bf, cos, sin_signed, w_conv)


def _inproj_s_kernel(seq_len, x_ref, sh_ref, sc_ref, ln_ref, w_ref, cos_ref, sin_ref, wc_ref, st0_ref, st1_ref,
                     q_ref, kf_ref, vf_ref, cv_ref, cu_ref):
    q, k, v, bg, cu = _inproj_common(x_ref[...], sh_ref[...], sc_ref[...], ln_ref[...], w_ref,
                                     cos_ref[...], sin_ref[...])
    q_ref[...] = q.astype(bf16)
    kf_ref[...] = k
    vf_ref[...] = v
    l = lax.broadcasted_iota(jnp.int32, cu.shape, 0) % seq_len
    st0 = st0_ref[...]
    st1 = st1_ref[...]
    cu1 = jnp.where(l == 0, st1, pltpu.roll(cu, 1, 0))
    cu2 = jnp.where(l == 0, st0, jnp.where(l == 1, st1, pltpu.roll(cu, 2, 0)))
    wc = wc_ref[...]
    conv = wc[0:1] * cu2 + wc[1:2] * cu1 + wc[2:3] * cu
    cv_ref[...] = (bg * conv).astype(bf16)
    cu_ref[...] = cu


def _inproj_s_call(x, sh, sc, ln_mix, w_in_bf, cos, sin_signed, w_conv, st0, st1, seq_len):
    n = x.shape[0]
    full = lambda shape: pl.BlockSpec(shape, lambda i: (0,) * len(shape))
    return pl.pallas_call(
        functools.partial(_inproj_s_kernel, seq_len),
        grid=(1,),
        in_specs=[full((n, D_MODEL)), full((n, D_MODEL)), full((n, D_MODEL)), full((1, D_MODEL)),
                  full((D_MODEL, 3072)), full((n, 512)), full((n, 512)), full((CONV_K, CONV_WIDTH)),
                  full((n, CONV_WIDTH)), full((n, CONV_WIDTH))],
        out_specs=[full((n, 512))] * 5,
        out_shape=[jax.ShapeDtypeStruct((n, 512), bf16), jax.ShapeDtypeStruct((n, 512), f32),
                   jax.ShapeDtypeStruct((n, 512), f32), jax.ShapeDtypeStruct((n, 512), bf16),
                   jax.ShapeDtypeStruct((n, CONV_WIDTH), f32)],
        compiler_params=_cparams(("arbitrary",)),
        name="inproj_s",
    )(x, sh, sc, ln_mix, w_in_bf, cos, sin_signed, w_conv, st0, st1)


def _lam(lq1, lk1, lq2, lk2):
    a = jnp.sum(lq1 * lk1, axis=-1, keepdims=True)
    b = jnp.sum(lq2 * lk2, axis=-1, keepdims=True)
    return jnp.exp(a) - jnp.exp(b) + LAM_INIT


def _attn_p_body(i, qt_ref, k_ref, vt_ref, lam, g_ref, o_ref, m_ref, acc_ref):
    tq = qt_ref.shape[1]
    tk = tq
    m_ref[...] = jnp.full(m_ref.shape, -jnp.inf, f32)
    acc_ref[...] = jnp.zeros(acc_ref.shape, f32)

    def step(j, masked):
        k0 = pl.multiple_of(j * tk, tk)
        qt = qt_ref[...]
        row = lax.broadcasted_iota(jnp.int32, qt.shape, 0)
        zero = jnp.zeros_like(qt)
        q2t = jnp.concatenate([jnp.where(row < HEAD_DIM, qt, zero), jnp.where(row >= HEAD_DIM, qt, zero)], axis=1)
        st = jnp.dot(k_ref[pl.ds(k0, tk), :], q2t, preferred_element_type=f32)
        if masked:
            kpos = lax.broadcasted_iota(jnp.int32, st.shape, 0)
            c = lax.broadcasted_iota(jnp.int32, st.shape, 1)
            st = jnp.where(kpos <= jnp.where(c >= tq, c - tq, c), st, -jnp.inf)
        m_prev = m_ref[...]
        m_new = jnp.maximum(m_prev, jnp.max(st, axis=0, keepdims=True))
        alpha = jnp.exp2(m_prev - m_new)
        pt = jnp.exp2(st - m_new).astype(bf16)
        vt1 = jnp.concatenate([vt_ref[j], jnp.ones((ONES_ROWS, tk), bf16)], axis=0)
        acc_ref[...] = alpha * acc_ref[...] + jnp.dot(vt1, pt, preferred_element_type=f32)
        m_ref[...] = m_new

    def below_diagonal(j, c):
        step(j, False)
        return c

    lax.fori_loop(0, i, below_diagonal, 0)
    step(i, True)
    ot = acc_ref[0:V_DIM, :] / acc_ref[V_DIM:V_DIM + 1, :]
    dt = ot[:, :tq] - lam * ot[:, tq:]
    ms = jnp.mean(dt * dt, axis=0, keepdims=True)
    nt = dt * lax.rsqrt(ms + SUBLN_EPS) * g_ref[...] * (1.0 - LAM_INIT)
    o_ref[...] = nt.T.astype(bf16)


def _attn_s_body(q_ref, kn_ref, vn_ref, lam, g_ref, kt_refs, v_refs, o_ref):
    n_pages = len(kt_refs)
    nq = q_ref.shape[0]
    nr = nq * N_SUB
    qf = q_ref[...].astype(f32)
    sub = lax.broadcasted_iota(jnp.int32, (N_SUB, ATTN_WIDTH), 0)
    col = lax.broadcasted_iota(jnp.int32, (N_SUB, ATTN_WIDTH), 1)
    diag = (col // HEAD_DIM) == sub
    qbd = jnp.concatenate([jnp.where(diag, jnp.broadcast_to(qf[a:a + 1], (N_SUB, ATTN_WIDTH)), 0.0)
                           for a in range(nq)], axis=0).astype(bf16)
    kt_all = jnp.concatenate([kt_refs[p][...].astype(bf16) for p in range(n_pages)], axis=1)
    s_past = jnp.dot(qbd, kt_all, preferred_element_type=f32)
    s_new = lax.dot_general(qbd, kn_ref[...].astype(bf16), (((1,), (1,)), ((), ())),
                            preferred_element_type=f32)
    r = lax.broadcasted_iota(jnp.int32, s_new.shape, 0)
    c = lax.broadcasted_iota(jnp.int32, s_new.shape, 1)
    s_new = jnp.where(c <= r // N_SUB, s_new, -jnp.inf)
    m = jnp.maximum(jnp.max(s_past, axis=-1, keepdims=True), jnp.max(s_new, axis=-1, keepdims=True))
    p_past = jnp.exp2(s_past - m)
    p_new = jnp.exp2(s_new - m)
    l = jnp.sum(p_past, axis=-1, keepdims=True) + jnp.sum(p_new, axis=-1, keepdims=True)
    p_bf = p_past.astype(bf16)
    vn = vn_ref[...]
    rowhead = (lax.broadcasted_iota(jnp.int32, (nr, V_DIM), 0) % N_SUB) // 2
    o = jnp.zeros((nr, V_DIM), f32)
    for h in range(N_HEADS):
        v_all = jnp.concatenate([v_refs[p][pl.ds(h, PAGE, stride=N_HEADS), :].astype(bf16)
                                 for p in range(n_pages)], axis=0)
        acc = jnp.dot(p_bf, v_all, preferred_element_type=f32)
        for a in range(nq):
            acc = acc + p_new[:, a:a + 1] * vn[a:a + 1, h * V_DIM:(h + 1) * V_DIM]
        o = jnp.where(rowhead == h, acc, o)
    o = o / l
    d = o - lam * pltpu.roll(o, nr - 1, 0)
    d = _rms(d, g_ref[...], SUBLN_EPS) * (1.0 - LAM_INIT)
    for a in range(nq):
        rowv = jnp.concatenate([d[a * N_SUB + 2 * h:a * N_SUB + 2 * h + 1] for h in range(N_HEADS)], axis=1)
        o_ref[a:a + 1, :] = rowv.astype(o_ref.dtype)


def _attn_kernel(n_pages, pt_ref, qt_ref, k_ref, vt_ref, lq1_ref, lk1_ref, lq2_ref, lk2_ref, gcol_ref,
                 q_ref, kn_ref, vn_ref, grow_ref, kt_hbm, v_hbm, op_ref, os_ref, m_ref, acc_ref, kbuf, vbuf, sems):
    t = (pl.program_id(0) * pl.num_programs(1) + pl.program_id(1)) * pl.num_programs(2) + pl.program_id(2)
    n_steps = pl.num_programs(0) * pl.num_programs(1) * pl.num_programs(2)
    slot = t % 2

    def fetch(seq, dst):
        for p in range(n_pages):
            page = pt_ref[seq * n_pages + p]
            pltpu.make_async_copy(kt_hbm.at[page], kbuf.at[dst, p], sems.at[dst]).start()
            pltpu.make_async_copy(v_hbm.at[page], vbuf.at[dst, p], sems.at[dst]).start()

    @pl.when(t == 0)
    def _():
        fetch(0, 0)

    pltpu.make_async_copy(kt_hbm.at[pl.ds(0, n_pages)], kbuf.at[slot], sems.at[slot]).wait()
    pltpu.make_async_copy(v_hbm.at[pl.ds(0, n_pages)], vbuf.at[slot], sems.at[slot]).wait()

    @pl.when(t + 1 < n_steps)
    def _():
        fetch(t + 1, 1 - slot)

    lam = _lam(lq1_ref[...], lk1_ref[...], lq2_ref[...], lk2_ref[...])
    _attn_p_body(pl.program_id(2), qt_ref, k_ref, vt_ref, lam, gcol_ref, op_ref, m_ref, acc_ref)
    _attn_s_body(q_ref, kn_ref, vn_ref, lam, grow_ref, [kbuf.at[slot, p] for p in range(n_pages)],
                 [vbuf.at[slot, p] for p in range(n_pages)], os_ref)


def _attn_call(pt_flat, qt, k, vt, lq1, lk1, lq2, lk2, g, q_s, kn_s, vn_s, kt, v2, n_pages):
    B, S, _ = k.shape
    nq_tiles = S // TQ
    nb, nq, _ = q_s.shape
    assert nb == B * N_HEADS * nq_tiles, "one sample sequence per prompt grid step"
    seq = lambda b, h, i: (b * N_HEADS + h) * nq_tiles + i
    small = lambda n: pl.BlockSpec((1, n), lambda b, h, i, pt: (0, 0))
    page_rows = kt.shape[1]
    return pl.pallas_call(
        functools.partial(_attn_kernel, n_pages),
        grid_spec=pltpu.PrefetchScalarGridSpec(
            num_scalar_prefetch=1,
            grid=(B, N_HEADS, nq_tiles),
            in_specs=[pl.BlockSpec((None, V_DIM, TQ), lambda b, h, i, pt: (b, h, i)),
                      pl.BlockSpec((None, S, V_DIM), lambda b, h, i, pt: (b, 0, h)),
                      pl.BlockSpec((None, nq_tiles, V_DIM, TQ), lambda b, h, i, pt: (b, 0, h, 0)),
                      small(HEAD_DIM), small(HEAD_DIM), small(HEAD_DIM), small(HEAD_DIM),
                      pl.BlockSpec((V_DIM, 1), lambda b, h, i, pt: (0, 0)),
                      pl.BlockSpec((None, nq, ATTN_WIDTH), lambda b, h, i, pt: (seq(b, h, i), 0, 0)),
                      pl.BlockSpec((None, nq, ATTN_WIDTH), lambda b, h, i, pt: (seq(b, h, i), 0, 0)),
                      pl.BlockSpec((None, nq, ATTN_WIDTH), lambda b, h, i, pt: (seq(b, h, i), 0, 0)),
                      small(V_DIM), pl.BlockSpec(memory_space=pl.ANY), pl.BlockSpec(memory_space=pl.ANY)],
            out_specs=[pl.BlockSpec((None, TQ, V_DIM), lambda b, h, i, pt: (b, i, h)),
                       pl.BlockSpec((None, nq, ATTN_WIDTH), lambda b, h, i, pt: (seq(b, h, i), 0, 0))],
            scratch_shapes=[pltpu.VMEM((1, 2 * TQ), f32), pltpu.VMEM((V_DIM + ONES_ROWS, 2 * TQ), f32),
                            pltpu.VMEM((2, n_pages, page_rows, PAGE), f32),
                            pltpu.VMEM((2, n_pages, page_rows, PAGE), f32), pltpu.SemaphoreType.DMA((2,))],
        ),
        out_shape=[jax.ShapeDtypeStruct((B, S, ATTN_WIDTH), bf16), jax.ShapeDtypeStruct((nb, nq, ATTN_WIDTH), bf16)],
        compiler_params=_cparams(("arbitrary",) * 3),
        name="attn",
    )(pt_flat, qt, k, vt, lq1, lk1, lq2, lk2, g.reshape(V_DIM, 1), q_s, kn_s, vn_s, g, kt, v2)


SLAB = D_MODEL // LANES


def _store_slabs(ref, row0, x):
    n = x.shape[0]
    for a in range(SLAB):
        ref[pl.ds(SLAB * row0 + a, n, stride=SLAB), :] = x[:, a * LANES:(a + 1) * LANES]


def _load_slabs(ref, row0, n):
    return jnp.concatenate([ref[pl.ds(SLAB * row0 + a, n, stride=SLAB), :] for a in range(SLAB)], axis=1)


def _slab_copy(src_ref, src_row, dst_ref, dst_row, sem):
    src = src_ref.at[pl.ds(pl.multiple_of(src_row * SLAB, SLAB), SLAB), :]
    dst = dst_ref.at[pl.ds(pl.multiple_of(dst_row * SLAB, SLAB), SLAB), :]
    return pltpu.make_async_copy(src, dst, sem)


def _post_tile(x_ref, att_ref, cv_ref, ga_ref, shf_ref, scf_ref, ln_ref, wo_ref, wr_ref, br_ref, ltri_ref,
               x1_ref, h2_ref, rt_ref, rw_ref, cnt_ref, carry_ref):
    mix = (jnp.dot(att_ref[...], wo_ref[0:ATTN_WIDTH, :], preferred_element_type=f32)
           + jnp.dot(cv_ref[...], wo_ref[ATTN_WIDTH:, :], preferred_element_type=f32))
    x1 = x_ref[...] + ga_ref[...] * mix
    x1_ref[...] = x1
    h2 = _rms(x1, ln_ref[...], NORM_EPS) * (1.0 + scf_ref[...]) + shf_ref[...]
    _store_slabs(h2_ref, 0, h2)
    logits = jnp.dot(h2.astype(bf16), wr_ref[...], preferred_element_type=f32) + br_ref[...]
    lane = lax.broadcasted_iota(jnp.int32, logits.shape, 1)
    lane_f = lane.astype(f32)
    big = jnp.float32(1e9)
    neg = -jnp.inf

    def first_max(vals):
        mx = jnp.max(vals, axis=-1, keepdims=True)
        idx = jnp.min(jnp.where(vals == mx, lane_f, big), axis=-1, keepdims=True)
        return mx, idx

    gl = jnp.where((lane >= ROUTE_GRP_LANE) & (lane < ROUTE_GRP_LANE + N_GROUPS), logits, neg)
    gmax, gidx = first_max(gl)
    g_p = 1.0 / jnp.sum(jnp.exp(gl - gmax), axis=-1, keepdims=True)
    lo = (gidx - ROUTE_GRP_LANE) * EXP_PER_GROUP
    el = jnp.where((lane_f >= lo) & (lane_f < lo + EXP_PER_GROUP), logits, neg)
    v1, i1 = first_max(el)
    el2 = jnp.where(lane_f == i1, neg, el)
    v2, i2 = first_max(el2)
    t = jnp.exp(v2 - v1)
    w1 = g_p / (1.0 + t)
    w2 = g_p * t / (1.0 + t)
    oh1 = lane_f == i1
    oh2 = lane_f == i2
    cnt = jnp.where(oh1 | oh2, 1.0, 0.0)
    prefix = jnp.dot(ltri_ref[...], cnt.astype(bf16), preferred_element_type=f32) + carry_ref[...]
    r1 = jnp.sum(jnp.where(oh1, prefix, 0.0), axis=-1, keepdims=True)
    r2 = jnp.sum(jnp.where(oh2, prefix, 0.0), axis=-1, keepdims=True)
    ri = jnp.where(lane == 0, i1, jnp.where(lane == 1, i2, jnp.where(lane == 2, r1, jnp.where(lane == 3, r2, 0.0))))
    rt_ref[...] = ri.T[0:8, :].astype(jnp.int32)
    rw_ref[...] = jnp.where(lane == 0, w1, jnp.where(lane == 1, w2, 0.0))
    new_carry = carry_ref[...] + jnp.sum(cnt, axis=0, keepdims=True)
    carry_ref[...] = new_carry
    cnt_ref[...] = new_carry


def _post_kernel(n_tiles_p, xp_ref, attp_ref, cvp_ref, gap_ref, shfp_ref, scfp_ref,
                 xs_ref, atts_ref, cvs_ref, gas_ref, shfs_ref, scfs_ref, *rest):
    i = pl.program_id(0)
    carry_ref = rest[-1]

    @pl.when(i == 0)
    def _():
        carry_ref[...] = jnp.zeros(carry_ref.shape, f32)

    @pl.when(i < n_tiles_p)
    def _():
        _post_tile(xp_ref, attp_ref, cvp_ref, gap_ref, shfp_ref, scfp_ref, *rest)

    @pl.when(i >= n_tiles_p)
    def _():
        _post_tile(xs_ref, atts_ref, cvs_ref, gas_ref, shfs_ref, scfs_ref, *rest)


def _post_call(x_p, att_p, cv_p, mod_p, x_s, att_s, cv_s, mod_s, tiles_per_batch, ln_ffn, w_out_bf, w_r_bf, b_r, ltri):
    n_p, n_s = x_p.shape[0], x_s.shape[0]
    tp, ts = n_p // TM, n_s // TM
    n = n_p + n_s
    prow = lambda w: pl.BlockSpec((TM, w), lambda i: (jnp.minimum(i, tp - 1), 0))
    srow = lambda w: pl.BlockSpec((TM, w), lambda i: (jnp.maximum(i - tp, 0), 0))
    pmod = lambda sec: pl.BlockSpec((None, 1, D_MODEL), lambda i: (jnp.minimum(i, tp - 1) // tiles_per_batch, 0, sec))
    smod = lambda sec: pl.BlockSpec((TM, D_MODEL), lambda i: (jnp.maximum(i - tp, 0), sec))
    row = lambda w: pl.BlockSpec((TM, w), lambda i: (i, 0))
    const = lambda shape: pl.BlockSpec(shape, lambda i: (0, 0))
    return pl.pallas_call(
        functools.partial(_post_kernel, tp),
        grid=(tp + ts,),
        in_specs=[prow(D_MODEL), prow(ATTN_WIDTH), prow(CONV_WIDTH), pmod(2), pmod(3), pmod(4),
                  srow(D_MODEL), srow(ATTN_WIDTH), srow(CONV_WIDTH), smod(2), smod(3), smod(4),
                  const((1, D_MODEL)), const((D_MODEL, D_MODEL)), const((D_MODEL, LANES)), const((1, LANES)),
                  const((TM, TM))],
        out_specs=[row(D_MODEL), pl.BlockSpec((TM * SLAB, LANES), lambda i: (i, 0)),
                   pl.BlockSpec((8, TM), lambda i: (0, i)), row(LANES), const((1, LANES))],
        out_shape=[jax.ShapeDtypeStruct((n, D_MODEL), f32), jax.ShapeDtypeStruct((n * SLAB, LANES), f32),
                   jax.ShapeDtypeStruct((8, n), jnp.int32), jax.ShapeDtypeStruct((n, LANES), f32),
                   jax.ShapeDtypeStruct((1, LANES), f32)],
        scratch_shapes=[pltpu.VMEM((1, LANES), f32)],
        compiler_params=_cparams(("arbitrary",)),
        name="post",
    )(x_p, att_p, cv_p, mod_p, mod_p, mod_p, x_s, att_s, cv_s, mod_s, mod_s, mod_s,
      ln_ffn, w_out_bf, w_r_bf, b_r, ltri)


def _dispatch_kernel(pos0_ref, pos1_ref, src_ref, xs_ref, sem):
    t0 = pl.program_id(0) * TM

    def issue(r, c):
        t = t0 + r
        _slab_copy(src_ref, r, xs_ref, pos0_ref[t], sem).start(priority=0)
        _slab_copy(src_ref, r, xs_ref, pos1_ref[t], sem).start(priority=1)
        return c

    lax.fori_loop(0, TM, issue, 0, unroll=ROW_DMA_UNROLL)
    tile_copy = pltpu.make_async_copy(src_ref, xs_ref.at[pl.ds(0, TM * SLAB), :], sem)
    tile_copy.wait()
    tile_copy.wait()


def _dispatch_call(pos0, pos1, h2):
    tiles = h2.shape[0] // (TM * SLAB)
    return pl.pallas_call(
        _dispatch_kernel,
        grid_spec=pltpu.PrefetchScalarGridSpec(
            num_scalar_prefetch=2,
            grid=(tiles,),
            in_specs=[pl.BlockSpec((TM * SLAB, LANES), lambda i, p0, p1: (i, 0))],
            out_specs=pl.BlockSpec(memory_space=pl.ANY),
            scratch_shapes=[pltpu.SemaphoreType.DMA(())],
        ),
        out_shape=jax.ShapeDtypeStruct((2 * h2.shape[0], LANES), f32),
        compiler_params=_cparams(("arbitrary",)),
        name="dispatch",
    )(pos0, pos1, h2)


def _tile_chunk_copies(hbm_ref, tile, buf_ref, slot, sem, to_hbm):
    r0 = pl.multiple_of(tile * TE, TE)
    pairs = [(hbm_ref.at[pl.ds(r0, TE), a, :], buf_ref.at[slot, a]) for a in range(SLAB)]
    return [pltpu.make_async_copy(v, h, sem) if to_hbm else pltpu.make_async_copy(h, v, sem) for h, v in pairs]


def _experts_kernel(n_tiles, tile_ref, exp_ref, lo_ref, hi_ref, xs_ref, wg_ref, wu_ref, wd_ref, ys_ref,
                    wg_bf, wu_bf, wd_bf, cur_ref, xbuf, ybuf, xsem, ysem):
    w = pl.program_id(0)
    lo = lo_ref[w]
    hi = hi_ref[w]
    k = tile_ref[w]
    base = k * TE
    slot = k % 2

    @pl.when(w == 0)
    def _():
        cur_ref[0] = -1
        for c in _tile_chunk_copies(xs_ref, 0, xbuf, 0, xsem.at[0], False):
            c.start()

    @pl.when(hi > lo)
    def _():
        first = lo == base
        last = hi == base + TE

        @pl.when(first)
        def _():
            for c in _tile_chunk_copies(xs_ref, k, xbuf, slot, xsem.at[slot], False):
                c.wait()

            @pl.when(k + 1 < n_tiles)
            def _():
                for c in _tile_chunk_copies(xs_ref, k + 1, xbuf, 1 - slot, xsem.at[1 - slot], False):
                    c.start()

            @pl.when(k >= 2)
            def _():
                for c in _tile_chunk_copies(ys_ref, k - 2, ybuf, slot, ysem.at[slot], True):
                    c.wait()

        e = exp_ref[w]

        @pl.when(cur_ref[0] != e)
        def _():
            wg_bf[...] = wg_ref[...].astype(bf16)
            wu_bf[...] = wu_ref[...].astype(bf16)
            wd_bf[...] = wd_ref[...].astype(bf16)
            cur_ref[0] = e

        x = jnp.concatenate([xbuf[slot, a] for a in range(SLAB)], axis=1).astype(bf16)
        g = jnp.dot(x, wg_bf[...], preferred_element_type=f32)
        u = jnp.dot(x, wu_bf[...], preferred_element_type=f32)
        hid = (_silu(g) * u).astype(bf16)
        y = jnp.dot(hid, wd_bf[...], preferred_element_type=f32)
        row = base + lax.broadcasted_iota(jnp.int32, (TE, LANES), 0)
        mine = (row >= lo) & (row < hi)

        @pl.when(first)
        def _():
            for a in range(SLAB):
                ybuf[slot, a] = jnp.where(mine, y[:, a * LANES:(a + 1) * LANES], 0.0)

        @pl.when(jnp.logical_not(first))
        def _():
            for a in range(SLAB):
                ybuf[slot, a] = jnp.where(mine, y[:, a * LANES:(a + 1) * LANES], ybuf[slot, a])

        @pl.when(last)
        def _():
            for c in _tile_chunk_copies(ys_ref, k, ybuf, slot, ysem.at[slot], True):
                c.start()

    @pl.when(w == pl.num_programs(0) - 1)
    def _():
        for kk in (n_tiles - 2, n_tiles - 1):
            for c in _tile_chunk_copies(ys_ref, kk, ybuf, kk % 2, ysem.at[kk % 2], True):
                c.wait()


def _experts_call(tile_id, exp_id, seg_lo, seg_hi, xs, w_gate, w_up, w_down):
    n_items = tile_id.shape[0]
    n_rows = xs.shape[0] // SLAB
    n_tiles = n_rows // TE
    assert n_tiles >= 2
    idx = lambda w, t, e, lo, hi: (e[w], 0, 0)
    ys = pl.pallas_call(
        functools.partial(_experts_kernel, n_tiles),
        grid_spec=pltpu.PrefetchScalarGridSpec(
            num_scalar_prefetch=4,
            grid=(n_items,),
            in_specs=[pl.BlockSpec(memory_space=pl.ANY),
                      pl.BlockSpec((None, D_MODEL, D_EXPERT), idx),
                      pl.BlockSpec((None, D_MODEL, D_EXPERT), idx),
                      pl.BlockSpec((None, D_EXPERT, D_MODEL), idx)],
            out_specs=pl.BlockSpec(memory_space=pl.ANY),
            scratch_shapes=[pltpu.VMEM((D_MODEL, D_EXPERT), bf16), pltpu.VMEM((D_MODEL, D_EXPERT), bf16),
                            pltpu.VMEM((D_EXPERT, D_MODEL), bf16), pltpu.SMEM((1,), jnp.int32),
                            pltpu.VMEM((2, SLAB, TE, LANES), f32), pltpu.VMEM((2, SLAB, TE, LANES), f32),
                            pltpu.SemaphoreType.DMA((2,)), pltpu.SemaphoreType.DMA((2,))],
        ),
        out_shape=jax.ShapeDtypeStruct((n_rows, SLAB, LANES), f32),
        compiler_params=_cparams(("arbitrary",)),
        name="experts",
    )(tile_id, exp_id, seg_lo, seg_hi, xs.reshape(n_rows, SLAB, LANES), w_gate, w_up, w_down)
    return ys.reshape(xs.shape)


def _combine_kernel(n_tiles_p, pos0_ref, pos1_ref, x1_ref, rw_ref, gfp_ref, gfs_ref, ln_ref, ys_ref,
                    op_ref, os_ref, ybuf, sems):
    i = pl.program_id(0)
    n = pl.num_programs(0)
    half = 2 * TM

    def issue(tile, buf):
        def body(r, c):
            t = tile * TM + r
            _slab_copy(ys_ref, pos0_ref[t], ybuf, buf * half + r, sems.at[buf]).start(priority=0)
            _slab_copy(ys_ref, pos1_ref[t], ybuf, buf * half + TM + r, sems.at[buf]).start(priority=1)
            return c

        lax.fori_loop(0, TM, body, 0, unroll=ROW_DMA_UNROLL)

    @pl.when(i == 0)
    def _():
        issue(0, 0)

    @pl.when(i + 1 < n)
    def _():
        issue(i + 1, (i + 1) % 2)

    buf = i % 2
    tile_copy = pltpu.make_async_copy(ys_ref.at[pl.ds(0, TM * SLAB), :], ybuf.at[pl.ds(0, TM * SLAB), :], sems.at[buf])
    tile_copy.wait()
    tile_copy.wait()
    rw = rw_ref[...]
    moe = rw[:, 0:1] * _load_slabs(ybuf, buf * half, TM) + rw[:, 1:2] * _load_slabs(ybuf, buf * half + TM, TM)

    def finish(gf_ref, o_ref):
        x2 = x1_ref[...] + gf_ref[...] * moe
        o_ref[...] = _rms(x2, ln_ref[...], NORM_EPS)

    @pl.when(i < n_tiles_p)
    def _():
        finish(gfp_ref, op_ref)

    @pl.when(i >= n_tiles_p)
    def _():
        finish(gfs_ref, os_ref)


def _combine_call(pos0, pos1, x1, rw, mod_p, mod_s, tiles_per_batch, ln_final, ys):
    n = x1.shape[0]
    n_s = mod_s.shape[0]
    tp, ts = (n - n_s) // TM, n_s // TM
    pidx = lambda i: jnp.minimum(i, tp - 1)
    sidx = lambda i: jnp.maximum(i - tp, 0)
    return pl.pallas_call(
        functools.partial(_combine_kernel, tp),
        grid_spec=pltpu.PrefetchScalarGridSpec(
            num_scalar_prefetch=2,
            grid=(tp + ts,),
            in_specs=[pl.BlockSpec((TM, D_MODEL), lambda i, p0, p1: (i, 0)),
                      pl.BlockSpec((TM, LANES), lambda i, p0, p1: (i, 0)),
                      pl.BlockSpec((None, 1, D_MODEL), lambda i, p0, p1: (pidx(i) // tiles_per_batch, 0, 5)),
                      pl.BlockSpec((TM, D_MODEL), lambda i, p0, p1: (sidx(i), 5)),
                      pl.BlockSpec((1, D_MODEL), lambda i, p0, p1: (0, 0)),
                      pl.BlockSpec(memory_space=pl.ANY)],
            out_specs=[pl.BlockSpec((TM, D_MODEL), lambda i, p0, p1: (pidx(i), 0)),
                       pl.BlockSpec((TM, D_MODEL), lambda i, p0, p1: (sidx(i), 0))],
            scratch_shapes=[pltpu.VMEM((2 * 2 * TM * SLAB, LANES), f32), pltpu.SemaphoreType.DMA((2,))],
        ),
        out_shape=[jax.ShapeDtypeStruct((n - n_s, D_MODEL), f32), jax.ShapeDtypeStruct((n_s, D_MODEL), f32)],
        compiler_params=_cparams(("arbitrary",)),
        name="combine",
    )(pos0, pos1, x1, rw, mod_p, mod_s, ln_final, ys)


def _rope_tables(pos):
    inv = 1.0 / (ROPE_THETA ** (np.arange(0, HEAD_DIM, 2, dtype=np.float64) / HEAD_DIM))
    ang = np.asarray(pos, np.float64)[:, None] * inv[None, :]
    ang = np.concatenate([ang, ang], axis=-1)
    sign = np.where(np.arange(HEAD_DIM) < HEAD_DIM // 2, -1.0, 1.0)
    cos = np.tile(np.cos(ang), (1, N_SUB)).astype(np.float32)
    sin_signed = np.tile(np.sin(ang) * sign[None, :], (1, N_SUB)).astype(np.float32)
    return jnp.asarray(cos), jnp.asarray(sin_signed)


def _segments(counts, n_rows):
    n_tiles = n_rows // TE
    offs = jnp.concatenate([jnp.zeros((1,), jnp.int32), jnp.cumsum(counts)[:-1].astype(jnp.int32)])
    tiles = jnp.arange(n_tiles, dtype=jnp.int32) * TE
    rank_t = jnp.arange(n_tiles, dtype=jnp.int32) + jnp.sum(offs[None, :] < tiles[:, None], axis=1).astype(jnp.int32)
    rank_o = jnp.arange(N_EXPERTS, dtype=jnp.int32) + jnp.minimum(offs // TE + 1, n_tiles)
    vals = jnp.concatenate([tiles, offs])
    ranks = jnp.concatenate([rank_t, rank_o])
    n_items = n_tiles + N_EXPERTS
    w = jnp.arange(n_items, dtype=jnp.int32)
    seg_lo = jnp.sum(jnp.where(ranks[None, :] == w[:, None], vals[None, :], 0), axis=1).astype(jnp.int32)
    seg_hi = jnp.concatenate([seg_lo[1:], jnp.full((1,), n_rows, jnp.int32)])
    tile_id = jnp.minimum(seg_lo // TE, n_tiles - 1)
    exp_id = jnp.sum(offs[None, :] <= seg_lo[:, None], axis=1).astype(jnp.int32) - 1
    return offs, tile_id, exp_id, seg_lo, seg_hi


def kernel(x_prompt, x_sample, cache_k, cache_v, state_conv, page_table, c_prompt, c_sample, w_ada, b_ada, ln_mix, w_in, lam_q1, lam_k1, lam_q2, lam_k2, subln_g, w_conv, w_out, ln_ffn, w_router_grp, b_router_grp, w_router_exp, b_router_exp, w_gate, w_up, w_down, ln_final):
    B, S, _ = x_prompt.shape
    DB, L, _ = x_sample.shape
    n_phys = cache_k.shape[1]
    n_pages = page_table.shape[1]
    past = n_pages * PAGE
    n_p = B * S
    n_s = DB * L
    n_tok = n_p + n_s

    w_in_bf = w_in[0].astype(bf16)
    w_out_bf = w_out[0].astype(bf16)
    assert ROUTE_GRP_LANE == N_EXPERTS
    pad = LANES - N_EXPERTS - N_GROUPS
    w_r_bf = jnp.concatenate([w_router_exp[0], w_router_grp[0], jnp.zeros((D_MODEL, pad), f32)], axis=1).astype(bf16)
    b_r = jnp.concatenate([b_router_exp[0], b_router_grp[0], jnp.zeros((pad,), f32)])[None]
    cos_p, sin_p = _rope_tables(np.arange(S))
    cos_s, sin_s = _rope_tables(past + np.tile(np.arange(L), DB))
    ltri = jnp.asarray(np.tril(np.ones((TM, TM), np.float32), -1), bf16)

    mod = _mod_call(jnp.concatenate([c_prompt, c_sample], axis=0), w_ada[0], b_ada)
    mod_p = mod[:B].reshape(B, 1, 6 * D_MODEL)
    seq_of_row = np.arange(n_s, dtype=np.int32) // L
    mod_s = jnp.take(mod, B + seq_of_row, axis=0)

    qt_p, kt_p, kb_p, v4_p, vt_p, cv_p, st_p = _inproj_p_call(x_prompt, mod_p, ln_mix, w_in_bf, cos_p, sin_p, w_conv[0])
    xs_l = x_sample.reshape(n_s, D_MODEL)
    st_rows = [jnp.take(state_conv[0, :, t], seq_of_row, axis=0) for t in range(CONV_K - 1)]
    q_s, kf_s, vf_s, cv_s, cu_s = _inproj_s_call(xs_l, mod_s[:, 0:D_MODEL], mod_s[:, D_MODEL:2 * D_MODEL], ln_mix,
                                                 w_in_bf, cos_s, sin_s, w_conv[0], st_rows[0], st_rows[1], L)
    by_seq = lambda a: a.reshape(DB, L, -1)
    kt = jnp.transpose(cache_k[0], (0, 2, 3, 1)).reshape(n_phys, N_SUB * HEAD_DIM, PAGE)
    v2 = cache_v[0].reshape(n_phys, PAGE * N_HEADS, V_DIM)
    att_p, att_s_b = _attn_call(page_table.reshape(-1), qt_p, kb_p, vt_p, lam_q1, lam_k1, lam_q2, lam_k2, subln_g,
                                by_seq(q_s), by_seq(kf_s), by_seq(vf_s), kt, v2, n_pages)
    att_s = att_s_b.reshape(n_s, ATTN_WIDTH)

    x1, h2, rt, rw, cnt = _post_call(x_prompt.reshape(n_p, D_MODEL), att_p.reshape(n_p, ATTN_WIDTH),
                                     cv_p.reshape(n_p, CONV_WIDTH), mod_p, xs_l, att_s, cv_s, mod_s, S // TM,
                                     ln_ffn, w_out_bf, w_r_bf, b_r, ltri)

    counts = cnt[0, :N_EXPERTS].astype(jnp.int32)
    offs, tile_id, exp_id, seg_lo, seg_hi = _segments(counts, 2 * n_tok)
    e_col = jnp.arange(N_EXPERTS, dtype=jnp.int32)[:, None]
    start = lambda e_row: jnp.sum(jnp.where(e_row[None, :] == e_col, offs[:, None], 0), axis=0)
    pos0 = start(rt[0]) + rt[2]
    pos1 = start(rt[1]) + rt[3]

    xs_sorted = _dispatch_call(pos0, pos1, h2)
    ys = _experts_call(tile_id, exp_id, seg_lo, seg_hi, xs_sorted, w_gate[0], w_up[0], w_down[0])
    y_p, y_s = _combine_call(pos0, pos1, x1, rw, mod_p, mod_s, S // TM, ln_final.reshape(1, D_MODEL), ys)

    y_prompt = y_p.reshape(B, S, D_MODEL)
    y_sample = y_s.reshape(DB, L, D_MODEL)
    k_prompt = kt_p.reshape(B, N_SUB, HEAD_DIM, S).transpose(0, 3, 1, 2)[None]
    v_prompt = v4_p.reshape(1, B, S, N_HEADS, V_DIM)
    conv_prompt = st_p[None]
    k_sample = kf_s.reshape(1, DB, L, N_SUB, HEAD_DIM)
    v_sample = vf_s.reshape(1, DB, L, N_HEADS, V_DIM)
    conv_sample = cu_s.reshape(DB, L, CONV_WIDTH)[None, :, L - (CONV_K - 1):]
    return (y_prompt, y_sample, k_prompt, v_prompt, conv_prompt, k_sample, v_sample, conv_sample)
```

```python
import functools
import math

import jax
import jax.numpy as jnp
import numpy as np
from jax import lax
from jax.experimental import pallas as pl
from jax.experimental.pallas import tpu as pltpu

D_MODEL = 1024
ATTN_WIDTH = 512
CONV_WIDTH = 512
N_HEADS = 4
N_SUB = 8
HEAD_DIM = 64
V_DIM = 2 * HEAD_DIM
CONV_K = 3
ROPE_THETA = 10000.0
N_GROUPS = 4
EXP_PER_GROUP = 8
N_EXPERTS = 32
D_EXPERT = 256
NORM_EPS = 1e-6
SUBLN_EPS = 1e-5
LAM_INIT = 0.8 - 0.6 * math.exp(-0.3 * 0)
LOG2E = math.log2(math.e)
PAGE = 128
LANES = 128
ROUTE_GRP_LANE = 32

TM = 512
TQ = 512
TE = 512
ROW_DMA_UNROLL = 8
ONES_ROWS = 16
VMEM_LIMIT = 56 * 1024 * 1024

f32 = jnp.float32
bf16 = jnp.bfloat16


def _cparams(sem):
    return pltpu.CompilerParams(dimension_semantics=sem, vmem_limit_bytes=VMEM_LIMIT)


def _rms(x, g, eps):
    return x * lax.rsqrt(jnp.mean(x * x, axis=-1, keepdims=True) + eps) * g


def _silu(x):
    return x * (1.0 / (1.0 + jnp.exp(-x)))


def _mod_kernel(c_ref, w_ref, b_ref, o_ref):
    a = _silu(c_ref[...]).astype(bf16)
    o_ref[...] = jnp.dot(a, w_ref[...].astype(bf16), preferred_element_type=f32) + b_ref[...]


def _mod_call(c_all, w_ada, b_ada):
    n = c_all.shape[0]
    return pl.pallas_call(
        _mod_kernel,
        grid=(6,),
        in_specs=[pl.BlockSpec((n, D_MODEL), lambda j: (0, 0)),
                  pl.BlockSpec((D_MODEL, D_MODEL), lambda j: (0, j)),
                  pl.BlockSpec((1, D_MODEL), lambda j: (0, j))],
        out_specs=pl.BlockSpec((n, D_MODEL), lambda j: (0, j)),
        out_shape=jax.ShapeDtypeStruct((n, 6 * D_MODEL), f32),
        compiler_params=_cparams(("arbitrary",)),
        name="mod",
    )(c_all, w_ada, b_ada)


def _rope(t, cos, sin_signed, lo_mask):
    n = t.shape[-1]
    rot = jnp.where(lo_mask, pltpu.roll(t, n - HEAD_DIM // 2, 1), pltpu.roll(t, HEAD_DIM // 2, 1))
    return t * cos + rot * sin_signed


def _inproj_common(x, sh, sc, ln, w_ref, cos, sin_signed):
    h = (_rms(x, ln, NORM_EPS) * (1.0 + sc) + sh).astype(bf16)

    def sec(i):
        return jnp.dot(h, w_ref[:, i * 512:(i + 1) * 512], preferred_element_type=f32)

    lane = lax.broadcasted_iota(jnp.int32, (x.shape[0], 512), 1)
    lo_mask = (lane % HEAD_DIM) < (HEAD_DIM // 2)
    q = _rope(sec(0), cos, sin_signed, lo_mask) * (HEAD_DIM ** -0.5 * LOG2E)
    k = _rope(sec(1), cos, sin_signed, lo_mask)
    v = sec(2)
    bg = sec(3)
    cu = sec(4) * sec(5)
    return q, k, v, bg, cu


def _inproj_p_kernel(x_ref, sh_ref, sc_ref, ln_ref, w_ref, cos_ref, sin_ref, wc_ref,
                     qt_ref, kt_ref, kb_ref, v4_ref, vt_ref, cv_ref, st_ref, carry_ref):
    s = pl.program_id(0)
    b = pl.program_id(1)
    q, k, v, bg, cu = _inproj_common(x_ref[...], sh_ref[...], sc_ref[...], ln_ref[...], w_ref,
                                     cos_ref[...], sin_ref[...])
    qt_ref[...] = q.T.astype(bf16)
    kt_ref[...] = k.T
    kb_ref[...] = k.astype(bf16)
    vt_ref[...] = v.T.astype(bf16)
    for h in range(N_HEADS):
        v4_ref[pl.ds(h, v.shape[0], stride=N_HEADS), :] = v[:, h * V_DIM:(h + 1) * V_DIM]
    tm = cu.shape[0]
    prev = jnp.where(s > 0, carry_ref[b], 0.0)
    row = lax.broadcasted_iota(jnp.int32, cu.shape, 0)
    cu1 = jnp.where(row == 0, prev[1:2], pltpu.roll(cu, 1, 0))
    cu2 = jnp.where(row == 0, prev[0:1], jnp.where(row == 1, prev[1:2], pltpu.roll(cu, 2, 0)))
    wc = wc_ref[...]
    conv = wc[0:1] * cu2 + wc[1:2] * cu1 + wc[2:3] * cu
    cv_ref[...] = (bg * conv).astype(bf16)
    last2 = cu[tm - 2:tm]
    carry_ref[b, 0:2, :] = last2
    st_ref[b] = last2


def _inproj_p_call(x, mod3, ln_mix, w_in_bf, cos, sin_signed, w_conv):
    B, S, _ = x.shape
    ns = S // TM
    row = lambda s, b: (b, s, 0)
    col = lambda s, b: (b, 0, s)
    rows = lambda dt: jax.ShapeDtypeStruct((B, S, 512), dt)
    cols = lambda dt: jax.ShapeDtypeStruct((B, 512, S), dt)
    row_spec = pl.BlockSpec((None, TM, 512), row)
    col_spec = pl.BlockSpec((None, 512, TM), col)
    return pl.pallas_call(
        _inproj_p_kernel,
        grid=(ns, B),
        in_specs=[pl.BlockSpec((None, TM, D_MODEL), row),
                  pl.BlockSpec((None, 1, D_MODEL), lambda s, b: (b, 0, 0)),
                  pl.BlockSpec((None, 1, D_MODEL), lambda s, b: (b, 0, 1)),
                  pl.BlockSpec((1, D_MODEL), lambda s, b: (0, 0)),
                  pl.BlockSpec((D_MODEL, 3072), lambda s, b: (0, 0)),
                  pl.BlockSpec((TM, 512), lambda s, b: (s, 0)),
                  pl.BlockSpec((TM, 512), lambda s, b: (s, 0)),
                  pl.BlockSpec((CONV_K, CONV_WIDTH), lambda s, b: (0, 0))],
        out_specs=[col_spec, col_spec, row_spec, pl.BlockSpec((None, N_HEADS * TM, V_DIM), row),
                   pl.BlockSpec((None, None, 512, TM), lambda s, b: (b, s, 0, 0)), row_spec,
                   pl.BlockSpec((B, 2, CONV_WIDTH), lambda s, b: (0, 0, 0))],
        out_shape=[cols(bf16), cols(f32), rows(bf16), jax.ShapeDtypeStruct((B, N_HEADS * S, V_DIM), f32),
                   jax.ShapeDtypeStruct((B, ns, 512, TM), bf16), rows(bf16),
                   jax.ShapeDtypeStruct((B, 2, CONV_WIDTH), f32)],
        scratch_shapes=[pltpu.VMEM((B, 8, CONV_WIDTH), f32)],
        compiler_params=_cparams(("arbitrary", "arbitrary")),
        name="inproj_p",
    )(x, mod3, mod3, ln_mix, w_in_bf, cos, sin_signed, w_conv)


def _inproj_s_kernel(seq_len, x_ref, sh_ref, sc_ref, ln_ref, w_ref, cos_ref, sin_ref, wc_ref, st0_ref, st1_ref,
                     q_ref, kf_ref, vf_ref, cv_ref, cu_ref):
    q, k, v, bg, cu = _inproj_common(x_ref[...], sh_ref[...], sc_ref[...], ln_ref[...], w_ref,
                                     cos_ref[...], sin_ref[...])
    q_ref[...] = q.astype(bf16)
    kf_ref[...] = k
    vf_ref[...] = v
    l = lax.broadcasted_iota(jnp.int32, cu.shape, 0) % seq_len
    st0 = st0_ref[...]
    st1 = st1_ref[...]
    cu1 = jnp.where(l == 0, st1, pltpu.roll(cu, 1, 0))
    cu2 = jnp.where(l == 0, st0, jnp.where(l == 1, st1, pltpu.roll(cu, 2, 0)))
    wc = wc_ref[...]
    conv = wc[0:1] * cu2 + wc[1:2] * cu1 + wc[2:3] * cu
    cv_ref[...] = (bg * conv).astype(bf16)
    cu_ref[...] = cu


def _inproj_s_call(x, sh, sc, ln_mix, w_in_bf, cos, sin_signed, w_conv, st0, st1, seq_len):
    n = x.shape[0]
    full = lambda shape: pl.BlockSpec(shape, lambda i: (0,) * len(shape))
    return pl.pallas_call(
        functools.partial(_inproj_s_kernel, seq_len),
        grid=(1,),
        in_specs=[full((n, D_MODEL)), full((n, D_MODEL)), full((n, D_MODEL)), full((1, D_MODEL)),
                  full((D_MODEL, 3072)), full((n, 512)), full((n, 512)), full((CONV_K, CONV_WIDTH)),
                  full((n, CONV_WIDTH)), full((n, CONV_WIDTH))],
        out_specs=[full((n, 512))] * 5,
        out_shape=[jax.ShapeDtypeStruct((n, 512), bf16), jax.ShapeDtypeStruct((n, 512), f32),
                   jax.ShapeDtypeStruct((n, 512), f32), jax.ShapeDtypeStruct((n, 512), bf16),
                   jax.ShapeDtypeStruct((n, CONV_WIDTH), f32)],
        compiler_params=_cparams(("arbitrary",)),
        name="inproj_s",
    )(x, sh, sc, ln_mix, w_in_bf, cos, sin_signed, w_conv, st0, st1)


def _lam(lq1, lk1, lq2, lk2):
    a = jnp.sum(lq1 * lk1, axis=-1, keepdims=True)
    b = jnp.sum(lq2 * lk2, axis=-1, keepdims=True)
    return jnp.exp(a) - jnp.exp(b) + LAM_INIT


def _attn_p_body(i, qt_ref, k_ref, vt_ref, lam, g_ref, o_ref, m_ref, acc_ref):
    tq = qt_ref.shape[1]
    tk = tq
    m_ref[...] = jnp.full(m_ref.shape, -jnp.inf, f32)
    acc_ref[...] = jnp.zeros(acc_ref.shape, f32)

    def step(j, masked):
        k0 = pl.multiple_of(j * tk, tk)
        qt = qt_ref[...]
        row = lax.broadcasted_iota(jnp.int32, qt.shape, 0)
        zero = jnp.zeros_like(qt)
        q2t = jnp.concatenate([jnp.where(row < HEAD_DIM, qt, zero), jnp.where(row >= HEAD_DIM, qt, zero)], axis=1)
        st = jnp.dot(k_ref[pl.ds(k0, tk), :], q2t, preferred_element_type=f32)
        if masked:
            kpos = lax.broadcasted_iota(jnp.int32, st.shape, 0)
            c = lax.broadcasted_iota(jnp.int32, st.shape, 1)
            st = jnp.where(kpos <= jnp.where(c >= tq, c - tq, c), st, -jnp.inf)
        m_prev = m_ref[...]
        m_new = jnp.maximum(m_prev, jnp.max(st, axis=0, keepdims=True))
        alpha = jnp.exp2(m_prev - m_new)
        pt = jnp.exp2(st - m_new).astype(bf16)
        vt1 = jnp.concatenate([vt_ref[j], jnp.ones((ONES_ROWS, tk), bf16)], axis=0)
        acc_ref[...] = alpha * acc_ref[...] + jnp.dot(vt1, pt, preferred_element_type=f32)
        m_ref[...] = m_new

    def below_diagonal(j, c):
        step(j, False)
        return c

    lax.fori_loop(0, i, below_diagonal, 0)
    step(i, True)
    ot = acc_ref[0:V_DIM, :] / acc_ref[V_DIM:V_DIM + 1, :]
    dt = ot[:, :tq] - lam * ot[:, tq:]
    ms = jnp.mean(dt * dt, axis=0, keepdims=True)
    nt = dt * lax.rsqrt(ms + SUBLN_EPS) * g_ref[...] * (1.0 - LAM_INIT)
    o_ref[...] = nt.T.astype(bf16)


def _attn_s_body(q_ref, kn_ref, vn_ref, lam, g_ref, kt_refs, v_refs, o_ref):
    n_pages = len(kt_refs)
    nq = q_ref.shape[0]
    nr = nq * N_SUB
    qf = q_ref[...].astype(f32)
    sub = lax.broadcasted_iota(jnp.int32, (N_SUB, ATTN_WIDTH), 0)
    col = lax.broadcasted_iota(jnp.int32, (N_SUB, ATTN_WIDTH), 1)
    diag = (col // HEAD_DIM) == sub
    qbd = jnp.concatenate([jnp.where(diag, jnp.broadcast_to(qf[a:a + 1], (N_SUB, ATTN_WIDTH)), 0.0)
                           for a in range(nq)], axis=0).astype(bf16)
    kt_all = jnp.concatenate([kt_refs[p][...].astype(bf16) for p in range(n_pages)], axis=1)
    s_past = jnp.dot(qbd, kt_all, preferred_element_type=f32)
    s_new = lax.dot_general(qbd, kn_ref[...].astype(bf16), (((1,), (1,)), ((), ())),
                            preferred_element_type=f32)
    r = lax.broadcasted_iota(jnp.int32, s_new.shape, 0)
    c = lax.broadcasted_iota(jnp.int32, s_new.shape, 1)
    s_new = jnp.where(c <= r // N_SUB, s_new, -jnp.inf)
    m = jnp.maximum(jnp.max(s_past, axis=-1, keepdims=True), jnp.max(s_new, axis=-1, keepdims=True))
    p_past = jnp.exp2(s_past - m)
    p_new = jnp.exp2(s_new - m)
    l = jnp.sum(p_past, axis=-1, keepdims=True) + jnp.sum(p_new, axis=-1, keepdims=True)
    p_bf = p_past.astype(bf16)
    vn = vn_ref[...]
    rowhead = (lax.broadcasted_iota(jnp.int32, (nr, V_DIM), 0) % N_SUB) // 2
    o = jnp.zeros((nr, V_DIM), f32)
    for h in range(N_HEADS):
        v_all = jnp.concatenate([v_refs[p][pl.ds(h, PAGE, stride=N_HEADS), :].astype(bf16)
                                 for p in range(n_pages)], axis=0)
        acc = jnp.dot(p_bf, v_all, preferred_element_type=f32)
        for a in range(nq):
            acc = acc + p_new[:, a:a + 1] * vn[a:a + 1, h * V_DIM:(h + 1) * V_DIM]
        o = jnp.where(rowhead == h, acc, o)
    o = o / l
    d = o - lam * pltpu.roll(o, nr - 1, 0)
    d = _rms(d, g_ref[...], SUBLN_EPS) * (1.0 - LAM_INIT)
    for a in range(nq):
        rowv = jnp.concatenate([d[a * N_SUB + 2 * h:a * N_SUB + 2 * h + 1] for h in range(N_HEADS)], axis=1)
        o_ref[a:a + 1, :] = rowv.astype(o_ref.dtype)


def _attn_kernel(n_pages, pt_ref, qt_ref, k_ref, vt_ref, lq1_ref, lk1_ref, lq2_ref, lk2_ref, gcol_ref,
                 q_ref, kn_ref, vn_ref, grow_ref, kt_hbm, v_hbm, op_ref, os_ref, m_ref, acc_ref, kbuf, vbuf, sems):
    t = (pl.program_id(0) * pl.num_programs(1) + pl.program_id(1)) * pl.num_programs(2) + pl.program_id(2)
    n_steps = pl.num_programs(0) * pl.num_programs(1) * pl.num_programs(2)
    slot = t % 2

    def fetch(seq, dst):
        for p in range(n_pages):
            page = pt_ref[seq * n_pages + p]
            pltpu.make_async_copy(kt_hbm.at[page], kbuf.at[dst, p], sems.at[dst]).start()
            pltpu.make_async_copy(v_hbm.at[page], vbuf.at[dst, p], sems.at[dst]).start()

    @pl.when(t == 0)
    def _():
        fetch(0, 0)

    pltpu.make_async_copy(kt_hbm.at[pl.ds(0, n_pages)], kbuf.at[slot], sems.at[slot]).wait()
    pltpu.make_async_copy(v_hbm.at[pl.ds(0, n_pages)], vbuf.at[slot], sems.at[slot]).wait()

    @pl.when(t + 1 < n_steps)
    def _():
        fetch(t + 1, 1 - slot)

    lam = _lam(lq1_ref[...], lk1_ref[...], lq2_ref[...], lk2_ref[...])
    _attn_p_body(pl.program_id(2), qt_ref, k_ref, vt_ref, lam, gcol_ref, op_ref, m_ref, acc_ref)
    _attn_s_body(q_ref, kn_ref, vn_ref, lam, grow_ref, [kbuf.at[slot, p] for p in range(n_pages)],
                 [vbuf.at[slot, p] for p in range(n_pages)], os_ref)


def _attn_call(pt_flat, qt, k, vt, lq1, lk1, lq2, lk2, g, q_s, kn_s, vn_s, kt, v2, n_pages):
    B, S, _ = k.shape
    nq_tiles = S // TQ
    nb, nq, _ = q_s.shape
    assert nb == B * N_HEADS * nq_tiles, "one sample sequence per prompt grid step"
    seq = lambda b, h, i: (b * N_HEADS + h) * nq_tiles + i
    small = lambda n: pl.BlockSpec((1, n), lambda b, h, i, pt: (0, 0))
    page_rows = kt.shape[1]
    return pl.pallas_call(
        functools.partial(_attn_kernel, n_pages),
        grid_spec=pltpu.PrefetchScalarGridSpec(
            num_scalar_prefetch=1,
            grid=(B, N_HEADS, nq_tiles),
            in_specs=[pl.BlockSpec((None, V_DIM, TQ), lambda b, h, i, pt: (b, h, i)),
                      pl.BlockSpec((None, S, V_DIM), lambda b, h, i, pt: (b, 0, h)),
                      pl.BlockSpec((None, nq_tiles, V_DIM, TQ), lambda b, h, i, pt: (b, 0, h, 0)),
                      small(HEAD_DIM), small(HEAD_DIM), small(HEAD_DIM), small(HEAD_DIM),
                      pl.BlockSpec((V_DIM, 1), lambda b, h, i, pt: (0, 0)),
                      pl.BlockSpec((None, nq, ATTN_WIDTH), lambda b, h, i, pt: (seq(b, h, i), 0, 0)),
                      pl.BlockSpec((None, nq, ATTN_WIDTH), lambda b, h, i, pt: (seq(b, h, i), 0, 0)),
                      pl.BlockSpec((None, nq, ATTN_WIDTH), lambda b, h, i, pt: (seq(b, h, i), 0, 0)),
                      small(V_DIM), pl.BlockSpec(memory_space=pl.ANY), pl.BlockSpec(memory_space=pl.ANY)],
            out_specs=[pl.BlockSpec((None, TQ, V_DIM), lambda b, h, i, pt: (b, i, h)),
                       pl.BlockSpec((None, nq, ATTN_WIDTH), lambda b, h, i, pt: (seq(b, h, i), 0, 0))],
            scratch_shapes=[pltpu.VMEM((1, 2 * TQ), f32), pltpu.VMEM((V_DIM + ONES_ROWS, 2 * TQ), f32),
                            pltpu.VMEM((2, n_pages, page_rows, PAGE), f32),
                            pltpu.VMEM((2, n_pages, page_rows, PAGE), f32), pltpu.SemaphoreType.DMA((2,))],
        ),
        out_shape=[jax.ShapeDtypeStruct((B, S, ATTN_WIDTH), bf16), jax.ShapeDtypeStruct((nb, nq, ATTN_WIDTH), bf16)],
        compiler_params=_cparams(("arbitrary",) * 3),
        name="attn",
    )(pt_flat, qt, k, vt, lq1, lk1, lq2, lk2, g.reshape(V_DIM, 1), q_s, kn_s, vn_s, g, kt, v2)


SLAB = D_MODEL // LANES


def _store_slabs(ref, row0, x):
    n = x.shape[0]
    for a in range(SLAB):
        ref[pl.ds(SLAB * row0 + a, n, stride=SLAB), :] = x[:, a * LANES:(a + 1) * LANES]


def _load_slabs(ref, row0, n):
    return jnp.concatenate([ref[pl.ds(SLAB * row0 + a, n, stride=SLAB), :] for a in range(SLAB)], axis=1)


def _slab_copy(src_ref, src_row, dst_ref, dst_row, sem):
    src = src_ref.at[pl.ds(pl.multiple_of(src_row * SLAB, SLAB), SLAB), :]
    dst = dst_ref.at[pl.ds(pl.multiple_of(dst_row * SLAB, SLAB), SLAB), :]
    return pltpu.make_async_copy(src, dst, sem)


def _post_tile(x_ref, att_ref, cv_ref, ga_ref, shf_ref, scf_ref, ln_ref, wo_ref, wr_ref, br_ref, ltri_ref,
               x1_ref, h2_ref, rt_ref, rw_ref, cnt_ref, carry_ref):
    mix = (jnp.dot(att_ref[...], wo_ref[0:ATTN_WIDTH, :], preferred_element_type=f32)
           + jnp.dot(cv_ref[...], wo_ref[ATTN_WIDTH:, :], preferred_element_type=f32))
    x1 = x_ref[...] + ga_ref[...] * mix
    x1_ref[...] = x1
    h2 = _rms(x1, ln_ref[...], NORM_EPS) * (1.0 + scf_ref[...]) + shf_ref[...]
    _store_slabs(h2_ref, 0, h2)
    logits = jnp.dot(h2.astype(bf16), wr_ref[...], preferred_element_type=f32) + br_ref[...]
    lane = lax.broadcasted_iota(jnp.int32, logits.shape, 1)
    lane_f = lane.astype(f32)
    big = jnp.float32(1e9)
    neg = -jnp.inf

    def first_max(vals):
        mx = jnp.max(vals, axis=-1, keepdims=True)
        idx = jnp.min(jnp.where(vals == mx, lane_f, big), axis=-1, keepdims=True)
        return mx, idx

    gl = jnp.where((lane >= ROUTE_GRP_LANE) & (lane < ROUTE_GRP_LANE + N_GROUPS), logits, neg)
    gmax, gidx = first_max(gl)
    g_p = 1.0 / jnp.sum(jnp.exp(gl - gmax), axis=-1, keepdims=True)
    lo = (gidx - ROUTE_GRP_LANE) * EXP_PER_GROUP
    el = jnp.where((lane_f >= lo) & (lane_f < lo + EXP_PER_GROUP), logits, neg)
    v1, i1 = first_max(el)
    el2 = jnp.where(lane_f == i1, neg, el)
    v2, i2 = first_max(el2)
    t = jnp.exp(v2 - v1)
    w1 = g_p / (1.0 + t)
    w2 = g_p * t / (1.0 + t)
    oh1 = lane_f == i1
    oh2 = lane_f == i2
    cnt = jnp.where(oh1 | oh2, 1.0, 0.0)
    prefix = jnp.dot(ltri_ref[...], cnt.astype(bf16), preferred_element_type=f32) + carry_ref[...]
    r1 = jnp.sum(jnp.where(oh1, prefix, 0.0), axis=-1, keepdims=True)
    r2 = jnp.sum(jnp.where(oh2, prefix, 0.0), axis=-1, keepdims=True)
    ri = jnp.where(lane == 0, i1, jnp.where(lane == 1, i2, jnp.where(lane == 2, r1, jnp.where(lane == 3, r2, 0.0))))
    rt_ref[...] = ri.T[0:8, :].astype(jnp.int32)
    rw_ref[...] = jnp.where(lane == 0, w1, jnp.where(lane == 1, w2, 0.0))
    new_carry = carry_ref[...] + jnp.sum(cnt, axis=0, keepdims=True)
    carry_ref[...] = new_carry
    cnt_ref[...] = new_carry


def _post_kernel(n_tiles_p, xp_ref, attp_ref, cvp_ref, gap_ref, shfp_ref, scfp_ref,
                 xs_ref, atts_ref, cvs_ref, gas_ref, shfs_ref, scfs_ref, *rest):
    i = pl.program_id(0)
    carry_ref = rest[-1]

    @pl.when(i == 0)
    def _():
        carry_ref[...] = jnp.zeros(carry_ref.shape, f32)

    @pl.when(i < n_tiles_p)
    def _():
        _post_tile(xp_ref, attp_ref, cvp_ref, gap_ref, shfp_ref, scfp_ref, *rest)

    @pl.when(i >= n_tiles_p)
    def _():
        _post_tile(xs_ref, atts_ref, cvs_ref, gas_ref, shfs_ref, scfs_ref, *rest)


def _post_call(x_p, att_p, cv_p, mod_p, x_s, att_s, cv_s, mod_s, tiles_per_batch, ln_ffn, w_out_bf, w_r_bf, b_r, ltri):
    n_p, n_s = x_p.shape[0], x_s.shape[0]
    tp, ts = n_p // TM, n_s // TM
    n = n_p + n_s
    prow = lambda w: pl.BlockSpec((TM, w), lambda i: (jnp.minimum(i, tp - 1), 0))
    srow = lambda w: pl.BlockSpec((TM, w), lambda i: (jnp.maximum(i - tp, 0), 0))
    pmod = lambda sec: pl.BlockSpec((None, 1, D_MODEL), lambda i: (jnp.minimum(i, tp - 1) // tiles_per_batch, 0, sec))
    smod = lambda sec: pl.BlockSpec((TM, D_MODEL), lambda i: (jnp.maximum(i - tp, 0), sec))
    row = lambda w: pl.BlockSpec((TM, w), lambda i: (i, 0))
    const = lambda shape: pl.BlockSpec(shape, lambda i: (0, 0))
    return pl.pallas_call(
        functools.partial(_post_kernel, tp),
        grid=(tp + ts,),
        in_specs=[prow(D_MODEL), prow(ATTN_WIDTH), prow(CONV_WIDTH), pmod(2), pmod(3), pmod(4),
                  srow(D_MODEL), srow(ATTN_WIDTH), srow(CONV_WIDTH), smod(2), smod(3), smod(4),
                  const((1, D_MODEL)), const((D_MODEL, D_MODEL)), const((D_MODEL, LANES)), const((1, LANES)),
                  const((TM, TM))],
        out_specs=[row(D_MODEL), pl.BlockSpec((TM * SLAB, LANES), lambda i: (i, 0)),
                   pl.BlockSpec((8, TM), lambda i: (0, i)), row(LANES), const((1, LANES))],
        out_shape=[jax.ShapeDtypeStruct((n, D_MODEL), f32), jax.ShapeDtypeStruct((n * SLAB, LANES), f32),
                   jax.ShapeDtypeStruct((8, n), jnp.int32), jax.ShapeDtypeStruct((n, LANES), f32),
                   jax.ShapeDtypeStruct((1, LANES), f32)],
        scratch_shapes=[pltpu.VMEM((1, LANES), f32)],
        compiler_params=_cparams(("arbitrary",)),
        name="post",
    )(x_p, att_p, cv_p, mod_p, mod_p, mod_p, x_s, att_s, cv_s, mod_s, mod_s, mod_s,
      ln_ffn, w_out_bf, w_r_bf, b_r, ltri)


def _dispatch_kernel(pos0_ref, pos1_ref, src_ref, xs_ref, sem):
    t0 = pl.program_id(0) * TM

    def issue(r, c):
        t = t0 + r
        _slab_copy(src_ref, r, xs_ref, pos0_ref[t], sem).start(priority=0)
        _slab_copy(src_ref, r, xs_ref, pos1_ref[t], sem).start(priority=1)
        return c

    lax.fori_loop(0, TM, issue, 0, unroll=ROW_DMA_UNROLL)
    tile_copy = pltpu.make_async_copy(src_ref, xs_ref.at[pl.ds(0, TM * SLAB), :], sem)
    tile_copy.wait()
    tile_copy.wait()


def _dispatch_call(pos0, pos1, h2):
    tiles = h2.shape[0] // (TM * SLAB)
    return pl.pallas_call(
        _dispatch_kernel,
        grid_spec=pltpu.PrefetchScalarGridSpec(
            num_scalar_prefetch=2,
            grid=(tiles,),
            in_specs=[pl.BlockSpec((TM * SLAB, LANES), lambda i, p0, p1: (i, 0))],
            out_specs=pl.BlockSpec(memory_space=pl.ANY),
            scratch_shapes=[pltpu.SemaphoreType.DMA(())],
        ),
        out_shape=jax.ShapeDtypeStruct((2 * h2.shape[0], LANES), f32),
        compiler_params=_cparams(("arbitrary",)),
        name="dispatch",
    )(pos0, pos1, h2)


def _tile_chunk_copies(hbm_ref, tile, buf_ref, slot, sem, to_hbm):
    r0 = pl.multiple_of(tile * TE, TE)
    pairs = [(hbm_ref.at[pl.ds(r0, TE), a, :], buf_ref.at[slot, a]) for a in range(SLAB)]
    return [pltpu.make_async_copy(v, h, sem) if to_hbm else pltpu.make_async_copy(h, v, sem) for h, v in pairs]


def _experts_kernel(n_tiles, tile_ref, exp_ref, lo_ref, hi_ref, xs_ref, wg_ref, wu_ref, wd_ref, ys_ref,
                    wg_bf, wu_bf, wd_bf, cur_ref, xbuf, ybuf, xsem, ysem):
    w = pl.program_id(0)
    lo = lo_ref[w]
    hi = hi_ref[w]
    k = tile_ref[w]
    base = k * TE
    slot = k % 2

    @pl.when(w == 0)
    def _():
        cur_ref[0] = -1
        for c in _tile_chunk_copies(xs_ref, 0, xbuf, 0, xsem.at[0], False):
            c.start()

    @pl.when(hi > lo)
    def _():
        first = lo == base
        last = hi == base + TE

        @pl.when(first)
        def _():
            for c in _tile_chunk_copies(xs_ref, k, xbuf, slot, xsem.at[slot], False):
                c.wait()

            @pl.when(k + 1 < n_tiles)
            def _():
                for c in _tile_chunk_copies(xs_ref, k + 1, xbuf, 1 - slot, xsem.at[1 - slot], False):
                    c.start()

            @pl.when(k >= 2)
            def _():
                for c in _tile_chunk_copies(ys_ref, k - 2, ybuf, slot, ysem.at[slot], True):
                    c.wait()

        e = exp_ref[w]

        @pl.when(cur_ref[0] != e)
        def _():
            wg_bf[...] = wg_ref[...].astype(bf16)
            wu_bf[...] = wu_ref[...].astype(bf16)
            wd_bf[...] = wd_ref[...].astype(bf16)
            cur_ref[0] = e

        x = jnp.concatenate([xbuf[slot, a] for a in range(SLAB)], axis=1).astype(bf16)
        g = jnp.dot(x, wg_bf[...], preferred_element_type=f32)
        u = jnp.dot(x, wu_bf[...], preferred_element_type=f32)
        hid = (_silu(g) * u).astype(bf16)
        y = jnp.dot(hid, wd_bf[...], preferred_element_type=f32)
        row = base + lax.broadcasted_iota(jnp.int32, (TE, LANES), 0)
        mine = (row >= lo) & (row < hi)

        @pl.when(first)
        def _():
            for a in range(SLAB):
                ybuf[slot, a] = jnp.where(mine, y[:, a * LANES:(a + 1) * LANES], 0.0)

        @pl.when(jnp.logical_not(first))
        def _():
            for a in range(SLAB):
                ybuf[slot, a] = jnp.where(mine, y[:, a * LANES:(a + 1) * LANES], ybuf[slot, a])

        @pl.when(last)
        def _():
            for c in _tile_chunk_copies(ys_ref, k, ybuf, slot, ysem.at[slot], True):
                c.start()

    @pl.when(w == pl.num_programs(0) - 1)
    def _():
        for kk in (n_tiles - 2, n_tiles - 1):
            for c in _tile_chunk_copies(ys_ref, kk, ybuf, kk % 2, ysem.at[kk % 2], True):
                c.wait()


def _experts_call(tile_id, exp_id, seg_lo, seg_hi, xs, w_gate, w_up, w_down):
    n_items = tile_id.shape[0]
    n_rows = xs.shape[0] // SLAB
    n_tiles = n_rows // TE
    assert n_tiles >= 2
    idx = lambda w, t, e, lo, hi: (e[w], 0, 0)
    ys = pl.pallas_call(
        functools.partial(_experts_kernel, n_tiles),
        grid_spec=pltpu.PrefetchScalarGridSpec(
            num_scalar_prefetch=4,
            grid=(n_items,),
            in_specs=[pl.BlockSpec(memory_space=pl.ANY),
                      pl.BlockSpec((None, D_MODEL, D_EXPERT), idx),
                      pl.BlockSpec((None, D_MODEL, D_EXPERT), idx),
                      pl.BlockSpec((None, D_EXPERT, D_MODEL), idx)],
            out_specs=pl.BlockSpec(memory_space=pl.ANY),
            scratch_shapes=[pltpu.VMEM((D_MODEL, D_EXPERT), bf16), pltpu.VMEM((D_MODEL, D_EXPERT), bf16),
                            pltpu.VMEM((D_EXPERT, D_MODEL), bf16), pltpu.SMEM((1,), jnp.int32),
                            pltpu.VMEM((2, SLAB, TE, LANES), f32), pltpu.VMEM((2, SLAB, TE, LANES), f32),
                            pltpu.SemaphoreType.DMA((2,)), pltpu.SemaphoreType.DMA((2,))],
        ),
        out_shape=jax.ShapeDtypeStruct((n_rows, SLAB, LANES), f32),
        compiler_params=_cparams(("arbitrary",)),
        name="experts",
    )(tile_id, exp_id, seg_lo, seg_hi, xs.reshape(n_rows, SLAB, LANES), w_gate, w_up, w_down)
    return ys.reshape(xs.shape)


def _combine_kernel(n_tiles_p, pos0_ref, pos1_ref, x1_ref, rw_ref, gfp_ref, gfs_ref, ln_ref, ys_ref,
                    op_ref, os_ref, ybuf, sems):
    i = pl.program_id(0)
    n = pl.num_programs(0)
    half = 2 * TM

    def issue(tile, buf):
        def body(r, c):
            t = tile * TM + r
            _slab_copy(ys_ref, pos0_ref[t], ybuf, buf * half + r, sems.at[buf]).start(priority=0)
            _slab_copy(ys_ref, pos1_ref[t], ybuf, buf * half + TM + r, sems.at[buf]).start(priority=1)
            return c

        lax.fori_loop(0, TM, body, 0, unroll=ROW_DMA_UNROLL)

    @pl.when(i == 0)
    def _():
        issue(0, 0)

    @pl.when(i + 1 < n)
    def _():
        issue(i + 1, (i + 1) % 2)

    buf = i % 2
    tile_copy = pltpu.make_async_copy(ys_ref.at[pl.ds(0, TM * SLAB), :], ybuf.at[pl.ds(0, TM * SLAB), :], sems.at[buf])
    tile_copy.wait()
    tile_copy.wait()
    rw = rw_ref[...]
    moe = rw[:, 0:1] * _load_slabs(ybuf, buf * half, TM) + rw[:, 1:2] * _load_slabs(ybuf, buf * half + TM, TM)

    def finish(gf_ref, o_ref):
        x2 = x1_ref[...] + gf_ref[...] * moe
        o_ref[...] = _rms(x2, ln_ref[...], NORM_EPS)

    @pl.when(i < n_tiles_p)
    def _():
        finish(gfp_ref, op_ref)

    @pl.when(i >= n_tiles_p)
    def _():
        finish(gfs_ref, os_ref)


def _combine_call(pos0, pos1, x1, rw, mod_p, mod_s, tiles_per_batch, ln_final, ys):
    n = x1.shape[0]
    n_s = mod_s.shape[0]
    tp, ts = (n - n_s) // TM, n_s // TM
    pidx = lambda i: jnp.minimum(i, tp - 1)
    sidx = lambda i: jnp.maximum(i - tp, 0)
    return pl.pallas_call(
        functools.partial(_combine_kernel, tp),
        grid_spec=pltpu.PrefetchScalarGridSpec(
            num_scalar_prefetch=2,
            grid=(tp + ts,),
            in_specs=[pl.BlockSpec((TM, D_MODEL), lambda i, p0, p1: (i, 0)),
                      pl.BlockSpec((TM, LANES), lambda i, p0, p1: (i, 0)),
                      pl.BlockSpec((None, 1, D_MODEL), lambda i, p0, p1: (pidx(i) // tiles_per_batch, 0, 5)),
                      pl.BlockSpec((TM, D_MODEL), lambda i, p0, p1: (sidx(i), 5)),
                      pl.BlockSpec((1, D_MODEL), lambda i, p0, p1: (0, 0)),
                      pl.BlockSpec(memory_space=pl.ANY)],
            out_specs=[pl.BlockSpec((TM, D_MODEL), lambda i, p0, p1: (pidx(i), 0)),
                       pl.BlockSpec((TM, D_MODEL), lambda i, p0, p1: (sidx(i), 0))],
            scratch_shapes=[pltpu.VMEM((2 * 2 * TM * SLAB, LANES), f32), pltpu.SemaphoreType.DMA((2,))],
        ),
        out_shape=[jax.ShapeDtypeStruct((n - n_s, D_MODEL), f32), jax.ShapeDtypeStruct((n_s, D_MODEL), f32)],
        compiler_params=_cparams(("arbitrary",)),
        name="combine",
    )(pos0, pos1, x1, rw, mod_p, mod_s, ln_final, ys)


def _rope_tables(pos):
    inv = 1.0 / (ROPE_THETA ** (np.arange(0, HEAD_DIM, 2, dtype=np.float64) / HEAD_DIM))
    ang = np.asarray(pos, np.float64)[:, None] * inv[None, :]
    ang = np.concatenate([ang, ang], axis=-1)
    sign = np.where(np.arange(HEAD_DIM) < HEAD_DIM // 2, -1.0, 1.0)
    cos = np.tile(np.cos(ang), (1, N_SUB)).astype(np.float32)
    sin_signed = np.tile(np.sin(ang) * sign[None, :], (1, N_SUB)).astype(np.float32)
    return jnp.asarray(cos), jnp.asarray(sin_signed)


def _segments(counts, n_rows):
    n_tiles = n_rows // TE
    offs = jnp.concatenate([jnp.zeros((1,), jnp.int32), jnp.cumsum(counts)[:-1].astype(jnp.int32)])
    tiles = jnp.arange(n_tiles, dtype=jnp.int32) * TE
    rank_t = jnp.arange(n_tiles, dtype=jnp.int32) + jnp.sum(offs[None, :] < tiles[:, None], axis=1).astype(jnp.int32)
    rank_o = jnp.arange(N_EXPERTS, dtype=jnp.int32) + jnp.minimum(offs // TE + 1, n_tiles)
    vals = jnp.concatenate([tiles, offs])
    ranks = jnp.concatenate([rank_t, rank_o])
    n_items = n_tiles + N_EXPERTS
    w = jnp.arange(n_items, dtype=jnp.int32)
    seg_lo = jnp.sum(jnp.where(ranks[None, :] == w[:, None], vals[None, :], 0), axis=1).astype(jnp.int32)
    seg_hi = jnp.concatenate([seg_lo[1:], jnp.full((1,), n_rows, jnp.int32)])
    tile_id = jnp.minimum(seg_lo // TE, n_tiles - 1)
    exp_id = jnp.sum(offs[None, :] <= seg_lo[:, None], axis=1).astype(jnp.int32) - 1
    return offs, tile_id, exp_id, seg_lo, seg_hi


def kernel(x_prompt, x_sample, cache_k, cache_v, state_conv, page_table, c_prompt, c_sample, w_ada, b_ada, ln_mix, w_in, lam_q1, lam_k1, lam_q2, lam_k2, subln_g, w_conv, w_out, ln_ffn, w_router_grp, b_router_grp, w_router_exp, b_router_exp, w_gate, w_up, w_down, ln_final):
    B, S, _ = x_prompt.shape
    DB, L, _ = x_sample.shape
    n_phys = cache_k.shape[1]
    n_pages = page_table.shape[1]
    past = n_pages * PAGE
    n_p = B * S
    n_s = DB * L
    n_tok = n_p + n_s

    w_in_bf = w_in[0].astype(bf16)
    w_out_bf = w_out[0].astype(bf16)
    assert ROUTE_GRP_LANE == N_EXPERTS
    pad = LANES - N_EXPERTS - N_GROUPS
    w_r_bf = jnp.concatenate([w_router_exp[0], w_router_grp[0], jnp.zeros((D_MODEL, pad), f32)], axis=1).astype(bf16)
    b_r = jnp.concatenate([b_router_exp[0], b_router_grp[0], jnp.zeros((pad,), f32)])[None]
    cos_p, sin_p = _rope_tables(np.arange(S))
    cos_s, sin_s = _rope_tables(past + np.tile(np.arange(L), DB))
    ltri = jnp.asarray(np.tril(np.ones((TM, TM), np.float32), -1), bf16)

    mod = _mod_call(jnp.concatenate([c_prompt, c_sample], axis=0), w_ada[0], b_ada)
    mod_p = mod[:B].reshape(B, 1, 6 * D_MODEL)
    seq_of_row = np.arange(n_s, dtype=np.int32) // L
    mod_s = jnp.take(mod, B + seq_of_row, axis=0)

    qt_p, kt_p, kb_p, v4_p, vt_p, cv_p, st_p = _inproj_p_call(x_prompt, mod_p, ln_mix, w_in_bf, cos_p, sin_p, w_conv[0])
    xs_l = x_sample.reshape(n_s, D_MODEL)
    st_rows = [jnp.take(state_conv[0, :, t], seq_of_row, axis=0) for t in range(CONV_K - 1)]
    q_s, kf_s, vf_s, cv_s, cu_s = _inproj_s_call(xs_l, mod_s[:, 0:D_MODEL], mod_s[:, D_MODEL:2 * D_MODEL], ln_mix,
                                                 w_in_bf, cos_s, sin_s, w_conv[0], st_rows[0], st_rows[1], L)
    by_seq = lambda a: a.reshape(DB, L, -1)
    kt = jnp.transpose(cache_k[0], (0, 2, 3, 1)).reshape(n_phys, N_SUB * HEAD_DIM, PAGE)
    v2 = cache_v[0].reshape(n_phys, PAGE * N_HEADS, V_DIM)
    att_p, att_s_b = _attn_call(page_table.reshape(-1), qt_p, kb_p, vt_p, lam_q1, lam_k1, lam_q2, lam_k2, subln_g,
                                by_seq(q_s), by_seq(kf_s), by_seq(vf_s), kt, v2, n_pages)
    att_s = att_s_b.reshape(n_s, ATTN_WIDTH)

    x1, h2, rt, rw, cnt = _post_call(x_prompt.reshape(n_p, D_MODEL), att_p.reshape(n_p, ATTN_WIDTH),
                                     cv_p.reshape(n_p, CONV_WIDTH), mod_p, xs_l, att_s, cv_s, mod_s, S // TM,
                                     ln_ffn, w_out_bf, w_r_bf, b_r, ltri)

    counts = cnt[0, :N_EXPERTS].astype(jnp.int32)
    offs, tile_id, exp_id, seg_lo, seg_hi = _segments(counts, 2 * n_tok)
    e_col = jnp.arange(N_EXPERTS, dtype=jnp.int32)[:, None]
    start = lambda e_row: jnp.sum(jnp.where(e_row[None, :] == e_col, offs[:, None], 0), axis=0)
    pos0 = start(rt[0]) + rt[2]
    pos1 = start(rt[1]) + rt[3]

    xs_sorted = _dispatch_call(pos0, pos1, h2)
    ys = _experts_call(tile_id, exp_id, seg_lo, seg_hi, xs_sorted, w_gate[0], w_up[0], w_down[0])
    y_p, y_s = _combine_call(pos0, pos1, x1, rw, mod_p, mod_s, S // TM, ln_final.reshape(1, D_MODEL), ys)

    y_prompt = y_p.reshape(B, S, D_MODEL)
    y_sample = y_s.reshape(DB, L, D_MODEL)
    k_prompt = kt_p.reshape(B, N_SUB, HEAD_DIM, S).transpose(0, 3, 1, 2)[None]
    v_prompt = v4_p.reshape(1, B, S, N_HEADS, V_DIM)
    conv_prompt = st_p[None]
    k_sample = kf_s.reshape(1, DB, L, N_SUB, HEAD_DIM)
    v_sample = vf_s.reshape(1, DB, L, N_HEADS, V_DIM)
    conv_sample = cu_s.reshape(DB, L, CONV_WIDTH)[None, :, L - (CONV_K - 1):]
    return (y_prompt, y_sample, k_prompt, v_prompt, conv_prompt, k_sample, v_sample, conv_sample)
```

```python
import functools
import math

import jax
import jax.numpy as jnp
import numpy as np
from jax import lax
from jax.experimental import pallas as pl
from jax.experimental.pallas import tpu as pltpu

D_MODEL = 1024
ATTN_WIDTH = 512
CONV_WIDTH = 512
N_HEADS = 4
N_SUB = 8
HEAD_DIM = 64
V_DIM = 2 * HEAD_DIM
CONV_K = 3
ROPE_THETA = 10000.0
N_GROUPS = 4
EXP_PER_GROUP = 8
N_EXPERTS = 32
D_EXPERT = 256
NORM_EPS = 1e-6
SUBLN_EPS = 1e-5
LAM_INIT = 0.8 - 0.6 * math.exp(-0.3 * 0)
LOG2E = math.log2(math.e)
PAGE = 128
LANES = 128
ROUTE_GRP_LANE = 32

TM = 512
TQ = 512
TE = 512
ROW_DMA_UNROLL = 8
X_RING = 3
ONES_ROWS = 16
VMEM_LIMIT = 56 * 1024 * 1024

f32 = jnp.float32
bf16 = jnp.bfloat16


def _cparams(sem):
    return pltpu.CompilerParams(dimension_semantics=sem, vmem_limit_bytes=VMEM_LIMIT)


def _rms(x, g, eps):
    return x * lax.rsqrt(jnp.mean(x * x, axis=-1, keepdims=True) + eps) * g


def _silu(x):
    return x * (1.0 / (1.0 + jnp.exp(-x)))


def _mod_kernel(c_ref, w_ref, b_ref, o_ref):
    a = _silu(c_ref[...]).astype(bf16)
    o_ref[...] = jnp.dot(a, w_ref[...].astype(bf16), preferred_element_type=f32) + b_ref[...]


def _mod_call(c_all, w_ada, b_ada):
    n = c_all.shape[0]
    return pl.pallas_call(
        _mod_kernel,
        grid=(6,),
        in_specs=[pl.BlockSpec((n, D_MODEL), lambda j: (0, 0)),
                  pl.BlockSpec((D_MODEL, D_MODEL), lambda j: (0, j)),
                  pl.BlockSpec((1, D_MODEL), lambda j: (0, j))],
        out_specs=pl.BlockSpec((n, D_MODEL), lambda j: (0, j)),
        out_shape=jax.ShapeDtypeStruct((n, 6 * D_MODEL), f32),
        compiler_params=_cparams(("arbitrary",)),
        name="mod",
    )(c_all, w_ada, b_ada)


def _rope(t, cos, sin_signed, lo_mask):
    n = t.shape[-1]
    rot = jnp.where(lo_mask, pltpu.roll(t, n - HEAD_DIM // 2, 1), pltpu.roll(t, HEAD_DIM // 2, 1))
    return t * cos + rot * sin_signed


def _inproj_common(x, sh, sc, ln, w_ref, cos, sin_signed):
    h = (_rms(x, ln, NORM_EPS) * (1.0 + sc) + sh).astype(bf16)

    def sec(i):
        return jnp.dot(h, w_ref[:, i * 512:(i + 1) * 512], preferred_element_type=f32)

    lane = lax.broadcasted_iota(jnp.int32, (x.shape[0], 512), 1)
    lo_mask = (lane % HEAD_DIM) < (HEAD_DIM // 2)
    q = _rope(sec(0), cos, sin_signed, lo_mask) * (HEAD_DIM ** -0.5 * LOG2E)
    k = _rope(sec(1), cos, sin_signed, lo_mask)
    v = sec(2)
    bg = sec(3)
    cu = sec(4) * sec(5)
    return q, k, v, bg, cu


def _inproj_p_kernel(x_ref, sh_ref, sc_ref, ln_ref, w_ref, cos_ref, sin_ref, wc_ref,
                     qt_ref, kt_ref, kb_ref, v4_ref, vt_ref, cv_ref, st_ref, carry_ref):
    s = pl.program_id(0)
    b = pl.program_id(1)
    q, k, v, bg, cu = _inproj_common(x_ref[...], sh_ref[...], sc_ref[...], ln_ref[...], w_ref,
                                     cos_ref[...], sin_ref[...])
    qt_ref[...] = q.T.astype(bf16)
    kt_ref[...] = k.T
    kb_ref[...] = k.astype(bf16)
    vt_ref[...] = v.T.astype(bf16)
    for h in range(N_HEADS):
        v4_ref[pl.ds(h, v.shape[0], stride=N_HEADS), :] = v[:, h * V_DIM:(h + 1) * V_DIM]
    tm = cu.shape[0]
    prev = jnp.where(s > 0, carry_ref[b], 0.0)
    row = lax.broadcasted_iota(jnp.int32, cu.shape, 0)
    cu1 = jnp.where(row == 0, prev[1:2], pltpu.roll(cu, 1, 0))
    cu2 = jnp.where(row == 0, prev[0:1], jnp.where(row == 1, prev[1:2], pltpu.roll(cu, 2, 0)))
    wc = wc_ref[...]
    conv = wc[0:1] * cu2 + wc[1:2] * cu1 + wc[2:3] * cu
    cv_ref[...] = (bg * conv).astype(bf16)
    last2 = cu[tm - 2:tm]
    carry_ref[b, 0:2, :] = last2
    st_ref[b] = last2


def _inproj_p_call(x, mod3, ln_mix, w_in_bf, cos, sin_signed, w_conv):
    B, S, _ = x.shape
    ns = S // TM
    row = lambda s, b: (b, s, 0)
    col = lambda s, b: (b, 0, s)
    rows = lambda dt: jax.ShapeDtypeStruct((B, S, 512), dt)
    cols = lambda dt: jax.ShapeDtypeStruct((B, 512, S), dt)
    row_spec = pl.BlockSpec((None, TM, 512), row)
    col_spec = pl.BlockSpec((None, 512, TM), col)
    return pl.pallas_call(
        _inproj_p_kernel,
        grid=(ns, B),
        in_specs=[pl.BlockSpec((None, TM, D_MODEL), row),
                  pl.BlockSpec((None, 1, D_MODEL), lambda s, b: (b, 0, 0)),
                  pl.BlockSpec((None, 1, D_MODEL), lambda s, b: (b, 0, 1)),
                  pl.BlockSpec((1, D_MODEL), lambda s, b: (0, 0)),
                  pl.BlockSpec((D_MODEL, 3072), lambda s, b: (0, 0)),
                  pl.BlockSpec((TM, 512), lambda s, b: (s, 0)),
                  pl.BlockSpec((TM, 512), lambda s, b: (s, 0)),
                  pl.BlockSpec((CONV_K, CONV_WIDTH), lambda s, b: (0, 0))],
        out_specs=[col_spec, col_spec, row_spec, pl.BlockSpec((None, N_HEADS * TM, V_DIM), row),
                   pl.BlockSpec((None, None, 512, TM), lambda s, b: (b, s, 0, 0)), row_spec,
                   pl.BlockSpec((B, 2, CONV_WIDTH), lambda s, b: (0, 0, 0))],
        out_shape=[cols(bf16), cols(f32), rows(bf16), jax.ShapeDtypeStruct((B, N_HEADS * S, V_DIM), f32),
                   jax.ShapeDtypeStruct((B, ns, 512, TM), bf16), rows(bf16),
                   jax.ShapeDtypeStruct((B, 2, CONV_WIDTH), f32)],
        scratch_shapes=[pltpu.VMEM((B, 8, CONV_WIDTH), f32)],
        compiler_params=_cparams(("arbitrary", "arbitrary")),
        name="inproj_p",
    )(x, mod3, mod3, ln_mix, w_in_bf, cos, sin_signed, w_conv)


def _inproj_s_kernel(seq_len, x_ref, sh_ref, sc_ref, ln_ref, w_ref, cos_ref, sin_ref, wc_ref, st0_ref, st1_ref,
                     q_ref, kf_ref, vf_ref, cv_ref, cu_ref):
    q, k, v, bg, cu = _inproj_common(x_ref[...], sh_ref[...], sc_ref[...], ln_ref[...], w_ref,
                                     cos_ref[...], sin_ref[...])
    q_ref[...] = q.astype(bf16)
    kf_ref[...] = k
    vf_ref[...] = v
    l = lax.broadcasted_iota(jnp.int32, cu.shape, 0) % seq_len
    st0 = st0_ref[...]
    st1 = st1_ref[...]
    cu1 = jnp.where(l == 0, st1, pltpu.roll(cu, 1, 0))
    cu2 = jnp.where(l == 0, st0, jnp.where(l == 1, st1, pltpu.roll(cu, 2, 0)))
    wc = wc_ref[...]
    conv = wc[0:1] * cu2 + wc[1:2] * cu1 + wc[2:3] * cu
    cv_ref[...] = (bg * conv).astype(bf16)
    cu_ref[...] = cu


def _inproj_s_call(x, sh, sc, ln_mix, w_in_bf, cos, sin_signed, w_conv, st0, st1, seq_len):
    n = x.shape[0]
    full = lambda shape: pl.BlockSpec(shape, lambda i: (0,) * len(shape))
    return pl.pallas_call(
        functools.partial(_inproj_s_kernel, seq_len),
        grid=(1,),
        in_specs=[full((n, D_MODEL)), full((n, D_MODEL)), full((n, D_MODEL)), full((1, D_MODEL)),
                  full((D_MODEL, 3072)), full((n, 512)), full((n, 512)), full((CONV_K, CONV_WIDTH)),
                  full((n, CONV_WIDTH)), full((n, CONV_WIDTH))],
        out_specs=[full((n, 512))] * 5,
        out_shape=[jax.ShapeDtypeStruct((n, 512), bf16), jax.ShapeDtypeStruct((n, 512), f32),
                   jax.ShapeDtypeStruct((n, 512), f32), jax.ShapeDtypeStruct((n, 512), bf16),
                   jax.ShapeDtypeStruct((n, CONV_WIDTH), f32)],
        compiler_params=_cparams(("arbitrary",)),
        name="inproj_s",
    )(x, sh, sc, ln_mix, w_in_bf, cos, sin_signed, w_conv, st0, st1)


def _lam(lq1, lk1, lq2, lk2):
    a = jnp.sum(lq1 * lk1, axis=-1, keepdims=True)
    b = jnp.sum(lq2 * lk2, axis=-1, keepdims=True)
    return jnp.exp(a) - jnp.exp(b) + LAM_INIT


def _attn_p_body(i, qt_ref, k_ref, vt_ref, lam, g_ref, o_ref, m_ref, acc_ref):
    tq = qt_ref.shape[1]
    tk = tq
    m_ref[...] = jnp.full(m_ref.shape, -jnp.inf, f32)
    acc_ref[...] = jnp.zeros(acc_ref.shape, f32)

    def step(j, masked):
        k0 = pl.multiple_of(j * tk, tk)
        qt = qt_ref[...]
        row = lax.broadcasted_iota(jnp.int32, qt.shape, 0)
        zero = jnp.zeros_like(qt)
        q2t = jnp.concatenate([jnp.where(row < HEAD_DIM, qt, zero), jnp.where(row >= HEAD_DIM, qt, zero)], axis=1)
        st = jnp.dot(k_ref[pl.ds(k0, tk), :], q2t, preferred_element_type=f32)
        if masked:
            kpos = lax.broadcasted_iota(jnp.int32, st.shape, 0)
            c = lax.broadcasted_iota(jnp.int32, st.shape, 1)
            st = jnp.where(kpos <= jnp.where(c >= tq, c - tq, c), st, -jnp.inf)
        m_prev = m_ref[...]
        m_new = jnp.maximum(m_prev, jnp.max(st, axis=0, keepdims=True))
        alpha = jnp.exp2(m_prev - m_new)
        pt = jnp.exp2(st - m_new).astype(bf16)
        vt1 = jnp.concatenate([vt_ref[j], jnp.ones((ONES_ROWS, tk), bf16)], axis=0)
        acc_ref[...] = alpha * acc_ref[...] + jnp.dot(vt1, pt, preferred_element_type=f32)
        m_ref[...] = m_new

    def below_diagonal(j, c):
        step(j, False)
        return c

    lax.fori_loop(0, i, below_diagonal, 0)
    step(i, True)
    ot = acc_ref[0:V_DIM, :] / acc_ref[V_DIM:V_DIM + 1, :]
    dt = ot[:, :tq] - lam * ot[:, tq:]
    ms = jnp.mean(dt * dt, axis=0, keepdims=True)
    nt = dt * lax.rsqrt(ms + SUBLN_EPS) * g_ref[...] * (1.0 - LAM_INIT)
    o_ref[...] = nt.T.astype(bf16)


def _attn_s_body(q_ref, kn_ref, vn_ref, lam, g_ref, kt_refs, v_refs, o_ref):
    n_pages = len(kt_refs)
    nq = q_ref.shape[0]
    nr = nq * N_SUB
    qf = q_ref[...].astype(f32)
    sub = lax.broadcasted_iota(jnp.int32, (N_SUB, ATTN_WIDTH), 0)
    col = lax.broadcasted_iota(jnp.int32, (N_SUB, ATTN_WIDTH), 1)
    diag = (col // HEAD_DIM) == sub
    qbd = jnp.concatenate([jnp.where(diag, jnp.broadcast_to(qf[a:a + 1], (N_SUB, ATTN_WIDTH)), 0.0)
                           for a in range(nq)], axis=0).astype(bf16)
    kt_all = jnp.concatenate([kt_refs[p][...].astype(bf16) for p in range(n_pages)], axis=1)
    s_past = jnp.dot(qbd, kt_all, preferred_element_type=f32)
    s_new = lax.dot_general(qbd, kn_ref[...].astype(bf16), (((1,), (1,)), ((), ())),
                            preferred_element_type=f32)
    r = lax.broadcasted_iota(jnp.int32, s_new.shape, 0)
    c = lax.broadcasted_iota(jnp.int32, s_new.shape, 1)
    s_new = jnp.where(c <= r // N_SUB, s_new, -jnp.inf)
    m = jnp.maximum(jnp.max(s_past, axis=-1, keepdims=True), jnp.max(s_new, axis=-1, keepdims=True))
    p_past = jnp.exp2(s_past - m)
    p_new = jnp.exp2(s_new - m)
    l = jnp.sum(p_past, axis=-1, keepdims=True) + jnp.sum(p_new, axis=-1, keepdims=True)
    p_bf = p_past.astype(bf16)
    vn = vn_ref[...]
    rowhead = (lax.broadcasted_iota(jnp.int32, (nr, V_DIM), 0) % N_SUB) // 2
    o = jnp.zeros((nr, V_DIM), f32)
    for h in range(N_HEADS):
        v_all = jnp.concatenate([v_refs[p][pl.ds(h, PAGE, stride=N_HEADS), :].astype(bf16)
                                 for p in range(n_pages)], axis=0)
        acc = jnp.dot(p_bf, v_all, preferred_element_type=f32)
        for a in range(nq):
            acc = acc + p_new[:, a:a + 1] * vn[a:a + 1, h * V_DIM:(h + 1) * V_DIM]
        o = jnp.where(rowhead == h, acc, o)
    o = o / l
    d = o - lam * pltpu.roll(o, nr - 1, 0)
    d = _rms(d, g_ref[...], SUBLN_EPS) * (1.0 - LAM_INIT)
    for a in range(nq):
        rowv = jnp.concatenate([d[a * N_SUB + 2 * h:a * N_SUB + 2 * h + 1] for h in range(N_HEADS)], axis=1)
        o_ref[a:a + 1, :] = rowv.astype(o_ref.dtype)


def _attn_kernel(n_pages, pt_ref, qt_ref, k_ref, vt_ref, lq1_ref, lk1_ref, lq2_ref, lk2_ref, gcol_ref,
                 q_ref, kn_ref, vn_ref, grow_ref, kt_hbm, v_hbm, op_ref, os_ref, m_ref, acc_ref, kbuf, vbuf, sems):
    t = (pl.program_id(0) * pl.num_programs(1) + pl.program_id(1)) * pl.num_programs(2) + pl.program_id(2)
    n_steps = pl.num_programs(0) * pl.num_programs(1) * pl.num_programs(2)
    slot = t % 2

    def fetch(seq, dst):
        for p in range(n_pages):
            page = pt_ref[seq * n_pages + p]
            pltpu.make_async_copy(kt_hbm.at[page], kbuf.at[dst, p], sems.at[dst]).start()
            pltpu.make_async_copy(v_hbm.at[page], vbuf.at[dst, p], sems.at[dst]).start()

    @pl.when(t == 0)
    def _():
        fetch(0, 0)

    pltpu.make_async_copy(kt_hbm.at[pl.ds(0, n_pages)], kbuf.at[slot], sems.at[slot]).wait()
    pltpu.make_async_copy(v_hbm.at[pl.ds(0, n_pages)], vbuf.at[slot], sems.at[slot]).wait()

    @pl.when(t + 1 < n_steps)
    def _():
        fetch(t + 1, 1 - slot)

    lam = _lam(lq1_ref[...], lk1_ref[...], lq2_ref[...], lk2_ref[...])
    _attn_p_body(pl.program_id(2), qt_ref, k_ref, vt_ref, lam, gcol_ref, op_ref, m_ref, acc_ref)
    _attn_s_body(q_ref, kn_ref, vn_ref, lam, grow_ref, [kbuf.at[slot, p] for p in range(n_pages)],
                 [vbuf.at[slot, p] for p in range(n_pages)], os_ref)


def _attn_call(pt_flat, qt, k, vt, lq1, lk1, lq2, lk2, g, q_s, kn_s, vn_s, kt, v2, n_pages):
    B, S, _ = k.shape
    nq_tiles = S // TQ
    nb, nq, _ = q_s.shape
    assert nb == B * N_HEADS * nq_tiles, "one sample sequence per prompt grid step"
    seq = lambda b, h, i: (b * N_HEADS + h) * nq_tiles + i
    small = lambda n: pl.BlockSpec((1, n), lambda b, h, i, pt: (0, 0))
    page_rows = kt.shape[1]
    return pl.pallas_call(
        functools.partial(_attn_kernel, n_pages),
        grid_spec=pltpu.PrefetchScalarGridSpec(
            num_scalar_prefetch=1,
            grid=(B, N_HEADS, nq_tiles),
            in_specs=[pl.BlockSpec((None, V_DIM, TQ), lambda b, h, i, pt: (b, h, i)),
                      pl.BlockSpec((None, S, V_DIM), lambda b, h, i, pt: (b, 0, h)),
                      pl.BlockSpec((None, nq_tiles, V_DIM, TQ), lambda b, h, i, pt: (b, 0, h, 0)),
                      small(HEAD_DIM), small(HEAD_DIM), small(HEAD_DIM), small(HEAD_DIM),
                      pl.BlockSpec((V_DIM, 1), lambda b, h, i, pt: (0, 0)),
                      pl.BlockSpec((None, nq, ATTN_WIDTH), lambda b, h, i, pt: (seq(b, h, i), 0, 0)),
                      pl.BlockSpec((None, nq, ATTN_WIDTH), lambda b, h, i, pt: (seq(b, h, i), 0, 0)),
                      pl.BlockSpec((None, nq, ATTN_WIDTH), lambda b, h, i, pt: (seq(b, h, i), 0, 0)),
                      small(V_DIM), pl.BlockSpec(memory_space=pl.ANY), pl.BlockSpec(memory_space=pl.ANY)],
            out_specs=[pl.BlockSpec((None, TQ, V_DIM), lambda b, h, i, pt: (b, i, h)),
                       pl.BlockSpec((None, nq, ATTN_WIDTH), lambda b, h, i, pt: (seq(b, h, i), 0, 0))],
            scratch_shapes=[pltpu.VMEM((1, 2 * TQ), f32), pltpu.VMEM((V_DIM + ONES_ROWS, 2 * TQ), f32),
                            pltpu.VMEM((2, n_pages, page_rows, PAGE), f32),
                            pltpu.VMEM((2, n_pages, page_rows, PAGE), f32), pltpu.SemaphoreType.DMA((2,))],
        ),
        out_shape=[jax.ShapeDtypeStruct((B, S, ATTN_WIDTH), bf16), jax.ShapeDtypeStruct((nb, nq, ATTN_WIDTH), bf16)],
        compiler_params=_cparams(("arbitrary",) * 3),
        name="attn",
    )(pt_flat, qt, k, vt, lq1, lk1, lq2, lk2, g.reshape(V_DIM, 1), q_s, kn_s, vn_s, g, kt, v2)


SLAB = D_MODEL // LANES


def _store_slabs(ref, row0, x):
    n = x.shape[0]
    for a in range(SLAB):
        ref[pl.ds(SLAB * row0 + a, n, stride=SLAB), :] = x[:, a * LANES:(a + 1) * LANES]


def _load_slabs(ref, row0, n):
    return jnp.concatenate([ref[pl.ds(SLAB * row0 + a, n, stride=SLAB), :] for a in range(SLAB)], axis=1)


def _slab_copy(src_ref, src_row, dst_ref, dst_row, sem):
    src = src_ref.at[pl.ds(pl.multiple_of(src_row * SLAB, SLAB), SLAB), :]
    dst = dst_ref.at[pl.ds(pl.multiple_of(dst_row * SLAB, SLAB), SLAB), :]
    return pltpu.make_async_copy(src, dst, sem)


def _post_tile(x_ref, att_ref, cv_ref, ga_ref, shf_ref, scf_ref, ln_ref, wo_ref, wr_ref, br_ref, ltri_ref,
               x1_ref, h2_ref, rt_ref, rw_ref, cnt_ref, carry_ref):
    mix = (jnp.dot(att_ref[...], wo_ref[0:ATTN_WIDTH, :], preferred_element_type=f32)
           + jnp.dot(cv_ref[...], wo_ref[ATTN_WIDTH:, :], preferred_element_type=f32))
    x1 = x_ref[...] + ga_ref[...] * mix
    x1_ref[...] = x1
    h2 = _rms(x1, ln_ref[...], NORM_EPS) * (1.0 + scf_ref[...]) + shf_ref[...]
    _store_slabs(h2_ref, 0, h2)
    logits = jnp.dot(h2.astype(bf16), wr_ref[...], preferred_element_type=f32) + br_ref[...]
    lane = lax.broadcasted_iota(jnp.int32, logits.shape, 1)
    lane_f = lane.astype(f32)
    big = jnp.float32(1e9)
    neg = -jnp.inf

    def first_max(vals):
        mx = jnp.max(vals, axis=-1, keepdims=True)
        idx = jnp.min(jnp.where(vals == mx, lane_f, big), axis=-1, keepdims=True)
        return mx, idx

    gl = jnp.where((lane >= ROUTE_GRP_LANE) & (lane < ROUTE_GRP_LANE + N_GROUPS), logits, neg)
    gmax, gidx = first_max(gl)
    g_p = 1.0 / jnp.sum(jnp.exp(gl - gmax), axis=-1, keepdims=True)
    lo = (gidx - ROUTE_GRP_LANE) * EXP_PER_GROUP
    el = jnp.where((lane_f >= lo) & (lane_f < lo + EXP_PER_GROUP), logits, neg)
    v1, i1 = first_max(el)
    el2 = jnp.where(lane_f == i1, neg, el)
    v2, i2 = first_max(el2)
    t = jnp.exp(v2 - v1)
    w1 = g_p / (1.0 + t)
    w2 = g_p * t / (1.0 + t)
    oh1 = lane_f == i1
    oh2 = lane_f == i2
    cnt = jnp.where(oh1 | oh2, 1.0, 0.0)
    prefix = jnp.dot(ltri_ref[...], cnt.astype(bf16), preferred_element_type=f32) + carry_ref[...]
    r1 = jnp.sum(jnp.where(oh1, prefix, 0.0), axis=-1, keepdims=True)
    r2 = jnp.sum(jnp.where(oh2, prefix, 0.0), axis=-1, keepdims=True)
    ri = jnp.where(lane == 0, i1, jnp.where(lane == 1, i2, jnp.where(lane == 2, r1, jnp.where(lane == 3, r2, 0.0))))
    rt_ref[...] = ri.T[0:8, :].astype(jnp.int32)
    rw_ref[...] = jnp.where(lane == 0, w1, jnp.where(lane == 1, w2, 0.0))
    new_carry = carry_ref[...] + jnp.sum(cnt, axis=0, keepdims=True)
    carry_ref[...] = new_carry
    cnt_ref[...] = new_carry


def _post_kernel(n_tiles_p, xp_ref, attp_ref, cvp_ref, gap_ref, shfp_ref, scfp_ref,
                 xs_ref, atts_ref, cvs_ref, gas_ref, shfs_ref, scfs_ref, *rest):
    i = pl.program_id(0)
    carry_ref = rest[-1]

    @pl.when(i == 0)
    def _():
        carry_ref[...] = jnp.zeros(carry_ref.shape, f32)

    @pl.when(i < n_tiles_p)
    def _():
        _post_tile(xp_ref, attp_ref, cvp_ref, gap_ref, shfp_ref, scfp_ref, *rest)

    @pl.when(i >= n_tiles_p)
    def _():
        _post_tile(xs_ref, atts_ref, cvs_ref, gas_ref, shfs_ref, scfs_ref, *rest)


def _post_call(x_p, att_p, cv_p, mod_p, x_s, att_s, cv_s, mod_s, tiles_per_batch, ln_ffn, w_out_bf, w_r_bf, b_r, ltri):
    n_p, n_s = x_p.shape[0], x_s.shape[0]
    tp, ts = n_p // TM, n_s // TM
    n = n_p + n_s
    prow = lambda w: pl.BlockSpec((TM, w), lambda i: (jnp.minimum(i, tp - 1), 0))
    srow = lambda w: pl.BlockSpec((TM, w), lambda i: (jnp.maximum(i - tp, 0), 0))
    pmod = lambda sec: pl.BlockSpec((None, 1, D_MODEL), lambda i: (jnp.minimum(i, tp - 1) // tiles_per_batch, 0, sec))
    smod = lambda sec: pl.BlockSpec((TM, D_MODEL), lambda i: (jnp.maximum(i - tp, 0), sec))
    row = lambda w: pl.BlockSpec((TM, w), lambda i: (i, 0))
    const = lambda shape: pl.BlockSpec(shape, lambda i: (0, 0))
    return pl.pallas_call(
        functools.partial(_post_kernel, tp),
        grid=(tp + ts,),
        in_specs=[prow(D_MODEL), prow(ATTN_WIDTH), prow(CONV_WIDTH), pmod(2), pmod(3), pmod(4),
                  srow(D_MODEL), srow(ATTN_WIDTH), srow(CONV_WIDTH), smod(2), smod(3), smod(4),
                  const((1, D_MODEL)), const((D_MODEL, D_MODEL)), const((D_MODEL, LANES)), const((1, LANES)),
                  const((TM, TM))],
        out_specs=[row(D_MODEL), pl.BlockSpec((TM * SLAB, LANES), lambda i: (i, 0)),
                   pl.BlockSpec((8, TM), lambda i: (0, i)), row(LANES), const((1, LANES))],
        out_shape=[jax.ShapeDtypeStruct((n, D_MODEL), f32), jax.ShapeDtypeStruct((n * SLAB, LANES), f32),
                   jax.ShapeDtypeStruct((8, n), jnp.int32), jax.ShapeDtypeStruct((n, LANES), f32),
                   jax.ShapeDtypeStruct((1, LANES), f32)],
        scratch_shapes=[pltpu.VMEM((1, LANES), f32)],
        compiler_params=_cparams(("arbitrary",)),
        name="post",
    )(x_p, att_p, cv_p, mod_p, mod_p, mod_p, x_s, att_s, cv_s, mod_s, mod_s, mod_s,
      ln_ffn, w_out_bf, w_r_bf, b_r, ltri)


def _dispatch_kernel(pos0_ref, pos1_ref, src_ref, xs_ref, sem):
    t0 = pl.program_id(0) * TM

    def issue(r, c):
        t = t0 + r
        _slab_copy(src_ref, r, xs_ref, pos0_ref[t], sem).start(priority=0)
        _slab_copy(src_ref, r, xs_ref, pos1_ref[t], sem).start(priority=1)
        return c

    lax.fori_loop(0, TM, issue, 0, unroll=ROW_DMA_UNROLL)
    tile_copy = pltpu.make_async_copy(src_ref, xs_ref.at[pl.ds(0, TM * SLAB), :], sem)
    tile_copy.wait()
    tile_copy.wait()


def _dispatch_call(pos0, pos1, h2):
    tiles = h2.shape[0] // (TM * SLAB)
    return pl.pallas_call(
        _dispatch_kernel,
        grid_spec=pltpu.PrefetchScalarGridSpec(
            num_scalar_prefetch=2,
            grid=(tiles,),
            in_specs=[pl.BlockSpec((TM * SLAB, LANES), lambda i, p0, p1: (i, 0))],
            out_specs=pl.BlockSpec(memory_space=pl.ANY),
            scratch_shapes=[pltpu.SemaphoreType.DMA(())],
        ),
        out_shape=jax.ShapeDtypeStruct((2 * h2.shape[0], LANES), f32),
        compiler_params=_cparams(("arbitrary",)),
        name="dispatch",
    )(pos0, pos1, h2)


def _tile_chunk_copies(hbm_ref, tile, buf_ref, slot, sem, to_hbm):
    r0 = pl.multiple_of(tile * TE, TE)
    pairs = [(hbm_ref.at[pl.ds(r0, TE), a, :], buf_ref.at[slot, a]) for a in range(SLAB)]
    return [pltpu.make_async_copy(v, h, sem) if to_hbm else pltpu.make_async_copy(h, v, sem) for h, v in pairs]


def _experts_kernel(n_tiles, tile_ref, exp_ref, lo_ref, hi_ref, xs_ref, wg_ref, wu_ref, wd_ref, ys_ref,
                    wg_bf, wu_bf, wd_bf, cur_ref, xbuf, ybuf, xsem, ysem):
    w = pl.program_id(0)
    lo = lo_ref[w]
    hi = hi_ref[w]
    k = tile_ref[w]
    base = k * TE
    slot = k % 2
    xslot = k % X_RING

    @pl.when(w == 0)
    def _():
        cur_ref[0] = -1
        for k0 in range(X_RING - 1):
            for c in _tile_chunk_copies(xs_ref, k0, xbuf, k0, xsem.at[k0], False):
                c.start()

    @pl.when(hi > lo)
    def _():
        first = lo == base
        last = hi == base + TE

        @pl.when(first)
        def _():
            for c in _tile_chunk_copies(xs_ref, k, xbuf, xslot, xsem.at[xslot], False):
                c.wait()

            @pl.when(k + X_RING - 1 < n_tiles)
            def _():
                nxt = (k + X_RING - 1) % X_RING
                for c in _tile_chunk_copies(xs_ref, k + X_RING - 1, xbuf, nxt, xsem.at[nxt], False):
                    c.start()

            @pl.when(k >= 2)
            def _():
                for c in _tile_chunk_copies(ys_ref, k - 2, ybuf, slot, ysem.at[slot], True):
                    c.wait()

        e = exp_ref[w]

        @pl.when(cur_ref[0] != e)
        def _():
            wg_bf[...] = wg_ref[...].astype(bf16)
            wu_bf[...] = wu_ref[...].astype(bf16)
            wd_bf[...] = wd_ref[...].astype(bf16)
            cur_ref[0] = e

        x = jnp.concatenate([xbuf[xslot, a] for a in range(SLAB)], axis=1).astype(bf16)
        g = jnp.dot(x, wg_bf[...], preferred_element_type=f32)
        u = jnp.dot(x, wu_bf[...], preferred_element_type=f32)
        hid = (_silu(g) * u).astype(bf16)
        y = jnp.dot(hid, wd_bf[...], preferred_element_type=f32)
        row = base + lax.broadcasted_iota(jnp.int32, (TE, LANES), 0)
        mine = (row >= lo) & (row < hi)

        @pl.when(first)
        def _():
            for a in range(SLAB):
                ybuf[slot, a] = jnp.where(mine, y[:, a * LANES:(a + 1) * LANES], 0.0)

        @pl.when(jnp.logical_not(first))
        def _():
            for a in range(SLAB):
                ybuf[slot, a] = jnp.where(mine, y[:, a * LANES:(a + 1) * LANES], ybuf[slot, a])

        @pl.when(last)
        def _():
            for c in _tile_chunk_copies(ys_ref, k, ybuf, slot, ysem.at[slot], True):
                c.start()

    @pl.when(w == pl.num_programs(0) - 1)
    def _():
        for kk in (n_tiles - 2, n_tiles - 1):
            for c in _tile_chunk_copies(ys_ref, kk, ybuf, kk % 2, ysem.at[kk % 2], True):
                c.wait()


def _experts_call(tile_id, exp_id, seg_lo, seg_hi, xs, w_gate, w_up, w_down):
    n_items = tile_id.shape[0]
    n_rows = xs.shape[0] // SLAB
    n_tiles = n_rows // TE
    assert n_tiles >= X_RING
    idx = lambda w, t, e, lo, hi: (e[w], 0, 0)
    ys = pl.pallas_call(
        functools.partial(_experts_kernel, n_tiles),
        grid_spec=pltpu.PrefetchScalarGridSpec(
            num_scalar_prefetch=4,
            grid=(n_items,),
            in_specs=[pl.BlockSpec(memory_space=pl.ANY),
                      pl.BlockSpec((None, D_MODEL, D_EXPERT), idx),
                      pl.BlockSpec((None, D_MODEL, D_EXPERT), idx),
                      pl.BlockSpec((None, D_EXPERT, D_MODEL), idx)],
            out_specs=pl.BlockSpec(memory_space=pl.ANY),
            scratch_shapes=[pltpu.VMEM((D_MODEL, D_EXPERT), bf16), pltpu.VMEM((D_MODEL, D_EXPERT), bf16),
                            pltpu.VMEM((D_EXPERT, D_MODEL), bf16), pltpu.SMEM((1,), jnp.int32),
                            pltpu.VMEM((X_RING, SLAB, TE, LANES), f32), pltpu.VMEM((2, SLAB, TE, LANES), f32),
                            pltpu.SemaphoreType.DMA((X_RING,)), pltpu.SemaphoreType.DMA((2,))],
        ),
        out_shape=jax.ShapeDtypeStruct((n_rows, SLAB, LANES), f32),
        compiler_params=_cparams(("arbitrary",)),
        name="experts",
    )(tile_id, exp_id, seg_lo, seg_hi, xs.reshape(n_rows, SLAB, LANES), w_gate, w_up, w_down)
    return ys.reshape(xs.shape)


def _combine_kernel(n_tiles_p, pos0_ref, pos1_ref, x1_ref, rw_ref, gfp_ref, gfs_ref, ln_ref, ys_ref,
                    op_ref, os_ref, ybuf, sems):
    i = pl.program_id(0)
    n = pl.num_programs(0)
    half = 2 * TM

    def issue(tile, buf):
        def body(r, c):
            t = tile * TM + r
            _slab_copy(ys_ref, pos0_ref[t], ybuf, buf * half + r, sems.at[buf]).start(priority=0)
            _slab_copy(ys_ref, pos1_ref[t], ybuf, buf * half + TM + r, sems.at[buf]).start(priority=1)
            return c

        lax.fori_loop(0, TM, body, 0, unroll=ROW_DMA_UNROLL)

    @pl.when(i == 0)
    def _():
        issue(0, 0)

    @pl.when(i + 1 < n)
    def _():
        issue(i + 1, (i + 1) % 2)

    buf = i % 2
    tile_copy = pltpu.make_async_copy(ys_ref.at[pl.ds(0, TM * SLAB), :], ybuf.at[pl.ds(0, TM * SLAB), :], sems.at[buf])
    tile_copy.wait()
    tile_copy.wait()
    rw = rw_ref[...]
    moe = rw[:, 0:1] * _load_slabs(ybuf, buf * half, TM) + rw[:, 1:2] * _load_slabs(ybuf, buf * half + TM, TM)

    def finish(gf_ref, o_ref):
        x2 = x1_ref[...] + gf_ref[...] * moe
        o_ref[...] = _rms(x2, ln_ref[...], NORM_EPS)

    @pl.when(i < n_tiles_p)
    def _():
        finish(gfp_ref, op_ref)

    @pl.when(i >= n_tiles_p)
    def _():
        finish(gfs_ref, os_ref)


def _combine_call(pos0, pos1, x1, rw, mod_p, mod_s, tiles_per_batch, ln_final, ys):
    n = x1.shape[0]
    n_s = mod_s.shape[0]
    tp, ts = (n - n_s) // TM, n_s // TM
    pidx = lambda i: jnp.minimum(i, tp - 1)
    sidx = lambda i: jnp.maximum(i - tp, 0)
    return pl.pallas_call(
        functools.partial(_combine_kernel, tp),
        grid_spec=pltpu.PrefetchScalarGridSpec(
            num_scalar_prefetch=2,
            grid=(tp + ts,),
            in_specs=[pl.BlockSpec((TM, D_MODEL), lambda i, p0, p1: (i, 0)),
                      pl.BlockSpec((TM, LANES), lambda i, p0, p1: (i, 0)),
                      pl.BlockSpec((None, 1, D_MODEL), lambda i, p0, p1: (pidx(i) // tiles_per_batch, 0, 5)),
                      pl.BlockSpec((TM, D_MODEL), lambda i, p0, p1: (sidx(i), 5)),
                      pl.BlockSpec((1, D_MODEL), lambda i, p0, p1: (0, 0)),
                      pl.BlockSpec(memory_space=pl.ANY)],
            out_specs=[pl.BlockSpec((TM, D_MODEL), lambda i, p0, p1: (pidx(i), 0)),
                       pl.BlockSpec((TM, D_MODEL), lambda i, p0, p1: (sidx(i), 0))],
            scratch_shapes=[pltpu.VMEM((2 * 2 * TM * SLAB, LANES), f32), pltpu.SemaphoreType.DMA((2,))],
        ),
        out_shape=[jax.ShapeDtypeStruct((n - n_s, D_MODEL), f32), jax.ShapeDtypeStruct((n_s, D_MODEL), f32)],
        compiler_params=_cparams(("arbitrary",)),
        name="combine",
    )(pos0, pos1, x1, rw, mod_p, mod_s, ln_final, ys)


def _rope_tables(pos):
    inv = 1.0 / (ROPE_THETA ** (np.arange(0, HEAD_DIM, 2, dtype=np.float64) / HEAD_DIM))
    ang = np.asarray(pos, np.float64)[:, None] * inv[None, :]
    ang = np.concatenate([ang, ang], axis=-1)
    sign = np.where(np.arange(HEAD_DIM) < HEAD_DIM // 2, -1.0, 1.0)
    cos = np.tile(np.cos(ang), (1, N_SUB)).astype(np.float32)
    sin_signed = np.tile(np.sin(ang) * sign[None, :], (1, N_SUB)).astype(np.float32)
    return jnp.asarray(cos), jnp.asarray(sin_signed)


def _segments(counts, n_rows):
    n_tiles = n_rows // TE
    offs = jnp.concatenate([jnp.zeros((1,), jnp.int32), jnp.cumsum(counts)[:-1].astype(jnp.int32)])
    tiles = jnp.arange(n_tiles, dtype=jnp.int32) * TE
    rank_t = jnp.arange(n_tiles, dtype=jnp.int32) + jnp.sum(offs[None, :] < tiles[:, None], axis=1).astype(jnp.int32)
    rank_o = jnp.arange(N_EXPERTS, dtype=jnp.int32) + jnp.minimum(offs // TE + 1, n_tiles)
    vals = jnp.concatenate([tiles, offs])
    ranks = jnp.concatenate([rank_t, rank_o])
    n_items = n_tiles + N_EXPERTS
    w = jnp.arange(n_items, dtype=jnp.int32)
    seg_lo = jnp.sum(jnp.where(ranks[None, :] == w[:, None], vals[None, :], 0), axis=1).astype(jnp.int32)
    seg_hi = jnp.concatenate([seg_lo[1:], jnp.full((1,), n_rows, jnp.int32)])
    tile_id = jnp.minimum(seg_lo // TE, n_tiles - 1)
    exp_id = jnp.sum(offs[None, :] <= seg_lo[:, None], axis=1).astype(jnp.int32) - 1
    return offs, tile_id, exp_id, seg_lo, seg_hi


def kernel(x_prompt, x_sample, cache_k, cache_v, state_conv, page_table, c_prompt, c_sample, w_ada, b_ada, ln_mix, w_in, lam_q1, lam_k1, lam_q2, lam_k2, subln_g, w_conv, w_out, ln_ffn, w_router_grp, b_router_grp, w_router_exp, b_router_exp, w_gate, w_up, w_down, ln_final):
    B, S, _ = x_prompt.shape
    DB, L, _ = x_sample.shape
    n_phys = cache_k.shape[1]
    n_pages = page_table.shape[1]
    past = n_pages * PAGE
    n_p = B * S
    n_s = DB * L
    n_tok = n_p + n_s

    w_in_bf = w_in[0].astype(bf16)
    w_out_bf = w_out[0].astype(bf16)
    assert ROUTE_GRP_LANE == N_EXPERTS
    pad = LANES - N_EXPERTS - N_GROUPS
    w_r_bf = jnp.concatenate([w_router_exp[0], w_router_grp[0], jnp.zeros((D_MODEL, pad), f32)], axis=1).astype(bf16)
    b_r = jnp.concatenate([b_router_exp[0], b_router_grp[0], jnp.zeros((pad,), f32)])[None]
    cos_p, sin_p = _rope_tables(np.arange(S))
    cos_s, sin_s = _rope_tables(past + np.tile(np.arange(L), DB))
    ltri = jnp.asarray(np.tril(np.ones((TM, TM), np.float32), -1), bf16)

    mod = _mod_call(jnp.concatenate([c_prompt, c_sample], axis=0), w_ada[0], b_ada)
    mod_p = mod[:B].reshape(B, 1, 6 * D_MODEL)
    seq_of_row = np.arange(n_s, dtype=np.int32) // L
    mod_s = jnp.take(mod, B + seq_of_row, axis=0)

    qt_p, kt_p, kb_p, v4_p, vt_p, cv_p, st_p = _inproj_p_call(x_prompt, mod_p, ln_mix, w_in_bf, cos_p, sin_p, w_conv[0])
    xs_l = x_sample.reshape(n_s, D_MODEL)
    st_rows = [jnp.take(state_conv[0, :, t], seq_of_row, axis=0) for t in range(CONV_K - 1)]
    q_s, kf_s, vf_s, cv_s, cu_s = _inproj_s_call(xs_l, mod_s[:, 0:D_MODEL], mod_s[:, D_MODEL:2 * D_MODEL], ln_mix,
                                                 w_in_bf, cos_s, sin_s, w_conv[0], st_rows[0], st_rows[1], L)
    by_seq = lambda a: a.reshape(DB, L, -1)
    kt = jnp.transpose(cache_k[0], (0, 2, 3, 1)).reshape(n_phys, N_SUB * HEAD_DIM, PAGE)
    v2 = cache_v[0].reshape(n_phys, PAGE * N_HEADS, V_DIM)
    att_p, att_s_b = _attn_call(page_table.reshape(-1), qt_p, kb_p, vt_p, lam_q1, lam_k1, lam_q2, lam_k2, subln_g,
                                by_seq(q_s), by_seq(kf_s), by_seq(vf_s), kt, v2, n_pages)
    att_s = att_s_b.reshape(n_s, ATTN_WIDTH)

    x1, h2, rt, rw, cnt = _post_call(x_prompt.reshape(n_p, D_MODEL), att_p.reshape(n_p, ATTN_WIDTH),
                                     cv_p.reshape(n_p, CONV_WIDTH), mod_p, xs_l, att_s, cv_s, mod_s, S // TM,
                                     ln_ffn, w_out_bf, w_r_bf, b_r, ltri)

    counts = cnt[0, :N_EXPERTS].astype(jnp.int32)
    offs, tile_id, exp_id, seg_lo, seg_hi = _segments(counts, 2 * n_tok)
    e_col = jnp.arange(N_EXPERTS, dtype=jnp.int32)[:, None]
    start = lambda e_row: jnp.sum(jnp.where(e_row[None, :] == e_col, offs[:, None], 0), axis=0)
    pos0 = start(rt[0]) + rt[2]
    pos1 = start(rt[1]) + rt[3]

    xs_sorted = _dispatch_call(pos0, pos1, h2)
    ys = _experts_call(tile_id, exp_id, seg_lo, seg_hi, xs_sorted, w_gate[0], w_up[0], w_down[0])
    y_p, y_s = _combine_call(pos0, pos1, x1, rw, mod_p, mod_s, S // TM, ln_final.reshape(1, D_MODEL), ys)

    y_prompt = y_p.reshape(B, S, D_MODEL)
    y_sample = y_s.reshape(DB, L, D_MODEL)
    k_prompt = kt_p.reshape(B, N_SUB, HEAD_DIM, S).transpose(0, 3, 1, 2)[None]
    v_prompt = v4_p.reshape(1, B, S, N_HEADS, V_DIM)
    conv_prompt = st_p[None]
    k_sample = kf_s.reshape(1, DB, L, N_SUB, HEAD_DIM)
    v_sample = vf_s.reshape(1, DB, L, N_HEADS, V_DIM)
    conv_sample = cu_s.reshape(DB, L, CONV_WIDTH)[None, :, L - (CONV_K - 1):]
    return (y_prompt, y_sample, k_prompt, v_prompt, conv_prompt, k_sample, v_sample, conv_sample)
```

```python
import functools
import math

import jax
import jax.numpy as jnp
import numpy as np
from jax import lax
from jax.experimental import pallas as pl
from jax.experimental.pallas import tpu as pltpu

D_MODEL = 1024
ATTN_WIDTH = 512
CONV_WIDTH = 512
N_HEADS = 4
N_SUB = 8
HEAD_DIM = 64
V_DIM = 2 * HEAD_DIM
CONV_K = 3
ROPE_THETA = 10000.0
N_GROUPS = 4
EXP_PER_GROUP = 8
N_EXPERTS = 32
D_EXPERT = 256
NORM_EPS = 1e-6
SUBLN_EPS = 1e-5
LAM_INIT = 0.8 - 0.6 * math.exp(-0.3 * 0)
LOG2E = math.log2(math.e)
PAGE = 128
LANES = 128
ROUTE_GRP_LANE = 32

TM = 512
TQ = 512
TE = 512
ROW_DMA_UNROLL = 8
X_RING = 3
ONES_ROWS = 16
VMEM_LIMIT = 56 * 1024 * 1024

f32 = jnp.float32
bf16 = jnp.bfloat16


def _cparams(sem):
    return pltpu.CompilerParams(dimension_semantics=sem, vmem_limit_bytes=VMEM_LIMIT)


def _rms(x, g, eps):
    return x * lax.rsqrt(jnp.mean(x * x, axis=-1, keepdims=True) + eps) * g


def _silu(x):
    return x * (1.0 / (1.0 + jnp.exp(-x)))


def _mod_kernel(c_ref, w_ref, b_ref, o_ref):
    a = _silu(c_ref[...]).astype(bf16)
    o_ref[...] = jnp.dot(a, w_ref[...].astype(bf16), preferred_element_type=f32) + b_ref[...]


def _mod_call(c_all, w_ada, b_ada):
    n = c_all.shape[0]
    return pl.pallas_call(
        _mod_kernel,
        grid=(6,),
        in_specs=[pl.BlockSpec((n, D_MODEL), lambda j: (0, 0)),
                  pl.BlockSpec((D_MODEL, D_MODEL), lambda j: (0, j)),
                  pl.BlockSpec((1, D_MODEL), lambda j: (0, j))],
        out_specs=pl.BlockSpec((n, D_MODEL), lambda j: (0, j)),
        out_shape=jax.ShapeDtypeStruct((n, 6 * D_MODEL), f32),
        compiler_params=_cparams(("arbitrary",)),
        name="mod",
    )(c_all, w_ada, b_ada)


def _rope(t, cos, sin_signed, lo_mask):
    n = t.shape[-1]
    rot = jnp.where(lo_mask, pltpu.roll(t, n - HEAD_DIM // 2, 1), pltpu.roll(t, HEAD_DIM // 2, 1))
    return t * cos + rot * sin_signed


def _inproj_common(x, sh, sc, ln, w_ref, cos, sin_signed):
    h = (_rms(x, ln, NORM_EPS) * (1.0 + sc) + sh).astype(bf16)

    def sec(i):
        return jnp.dot(h, w_ref[:, i * 512:(i + 1) * 512], preferred_element_type=f32)

    lane = lax.broadcasted_iota(jnp.int32, (x.shape[0], 512), 1)
    lo_mask = (lane % HEAD_DIM) < (HEAD_DIM // 2)
    q = _rope(sec(0), cos, sin_signed, lo_mask) * (HEAD_DIM ** -0.5 * LOG2E)
    k = _rope(sec(1), cos, sin_signed, lo_mask)
    v = sec(2)
    bg = sec(3)
    cu = sec(4) * sec(5)
    return q, k, v, bg, cu


def _inproj_p_kernel(x_ref, sh_ref, sc_ref, ln_ref, w_ref, cos_ref, sin_ref, wc_ref,
                     qt_ref, kt_ref, kb_ref, v4_ref, vt_ref, cv_ref, st_ref, carry_ref):
    s = pl.program_id(0)
    b = pl.program_id(1)
    q, k, v, bg, cu = _inproj_common(x_ref[...], sh_ref[...], sc_ref[...], ln_ref[...], w_ref,
                                     cos_ref[...], sin_ref[...])
    qt_ref[...] = q.T.astype(bf16)
    kt_ref[...] = k.T
    kb_ref[...] = k.astype(bf16)
    vt_ref[...] = v.T.astype(bf16)
    for h in range(N_HEADS):
        v4_ref[pl.ds(h, v.shape[0], stride=N_HEADS), :] = v[:, h * V_DIM:(h + 1) * V_DIM]
    tm = cu.shape[0]
    prev = jnp.where(s > 0, carry_ref[b], 0.0)
    row = lax.broadcasted_iota(jnp.int32, cu.shape, 0)
    cu1 = jnp.where(row == 0, prev[1:2], pltpu.roll(cu, 1, 0))
    cu2 = jnp.where(row == 0, prev[0:1], jnp.where(row == 1, prev[1:2], pltpu.roll(cu, 2, 0)))
    wc = wc_ref[...]
    conv = wc[0:1] * cu2 + wc[1:2] * cu1 + wc[2:3] * cu
    cv_ref[...] = (bg * conv).astype(bf16)
    last2 = cu[tm - 2:tm]
    carry_ref[b, 0:2, :] = last2
    st_ref[b] = last2


def _inproj_p_call(x, mod3, ln_mix, w_in_bf, cos, sin_signed, w_conv):
    B, S, _ = x.shape
    ns = S // TM
    row = lambda s, b: (b, s, 0)
    col = lambda s, b: (b, 0, s)
    rows = lambda dt: jax.ShapeDtypeStruct((B, S, 512), dt)
    cols = lambda dt: jax.ShapeDtypeStruct((B, 512, S), dt)
    row_spec = pl.BlockSpec((None, TM, 512), row)
    col_spec = pl.BlockSpec((None, 512, TM), col)
    return pl.pallas_call(
        _inproj_p_kernel,
        grid=(ns, B),
        in_specs=[pl.BlockSpec((None, TM, D_MODEL), row),
                  pl.BlockSpec((None, 1, D_MODEL), lambda s, b: (b, 0, 0)),
                  pl.BlockSpec((None, 1, D_MODEL), lambda s, b: (b, 0, 1)),
                  pl.BlockSpec((1, D_MODEL), lambda s, b: (0, 0)),
                  pl.BlockSpec((D_MODEL, 3072), lambda s, b: (0, 0)),
                  pl.BlockSpec((TM, 512), lambda s, b: (s, 0)),
                  pl.BlockSpec((TM, 512), lambda s, b: (s, 0)),
                  pl.BlockSpec((CONV_K, CONV_WIDTH), lambda s, b: (0, 0))],
        out_specs=[col_spec, col_spec, row_spec, pl.BlockSpec((None, N_HEADS * TM, V_DIM), row),
                   pl.BlockSpec((None, None, 512, TM), lambda s, b: (b, s, 0, 0)), row_spec,
                   pl.BlockSpec((B, 2, CONV_WIDTH), lambda s, b: (0, 0, 0))],
        out_shape=[cols(bf16), cols(f32), rows(bf16), jax.ShapeDtypeStruct((B, N_HEADS * S, V_DIM), f32),
                   jax.ShapeDtypeStruct((B, ns, 512, TM), bf16), rows(bf16),
                   jax.ShapeDtypeStruct((B, 2, CONV_WIDTH), f32)],
        scratch_shapes=[pltpu.VMEM((B, 8, CONV_WIDTH), f32)],
        compiler_params=_cparams(("arbitrary", "arbitrary")),
        name="inproj_p",
    )(x, mod3, mod3, ln_mix, w_in_bf, cos, sin_signed, w_conv)


def _inproj_s_kernel(seq_len, x_ref, sh_ref, sc_ref, ln_ref, w_ref, cos_ref, sin_ref, wc_ref, st0_ref, st1_ref,
                     q_ref, kf_ref, vf_ref, cv_ref, cu_ref):
    q, k, v, bg, cu = _inproj_common(x_ref[...], sh_ref[...], sc_ref[...], ln_ref[...], w_ref,
                                     cos_ref[...], sin_ref[...])
    q_ref[...] = q.astype(bf16)
    kf_ref[...] = k
    vf_ref[...] = v
    l = lax.broadcasted_iota(jnp.int32, cu.shape, 0) % seq_len
    st0 = st0_ref[...]
    st1 = st1_ref[...]
    cu1 = jnp.where(l == 0, st1, pltpu.roll(cu, 1, 0))
    cu2 = jnp.where(l == 0, st0, jnp.where(l == 1, st1, pltpu.roll(cu, 2, 0)))
    wc = wc_ref[...]
    conv = wc[0:1] * cu2 + wc[1:2] * cu1 + wc[2:3] * cu
    cv_ref[...] = (bg * conv).astype(bf16)
    cu_ref[...] = cu


def _inproj_s_call(x, sh, sc, ln_mix, w_in_bf, cos, sin_signed, w_conv, st0, st1, seq_len):
    n = x.shape[0]
    full = lambda shape: pl.BlockSpec(shape, lambda i: (0,) * len(shape))
    return pl.pallas_call(
        functools.partial(_inproj_s_kernel, seq_len),
        grid=(1,),
        in_specs=[full((n, D_MODEL)), full((n, D_MODEL)), full((n, D_MODEL)), full((1, D_MODEL)),
                  full((D_MODEL, 3072)), full((n, 512)), full((n, 512)), full((CONV_K, CONV_WIDTH)),
                  full((n, CONV_WIDTH)), full((n, CONV_WIDTH))],
        out_specs=[full((n, 512))] * 5,
        out_shape=[jax.ShapeDtypeStruct((n, 512), bf16), jax.ShapeDtypeStruct((n, 512), f32),
                   jax.ShapeDtypeStruct((n, 512), f32), jax.ShapeDtypeStruct((n, 512), bf16),
                   jax.ShapeDtypeStruct((n, CONV_WIDTH), f32)],
        compiler_params=_cparams(("arbitrary",)),
        name="inproj_s",
    )(x, sh, sc, ln_mix, w_in_bf, cos, sin_signed, w_conv, st0, st1)


def _lam(lq1, lk1, lq2, lk2):
    a = jnp.sum(lq1 * lk1, axis=-1, keepdims=True)
    b = jnp.sum(lq2 * lk2, axis=-1, keepdims=True)
    return jnp.exp(a) - jnp.exp(b) + LAM_INIT


def _attn_p_body(i, qt_ref, k_ref, vt_ref, lam, g_ref, o_ref, m_ref, acc_ref):
    tq = qt_ref.shape[1]
    tk = tq
    m_ref[...] = jnp.full(m_ref.shape, -jnp.inf, f32)
    acc_ref[...] = jnp.zeros(acc_ref.shape, f32)

    def step(j, masked):
        k0 = pl.multiple_of(j * tk, tk)
        qt = qt_ref[...]
        row = lax.broadcasted_iota(jnp.int32, qt.shape, 0)
        zero = jnp.zeros_like(qt)
        q2t = jnp.concatenate([jnp.where(row < HEAD_DIM, qt, zero), jnp.where(row >= HEAD_DIM, qt, zero)], axis=1)
        st = jnp.dot(k_ref[pl.ds(k0, tk), :], q2t, preferred_element_type=f32)
        if masked:
            kpos = lax.broadcasted_iota(jnp.int32, st.shape, 0)
            c = lax.broadcasted_iota(jnp.int32, st.shape, 1)
            st = jnp.where(kpos <= jnp.where(c >= tq, c - tq, c), st, -jnp.inf)
        m_prev = m_ref[...]
        m_new = jnp.maximum(m_prev, jnp.max(st, axis=0, keepdims=True))
        alpha = jnp.exp2(m_prev - m_new)
        pt = jnp.exp2(st - m_new).astype(bf16)
        vt1 = jnp.concatenate([vt_ref[j], jnp.ones((ONES_ROWS, tk), bf16)], axis=0)
        acc_ref[...] = alpha * acc_ref[...] + jnp.dot(vt1, pt, preferred_element_type=f32)
        m_ref[...] = m_new

    def below_diagonal(j, c):
        step(j, False)
        return c

    lax.fori_loop(0, i, below_diagonal, 0)
    step(i, True)
    ot = acc_ref[0:V_DIM, :] / acc_ref[V_DIM:V_DIM + 1, :]
    dt = ot[:, :tq] - lam * ot[:, tq:]
    ms = jnp.mean(dt * dt, axis=0, keepdims=True)
    nt = dt * lax.rsqrt(ms + SUBLN_EPS) * g_ref[...] * (1.0 - LAM_INIT)
    o_ref[...] = nt.T.astype(bf16)


def _attn_s_body(q_ref, kn_ref, vn_ref, lam, g_ref, kt_refs, v_refs, o_ref):
    n_pages = len(kt_refs)
    nq = q_ref.shape[0]
    nr = nq * N_SUB
    qf = q_ref[...].astype(f32)
    sub = lax.broadcasted_iota(jnp.int32, (N_SUB, ATTN_WIDTH), 0)
    col = lax.broadcasted_iota(jnp.int32, (N_SUB, ATTN_WIDTH), 1)
    diag = (col // HEAD_DIM) == sub
    qbd = jnp.concatenate([jnp.where(diag, jnp.broadcast_to(qf[a:a + 1], (N_SUB, ATTN_WIDTH)), 0.0)
                           for a in range(nq)], axis=0).astype(bf16)
    kt_all = jnp.concatenate([kt_refs[p][...].astype(bf16) for p in range(n_pages)], axis=1)
    s_past = jnp.dot(qbd, kt_all, preferred_element_type=f32)
    s_new = lax.dot_general(qbd, kn_ref[...].astype(bf16), (((1,), (1,)), ((), ())),
                            preferred_element_type=f32)
    r = lax.broadcasted_iota(jnp.int32, s_new.shape, 0)
    c = lax.broadcasted_iota(jnp.int32, s_new.shape, 1)
    s_new = jnp.where(c <= r // N_SUB, s_new, -jnp.inf)
    m = jnp.maximum(jnp.max(s_past, axis=-1, keepdims=True), jnp.max(s_new, axis=-1, keepdims=True))
    p_past = jnp.exp2(s_past - m)
    p_new = jnp.exp2(s_new - m)
    l = jnp.sum(p_past, axis=-1, keepdims=True) + jnp.sum(p_new, axis=-1, keepdims=True)
    p_bf = p_past.astype(bf16)
    vn = vn_ref[...]
    rowhead = (lax.broadcasted_iota(jnp.int32, (nr, V_DIM), 0) % N_SUB) // 2
    o = jnp.zeros((nr, V_DIM), f32)
    for h in range(N_HEADS):
        v_all = jnp.concatenate([v_refs[p][pl.ds(h, PAGE, stride=N_HEADS), :].astype(bf16)
                                 for p in range(n_pages)], axis=0)
        acc = jnp.dot(p_bf, v_all, preferred_element_type=f32)
        for a in range(nq):
            acc = acc + p_new[:, a:a + 1] * vn[a:a + 1, h * V_DIM:(h + 1) * V_DIM]
        o = jnp.where(rowhead == h, acc, o)
    o = o / l
    d = o - lam * pltpu.roll(o, nr - 1, 0)
    d = _rms(d, g_ref[...], SUBLN_EPS) * (1.0 - LAM_INIT)
    for a in range(nq):
        rowv = jnp.concatenate([d[a * N_SUB + 2 * h:a * N_SUB + 2 * h + 1] for h in range(N_HEADS)], axis=1)
        o_ref[a:a + 1, :] = rowv.astype(o_ref.dtype)


def _attn_kernel(n_pages, pt_ref, qt_ref, k_ref, vt_ref, lq1_ref, lk1_ref, lq2_ref, lk2_ref, gcol_ref,
                 q_ref, kn_ref, vn_ref, grow_ref, kt_hbm, v_hbm, op_ref, os_ref, m_ref, acc_ref, kbuf, vbuf, sems):
    t = (pl.program_id(0) * pl.num_programs(1) + pl.program_id(1)) * pl.num_programs(2) + pl.program_id(2)
    n_steps = pl.num_programs(0) * pl.num_programs(1) * pl.num_programs(2)
    slot = t % 2

    def fetch(seq, dst):
        for p in range(n_pages):
            page = pt_ref[seq * n_pages + p]
            pltpu.make_async_copy(kt_hbm.at[page], kbuf.at[dst, p], sems.at[dst]).start()
            pltpu.make_async_copy(v_hbm.at[page], vbuf.at[dst, p], sems.at[dst]).start()

    @pl.when(t == 0)
    def _():
        fetch(0, 0)

    pltpu.make_async_copy(kt_hbm.at[pl.ds(0, n_pages)], kbuf.at[slot], sems.at[slot]).wait()
    pltpu.make_async_copy(v_hbm.at[pl.ds(0, n_pages)], vbuf.at[slot], sems.at[slot]).wait()

    @pl.when(t + 1 < n_steps)
    def _():
        fetch(t + 1, 1 - slot)

    lam = _lam(lq1_ref[...], lk1_ref[...], lq2_ref[...], lk2_ref[...])
    _attn_p_body(pl.program_id(2), qt_ref, k_ref, vt_ref, lam, gcol_ref, op_ref, m_ref, acc_ref)
    _attn_s_body(q_ref, kn_ref, vn_ref, lam, grow_ref, [kbuf.at[slot, p] for p in range(n_pages)],
                 [vbuf.at[slot, p] for p in range(n_pages)], os_ref)


def _attn_call(pt_flat, qt, k, vt, lq1, lk1, lq2, lk2, g, q_s, kn_s, vn_s, kt, v2, n_pages):
    B, S, _ = k.shape
    nq_tiles = S // TQ
    nb, nq, _ = q_s.shape
    assert nb == B * N_HEADS * nq_tiles, "one sample sequence per prompt grid step"
    seq = lambda b, h, i: (b * N_HEADS + h) * nq_tiles + i
    small = lambda n: pl.BlockSpec((1, n), lambda b, h, i, pt: (0, 0))
    page_rows = kt.shape[1]
    return pl.pallas_call(
        functools.partial(_attn_kernel, n_pages),
        grid_spec=pltpu.PrefetchScalarGridSpec(
            num_scalar_prefetch=1,
            grid=(B, N_HEADS, nq_tiles),
            in_specs=[pl.BlockSpec((None, V_DIM, TQ), lambda b, h, i, pt: (b, h, i)),
                      pl.BlockSpec((None, S, V_DIM), lambda b, h, i, pt: (b, 0, h)),
                      pl.BlockSpec((None, nq_tiles, V_DIM, TQ), lambda b, h, i, pt: (b, 0, h, 0)),
                      small(HEAD_DIM), small(HEAD_DIM), small(HEAD_DIM), small(HEAD_DIM),
                      pl.BlockSpec((V_DIM, 1), lambda b, h, i, pt: (0, 0)),
                      pl.BlockSpec((None, nq, ATTN_WIDTH), lambda b, h, i, pt: (seq(b, h, i), 0, 0)),
                      pl.BlockSpec((None, nq, ATTN_WIDTH), lambda b, h, i, pt: (seq(b, h, i), 0, 0)),
                      pl.BlockSpec((None, nq, ATTN_WIDTH), lambda b, h, i, pt: (seq(b, h, i), 0, 0)),
                      small(V_DIM), pl.BlockSpec(memory_space=pl.ANY), pl.BlockSpec(memory_space=pl.ANY)],
            out_specs=[pl.BlockSpec((None, TQ, V_DIM), lambda b, h, i, pt: (b, i, h)),
                       pl.BlockSpec((None, nq, ATTN_WIDTH), lambda b, h, i, pt: (seq(b, h, i), 0, 0))],
            scratch_shapes=[pltpu.VMEM((1, 2 * TQ), f32), pltpu.VMEM((V_DIM + ONES_ROWS, 2 * TQ), f32),
                            pltpu.VMEM((2, n_pages, page_rows, PAGE), f32),
                            pltpu.VMEM((2, n_pages, page_rows, PAGE), f32), pltpu.SemaphoreType.DMA((2,))],
        ),
        out_shape=[jax.ShapeDtypeStruct((B, S, ATTN_WIDTH), bf16), jax.ShapeDtypeStruct((nb, nq, ATTN_WIDTH), bf16)],
        compiler_params=_cparams(("arbitrary",) * 3),
        name="attn",
    )(pt_flat, qt, k, vt, lq1, lk1, lq2, lk2, g.reshape(V_DIM, 1), q_s, kn_s, vn_s, g, kt, v2)


SLAB = D_MODEL // LANES


def _store_slabs(ref, row0, x):
    n = x.shape[0]
    for a in range(SLAB):
        ref[pl.ds(SLAB * row0 + a, n, stride=SLAB), :] = x[:, a * LANES:(a + 1) * LANES]


def _load_slabs(ref, row0, n):
    return jnp.concatenate([ref[pl.ds(SLAB * row0 + a, n, stride=SLAB), :] for a in range(SLAB)], axis=1)


def _slab_copy(src_ref, src_row, dst_ref, dst_row, sem):
    src = src_ref.at[pl.ds(pl.multiple_of(src_row * SLAB, SLAB), SLAB), :]
    dst = dst_ref.at[pl.ds(pl.multiple_of(dst_row * SLAB, SLAB), SLAB), :]
    return pltpu.make_async_copy(src, dst, sem)


def _post_tile(x_ref, att_ref, cv_ref, ga_ref, shf_ref, scf_ref, ln_ref, wo_ref, wr_ref, br_ref, ltri_ref,
               x1_ref, h2_ref, rt_ref, rw_ref, cnt_ref, carry_ref):
    mix = (jnp.dot(att_ref[...], wo_ref[0:ATTN_WIDTH, :], preferred_element_type=f32)
           + jnp.dot(cv_ref[...], wo_ref[ATTN_WIDTH:, :], preferred_element_type=f32))
    x1 = x_ref[...] + ga_ref[...] * mix
    x1_ref[...] = x1
    h2 = _rms(x1, ln_ref[...], NORM_EPS) * (1.0 + scf_ref[...]) + shf_ref[...]
    _store_slabs(h2_ref, 0, h2)
    logits = jnp.dot(h2.astype(bf16), wr_ref[...], preferred_element_type=f32) + br_ref[...]
    lane = lax.broadcasted_iota(jnp.int32, logits.shape, 1)
    lane_f = lane.astype(f32)
    big = jnp.float32(1e9)
    neg = -jnp.inf

    def first_max(vals):
        mx = jnp.max(vals, axis=-1, keepdims=True)
        idx = jnp.min(jnp.where(vals == mx, lane_f, big), axis=-1, keepdims=True)
        return mx, idx

    gl = jnp.where((lane >= ROUTE_GRP_LANE) & (lane < ROUTE_GRP_LANE + N_GROUPS), logits, neg)
    gmax, gidx = first_max(gl)
    g_p = 1.0 / jnp.sum(jnp.exp(gl - gmax), axis=-1, keepdims=True)
    lo = (gidx - ROUTE_GRP_LANE) * EXP_PER_GROUP
    el = jnp.where((lane_f >= lo) & (lane_f < lo + EXP_PER_GROUP), logits, neg)
    v1, i1 = first_max(el)
    el2 = jnp.where(lane_f == i1, neg, el)
    v2, i2 = first_max(el2)
    t = jnp.exp(v2 - v1)
    w1 = g_p / (1.0 + t)
    w2 = g_p * t / (1.0 + t)
    oh1 = lane_f == i1
    oh2 = lane_f == i2
    cnt = jnp.where(oh1 | oh2, 1.0, 0.0)
    prefix = jnp.dot(ltri_ref[...], cnt.astype(bf16), preferred_element_type=f32) + carry_ref[...]
    r1 = jnp.sum(jnp.where(oh1, prefix, 0.0), axis=-1, keepdims=True)
    r2 = jnp.sum(jnp.where(oh2, prefix, 0.0), axis=-1, keepdims=True)
    ri = jnp.where(lane == 0, i1, jnp.where(lane == 1, i2, jnp.where(lane == 2, r1, jnp.where(lane == 3, r2, 0.0))))
    rt_ref[...] = ri.T[0:8, :].astype(jnp.int32)
    rw_ref[...] = jnp.where(lane == 0, w1, jnp.where(lane == 1, w2, 0.0))
    new_carry = carry_ref[...] + jnp.sum(cnt, axis=0, keepdims=True)
    carry_ref[...] = new_carry
    cnt_ref[...] = new_carry


def _post_kernel(n_tiles_p, xp_ref, attp_ref, cvp_ref, gap_ref, shfp_ref, scfp_ref,
                 xs_ref, atts_ref, cvs_ref, gas_ref, shfs_ref, scfs_ref, *rest):
    i = pl.program_id(0)
    carry_ref = rest[-1]

    @pl.when(i == 0)
    def _():
        carry_ref[...] = jnp.zeros(carry_ref.shape, f32)

    @pl.when(i < n_tiles_p)
    def _():
        _post_tile(xp_ref, attp_ref, cvp_ref, gap_ref, shfp_ref, scfp_ref, *rest)

    @pl.when(i >= n_tiles_p)
    def _():
        _post_tile(xs_ref, atts_ref, cvs_ref, gas_ref, shfs_ref, scfs_ref, *rest)


def _post_call(x_p, att_p, cv_p, mod_p, x_s, att_s, cv_s, mod_s, tiles_per_batch, ln_ffn, w_out_bf, w_r_bf, b_r, ltri):
    n_p, n_s = x_p.shape[0], x_s.shape[0]
    tp, ts = n_p // TM, n_s // TM
    n = n_p + n_s
    prow = lambda w: pl.BlockSpec((TM, w), lambda i: (jnp.minimum(i, tp - 1), 0))
    srow = lambda w: pl.BlockSpec((TM, w), lambda i: (jnp.maximum(i - tp, 0), 0))
    pmod = lambda sec: pl.BlockSpec((None, 1, D_MODEL), lambda i: (jnp.minimum(i, tp - 1) // tiles_per_batch, 0, sec))
    smod = lambda sec: pl.BlockSpec((TM, D_MODEL), lambda i: (jnp.maximum(i - tp, 0), sec))
    row = lambda w: pl.BlockSpec((TM, w), lambda i: (i, 0))
    const = lambda shape: pl.BlockSpec(shape, lambda i: (0, 0))
    return pl.pallas_call(
        functools.partial(_post_kernel, tp),
        grid=(tp + ts,),
        in_specs=[prow(D_MODEL), prow(ATTN_WIDTH), prow(CONV_WIDTH), pmod(2), pmod(3), pmod(4),
                  srow(D_MODEL), srow(ATTN_WIDTH), srow(CONV_WIDTH), smod(2), smod(3), smod(4),
                  const((1, D_MODEL)), const((D_MODEL, D_MODEL)), const((D_MODEL, LANES)), const((1, LANES)),
                  const((TM, TM))],
        out_specs=[row(D_MODEL), pl.BlockSpec((TM * SLAB, LANES), lambda i: (i, 0)),
                   pl.BlockSpec((8, TM), lambda i: (0, i)), row(LANES), const((1, LANES))],
        out_shape=[jax.ShapeDtypeStruct((n, D_MODEL), f32), jax.ShapeDtypeStruct((n * SLAB, LANES), f32),
                   jax.ShapeDtypeStruct((8, n), jnp.int32), jax.ShapeDtypeStruct((n, LANES), f32),
                   jax.ShapeDtypeStruct((1, LANES), f32)],
        scratch_shapes=[pltpu.VMEM((1, LANES), f32)],
        compiler_params=_cparams(("arbitrary",)),
        name="post",
    )(x_p, att_p, cv_p, mod_p, mod_p, mod_p, x_s, att_s, cv_s, mod_s, mod_s, mod_s,
      ln_ffn, w_out_bf, w_r_bf, b_r, ltri)


def _dispatch_kernel(pos0_ref, pos1_ref, h2_hbm, xs_ref, buf, lsem, wsem):
    i = pl.program_id(0)
    n = pl.num_programs(0)
    rows = TM * SLAB
    slot = i % 3

    def load(tile, dst):
        src = h2_hbm.at[pl.ds(pl.multiple_of(tile * rows, rows), rows), :]
        return pltpu.make_async_copy(src, buf.at[dst], lsem.at[dst])

    def wait_rows(parity):
        tile_copy = pltpu.make_async_copy(buf.at[0], xs_ref.at[pl.ds(0, rows), :], wsem.at[parity])
        tile_copy.wait()
        tile_copy.wait()

    @pl.when(i == 0)
    def _():
        load(0, 0).start()

        @pl.when(n > 1)
        def _():
            load(1, 1).start()

    load(i, slot).wait()
    src_ref = buf.at[slot]
    par = i % 2

    def issue(r, c):
        t = i * TM + r
        _slab_copy(src_ref, r, xs_ref, pos0_ref[t], wsem.at[par]).start(priority=0)
        _slab_copy(src_ref, r, xs_ref, pos1_ref[t], wsem.at[par]).start(priority=1)
        return c

    lax.fori_loop(0, TM, issue, 0, unroll=ROW_DMA_UNROLL)

    @pl.when(i >= 1)
    def _():
        wait_rows(1 - par)

    @pl.when(i + 2 < n)
    def _():
        load(i + 2, (i + 2) % 3).start()

    @pl.when(i == n - 1)
    def _():
        wait_rows(par)


def _dispatch_call(pos0, pos1, h2):
    tiles = h2.shape[0] // (TM * SLAB)
    return pl.pallas_call(
        _dispatch_kernel,
        grid_spec=pltpu.PrefetchScalarGridSpec(
            num_scalar_prefetch=2,
            grid=(tiles,),
            in_specs=[pl.BlockSpec(memory_space=pl.ANY)],
            out_specs=pl.BlockSpec(memory_space=pl.ANY),
            scratch_shapes=[pltpu.VMEM((3, TM * SLAB, LANES), f32), pltpu.SemaphoreType.DMA((3,)),
                            pltpu.SemaphoreType.DMA((2,))],
        ),
        out_shape=jax.ShapeDtypeStruct((2 * h2.shape[0], LANES), f32),
        compiler_params=_cparams(("arbitrary",)),
        name="dispatch",
    )(pos0, pos1, h2)


def _tile_chunk_copies(hbm_ref, tile, buf_ref, slot, sem, to_hbm):
    r0 = pl.multiple_of(tile * TE, TE)
    pairs = [(hbm_ref.at[pl.ds(r0, TE), a, :], buf_ref.at[slot, a]) for a in range(SLAB)]
    return [pltpu.make_async_copy(v, h, sem) if to_hbm else pltpu.make_async_copy(h, v, sem) for h, v in pairs]


def _experts_kernel(n_tiles, tile_ref, exp_ref, lo_ref, hi_ref, xs_ref, wg_ref, wu_ref, wd_ref, ys_ref,
                    wg_bf, wu_bf, wd_bf, cur_ref, xbuf, ybuf, xsem, ysem):
    w = pl.program_id(0)
    lo = lo_ref[w]
    hi = hi_ref[w]
    k = tile_ref[w]
    base = k * TE
    slot = k % 2
    xslot = k % X_RING

    @pl.when(w == 0)
    def _():
        cur_ref[0] = -1
        for k0 in range(X_RING - 1):
            for c in _tile_chunk_copies(xs_ref, k0, xbuf, k0, xsem.at[k0], False):
                c.start()

    @pl.when(hi > lo)
    def _():
        first = lo == base
        last = hi == base + TE

        @pl.when(first)
        def _():
            for c in _tile_chunk_copies(xs_ref, k, xbuf, xslot, xsem.at[xslot], False):
                c.wait()

            @pl.when(k + X_RING - 1 < n_tiles)
            def _():
                nxt = (k + X_RING - 1) % X_RING
                for c in _tile_chunk_copies(xs_ref, k + X_RING - 1, xbuf, nxt, xsem.at[nxt], False):
                    c.start()

            @pl.when(k >= 2)
            def _():
                for c in _tile_chunk_copies(ys_ref, k - 2, ybuf, slot, ysem.at[slot], True):
                    c.wait()

        e = exp_ref[w]

        @pl.when(cur_ref[0] != e)
        def _():
            wg_bf[...] = wg_ref[...].astype(bf16)
            wu_bf[...] = wu_ref[...].astype(bf16)
            wd_bf[...] = wd_ref[...].astype(bf16)
            cur_ref[0] = e

        x = jnp.concatenate([xbuf[xslot, a] for a in range(SLAB)], axis=1).astype(bf16)
        g = jnp.dot(x, wg_bf[...], preferred_element_type=f32)
        u = jnp.dot(x, wu_bf[...], preferred_element_type=f32)
        hid = (_silu(g) * u).astype(bf16)
        y = jnp.dot(hid, wd_bf[...], preferred_element_type=f32)
        row = base + lax.broadcasted_iota(jnp.int32, (TE, LANES), 0)
        mine = (row >= lo) & (row < hi)

        @pl.when(first)
        def _():
            for a in range(SLAB):
                ybuf[slot, a] = jnp.where(mine, y[:, a * LANES:(a + 1) * LANES], 0.0)

        @pl.when(jnp.logical_not(first))
        def _():
            for a in range(SLAB):
                ybuf[slot, a] = jnp.where(mine, y[:, a * LANES:(a + 1) * LANES], ybuf[slot, a])

        @pl.when(last)
        def _():
            for c in _tile_chunk_copies(ys_ref, k, ybuf, slot, ysem.at[slot], True):
                c.start()

    @pl.when(w == pl.num_programs(0) - 1)
    def _():
        for kk in (n_tiles - 2, n_tiles - 1):
            for c in _tile_chunk_copies(ys_ref, kk, ybuf, kk % 2, ysem.at[kk % 2], True):
                c.wait()


def _experts_call(tile_id, exp_id, seg_lo, seg_hi, xs, w_gate, w_up, w_down):
    n_items = tile_id.shape[0]
    n_rows = xs.shape[0] // SLAB
    n_tiles = n_rows // TE
    assert n_tiles >= X_RING
    idx = lambda w, t, e, lo, hi: (e[w], 0, 0)
    ys = pl.pallas_call(
        functools.partial(_experts_kernel, n_tiles),
        grid_spec=pltpu.PrefetchScalarGridSpec(
            num_scalar_prefetch=4,
            grid=(n_items,),
            in_specs=[pl.BlockSpec(memory_space=pl.ANY),
                      pl.BlockSpec((None, D_MODEL, D_EXPERT), idx),
                      pl.BlockSpec((None, D_MODEL, D_EXPERT), idx),
                      pl.BlockSpec((None, D_EXPERT, D_MODEL), idx)],
            out_specs=pl.BlockSpec(memory_space=pl.ANY),
            scratch_shapes=[pltpu.VMEM((D_MODEL, D_EXPERT), bf16), pltpu.VMEM((D_MODEL, D_EXPERT), bf16),
                            pltpu.VMEM((D_EXPERT, D_MODEL), bf16), pltpu.SMEM((1,), jnp.int32),
                            pltpu.VMEM((X_RING, SLAB, TE, LANES), f32), pltpu.VMEM((2, SLAB, TE, LANES), f32),
                            pltpu.SemaphoreType.DMA((X_RING,)), pltpu.SemaphoreType.DMA((2,))],
        ),
        out_shape=jax.ShapeDtypeStruct((n_rows, SLAB, LANES), f32),
        compiler_params=_cparams(("arbitrary",)),
        name="experts",
    )(tile_id, exp_id, seg_lo, seg_hi, xs.reshape(n_rows, SLAB, LANES), w_gate, w_up, w_down)
    return ys.reshape(xs.shape)


def _combine_kernel(n_tiles_p, pos0_ref, pos1_ref, x1_ref, rw_ref, gfp_ref, gfs_ref, ln_ref, ys_ref,
                    op_ref, os_ref, ybuf, sems):
    i = pl.program_id(0)
    n = pl.num_programs(0)
    half = 2 * TM

    def issue(tile, buf):
        def body(r, c):
            t = tile * TM + r
            _slab_copy(ys_ref, pos0_ref[t], ybuf, buf * half + r, sems.at[buf]).start(priority=0)
            _slab_copy(ys_ref, pos1_ref[t], ybuf, buf * half + TM + r, sems.at[buf]).start(priority=1)
            return c

        lax.fori_loop(0, TM, body, 0, unroll=ROW_DMA_UNROLL)

    @pl.when(i == 0)
    def _():
        issue(0, 0)

    @pl.when(i + 1 < n)
    def _():
        issue(i + 1, (i + 1) % 2)

    buf = i % 2
    tile_copy = pltpu.make_async_copy(ys_ref.at[pl.ds(0, TM * SLAB), :], ybuf.at[pl.ds(0, TM * SLAB), :], sems.at[buf])
    tile_copy.wait()
    tile_copy.wait()
    rw = rw_ref[...]
    moe = rw[:, 0:1] * _load_slabs(ybuf, buf * half, TM) + rw[:, 1:2] * _load_slabs(ybuf, buf * half + TM, TM)

    def finish(gf_ref, o_ref):
        x2 = x1_ref[...] + gf_ref[...] * moe
        o_ref[...] = _rms(x2, ln_ref[...], NORM_EPS)

    @pl.when(i < n_tiles_p)
    def _():
        finish(gfp_ref, op_ref)

    @pl.when(i >= n_tiles_p)
    def _():
        finish(gfs_ref, os_ref)


def _combine_call(pos0, pos1, x1, rw, mod_p, mod_s, tiles_per_batch, ln_final, ys):
    n = x1.shape[0]
    n_s = mod_s.shape[0]
    tp, ts = (n - n_s) // TM, n_s // TM
    pidx = lambda i: jnp.minimum(i, tp - 1)
    sidx = lambda i: jnp.maximum(i - tp, 0)
    return pl.pallas_call(
        functools.partial(_combine_kernel, tp),
        grid_spec=pltpu.PrefetchScalarGridSpec(
            num_scalar_prefetch=2,
            grid=(tp + ts,),
            in_specs=[pl.BlockSpec((TM, D_MODEL), lambda i, p0, p1: (i, 0)),
                      pl.BlockSpec((TM, LANES), lambda i, p0, p1: (i, 0)),
                      pl.BlockSpec((None, 1, D_MODEL), lambda i, p0, p1: (pidx(i) // tiles_per_batch, 0, 5)),
                      pl.BlockSpec((TM, D_MODEL), lambda i, p0, p1: (sidx(i), 5)),
                      pl.BlockSpec((1, D_MODEL), lambda i, p0, p1: (0, 0)),
                      pl.BlockSpec(memory_space=pl.ANY)],
            out_specs=[pl.BlockSpec((TM, D_MODEL), lambda i, p0, p1: (pidx(i), 0)),
                       pl.BlockSpec((TM, D_MODEL), lambda i, p0, p1: (sidx(i), 0))],
            scratch_shapes=[pltpu.VMEM((2 * 2 * TM * SLAB, LANES), f32), pltpu.SemaphoreType.DMA((2,))],
        ),
        out_shape=[jax.ShapeDtypeStruct((n - n_s, D_MODEL), f32), jax.ShapeDtypeStruct((n_s, D_MODEL), f32)],
        compiler_params=_cparams(("arbitrary",)),
        name="combine",
    )(pos0, pos1, x1, rw, mod_p, mod_s, ln_final, ys)


def _rope_tables(pos):
    inv = 1.0 / (ROPE_THETA ** (np.arange(0, HEAD_DIM, 2, dtype=np.float64) / HEAD_DIM))
    ang = np.asarray(pos, np.float64)[:, None] * inv[None, :]
    ang = np.concatenate([ang, ang], axis=-1)
    sign = np.where(np.arange(HEAD_DIM) < HEAD_DIM // 2, -1.0, 1.0)
    cos = np.tile(np.cos(ang), (1, N_SUB)).astype(np.float32)
    sin_signed = np.tile(np.sin(ang) * sign[None, :], (1, N_SUB)).astype(np.float32)
    return jnp.asarray(cos), jnp.asarray(sin_signed)


def _segments(counts, n_rows):
    n_tiles = n_rows // TE
    offs = jnp.concatenate([jnp.zeros((1,), jnp.int32), jnp.cumsum(counts)[:-1].astype(jnp.int32)])
    tiles = jnp.arange(n_tiles, dtype=jnp.int32) * TE
    rank_t = jnp.arange(n_tiles, dtype=jnp.int32) + jnp.sum(offs[None, :] < tiles[:, None], axis=1).astype(jnp.int32)
    rank_o = jnp.arange(N_EXPERTS, dtype=jnp.int32) + jnp.minimum(offs // TE + 1, n_tiles)
    vals = jnp.concatenate([tiles, offs])
    ranks = jnp.concatenate([rank_t, rank_o])
    n_items = n_tiles + N_EXPERTS
    w = jnp.arange(n_items, dtype=jnp.int32)
    seg_lo = jnp.sum(jnp.where(ranks[None, :] == w[:, None], vals[None, :], 0), axis=1).astype(jnp.int32)
    seg_hi = jnp.concatenate([seg_lo[1:], jnp.full((1,), n_rows, jnp.int32)])
    tile_id = jnp.minimum(seg_lo // TE, n_tiles - 1)
    exp_id = jnp.sum(offs[None, :] <= seg_lo[:, None], axis=1).astype(jnp.int32) - 1
    return offs, tile_id, exp_id, seg_lo, seg_hi


def kernel(x_prompt, x_sample, cache_k, cache_v, state_conv, page_table, c_prompt, c_sample, w_ada, b_ada, ln_mix, w_in, lam_q1, lam_k1, lam_q2, lam_k2, subln_g, w_conv, w_out, ln_ffn, w_router_grp, b_router_grp, w_router_exp, b_router_exp, w_gate, w_up, w_down, ln_final):
    B, S, _ = x_prompt.shape
    DB, L, _ = x_sample.shape
    n_phys = cache_k.shape[1]
    n_pages = page_table.shape[1]
    past = n_pages * PAGE
    n_p = B * S
    n_s = DB * L
    n_tok = n_p + n_s

    w_in_bf = w_in[0].astype(bf16)
    w_out_bf = w_out[0].astype(bf16)
    assert ROUTE_GRP_LANE == N_EXPERTS
    pad = LANES - N_EXPERTS - N_GROUPS
    w_r_bf = jnp.concatenate([w_router_exp[0], w_router_grp[0], jnp.zeros((D_MODEL, pad), f32)], axis=1).astype(bf16)
    b_r = jnp.concatenate([b_router_exp[0], b_router_grp[0], jnp.zeros((pad,), f32)])[None]
    cos_p, sin_p = _rope_tables(np.arange(S))
    cos_s, sin_s = _rope_tables(past + np.tile(np.arange(L), DB))
    ltri = jnp.asarray(np.tril(np.ones((TM, TM), np.float32), -1), bf16)

    mod = _mod_call(jnp.concatenate([c_prompt, c_sample], axis=0), w_ada[0], b_ada)
    mod_p = mod[:B].reshape(B, 1, 6 * D_MODEL)
    seq_of_row = np.arange(n_s, dtype=np.int32) // L
    mod_s = jnp.take(mod, B + seq_of_row, axis=0)

    qt_p, kt_p, kb_p, v4_p, vt_p, cv_p, st_p = _inproj_p_call(x_prompt, mod_p, ln_mix, w_in_bf, cos_p, sin_p, w_conv[0])
    xs_l = x_sample.reshape(n_s, D_MODEL)
    st_rows = [jnp.take(state_conv[0, :, t], seq_of_row, axis=0) for t in range(CONV_K - 1)]
    q_s, kf_s, vf_s, cv_s, cu_s = _inproj_s_call(xs_l, mod_s[:, 0:D_MODEL], mod_s[:, D_MODEL:2 * D_MODEL], ln_mix,
                                                 w_in_bf, cos_s, sin_s, w_conv[0], st_rows[0], st_rows[1], L)
    by_seq = lambda a: a.reshape(DB, L, -1)
    kt = jnp.transpose(cache_k[0], (0, 2, 3, 1)).reshape(n_phys, N_SUB * HEAD_DIM, PAGE)
    v2 = cache_v[0].reshape(n_phys, PAGE * N_HEADS, V_DIM)
    att_p, att_s_b = _attn_call(page_table.reshape(-1), qt_p, kb_p, vt_p, lam_q1, lam_k1, lam_q2, lam_k2, subln_g,
                                by_seq(q_s), by_seq(kf_s), by_seq(vf_s), kt, v2, n_pages)
    att_s = att_s_b.reshape(n_s, ATTN_WIDTH)

    x1, h2, rt, rw, cnt = _post_call(x_prompt.reshape(n_p, D_MODEL), att_p.reshape(n_p, ATTN_WIDTH),
                                     cv_p.reshape(n_p, CONV_WIDTH), mod_p, xs_l, att_s, cv_s, mod_s, S // TM,
                                     ln_ffn, w_out_bf, w_r_bf, b_r, ltri)

    counts = cnt[0, :N_EXPERTS].astype(jnp.int32)
    offs, tile_id, exp_id, seg_lo, seg_hi = _segments(counts, 2 * n_tok)
    e_col = jnp.arange(N_EXPERTS, dtype=jnp.int32)[:, None]
    start = lambda e_row: jnp.sum(jnp.where(e_row[None, :] == e_col, offs[:, None], 0), axis=0)
    pos0 = start(rt[0]) + rt[2]
    pos1 = start(rt[1]) + rt[3]

    xs_sorted = _dispatch_call(pos0, pos1, h2)
    ys = _experts_call(tile_id, exp_id, seg_lo, seg_hi, xs_sorted, w_gate[0], w_up[0], w_down[0])
    y_p, y_s = _combine_call(pos0, pos1, x1, rw, mod_p, mod_s, S // TM, ln_final.reshape(1, D_MODEL), ys)

    y_prompt = y_p.reshape(B, S, D_MODEL)
    y_sample = y_s.reshape(DB, L, D_MODEL)
    k_prompt = kt_p.reshape(B, N_SUB, HEAD_DIM, S).transpose(0, 3, 1, 2)[None]
    v_prompt = v4_p.reshape(1, B, S, N_HEADS, V_DIM)
    conv_prompt = st_p[None]
    k_sample = kf_s.reshape(1, DB, L, N_SUB, HEAD_DIM)
    v_sample = vf_s.reshape(1, DB, L, N_HEADS, V_DIM)
    conv_sample = cu_s.reshape(DB, L, CONV_WIDTH)[None, :, L - (CONV_K - 1):]
    return (y_prompt, y_sample, k_prompt, v_prompt, conv_prompt, k_sample, v_sample, conv_sample)
```
